```python
import math
import jax, jax.numpy as jnp
from jax import lax
import numpy as np

D_MODEL = 2048
BATCH = 2
SEQ = 8192
DEPTH = 2

HEAD_DIM = 128
N_MOBA_HEADS = 8
N_FOX_HEADS = 8
MOBA_BLOCK = 256
MOBA_TOPK = 3
MOBA_Q_CHUNK = 64
Q_CHUNK = 128
N_DSA_HEADS = 16
DSA_Q_RANK = 512
DSA_KV_RANK = 512
IDX_HEADS = 16
IDX_DIM = 64
DSA_TOPK_MAX = 256
REL_BUCKETS = 32
REL_MAX_DIST = 128
N_BIAS_HEADS = max(N_MOBA_HEADS, N_DSA_HEADS)
D_FF_DENSE = 5632
N_EXPERTS = 8
MOE_TOPK = 2
D_FF_EXPERT = 7168
EXPERT_ROWS = 256
FORGET_BIAS_CENTER = 2.0
LN_EPS = 1e-5
RMS_EPS = 1e-6
N_EVEN = (DEPTH + 1) // 2
N_ODD = DEPTH // 2
DEEPNORM_ALPHA = (2 * DEPTH) ** 0.25
DEEPNORM_BETA = (8 * DEPTH) ** -0.25
F32 = jnp.float32

kernel_name = 'hybrid_moba_fox_dsa_moe_deepnorm'


def layer_norm(x, g, b):
    xf = x.astype(F32)
    mu = jnp.mean(xf, axis=-1, keepdims=True)
    var = jnp.mean(jnp.square(xf - mu), axis=-1, keepdims=True)
    return ((xf - mu) * lax.rsqrt(var + LN_EPS) * g.astype(F32) + b.astype(F32)).astype(x.dtype)


def rms_norm(x, g):
    xf = x.astype(F32)
    return (xf * lax.rsqrt(jnp.mean(xf * xf, axis=-1, keepdims=True) + RMS_EPS) * g.astype(F32)).astype(x.dtype)


def rel_bucket(dist):
    n = jnp.maximum(dist, 0)
    exact = REL_BUCKETS // 2
    nf = jnp.maximum(n, 1).astype(F32)
    large = exact + (jnp.log(nf / exact) / math.log(REL_MAX_DIST / exact) * (REL_BUCKETS - exact)).astype(jnp.int32)
    large = jnp.minimum(large, REL_BUCKETS - 1)
    return jnp.where(n < exact, n, large)


def rel_bias_per_head(table, bucket):
    n_heads = bucket.shape[1]
    tab = table[:, :n_heads].T.astype(F32)
    hid = jnp.arange(n_heads).reshape((1, n_heads) + (1,) * (bucket.ndim - 2))
    return tab[hid, bucket]


def rel_bias_shared(table, bucket, n_heads):
    return jnp.moveaxis(table[bucket, :n_heads].astype(F32), -1, 0)


def split_heads(z, n_heads):
    b, t, _ = z.shape
    return z.reshape(b, t, n_heads, HEAD_DIM).transpose(0, 2, 1, 3)


def moba_attention(q, k, v, rel_table):
    B, H, T, Dh = q.shape
    nb = -(-T // MOBA_BLOCK)
    tp = nb * MOBA_BLOCK
    pad = ((0, 0), (0, 0), (0, tp - T), (0, 0))
    kp = jnp.pad(k, pad)
    vp = jnp.pad(v, pad)
    kb = kp.reshape(B, H, nb, MOBA_BLOCK, Dh)
    vb = vp.reshape(B, H, nb, MOBA_BLOCK, Dh)
    k_mean = jnp.mean(kb.astype(F32), axis=3)
    n_sel = min(MOBA_TOPK, nb)
    scale = Dh ** -0.5
    blk_ids = jnp.arange(nb)
    offs = jnp.arange(MOBA_BLOCK)
    gather_blocks = jax.vmap(jax.vmap(lambda blocks, idx: blocks[idx]))

    def chunk(ci):
        start = ci * MOBA_Q_CHUNK
        qc = lax.dynamic_slice_in_dim(q, start, MOBA_Q_CHUNK, axis=2)
        t = start + jnp.arange(MOBA_Q_CHUNK)
        own = start // MOBA_BLOCK
        gate = jnp.einsum('bhqd,bhnd->bhqn', qc.astype(F32), k_mean)
        gate = jnp.where(blk_ids < own, gate, -jnp.inf)
        _, sel = lax.top_k(gate, n_sel)
        valid = sel < own
        kg = gather_blocks(kb, sel)
        vg = gather_blocks(vb, sel)
        s_sel = jnp.einsum('bhqd,bhqnld->bhqnl', qc, kg).astype(F32) * scale
        pos = sel[..., None] * MOBA_BLOCK + offs
        s_sel = s_sel + rel_bias_per_head(rel_table, rel_bucket(t[:, None, None] - pos))
        s_sel = jnp.where(valid[..., None], s_sel, -jnp.inf)
        own_start = own * MOBA_BLOCK
        ko = lax.dynamic_slice_in_dim(kp, own_start, MOBA_BLOCK, axis=2)
        vo = lax.dynamic_slice_in_dim(vp, own_start, MOBA_BLOCK, axis=2)
        pos_o = own_start + offs
        s_own = jnp.einsum('bhqd,bhld->bhql', qc, ko).astype(F32) * scale + rel_bias_shared(rel_table, rel_bucket(t[:, None] - pos_o[None, :]), H)
        s_own = jnp.where(pos_o[None, :] <= t[:, None], s_own, -jnp.inf)
        logits = jnp.concatenate([s_sel.reshape(B, H, MOBA_Q_CHUNK, n_sel * MOBA_BLOCK), s_own], axis=-1)
        p = jax.nn.softmax(logits, axis=-1).astype(v.dtype)
        p_sel = p[..., :n_sel * MOBA_BLOCK].reshape(B, H, MOBA_Q_CHUNK, n_sel, MOBA_BLOCK)
        p_own = p[..., n_sel * MOBA_BLOCK:]
        return jnp.einsum('bhqnl,bhqnld->bhqd', p_sel, vg) + jnp.einsum('bhql,bhld->bhqd', p_own, vo)

    outs = lax.map(chunk, jnp.arange(T // MOBA_Q_CHUNK))
    return outs.transpose(1, 0, 3, 2, 4).reshape(B, T, H * Dh)


def forgetting_attention(q, k, v, log_f):
    B, H, T, Dh = q.shape
    csum = jnp.cumsum(log_f, axis=-1)
    scale = Dh ** -0.5
    s_pos = jnp.arange(T)

    def chunk(ci):
        start = ci * Q_CHUNK
        qc = lax.dynamic_slice_in_dim(q, start, Q_CHUNK, axis=2)
        cq = lax.dynamic_slice_in_dim(csum, start, Q_CHUNK, axis=2)
        t = start + jnp.arange(Q_CHUNK)
        s = jnp.einsum('bhqd,bhsd->bhqs', qc, k).astype(F32) * scale + (cq[..., :, None] - csum[:, :, None, :])
        s = jnp.where(s_pos[None, :] <= t[:, None], s, -jnp.inf)
        p = jax.nn.softmax(s, axis=-1).astype(v.dtype)
        return jnp.einsum('bhqs,bhsd->bhqd', p, v)

    outs = lax.map(chunk, jnp.arange(T // Q_CHUNK))
    return outs.transpose(1, 0, 3, 2, 4).reshape(B, T, H * Dh)


def moba_fox_mixer(x, w_in, b_forget, w_out, rel_table):
    wa = N_MOBA_HEADS * HEAD_DIM
    wb = N_FOX_HEADS * HEAD_DIM
    proj = x @ w_in
    cuts = [wa, 2 * wa, 3 * wa, 3 * wa + wb, 3 * wa + 2 * wb, 3 * wa + 3 * wb]
    qa, ka, va, qb, kb, vb, fb = jnp.split(proj, cuts, axis=-1)
    log_f = jax.nn.log_sigmoid(fb.astype(F32) + b_forget.astype(F32)).transpose(0, 2, 1)
    oa = moba_attention(split_heads(qa, N_MOBA_HEADS), split_heads(ka, N_MOBA_HEADS), split_heads(va, N_MOBA_HEADS), rel_table)
    ob = forgetting_attention(split_heads(qb, N_FOX_HEADS), split_heads(kb, N_FOX_HEADS), split_heads(vb, N_FOX_HEADS), log_f)
    return jnp.concatenate([oa, ob], axis=-1) @ w_out


def dsa_mixer(x, w_in, q_norm_g, kv_norm_g, w_uq, w_qidx, w_uk, w_uv, w_out, rel_table):
    B, T, _ = x.shape
    proj = x @ w_in
    c_q, c_kv, k_idx, w_idx = jnp.split(proj, [DSA_Q_RANK, DSA_Q_RANK + DSA_KV_RANK, DSA_Q_RANK + DSA_KV_RANK + IDX_DIM], axis=-1)
    c_q = rms_norm(c_q, q_norm_g)
    c_kv = rms_norm(c_kv, kv_norm_g)
    q_idx = (c_q @ w_qidx).reshape(B, T, IDX_HEADS, IDX_DIM)
    w_idx = w_idx.astype(F32) * IDX_HEADS ** -0.5
    k_sel = min(DSA_TOPK_MAX, T // 4)
    gather_rows = jax.vmap(lambda c, i: c[i])
    s_pos = jnp.arange(T)
    scale = HEAD_DIM ** -0.5

    def chunk(ci):
        start = ci * Q_CHUNK
        t = start + jnp.arange(Q_CHUNK)
        qi = lax.dynamic_slice_in_dim(q_idx, start, Q_CHUNK, axis=1)
        wi = lax.dynamic_slice_in_dim(w_idx, start, Q_CHUNK, axis=1)
        dots = jnp.einsum('bqhd,bsd->bqhs', qi, k_idx).astype(F32) * IDX_DIM ** -0.5
        score = jnp.einsum('bqhs,bqh->bqs', jax.nn.relu(dots), wi)
        score = jnp.where(s_pos[None, None, :] <= t[None, :, None], score, -jnp.inf)
        _, sel = lax.top_k(score, k_sel)
        valid = sel <= t[None, :, None]
        ckv_g = gather_rows(c_kv, sel)
        cq = lax.dynamic_slice_in_dim(c_q, start, Q_CHUNK, axis=1)
        q = (cq @ w_uq).reshape(B, Q_CHUNK, N_DSA_HEADS, HEAD_DIM)
        q_lat = jnp.einsum('bqhd,hrd->bqhr', q, w_uk)
        s = jnp.einsum('bqhr,bqkr->bhqk', q_lat, ckv_g).astype(F32) * scale
        bias = rel_table[rel_bucket(t[None, :, None] - sel), :N_DSA_HEADS].astype(F32)
        s = s + bias.transpose(0, 3, 1, 2)
        s = jnp.where(valid[:, None], s, -jnp.inf)
        p = jax.nn.softmax(s, axis=-1).astype(ckv_g.dtype)
        o_lat = jnp.einsum('bhqk,bqkr->bqhr', p, ckv_g)
        o = jnp.einsum('bqhr,hrd->bqhd', o_lat, w_uv)
        return o.reshape(B, Q_CHUNK, N_DSA_HEADS * HEAD_DIM)

    outs = lax.map(chunk, jnp.arange(T // Q_CHUNK))
    return outs.transpose(1, 0, 2, 3).reshape(B, T, N_DSA_HEADS * HEAD_DIM) @ w_out


def swiglu(x, w1, w3, w2):
    return (jax.nn.silu(x @ w1) * (x @ w3)) @ w2


def moe_swiglu(x, router, w1, w3, w2):
    B, T, D = x.shape
    N = B * T
    xf = x.reshape(N, D)
    logits = (xf @ router).astype(F32)
    top_logits, top_e = lax.top_k(logits, MOE_TOPK)
    gates = jax.nn.softmax(top_logits, axis=-1)
    n_assign = N * MOE_TOPK
    flat_e = top_e.reshape(-1).astype(jnp.int32)
    order = jnp.argsort(flat_e)
    sorted_e = flat_e[order]
    sorted_tok = (order // MOE_TOPK).astype(jnp.int32)
    counts = jnp.bincount(flat_e, length=N_EXPERTS)
    padded = (counts + EXPERT_ROWS - 1) // EXPERT_ROWS * EXPERT_ROWS
    pend = jnp.cumsum(padded)
    pstart = pend - padded
    ustart = jnp.cumsum(counts) - counts
    dest = (pstart[sorted_e] + jnp.arange(n_assign) - ustart[sorted_e]).astype(jnp.int32)
    n_rows = -(-(n_assign + N_EXPERTS * (EXPERT_ROWS - 1)) // EXPERT_ROWS) * EXPERT_ROWS
    n_blk = n_rows // EXPERT_ROWS
    row_tok = jnp.zeros((n_rows,), jnp.int32).at[dest].set(sorted_tok)
    blk_e = jnp.minimum(jnp.searchsorted(pend, jnp.arange(n_blk) * EXPERT_ROWS, side='right'), N_EXPERTS - 1)

    def expert_block(args):
        tok, e = args
        xb = xf[tok]
        h = jax.nn.silu(xb @ w1[e]) * (xb @ w3[e])
        return h @ w2[e]

    rows = lax.map(expert_block, (row_tok.reshape(n_blk, EXPERT_ROWS), blk_e)).reshape(n_rows, D)
    assign_row = jnp.zeros((n_assign,), jnp.int32).at[order].set(dest)
    y = rows[assign_row].reshape(N, MOE_TOPK, D)
    out = jnp.einsum('nkd,nk->nd', y, gates.astype(y.dtype))
    return out.reshape(B, T, D)


def setup_inputs(seed: int = 0) -> dict:
    key = jax.random.key(seed)
    keys = iter(jax.random.split(key, 48))

    def nrm(shape, scale):
        return jax.random.normal(next(keys), shape, F32) * scale

    D = D_MODEL
    beta = DEEPNORM_BETA
    wa = N_MOBA_HEADS * HEAD_DIM
    wb = N_FOX_HEADS * HEAD_DIM
    wc = N_DSA_HEADS * HEAD_DIM
    sd = D ** -0.5
    inp = {}
    inp['x'] = nrm((BATCH, SEQ, D), 1.0)
    inp['rel_table'] = nrm((REL_BUCKETS, N_BIAS_HEADS), 0.3)
    inp['ev_w_in'] = jnp.concatenate([
        nrm((N_EVEN, D, wa), sd), nrm((N_EVEN, D, wa), sd), nrm((N_EVEN, D, wa), beta * sd),
        nrm((N_EVEN, D, wb), sd), nrm((N_EVEN, D, wb), sd), nrm((N_EVEN, D, wb), beta * sd),
        nrm((N_EVEN, D, N_FOX_HEADS), sd)], axis=-1)
    inp['ev_b_forget'] = FORGET_BIAS_CENTER + nrm((N_EVEN, N_FOX_HEADS), 0.1)
    inp['ev_w_out'] = nrm((N_EVEN, wa + wb, D), beta * (wa + wb) ** -0.5)
    inp['ev_ln1_g'] = 1.0 + nrm((N_EVEN, D), 0.01)
    inp['ev_ln1_b'] = nrm((N_EVEN, D), 0.01)
    inp['ev_ffn_w1'] = nrm((N_EVEN, D, D_FF_DENSE), beta * sd)
    inp['ev_ffn_w3'] = nrm((N_EVEN, D, D_FF_DENSE), beta * sd)
    inp['ev_ffn_w2'] = nrm((N_EVEN, D_FF_DENSE, D), beta * D_FF_DENSE ** -0.5)
    inp['ev_ln2_g'] = 1.0 + nrm((N_EVEN, D), 0.01)
    inp['ev_ln2_b'] = nrm((N_EVEN, D), 0.01)
    inp['od_w_in'] = nrm((N_ODD, D, DSA_Q_RANK + DSA_KV_RANK + IDX_DIM + IDX_HEADS), sd)
    inp['od_q_norm_g'] = 1.0 + nrm((N_ODD, DSA_Q_RANK), 0.01)
    inp['od_kv_norm_g'] = 1.0 + nrm((N_ODD, DSA_KV_RANK), 0.01)
    inp['od_w_uq'] = nrm((N_ODD, DSA_Q_RANK, wc), DSA_Q_RANK ** -0.5)
    inp['od_w_qidx'] = nrm((N_ODD, DSA_Q_RANK, IDX_HEADS * IDX_DIM), DSA_Q_RANK ** -0.5)
    inp['od_w_uk'] = nrm((N_ODD, N_DSA_HEADS, DSA_KV_RANK, HEAD_DIM), DSA_KV_RANK ** -0.5)
    inp['od_w_uv'] = nrm((N_ODD, N_DSA_HEADS, DSA_KV_RANK, HEAD_DIM), beta * DSA_KV_RANK ** -0.5)
    inp['od_w_out'] = nrm((N_ODD, wc, D), beta * wc ** -0.5)
    inp['od_ln1_g'] = 1.0 + nrm((N_ODD, D), 0.01)
    inp['od_ln1_b'] = nrm((N_ODD, D), 0.01)
    inp['od_router'] = nrm((N_ODD, D, N_EXPERTS), sd)
    inp['od_exp_w1'] = nrm((N_ODD, N_EXPERTS, D, D_FF_EXPERT), beta * sd)
    inp['od_exp_w3'] = nrm((N_ODD, N_EXPERTS, D, D_FF_EXPERT), beta * sd)
    inp['od_exp_w2'] = nrm((N_ODD, N_EXPERTS, D_FF_EXPERT, D), beta * D_FF_EXPERT ** -0.5)
    inp['od_ln2_g'] = 1.0 + nrm((N_ODD, D), 0.01)
    inp['od_ln2_b'] = nrm((N_ODD, D), 0.01)
    return inp


def reference(x, rel_table, ev_w_in, ev_b_forget, ev_w_out, ev_ln1_g, ev_ln1_b, ev_ffn_w1, ev_ffn_w3, ev_ffn_w2, ev_ln2_g, ev_ln2_b, od_w_in, od_q_norm_g, od_kv_norm_g, od_w_uq, od_w_qidx, od_w_uk, od_w_uv, od_w_out, od_ln1_g, od_ln1_b, od_router, od_exp_w1, od_exp_w3, od_exp_w2, od_ln2_g, od_ln2_b):
    h = x
    for layer in range(DEPTH):
        i = layer // 2
        if layer % 2 == 0:
            mix = moba_fox_mixer(h, ev_w_in[i], ev_b_forget[i], ev_w_out[i], rel_table)
            h = layer_norm(DEEPNORM_ALPHA * h + mix, ev_ln1_g[i], ev_ln1_b[i])
            ff = swiglu(h, ev_ffn_w1[i], ev_ffn_w3[i], ev_ffn_w2[i])
            h = layer_norm(DEEPNORM_ALPHA * h + ff, ev_ln2_g[i], ev_ln2_b[i])
        else:
            mix = dsa_mixer(h, od_w_in[i], od_q_norm_g[i], od_kv_norm_g[i], od_w_uq[i], od_w_qidx[i], od_w_uk[i], od_w_uv[i], od_w_out[i], rel_table)
            h = layer_norm(DEEPNORM_ALPHA * h + mix, od_ln1_g[i], od_ln1_b[i])
            ff = moe_swiglu(h, od_router[i], od_exp_w1[i], od_exp_w3[i], od_exp_w2[i])
            h = layer_norm(DEEPNORM_ALPHA * h + ff, od_ln2_g[i], od_ln2_b[i])
    return h
```

```python
import functools
import math

import jax
import jax.numpy as jnp
from jax import lax
from jax.experimental import pallas as pl
from jax.experimental.pallas import tpu as pltpu

F32 = jnp.float32
BF16 = jnp.bfloat16
I32 = jnp.int32

HEAD_DIM = 128
N_MOBA_HEADS = 8
N_FOX_HEADS = 8
MOBA_BLOCK = 256
MOBA_TOPK = 3
N_DSA_HEADS = 16
DSA_Q_RANK = 512
DSA_KV_RANK = 512
IDX_HEADS = 16
IDX_DIM = 64
DSA_TOPK_MAX = 256
REL_BUCKETS = 32
REL_MAX_DIST = 128
N_EXPERTS = 8
MOE_TOPK = 2
LN_EPS = 1e-5
RMS_EPS = 1e-6
DEPTH = 2
DEEPNORM_ALPHA = (2 * DEPTH) ** 0.25

LANES = 128
ATTN_TILE = 256
EXPERT_TILE = 512
NEG = -1e30
INT_MIN = -(2 ** 31)
VMEM_LIMIT = 56 * 1024 * 1024


def _cparams(sem, vmem=VMEM_LIMIT):
    return pltpu.CompilerParams(dimension_semantics=sem, vmem_limit_bytes=vmem)


def _dot(a, b):
    return jnp.dot(a, b, preferred_element_type=F32)


def _dot_t(a, b):
    return lax.dot_general(a, b, (((1,), (1,)), ((), ())), preferred_element_type=F32)


def _layer_norm(y, g, b):
    mu = jnp.mean(y, axis=-1, keepdims=True)
    d = y - mu
    var = jnp.mean(d * d, axis=-1, keepdims=True)
    return d * lax.rsqrt(var + LN_EPS) * g + b


def _mm_kernel(x_ref, w_ref, cs_ref, o_ref, *, head_major):
    acc = _dot(x_ref[...].astype(BF16), w_ref[...]) * cs_ref[...]
    if head_major:
        for c in range(o_ref.shape[0]):
            o_ref[c] = acc[:, c * LANES:(c + 1) * LANES].astype(o_ref.dtype)
    else:
        o_ref[...] = acc.astype(o_ref.dtype)


def _mm(x, w, colscale, *, out_dtype, head_major=False, tm=512, tn=512, name):
    m, k = x.shape
    n = w.shape[1]
    tm, tn = min(tm, m), min(tn, n)
    assert m % tm == 0 and n % tn == 0 and tn % LANES == 0
    if head_major:
        out_shape = jax.ShapeDtypeStruct((n // LANES, m, LANES), out_dtype)
        out_spec = pl.BlockSpec((tn // LANES, tm, LANES), lambda i, j: (j, i, 0))
    else:
        out_shape = jax.ShapeDtypeStruct((m, n), out_dtype)
        out_spec = pl.BlockSpec((tm, tn), lambda i, j: (i, j))
    return pl.pallas_call(
        functools.partial(_mm_kernel, head_major=head_major),
        out_shape=out_shape,
        grid=(m // tm, n // tn),
        in_specs=[pl.BlockSpec((tm, k), lambda i, j: (i, 0)),
                  pl.BlockSpec((k, tn), lambda i, j: (0, j)),
                  pl.BlockSpec((1, tn), lambda i, j: (0, j))],
        out_specs=out_spec,
        compiler_params=_cparams(("parallel", "arbitrary")),
        name=name,
    )(x, w, colscale)


def _mm_ln_kernel(x_ref, w_ref, res_ref, g_ref, b_ref, o_ref):
    y = DEEPNORM_ALPHA * res_ref[...] + _dot(x_ref[...], w_ref[...])
    o_ref[...] = _layer_norm(y, g_ref[...], b_ref[...])


def _mm_ln(x, w, res, g, b, *, tm=256, name):
    m, k = x.shape
    d = w.shape[1]
    tm = min(tm, m)
    return pl.pallas_call(
        _mm_ln_kernel,
        out_shape=jax.ShapeDtypeStruct((m, d), F32),
        grid=(m // tm,),
        in_specs=[pl.BlockSpec((tm, k), lambda i: (i, 0)),
                  pl.BlockSpec((k, d), lambda i: (0, 0)),
                  pl.BlockSpec((tm, d), lambda i: (i, 0)),
                  pl.BlockSpec((1, d), lambda i: (0, 0)),
                  pl.BlockSpec((1, d), lambda i: (0, 0))],
        out_specs=pl.BlockSpec((tm, d), lambda i: (i, 0)),
        compiler_params=_cparams(("parallel",)),
        name=name,
    )(x, w, res, g, b)


def _swiglu_step(xb, w1_ref, w3_ref, w2_ref):
    a = _dot(xb, w1_ref[...])
    c = _dot(xb, w3_ref[...])
    hmid = a / (1.0 + jnp.exp(-a)) * c
    return _dot(hmid.astype(BF16), w2_ref[...])


def _ffn_ln_kernel(x_ref, w1_ref, w3_ref, w2_ref, g_ref, b_ref, o_ref, acc_ref, xb_ref):
    f = pl.program_id(1)

    @pl.when(f == 0)
    def _():
        acc_ref[...] = jnp.zeros_like(acc_ref)
        xb_ref[...] = x_ref[...].astype(BF16)

    acc_ref[...] += _swiglu_step(xb_ref[...], w1_ref, w3_ref, w2_ref)

    @pl.when(f == pl.num_programs(1) - 1)
    def _():
        y = DEEPNORM_ALPHA * x_ref[...] + acc_ref[...]
        o_ref[...] = _layer_norm(y, g_ref[...], b_ref[...])


def _ffn_ln(x, w1, w3, w2, g, b, *, tm=512, tf=512, name):
    m, d = x.shape
    dff = w1.shape[1]
    tm, tf = min(tm, m), min(tf, dff)
    assert m % tm == 0 and dff % tf == 0
    return pl.pallas_call(
        _ffn_ln_kernel,
        out_shape=jax.ShapeDtypeStruct((m, d), F32),
        grid=(m // tm, dff // tf),
        in_specs=[pl.BlockSpec((tm, d), lambda i, f: (i, 0)),
                  pl.BlockSpec((d, tf), lambda i, f: (0, f)),
                  pl.BlockSpec((d, tf), lambda i, f: (0, f)),
                  pl.BlockSpec((tf, d), lambda i, f: (f, 0)),
                  pl.BlockSpec((1, d), lambda i, f: (0, 0)),
                  pl.BlockSpec((1, d), lambda i, f: (0, 0))],
        out_specs=pl.BlockSpec((tm, d), lambda i, f: (i, 0)),
        scratch_shapes=[pltpu.VMEM((tm, d), F32), pltpu.VMEM((tm, d), BF16)],
        compiler_params=_cparams(("parallel", "arbitrary")),
        name=name,
    )(x, w1, w3, w2, g, b)


def _moe_ffn_kernel(te_ref, tv_ref, x_ref, w1_ref, w3_ref, w2_ref, o_ref, acc_ref, xb_ref):
    i, f = pl.program_id(0), pl.program_id(1)

    @pl.when(tv_ref[i] > 0)
    def _():
        @pl.when(f == 0)
        def _():
            acc_ref[...] = jnp.zeros_like(acc_ref)
            xb_ref[...] = x_ref[...].astype(BF16)

        acc_ref[...] += _swiglu_step(xb_ref[...], w1_ref, w3_ref, w2_ref)

        @pl.when(f == pl.num_programs(1) - 1)
        def _():
            o_ref[...] = acc_ref[...]

    @pl.when(tv_ref[i] == 0)
    def _():
        o_ref[...] = jnp.zeros_like(o_ref)


def _moe_ffn(tile_e, tile_valid, xs, w1, w3, w2, *, tm, tf=512, name):
    n_rows, d = xs.shape
    dff = w1.shape[2]
    tf = min(tf, dff)
    nf = dff // tf
    assert n_rows % tm == 0 and dff % tf == 0

    def fidx(i, f, tv):
        return jnp.where(tv[i] > 0, f, nf - 1)

    return pl.pallas_call(
        _moe_ffn_kernel,
        out_shape=jax.ShapeDtypeStruct((n_rows, d), F32),
        grid_spec=pltpu.PrefetchScalarGridSpec(
            num_scalar_prefetch=2,
            grid=(n_rows // tm, nf),
            in_specs=[pl.BlockSpec((tm, d), lambda i, f, te, tv: (i, 0)),
                      pl.BlockSpec((None, d, tf), lambda i, f, te, tv: (te[i], 0, fidx(i, f, tv))),
                      pl.BlockSpec((None, d, tf), lambda i, f, te, tv: (te[i], 0, fidx(i, f, tv))),
                      pl.BlockSpec((None, tf, d), lambda i, f, te, tv: (te[i], fidx(i, f, tv), 0))],
            out_specs=pl.BlockSpec((tm, d), lambda i, f, te, tv: (i, 0)),
            scratch_shapes=[pltpu.VMEM((tm, d), F32), pltpu.VMEM((tm, d), BF16)]),
        compiler_params=_cparams(("arbitrary", "arbitrary")),
        name=name,
    )(tile_e, tile_valid, xs, w1, w3, w2)


def _online_softmax_step(s, v, carry):
    m, l, acc = carry
    m_new = jnp.maximum(m, jnp.max(s, axis=-1, keepdims=True))
    alpha = jnp.exp(m - m_new)
    p = jnp.exp(s - m_new)
    l = alpha * l + jnp.sum(p, axis=-1, keepdims=True)
    acc = alpha * acc + _dot(p.astype(BF16), v)
    return m_new, l, acc


def _softmax_init(tq):
    return (jnp.full((tq, 1), NEG, F32), jnp.zeros((tq, 1), F32), jnp.zeros((tq, HEAD_DIM), F32))


def _kv_block(k_ref, v_ref, n, blk):
    rows = pl.ds(pl.multiple_of(n * blk, blk), blk)
    return k_ref[rows, :], v_ref[rows, :]


def _causal_mask(blk):
    r = lax.broadcasted_iota(I32, (blk, blk), 0)
    c = lax.broadcasted_iota(I32, (blk, blk), 1)
    return c <= r


def _moba_kernel(q_ref, k_ref, v_ref, bias_ref, o_ref, km_ref, *, blk, nb):
    i = pl.program_id(2)
    nbp = km_ref.shape[0]

    @pl.when(i == 0)
    def _():
        km_ref[...] = jnp.zeros_like(km_ref)

        def mean_body(n, _):
            kblk, _ = _kv_block(k_ref, v_ref, n, blk)
            km_ref[pl.ds(n, 1), :] = jnp.mean(kblk.astype(F32), axis=0, keepdims=True)
            return 0

        lax.fori_loop(0, nb, mean_body, 0)

    q = q_ref[...]
    gate = _dot_t(q, km_ref[...].astype(BF16))
    lane = lax.broadcasted_iota(I32, (blk, nbp), 1)
    lane_f = lane.astype(F32)
    g = jnp.where(lane < i, gate, -jnp.inf)
    sel = jnp.zeros((blk, nbp), F32)
    for _ in range(MOBA_TOPK):
        mx = jnp.max(g, axis=-1, keepdims=True)
        first = jnp.min(jnp.where(g == mx, lane_f, float(nbp)), axis=-1, keepdims=True)
        pick = jnp.logical_and(lane_f == first, mx > -jnp.inf)
        sel = jnp.where(pick, 1.0, sel)
        g = jnp.where(pick, -jnp.inf, g)

    def past_step(n, carry, bias):
        kblk, vblk = _kv_block(k_ref, v_ref, n, blk)
        s = _dot_t(q, kblk)
        if bias is not None:
            s = s + bias
        chosen = jnp.sum(jnp.where(lane == n, sel, 0.0), axis=-1, keepdims=True)
        s = jnp.where(chosen > 0.0, s, NEG)
        return _online_softmax_step(s, vblk, carry)

    carry = lax.fori_loop(0, jnp.maximum(i - 1, 0),
                          lambda n, c: past_step(n, c, None), _softmax_init(blk))
    carry = past_step(jnp.maximum(i - 1, 0), carry, bias_ref[1])
    kblk, vblk = _kv_block(k_ref, v_ref, i, blk)
    s = jnp.where(_causal_mask(blk), _dot_t(q, kblk) + bias_ref[0], NEG)
    _, l, acc = _online_softmax_step(s, vblk, carry)
    o_ref[...] = (acc / l).astype(o_ref.dtype)


def _moba_attention(qkv, bias, *, batch, seq, name):
    blk = MOBA_BLOCK
    nb = seq // blk
    nbp = -(-nb // LANES) * LANES
    h = N_MOBA_HEADS
    nq = seq // blk
    return pl.pallas_call(
        functools.partial(_moba_kernel, blk=blk, nb=nb),
        out_shape=jax.ShapeDtypeStruct((batch * seq, h * HEAD_DIM), BF16),
        grid=(batch, h, nq),
        in_specs=[pl.BlockSpec((blk, HEAD_DIM), lambda b, hh, i: (b * nq + i, hh)),
                  pl.BlockSpec((seq, HEAD_DIM), lambda b, hh, i: (b, h + hh)),
                  pl.BlockSpec((seq, HEAD_DIM), lambda b, hh, i: (b, 2 * h + hh)),
                  pl.BlockSpec((2, None, blk, blk), lambda b, hh, i: (0, hh, 0, 0))],
        out_specs=pl.BlockSpec((blk, HEAD_DIM), lambda b, hh, i: (b * nq + i, hh)),
        scratch_shapes=[pltpu.VMEM((nbp, HEAD_DIM), F32)],
        compiler_params=_cparams(("parallel", "parallel", "arbitrary")),
        name=name,
    )(qkv, qkv, qkv, bias)


def _fox_kernel(q_ref, k_ref, v_ref, ck_ref, o_ref, *, blk):
    i = pl.program_id(2)
    q = q_ref[...]

    def step(n, carry, causal):
        kblk, vblk = _kv_block(k_ref, v_ref, n, blk)
        s = _dot_t(q, kblk) - ck_ref[n]
        if causal:
            s = jnp.where(_causal_mask(blk), s, NEG)
        return _online_softmax_step(s, vblk, carry)

    carry = lax.fori_loop(0, i, lambda n, c: step(n, c, False), _softmax_init(blk))
    _, l, acc = step(i, carry, True)
    o_ref[...] = (acc / l).astype(o_ref.dtype)


def _fox_attention(qkv, csum, *, batch, seq, name):
    blk = ATTN_TILE
    h = N_FOX_HEADS
    nq = seq // blk
    base = 3 * N_MOBA_HEADS
    return pl.pallas_call(
        functools.partial(_fox_kernel, blk=blk),
        out_shape=jax.ShapeDtypeStruct((batch * seq, h * HEAD_DIM), BF16),
        grid=(batch, h, nq),
        in_specs=[pl.BlockSpec((blk, HEAD_DIM), lambda b, hh, i: (b * nq + i, base + hh)),
                  pl.BlockSpec((seq, HEAD_DIM), lambda b, hh, i: (b, base + h + hh)),
                  pl.BlockSpec((seq, HEAD_DIM), lambda b, hh, i: (b, base + 2 * h + hh)),
                  pl.BlockSpec((None, None, nq, 1, blk), lambda b, hh, i: (b, hh, 0, 0, 0))],
        out_specs=pl.BlockSpec((blk, HEAD_DIM), lambda b, hh, i: (b * nq + i, hh)),
        compiler_params=_cparams(("parallel", "parallel", "arbitrary")),
        name=name,
    )(qkv, qkv, qkv, csum)


def _dsa_in_kernel(x_ref, w_ref, gq_ref, gkv_ref, cq_ref, ckv_ref, kidx_ref, widx_ref):
    acc = _dot(x_ref[...].astype(BF16), w_ref[...])
    rq, rkv = DSA_Q_RANK, DSA_KV_RANK

    def rms(z, g):
        return z * lax.rsqrt(jnp.mean(z * z, axis=-1, keepdims=True) + RMS_EPS) * g

    cq_ref[...] = rms(acc[:, :rq], gq_ref[...]).astype(cq_ref.dtype)
    ckv_ref[...] = rms(acc[:, rq:rq + rkv], gkv_ref[...]).astype(ckv_ref.dtype)
    kidx_ref[...] = acc[:, rq + rkv:rq + rkv + LANES].astype(kidx_ref.dtype)
    widx_ref[...] = acc[:, rq + rkv + LANES:] * (IDX_HEADS ** -0.5 * IDX_DIM ** -0.5)


def _dsa_in_proj(x, w, gq, gkv, *, tm=512, name):
    m, k = x.shape
    n = w.shape[1]
    tm = min(tm, m)
    rq, rkv = DSA_Q_RANK, DSA_KV_RANK
    row = lambda i: (i, 0)
    const = lambda i: (0, 0)
    return pl.pallas_call(
        _dsa_in_kernel,
        out_shape=(jax.ShapeDtypeStruct((m, rq), BF16), jax.ShapeDtypeStruct((m, rkv), BF16),
                   jax.ShapeDtypeStruct((m, LANES), BF16), jax.ShapeDtypeStruct((m, LANES), F32)),
        grid=(m // tm,),
        in_specs=[pl.BlockSpec((tm, k), row), pl.BlockSpec((k, n), const),
                  pl.BlockSpec((1, rq), const), pl.BlockSpec((1, rkv), const)],
        out_specs=(pl.BlockSpec((tm, rq), row), pl.BlockSpec((tm, rkv), row),
                   pl.BlockSpec((tm, LANES), row), pl.BlockSpec((tm, LANES), row)),
        compiler_params=_cparams(("parallel",)),
        name=name,
    )(x, w, gq, gkv)


def _fold_lanes(x):
    part = x[:, :LANES]
    for g in range(1, x.shape[1] // LANES):
        part = part + x[:, g * LANES:(g + 1) * LANES]
    return part


def _idx_topk_kernel(qi_ref, kj_ref, q_ref, k_ref, w_ref, sc_ref, thr_ref, jc_ref, keys_ref,
                     *, t, ksel, seq):
    p = pl.program_id(1)
    i, j = qi_ref[p], kj_ref[p]
    k = k_ref[...]
    acc = jnp.zeros((t, t), F32)
    for h in range(IDX_HEADS):
        acc = acc + jnp.maximum(_dot_t(q_ref[h], k), 0.0) * w_ref[:, h:h + 1]
    sc_ref[...] = acc

    row = i * t + lax.broadcasted_iota(I32, (t, t), 0)
    lane_col = lax.broadcasted_iota(I32, (t, t), 1)
    bits = pltpu.bitcast(acc, I32)
    key = jnp.where(bits < 0, bits ^ jnp.int32(0x7FFFFFFF), bits)
    key = jnp.where(bits == jnp.int32(INT_MIN), 0, key)
    keys_ref[j] = jnp.where(j * t + lane_col <= row, key, jnp.int32(INT_MIN))

    @pl.when(j == i)
    def _():
        def count(pred):
            def body(c, cnt):
                return cnt + _fold_lanes(pred(keys_ref[c], c).astype(I32))
            cnt = lax.fori_loop(0, i + 1, body, jnp.zeros((t, LANES), I32))
            return jnp.sum(cnt.astype(F32), axis=-1, keepdims=True).astype(I32)

        ans = jnp.where(count(lambda k, c: k >= 0) >= ksel, 0, jnp.int32(INT_MIN))

        def bit_body(b, ans):
            cand = ans + lax.shift_left(jnp.int32(1), 30 - b)
            return jnp.where(count(lambda k, c: k >= cand) >= ksel, cand, ans)

        ans = lax.fori_loop(0, 31, bit_body, ans)
        need = ksel - count(lambda k, c: k > ans)
        ties = count(lambda k, c: k == ans)
        row1 = i * t + lax.broadcasted_iota(I32, (t, 1), 0)
        full = row1 >= ksel

        def tie_cut(_):
            nbits = seq.bit_length() - 1

            def bit_body(b, cut):
                cand = cut + lax.shift_left(jnp.int32(1), nbits - 1 - b)
                below = count(lambda k, c: jnp.logical_and(k == ans, c * t + lane_col < cand))
                return jnp.where(below < need, cand, cut)
            return lax.fori_loop(0, nbits, bit_body, jnp.zeros((t, 1), I32))

        contested = jnp.max(jnp.where(jnp.logical_and(full, ties > need), 1.0, 0.0)) > 0.0
        cut = lax.cond(contested, tie_cut, lambda _: jnp.full((t, 1), seq - 1, I32), 0)
        tbits = jnp.where(ans < 0, ans ^ jnp.int32(0x7FFFFFFF), ans)
        thr = jnp.where(full, pltpu.bitcast(tbits, F32), -jnp.inf)
        thr_ref[...] = jnp.broadcast_to(thr, thr_ref.shape)
        jc_ref[...] = jnp.broadcast_to(jnp.where(full, cut, seq - 1), jc_ref.shape)


def _tri_pairs(n):
    qi = [i for i in range(n) for _ in range(i + 1)]
    kj = [j for i in range(n) for j in range(i + 1)]
    return jnp.asarray(qi, I32), jnp.asarray(kj, I32)


def _idx_topk(qh, kidx, widx, *, batch, seq, ksel, name):
    t = ATTN_TILE
    nq = seq // t
    assert seq & (seq - 1) == 0 and t >= ksel
    qi, kj = _tri_pairs(nq)
    npairs = qi.shape[0]
    qrow = lambda b, p, qi, kj: (b * nq + qi[p], 0)
    return pl.pallas_call(
        functools.partial(_idx_topk_kernel, t=t, ksel=ksel, seq=seq),
        out_shape=(jax.ShapeDtypeStruct((batch, npairs, t, t), F32),
                   jax.ShapeDtypeStruct((batch * seq, LANES), F32),
                   jax.ShapeDtypeStruct((batch * seq, LANES), I32)),
        grid_spec=pltpu.PrefetchScalarGridSpec(
            num_scalar_prefetch=2,
            grid=(batch, npairs),
            in_specs=[pl.BlockSpec((IDX_HEADS, t, LANES), lambda b, p, qi, kj: (1, b * nq + qi[p], 0)),
                      pl.BlockSpec((t, LANES), lambda b, p, qi, kj: (b * nq + kj[p], 0)),
                      pl.BlockSpec((t, LANES), qrow)],
            out_specs=(pl.BlockSpec((None, None, t, t), lambda b, p, qi, kj: (b, p, 0, 0)),
                       pl.BlockSpec((t, LANES), qrow), pl.BlockSpec((t, LANES), qrow)),
            scratch_shapes=[pltpu.VMEM((nq, t, t), I32)]),
        compiler_params=_cparams(("parallel", "arbitrary")),
        name=name,
    )(qi, kj, qh, kidx, widx)


def _dsa_attn_kernel(qi_ref, kj_ref, q_ref, k_ref, v_ref, sc_ref, thr_ref, jc_ref, bias_ref,
                     o_ref, m_ref, l_ref, acc_ref, *, t):
    p = pl.program_id(1)
    i, j = qi_ref[p], kj_ref[p]

    @pl.when(j == 0)
    def _():
        m_ref[...] = jnp.full(m_ref.shape, NEG, F32)
        l_ref[...] = jnp.zeros_like(l_ref)
        acc_ref[...] = jnp.zeros_like(acc_ref)

    sc = sc_ref[...]
    thr = thr_ref[:, 0:1]
    col = j * t + lax.broadcasted_iota(I32, (t, t), 1)
    row = i * t + lax.broadcasted_iota(I32, (t, t), 0)
    tie = jnp.logical_and(sc == thr, col <= jc_ref[:, 0:1])
    keep = jnp.logical_and(jnp.logical_or(sc > thr, tie), col <= row)
    near = j >= i - 1

    def head(h, _):
        s = _dot_t(q_ref[h], k_ref[h])
        s = s + jnp.where(near, bias_ref[h], 0.0)
        s = jnp.where(keep, s, NEG)
        m, l, acc = _online_softmax_step(s, v_ref[h], (m_ref[h], l_ref[h], acc_ref[h]))
        m_ref[h], l_ref[h], acc_ref[h] = m, l, acc
        return 0

    lax.fori_loop(0, N_DSA_HEADS, head, 0, unroll=2)

    @pl.when(j == i)
    def _():
        for h in range(N_DSA_HEADS):
            o_ref[:, h * HEAD_DIM:(h + 1) * HEAD_DIM] = (acc_ref[h] / l_ref[h]).astype(o_ref.dtype)


def _dsa_attention(qh, kvh, scores, thr, jcut, bias, *, batch, seq, name):
    t = ATTN_TILE
    nq = seq // t
    h = N_DSA_HEADS
    qi, kj = _tri_pairs(nq)
    qrow = lambda b, p, qi, kj: (b * nq + qi[p], 0)
    return pl.pallas_call(
        functools.partial(_dsa_attn_kernel, t=t),
        out_shape=jax.ShapeDtypeStruct((batch * seq, h * HEAD_DIM), BF16),
        grid_spec=pltpu.PrefetchScalarGridSpec(
            num_scalar_prefetch=2,
            grid=(batch, qi.shape[0]),
            in_specs=[pl.BlockSpec((h, t, HEAD_DIM), lambda b, p, qi, kj: (0, b * nq + qi[p], 0)),
                      pl.BlockSpec((h, t, HEAD_DIM), lambda b, p, qi, kj: (0, b * nq + kj[p], 0)),
                      pl.BlockSpec((h, t, HEAD_DIM), lambda b, p, qi, kj: (1, b * nq + kj[p], 0)),
                      pl.BlockSpec((None, None, t, t), lambda b, p, qi, kj: (b, p, 0, 0)),
                      pl.BlockSpec((t, LANES), qrow),
                      pl.BlockSpec((t, LANES), qrow),
                      pl.BlockSpec((None, h, t, t),
                                   lambda b, p, qi, kj: (jnp.where(kj[p] == qi[p], 0, 1), 0, 0, 0))],
            out_specs=pl.BlockSpec((t, h * HEAD_DIM), qrow),
            scratch_shapes=[pltpu.VMEM((h, t, 1), F32), pltpu.VMEM((h, t, 1), F32),
                            pltpu.VMEM((h, t, HEAD_DIM), F32)]),
        compiler_params=_cparams(("parallel", "arbitrary")),
        name=name,
    )(qi, kj, qh, kvh, kvh, scores, thr, jcut, bias)


def _route_kernel(lg_ref, info_ref, gate_ref, cnt_ref, carry_ref, *, tm):
    i = pl.program_id(0)

    @pl.when(i == 0)
    def _():
        carry_ref[...] = jnp.zeros_like(carry_ref)

    lane = lax.broadcasted_iota(I32, (tm, LANES), 1)
    lane_f = lane.astype(F32)
    lg = jnp.where(lane < N_EXPERTS, lg_ref[...], -jnp.inf)
    m1 = jnp.max(lg, axis=-1, keepdims=True)
    e1 = jnp.min(jnp.where(lg == m1, lane_f, float(LANES)), axis=-1, keepdims=True).astype(I32)
    lg2 = jnp.where(lane == e1, -jnp.inf, lg)
    m2 = jnp.max(lg2, axis=-1, keepdims=True)
    e2 = jnp.min(jnp.where(lg2 == m2, lane_f, float(LANES)), axis=-1, keepdims=True).astype(I32)
    ex = jnp.exp(m2 - m1)
    g1 = 1.0 / (1.0 + ex)
    g2 = ex / (1.0 + ex)
    onehot = jnp.where(jnp.logical_or(lane == e1, lane == e2), 1.0, 0.0)
    r = lax.broadcasted_iota(I32, (tm, tm), 0)
    c = lax.broadcasted_iota(I32, (tm, tm), 1)
    before = _dot(jnp.where(c < r, 1.0, 0.0).astype(BF16), onehot.astype(BF16)) + carry_ref[0:1, :]
    r1 = jnp.sum(jnp.where(lane == e1, before, 0.0), axis=-1, keepdims=True).astype(I32)
    r2 = jnp.sum(jnp.where(lane == e2, before, 0.0), axis=-1, keepdims=True).astype(I32)
    info = jnp.where(lane == 0, e1, jnp.where(lane == 1, e2, jnp.where(lane == 2, r1, r2)))
    info_ref[...] = info
    gate_ref[...] = jnp.where(lane == 0, g1, g2)
    total = carry_ref[0:1, :] + jnp.sum(onehot, axis=0, keepdims=True)
    carry_ref[...] = jnp.broadcast_to(total, carry_ref.shape)
    cnt_ref[...] = jnp.broadcast_to(total, cnt_ref.shape)


def _route(logits, *, tm=256, name):
    n = logits.shape[0]
    tm = min(tm, n)
    return pl.pallas_call(
        functools.partial(_route_kernel, tm=tm),
        out_shape=(jax.ShapeDtypeStruct((n, LANES), I32), jax.ShapeDtypeStruct((n, LANES), F32),
                   jax.ShapeDtypeStruct((8, LANES), F32)),
        grid=(n // tm,),
        in_specs=[pl.BlockSpec((tm, LANES), lambda i: (i, 0))],
        out_specs=(pl.BlockSpec((tm, LANES), lambda i: (i, 0)),
                   pl.BlockSpec((tm, LANES), lambda i: (i, 0)),
                   pl.BlockSpec((8, LANES), lambda i: (0, 0))),
        scratch_shapes=[pltpu.VMEM((8, LANES), F32)],
        compiler_params=_cparams(("arbitrary",)),
        name=name,
    )(logits)


def _row_copy_kernel(sidx_ref, didx_ref, src_ref, init_ref, out_ref, sem, *, chunk):
    del init_ref
    base = pl.program_id(0) * chunk

    def copy(a):
        return pltpu.make_async_copy(src_ref.at[sidx_ref[a]], out_ref.at[didx_ref[a]], sem)

    def start(r, _):
        copy(base + r).start()
        return 0

    def wait(r, _):
        copy(base + r).wait()
        return 0

    lax.fori_loop(0, chunk, start, 0)
    lax.fori_loop(0, chunk, wait, 0)


def _row_copy(src, src_idx, dst_idx, out_init, *, chunk=256, name):
    n = src_idx.shape[0]
    chunk = min(chunk, n)
    assert n % chunk == 0
    return pl.pallas_call(
        functools.partial(_row_copy_kernel, chunk=chunk),
        out_shape=jax.ShapeDtypeStruct(out_init.shape, out_init.dtype),
        grid_spec=pltpu.PrefetchScalarGridSpec(
            num_scalar_prefetch=2,
            grid=(n // chunk,),
            in_specs=[pl.BlockSpec(memory_space=pl.ANY), pl.BlockSpec(memory_space=pl.ANY)],
            out_specs=pl.BlockSpec(memory_space=pl.ANY),
            scratch_shapes=[pltpu.SemaphoreType.DMA]),
        input_output_aliases={3: 0},
        compiler_params=_cparams(("arbitrary",)),
        name=name,
    )(src_idx, dst_idx, src, out_init)


def _combine_ln_kernel(h_ref, y_ref, gate_ref, g_ref, b_ref, o_ref):
    ff = gate_ref[:, 0:1] * y_ref[0] + gate_ref[:, 1:2] * y_ref[1]
    o_ref[...] = _layer_norm(DEEPNORM_ALPHA * h_ref[...] + ff, g_ref[...], b_ref[...])


def _combine_ln(h, y, gates, g, b, *, tm=256, name):
    n, d = h.shape
    tm = min(tm, n)
    return pl.pallas_call(
        _combine_ln_kernel,
        out_shape=jax.ShapeDtypeStruct((n, d), F32),
        grid=(n // tm,),
        in_specs=[pl.BlockSpec((tm, d), lambda i: (i, 0)),
                  pl.BlockSpec((2, tm, d), lambda i: (0, i, 0)),
                  pl.BlockSpec((tm, LANES), lambda i: (i, 0)),
                  pl.BlockSpec((1, d), lambda i: (0, 0)),
                  pl.BlockSpec((1, d), lambda i: (0, 0))],
        out_specs=pl.BlockSpec((tm, d), lambda i: (i, 0)),
        compiler_params=_cparams(("parallel",)),
        name=name,
    )(h, y, gates, g, b)


def _rel_bucket(dist):
    n = jnp.maximum(dist, 0)
    exact = REL_BUCKETS // 2
    nf = jnp.maximum(n, 1).astype(F32)
    large = exact + (jnp.log(nf / exact) / math.log(REL_MAX_DIST / exact) * (REL_BUCKETS - exact)).astype(I32)
    large = jnp.minimum(large, REL_BUCKETS - 1)
    return jnp.where(n < exact, n, large)


def _bias_tiles(rel_table, n_heads, t):
    assert t >= REL_MAX_DIST
    r = jnp.arange(t)[:, None]
    c = jnp.arange(t)[None, :]
    tab = rel_table[:, :n_heads].astype(F32)
    far = tab[REL_BUCKETS - 1]
    tiles = [jnp.moveaxis(tab[_rel_bucket(r - c + off)] - far, -1, 0) for off in (0, t)]
    return jnp.stack(tiles)


def _pad_cols(w, n):
    return jnp.pad(w, ((0, 0), (0, n - w.shape[1])))


def _even_layer(h, rel_table, w_in, b_forget, w_out, ln1_g, ln1_b, w1, w3, w2, ln2_g, ln2_b,
                *, batch, seq):
    d = h.shape[1]
    wa = N_MOBA_HEADS * HEAD_DIM
    wb = N_FOX_HEADS * HEAD_DIM
    n_qkv = 3 * wa + 3 * wb
    scale = HEAD_DIM ** -0.5
    ones, scl = jnp.ones((wa,), F32), jnp.full((wa,), scale, F32)
    colscale = jnp.concatenate([scl, ones, ones, scl, ones, ones])[None, :]
    qkv = _mm(h, w_in[:, :n_qkv].astype(BF16), colscale, out_dtype=BF16, name="ev_qkv_proj")
    fb = _mm(h, _pad_cols(w_in[:, n_qkv:], LANES).astype(BF16), jnp.ones((1, LANES), F32),
             out_dtype=F32, tn=LANES, name="ev_forget_proj")
    log_f = jax.nn.log_sigmoid(fb[:, :N_FOX_HEADS] + b_forget.astype(F32))
    csum = jnp.cumsum(log_f.reshape(batch, seq, N_FOX_HEADS).transpose(0, 2, 1), axis=-1)
    csum = csum.reshape(batch, N_FOX_HEADS, seq // ATTN_TILE, 1, ATTN_TILE)
    oa = _moba_attention(qkv, _bias_tiles(rel_table, N_MOBA_HEADS, MOBA_BLOCK),
                         batch=batch, seq=seq, name="ev_moba_attn")
    ob = _fox_attention(qkv, csum, batch=batch, seq=seq, name="ev_fox_attn")
    attn = jnp.concatenate([oa, ob], axis=-1)
    h = _mm_ln(attn, w_out.astype(BF16), h, ln1_g[None, :], ln1_b[None, :], name="ev_out_proj_ln")
    return _ffn_ln(h, w1.astype(BF16), w3.astype(BF16), w2.astype(BF16),
                   ln2_g[None, :], ln2_b[None, :], name="ev_swiglu_ln")


def _moe(h, router, w1, w3, w2, ln_g, ln_b):
    n, d = h.shape
    tm = EXPERT_TILE
    logits = _mm(h, _pad_cols(router, LANES).astype(BF16), jnp.ones((1, LANES), F32),
                 out_dtype=F32, tn=LANES, name="od_router")
    info, gates, cnt = _route(logits, name="od_route")
    counts = cnt[0, :N_EXPERTS].astype(I32)
    padded = (counts + tm - 1) // tm * tm
    pend = jnp.cumsum(padded)
    pstart = pend - padded
    e = info[:, :MOE_TOPK]
    dest = (pstart[e] + info[:, MOE_TOPK:2 * MOE_TOPK]).astype(I32)
    n_rows = -(-(n * MOE_TOPK + N_EXPERTS * (tm - 1)) // tm) * tm
    n_tiles = n_rows // tm
    tile_start = jnp.arange(n_tiles, dtype=I32) * tm
    tile_valid = (tile_start < pend[-1]).astype(I32)
    last = jnp.maximum(pend[-1] - 1, 0)
    tile_e = jnp.minimum(jnp.searchsorted(pend, jnp.minimum(tile_start, last), side="right"),
                         N_EXPERTS - 1).astype(I32)
    slabs = d // LANES
    tok = jnp.arange(n, dtype=I32)
    src_tok = jnp.concatenate([tok, tok])
    dest_flat = jnp.concatenate([dest[:, 0], dest[:, 1]])
    xs = _row_copy(h.reshape(n, slabs, LANES), src_tok, dest_flat,
                   jnp.zeros((n_rows, slabs, LANES), F32), name="od_moe_scatter")
    ys = _moe_ffn(tile_e, tile_valid, xs.reshape(n_rows, d), w1.astype(BF16), w3.astype(BF16),
                  w2.astype(BF16), tm=tm, name="od_moe_swiglu")
    y = _row_copy(ys.reshape(n_rows, slabs, LANES), dest_flat, jnp.arange(MOE_TOPK * n, dtype=I32),
                  jnp.zeros((MOE_TOPK * n, slabs, LANES), F32), name="od_moe_gather")
    return _combine_ln(h, y.reshape(MOE_TOPK, n, d), gates, ln_g[None, :], ln_b[None, :],
                       name="od_moe_combine_ln")


def _odd_layer(h, rel_table, w_in, q_norm_g, kv_norm_g, w_uq, w_qidx, w_uk, w_uv, w_out,
               ln1_g, ln1_b, router, w1, w3, w2, ln2_g, ln2_b, *, batch, seq):
    nh = N_DSA_HEADS
    rq, rkv = DSA_Q_RANK, DSA_KV_RANK
    scale = HEAD_DIM ** -0.5
    w_in_p = jnp.concatenate([w_in[:, :rq + rkv],
                              _pad_cols(w_in[:, rq + rkv:rq + rkv + IDX_DIM], LANES),
                              _pad_cols(w_in[:, rq + rkv + IDX_DIM:], LANES)], axis=1)
    cq, ckv, kidx, widx = _dsa_in_proj(h, w_in_p.astype(BF16), q_norm_g[None, :], kv_norm_g[None, :],
                                       name="od_in_proj_rms")
    w_qidx_p = jnp.pad(w_qidx.reshape(rq, IDX_HEADS, IDX_DIM), ((0, 0), (0, 0), (0, LANES - IDX_DIM)))
    wq = jnp.concatenate([w_uq, w_qidx_p.reshape(rq, IDX_HEADS * LANES)], axis=1)
    qscale = jnp.concatenate([jnp.full((nh * HEAD_DIM,), scale, F32),
                              jnp.ones((IDX_HEADS * LANES,), F32)])[None, :]
    qh = _mm(cq, wq.astype(BF16), qscale, out_dtype=BF16, head_major=True, tm=1024, name="od_q_proj")
    wkv = jnp.concatenate([w_uk.transpose(1, 0, 2).reshape(rkv, nh * HEAD_DIM),
                           w_uv.transpose(1, 0, 2).reshape(rkv, nh * HEAD_DIM)], axis=1)
    kvh = _mm(ckv, wkv.astype(BF16), jnp.ones((1, 2 * nh * HEAD_DIM), F32), out_dtype=BF16,
              head_major=True, tm=1024, name="od_kv_proj")
    scores, thr, jcut = _idx_topk(qh, kidx, widx, batch=batch, seq=seq,
                                  ksel=min(DSA_TOPK_MAX, seq // 4), name="od_idx_topk")
    attn = _dsa_attention(qh, kvh, scores, thr, jcut, _bias_tiles(rel_table, nh, ATTN_TILE),
                          batch=batch, seq=seq, name="od_dsa_attn")
    h = _mm_ln(attn, w_out.astype(BF16), h, ln1_g[None, :], ln1_b[None, :], name="od_out_proj_ln")
    return _moe(h, router, w1, w3, w2, ln2_g, ln2_b)


def kernel(x, rel_table, ev_w_in, ev_b_forget, ev_w_out, ev_ln1_g, ev_ln1_b, ev_ffn_w1, ev_ffn_w3, ev_ffn_w2, ev_ln2_g, ev_ln2_b, od_w_in, od_q_norm_g, od_kv_norm_g, od_w_uq, od_w_qidx, od_w_uk, od_w_uv, od_w_out, od_ln1_g, od_ln1_b, od_router, od_exp_w1, od_exp_w3, od_exp_w2, od_ln2_g, od_ln2_b):
    batch, seq, d = x.shape
    h = x.reshape(batch * seq, d)
    for layer in range(DEPTH):
        i = layer // 2
        if layer % 2 == 0:
            h = _even_layer(h, rel_table, ev_w_in[i], ev_b_forget[i], ev_w_out[i], ev_ln1_g[i],
                            ev_ln1_b[i], ev_ffn_w1[i], ev_ffn_w3[i], ev_ffn_w2[i], ev_ln2_g[i],
                            ev_ln2_b[i], batch=batch, seq=seq)
        else:
            h = _odd_layer(h, rel_table, od_w_in[i], od_q_norm_g[i], od_kv_norm_g[i], od_w_uq[i],
                           od_w_qidx[i], od_w_uk[i], od_w_uv[i], od_w_out[i], od_ln1_g[i],
                           od_ln1_b[i], od_router[i], od_exp_w1[i], od_exp_w3[i], od_exp_w2[i],
                           od_ln2_g[i], od_ln2_b[i], batch=batch, seq=seq)
    return h.reshape(batch, seq, d)
```

```python
import functools
import math

import jax
import jax.numpy as jnp
from jax import lax
from jax.experimental import pallas as pl
from jax.experimental.pallas import tpu as pltpu

F32 = jnp.float32
BF16 = jnp.bfloat16
I32 = jnp.int32

HEAD_DIM = 128
N_MOBA_HEADS = 8
N_FOX_HEADS = 8
MOBA_BLOCK = 256
MOBA_TOPK = 3
N_DSA_HEADS = 16
DSA_Q_RANK = 512
DSA_KV_RANK = 512
IDX_HEADS = 16
IDX_DIM = 64
DSA_TOPK_MAX = 256
REL_BUCKETS = 32
REL_MAX_DIST = 128
N_EXPERTS = 8
MOE_TOPK = 2
LN_EPS = 1e-5
RMS_EPS = 1e-6
DEPTH = 2
DEEPNORM_ALPHA = (2 * DEPTH) ** 0.25

LANES = 128
ATTN_TILE = 256
WIDE_CHUNK = 1024
DSA_KEY_TILE = 512
DSA_HEAD_GROUP = 4
EXPERT_TILE = 512
NEG = -1e30
LOG2E = math.log2(math.e)
INT_MIN = -(2 ** 31)
VMEM_LIMIT = 56 * 1024 * 1024


def _cparams(sem, vmem=VMEM_LIMIT, flags=None):
    return pltpu.CompilerParams(dimension_semantics=sem, vmem_limit_bytes=vmem, flags=flags)


INTERLEAVE_CHAINS = None


def _dot(a, b):
    return jnp.dot(a, b, preferred_element_type=F32)


def _dot_t(a, b):
    return lax.dot_general(a, b, (((1,), (1,)), ((), ())), preferred_element_type=F32)


def _layer_norm(y, g, b):
    mu = jnp.mean(y, axis=-1, keepdims=True)
    d = y - mu
    var = jnp.mean(d * d, axis=-1, keepdims=True)
    return d * lax.rsqrt(var + LN_EPS) * g + b


def _mm_kernel(x_ref, w_ref, cs_ref, o_ref, *, head_major):
    acc = _dot(x_ref[...].astype(BF16), w_ref[...]) * cs_ref[...]
    if head_major:
        for c in range(o_ref.shape[0]):
            o_ref[c] = acc[:, c * LANES:(c + 1) * LANES].astype(o_ref.dtype)
    else:
        o_ref[...] = acc.astype(o_ref.dtype)


def _mm(x, w, colscale, *, out_dtype, head_major=False, tm=512, tn=512, name):
    m, k = x.shape
    n = w.shape[1]
    tm, tn = min(tm, m), min(tn, n)
    assert m % tm == 0 and n % tn == 0 and tn % LANES == 0
    if head_major:
        out_shape = jax.ShapeDtypeStruct((n // LANES, m, LANES), out_dtype)
        out_spec = pl.BlockSpec((tn // LANES, tm, LANES), lambda i, j: (j, i, 0))
    else:
        out_shape = jax.ShapeDtypeStruct((m, n), out_dtype)
        out_spec = pl.BlockSpec((tm, tn), lambda i, j: (i, j))
    return pl.pallas_call(
        functools.partial(_mm_kernel, head_major=head_major),
        out_shape=out_shape,
        grid=(m // tm, n // tn),
        in_specs=[pl.BlockSpec((tm, k), lambda i, j: (i, 0)),
                  pl.BlockSpec((k, tn), lambda i, j: (0, j)),
                  pl.BlockSpec((1, tn), lambda i, j: (0, j))],
        out_specs=out_spec,
        compiler_params=_cparams(("parallel", "arbitrary")),
        name=name,
    )(x, w, colscale)


def _mm_ln_kernel(x_ref, w_ref, res_ref, g_ref, b_ref, o_ref):
    y = DEEPNORM_ALPHA * res_ref[...] + _dot(x_ref[...], w_ref[...])
    o_ref[...] = _layer_norm(y, g_ref[...], b_ref[...])


def _mm_ln(x, w, res, g, b, *, tm=256, name):
    m, k = x.shape
    d = w.shape[1]
    tm = min(tm, m)
    return pl.pallas_call(
        _mm_ln_kernel,
        out_shape=jax.ShapeDtypeStruct((m, d), F32),
        grid=(m // tm,),
        in_specs=[pl.BlockSpec((tm, k), lambda i: (i, 0)),
                  pl.BlockSpec((k, d), lambda i: (0, 0)),
                  pl.BlockSpec((tm, d), lambda i: (i, 0)),
                  pl.BlockSpec((1, d), lambda i: (0, 0)),
                  pl.BlockSpec((1, d), lambda i: (0, 0))],
        out_specs=pl.BlockSpec((tm, d), lambda i: (i, 0)),
        compiler_params=_cparams(("parallel",)),
        name=name,
    )(x, w, res, g, b)


def _swiglu_step(xb, w1_ref, w3_ref, w2_ref):
    a = _dot(xb, w1_ref[...])
    c = _dot(xb, w3_ref[...])
    hmid = a / (1.0 + jnp.exp(-a)) * c
    return _dot(hmid.astype(BF16), w2_ref[...])


def _ffn_ln_kernel(x_ref, w1_ref, w3_ref, w2_ref, g_ref, b_ref, o_ref, acc_ref, xb_ref):
    f = pl.program_id(1)

    @pl.when(f == 0)
    def _():
        acc_ref[...] = jnp.zeros_like(acc_ref)
        xb_ref[...] = x_ref[...].astype(BF16)

    acc_ref[...] += _swiglu_step(xb_ref[...], w1_ref, w3_ref, w2_ref)

    @pl.when(f == pl.num_programs(1) - 1)
    def _():
        y = DEEPNORM_ALPHA * x_ref[...] + acc_ref[...]
        o_ref[...] = _layer_norm(y, g_ref[...], b_ref[...])


def _ffn_ln(x, w1, w3, w2, g, b, *, tm=512, tf=512, name):
    m, d = x.shape
    dff = w1.shape[1]
    tm, tf = min(tm, m), min(tf, dff)
    assert m % tm == 0 and dff % tf == 0
    return pl.pallas_call(
        _ffn_ln_kernel,
        out_shape=jax.ShapeDtypeStruct((m, d), F32),
        grid=(m // tm, dff // tf),
        in_specs=[pl.BlockSpec((tm, d), lambda i, f: (i, 0)),
                  pl.BlockSpec((d, tf), lambda i, f: (0, f)),
                  pl.BlockSpec((d, tf), lambda i, f: (0, f)),
                  pl.BlockSpec((tf, d), lambda i, f: (f, 0)),
                  pl.BlockSpec((1, d), lambda i, f: (0, 0)),
                  pl.BlockSpec((1, d), lambda i, f: (0, 0))],
        out_specs=pl.BlockSpec((tm, d), lambda i, f: (i, 0)),
        scratch_shapes=[pltpu.VMEM((tm, d), F32), pltpu.VMEM((tm, d), BF16)],
        compiler_params=_cparams(("parallel", "arbitrary")),
        name=name,
    )(x, w1, w3, w2, g, b)


def _row_gather(src_hbm, idx_ref, base, bufs, sems, slot, n_rows, *, wait):
    for s, buf in enumerate(bufs):
        @pl.when(slot == s)
        def _(s=s, buf=buf):
            def body(r, _):
                cp = pltpu.make_async_copy(src_hbm.at[idx_ref[base + r]], buf.at[r], sems.at[s])
                if wait:
                    cp.wait()
                else:
                    cp.start()
                return 0
            lax.fori_loop(0, n_rows, body, 0)


def _slabs_to_rows(buf):
    return jnp.concatenate([buf[:, c, :] for c in range(buf.shape[1])], axis=1)


def _moe_ffn_kernel(te_ref, tv_ref, tok_ref, h_ref, w1_ref, w3_ref, w2_ref, o_ref,
                    acc_ref, xb_ref, xg0_ref, xg1_ref, sems, *, tm):
    i, f = pl.program_id(0), pl.program_id(1)
    nt = pl.num_programs(0)
    bufs = (xg0_ref, xg1_ref)
    gather = functools.partial(_row_gather, h_ref, tok_ref, bufs=bufs, sems=sems, n_rows=tm)

    @pl.when(tv_ref[i] > 0)
    def _():
        @pl.when(f == 0)
        def _():
            @pl.when(i == 0)
            def _():
                gather(base=0, slot=0, wait=False)

            gather(base=i * tm, slot=i % 2, wait=True)
            nxt = jnp.minimum(i + 1, nt - 1)

            @pl.when(jnp.logical_and(i + 1 < nt, tv_ref[nxt] > 0))
            def _():
                gather(base=nxt * tm, slot=nxt % 2, wait=False)

            for s, buf in enumerate(bufs):
                @pl.when(i % 2 == s)
                def _(buf=buf):
                    xb_ref[...] = _slabs_to_rows(buf).astype(BF16)

            acc_ref[...] = jnp.zeros_like(acc_ref)

        acc_ref[...] += _swiglu_step(xb_ref[...], w1_ref, w3_ref, w2_ref)

        @pl.when(f == pl.num_programs(1) - 1)
        def _():
            for c in range(o_ref.shape[1]):
                o_ref[:, c, :] = acc_ref[:, c * LANES:(c + 1) * LANES]

    @pl.when(tv_ref[i] == 0)
    def _():
        o_ref[...] = jnp.zeros_like(o_ref)


def _moe_ffn(tile_e, tile_valid, row_tok, h_slabs, w1, w3, w2, *, tm, tf=512, name):
    n_rows = row_tok.shape[0]
    _, slabs, _ = h_slabs.shape
    d = slabs * LANES
    dff = w1.shape[2]
    tf = min(tf, dff)
    nf = dff // tf
    assert n_rows % tm == 0 and dff % tf == 0

    def fidx(i, f, tv):
        return jnp.where(tv[i] > 0, f, nf - 1)

    return pl.pallas_call(
        functools.partial(_moe_ffn_kernel, tm=tm),
        out_shape=jax.ShapeDtypeStruct((n_rows, slabs, LANES), F32),
        grid_spec=pltpu.PrefetchScalarGridSpec(
            num_scalar_prefetch=3,
            grid=(n_rows // tm, nf),
            in_specs=[pl.BlockSpec(memory_space=pl.ANY),
                      pl.BlockSpec((None, d, tf), lambda i, f, te, tv, tok: (te[i], 0, fidx(i, f, tv))),
                      pl.BlockSpec((None, d, tf), lambda i, f, te, tv, tok: (te[i], 0, fidx(i, f, tv))),
                      pl.BlockSpec((None, tf, d), lambda i, f, te, tv, tok: (te[i], fidx(i, f, tv), 0))],
            out_specs=pl.BlockSpec((tm, slabs, LANES), lambda i, f, te, tv, tok: (i, 0, 0)),
            scratch_shapes=[pltpu.VMEM((tm, d), F32), pltpu.VMEM((tm, d), BF16),
                            pltpu.VMEM((tm, slabs, LANES), F32), pltpu.VMEM((tm, slabs, LANES), F32),
                            pltpu.SemaphoreType.DMA((2,))]),
        compiler_params=_cparams(("arbitrary", "arbitrary")),
        name=name,
    )(tile_e, tile_valid, row_tok, h_slabs, w1, w3, w2)


def _online_softmax_step(s, v, carry):
    m, l, acc = carry
    m_new = jnp.maximum(m, jnp.max(s, axis=-1, keepdims=True))
    alpha = jnp.exp2(m - m_new)
    p = jnp.exp2(s - m_new)
    l = alpha * l + jnp.sum(p, axis=-1, keepdims=True)
    acc = alpha * acc + _dot(p.astype(BF16), v)
    return m_new, l, acc


def _softmax_init(tq):
    return (jnp.full((tq, 1), NEG, F32), jnp.zeros((tq, 1), F32), jnp.zeros((tq, HEAD_DIM), F32))


def _head_cols(h):
    return slice(h * HEAD_DIM, (h + 1) * HEAD_DIM)


def _kv_block(k_ref, v_ref, n, width, h):
    rows = pl.ds(pl.multiple_of(n * width, width), width)
    return k_ref[rows, _head_cols(h)], v_ref[rows, _head_cols(h)]


def _causal_mask(blk):
    r = lax.broadcasted_iota(I32, (blk, blk), 0)
    c = lax.broadcasted_iota(I32, (blk, blk), 1)
    return c <= r


def _moba_select(q, km, i, blk):
    nbp = km.shape[0]
    gate = _dot_t(q, km.astype(BF16))
    lane = lax.broadcasted_iota(I32, (blk, nbp), 1)
    lane_f = lane.astype(F32)
    g = jnp.where(lane < i, gate, -jnp.inf)
    sel = jnp.zeros((blk, nbp), F32)
    for _ in range(MOBA_TOPK):
        mx = jnp.max(g, axis=-1, keepdims=True)
        first = jnp.min(jnp.where(g == mx, lane_f, float(nbp)), axis=-1, keepdims=True)
        pick = jnp.logical_and(lane_f == first, mx > -jnp.inf)
        sel = jnp.where(pick, 1.0, sel)
        g = jnp.where(pick, -jnp.inf, g)
    return sel, lane


def _moba_kernel(q_ref, k_ref, v_ref, bias_ref, o_ref, km_ref, *, blk, nb, wide, hp):
    i = pl.program_id(2)
    per = wide // blk

    @pl.when(i == 0)
    def _():
        km_ref[...] = jnp.zeros_like(km_ref)

        def mean_body(n, _):
            for h in range(hp):
                kblk, _ = _kv_block(k_ref, v_ref, n, blk, h)
                km_ref[h, pl.ds(n, 1), :] = jnp.mean(kblk.astype(F32), axis=0, keepdims=True)
            return 0

        lax.fori_loop(0, nb, mean_body, 0)

    qs = [q_ref[:, _head_cols(h)] for h in range(hp)]
    sels = [_moba_select(qs[h], km_ref[h], i, blk) for h in range(hp)]

    def past_step(n, carries, width, bias_idx):
        kv = [_kv_block(k_ref, v_ref, n, width, h) for h in range(hp)]
        logits = [_dot_t(qs[h], kv[h][0]) for h in range(hp)]
        out = []
        for h in range(hp):
            s, vblk = logits[h], kv[h][1]
            if bias_idx is not None:
                s = s + bias_ref[bias_idx, h]
            sel, lane = sels[h]
            parts = []
            for u in range(width // blk):
                blk_id = n * (width // blk) + u
                chosen = jnp.sum(jnp.where(lane == blk_id, sel, 0.0), axis=-1, keepdims=True)
                parts.append(jnp.where(chosen > 0.0, s[:, u * blk:(u + 1) * blk], NEG))
            s = parts[0] if len(parts) == 1 else jnp.concatenate(parts, axis=1)
            out.append(_online_softmax_step(s, vblk, carries[h]))
        return tuple(out)

    n_far = jnp.maximum(i - 1, 0)
    n_wide = n_far // per
    carries = tuple(_softmax_init(blk) for _ in range(hp))
    carries = lax.fori_loop(0, n_wide, lambda n, c: past_step(n, c, wide, None), carries)
    carries = lax.fori_loop(n_wide * per, n_far, lambda n, c: past_step(n, c, blk, None), carries)
    carries = past_step(n_far, carries, blk, 1)
    kv = [_kv_block(k_ref, v_ref, i, blk, h) for h in range(hp)]
    logits = [_dot_t(qs[h], kv[h][0]) for h in range(hp)]
    for h in range(hp):
        s = jnp.where(_causal_mask(blk), logits[h] + bias_ref[0, h], NEG)
        _, l, acc = _online_softmax_step(s, kv[h][1], carries[h])
        o_ref[:, _head_cols(h)] = (acc / l).astype(o_ref.dtype)


def _moba_attention(qkv, bias, *, batch, seq, hp=2, name):
    blk = MOBA_BLOCK
    nb = seq // blk
    nbp = -(-nb // LANES) * LANES
    h = N_MOBA_HEADS
    nq = seq // blk
    hg = h // hp
    w = hp * HEAD_DIM
    return pl.pallas_call(
        functools.partial(_moba_kernel, blk=blk, nb=nb, wide=min(WIDE_CHUNK, seq), hp=hp),
        out_shape=jax.ShapeDtypeStruct((batch * seq, h * HEAD_DIM), BF16),
        grid=(batch, hg, nq),
        in_specs=[pl.BlockSpec((blk, w), lambda b, g, i: (b * nq + i, g)),
                  pl.BlockSpec((seq, w), lambda b, g, i: (b, hg + g)),
                  pl.BlockSpec((seq, w), lambda b, g, i: (b, 2 * hg + g)),
                  pl.BlockSpec((2, hp, blk, blk), lambda b, g, i: (0, g, 0, 0))],
        out_specs=pl.BlockSpec((blk, w), lambda b, g, i: (b * nq + i, g)),
        scratch_shapes=[pltpu.VMEM((hp, nbp, HEAD_DIM), F32)],
        compiler_params=_cparams(("parallel", "parallel", "arbitrary"), flags=INTERLEAVE_CHAINS),
        name=name,
    )(qkv, qkv, qkv, bias)


def _fox_kernel(q_ref, k_ref, v_ref, ck_ref, o_ref, *, blk, wide, hp):
    i = pl.program_id(2)
    per = wide // blk
    qs = [q_ref[:, _head_cols(h)] for h in range(hp)]

    def step(n, carries, width, causal):
        kv = [_kv_block(k_ref, v_ref, n, width, h) for h in range(hp)]
        logits = [_dot_t(qs[h], kv[h][0]) for h in range(hp)]
        out = []
        for h in range(hp):
            ck = [ck_ref[h, n * (width // blk) + u] for u in range(width // blk)]
            s = logits[h] - (ck[0] if len(ck) == 1 else jnp.concatenate(ck, axis=1))
            if causal:
                s = jnp.where(_causal_mask(blk), s, NEG)
            out.append(_online_softmax_step(s, kv[h][1], carries[h]))
        return tuple(out)

    n_wide = i // per
    carries = tuple(_softmax_init(blk) for _ in range(hp))
    carries = lax.fori_loop(0, n_wide, lambda n, c: step(n, c, wide, False), carries)
    carries = lax.fori_loop(n_wide * per, i, lambda n, c: step(n, c, blk, False), carries)
    carries = step(i, carries, blk, True)
    for h in range(hp):
        _, l, acc = carries[h]
        o_ref[:, _head_cols(h)] = (acc / l).astype(o_ref.dtype)


def _fox_attention(qkv, csum, *, batch, seq, hp=2, name):
    blk = ATTN_TILE
    h = N_FOX_HEADS
    nq = seq // blk
    hg = h // hp
    base = 3 * N_MOBA_HEADS // hp
    w = hp * HEAD_DIM
    return pl.pallas_call(
        functools.partial(_fox_kernel, blk=blk, wide=min(WIDE_CHUNK, seq), hp=hp),
        out_shape=jax.ShapeDtypeStruct((batch * seq, h * HEAD_DIM), BF16),
        grid=(batch, hg, nq),
        in_specs=[pl.BlockSpec((blk, w), lambda b, g, i: (b * nq + i, base + g)),
                  pl.BlockSpec((seq, w), lambda b, g, i: (b, base + hg + g)),
                  pl.BlockSpec((seq, w), lambda b, g, i: (b, base + 2 * hg + g)),
                  pl.BlockSpec((None, hp, nq, 1, blk), lambda b, g, i: (b, g, 0, 0, 0))],
        out_specs=pl.BlockSpec((blk, w), lambda b, g, i: (b * nq + i, g)),
        compiler_params=_cparams(("parallel", "parallel", "arbitrary"), flags=INTERLEAVE_CHAINS),
        name=name,
    )(qkv, qkv, qkv, csum)


def _dsa_in_kernel(x_ref, w_ref, gq_ref, gkv_ref, cq_ref, ckv_ref, kidx_ref, widx_ref):
    acc = _dot(x_ref[...].astype(BF16), w_ref[...])
    rq, rkv = DSA_Q_RANK, DSA_KV_RANK

    def rms(z, g):
        return z * lax.rsqrt(jnp.mean(z * z, axis=-1, keepdims=True) + RMS_EPS) * g

    cq_ref[...] = rms(acc[:, :rq], gq_ref[...]).astype(cq_ref.dtype)
    ckv_ref[...] = rms(acc[:, rq:rq + rkv], gkv_ref[...]).astype(ckv_ref.dtype)
    kidx_ref[...] = acc[:, rq + rkv:rq + rkv + LANES].astype(kidx_ref.dtype)
    widx_ref[...] = acc[:, rq + rkv + LANES:] * (IDX_HEADS ** -0.5 * IDX_DIM ** -0.5)


def _dsa_in_proj(x, w, gq, gkv, *, tm=512, name):
    m, k = x.shape
    n = w.shape[1]
    tm = min(tm, m)
    rq, rkv = DSA_Q_RANK, DSA_KV_RANK
    row = lambda i: (i, 0)
    const = lambda i: (0, 0)
    return pl.pallas_call(
        _dsa_in_kernel,
        out_shape=(jax.ShapeDtypeStruct((m, rq), BF16), jax.ShapeDtypeStruct((m, rkv), BF16),
                   jax.ShapeDtypeStruct((m, LANES), BF16), jax.ShapeDtypeStruct((m, LANES), F32)),
        grid=(m // tm,),
        in_specs=[pl.BlockSpec((tm, k), row), pl.BlockSpec((k, n), const),
                  pl.BlockSpec((1, rq), const), pl.BlockSpec((1, rkv), const)],
        out_specs=(pl.BlockSpec((tm, rq), row), pl.BlockSpec((tm, rkv), row),
                   pl.BlockSpec((tm, LANES), row), pl.BlockSpec((tm, LANES), row)),
        compiler_params=_cparams(("parallel",)),
        name=name,
    )(x, w, gq, gkv)


def _fold_lanes(x):
    part = x[:, :LANES]
    for g in range(1, x.shape[1] // LANES):
        part = part + x[:, g * LANES:(g + 1) * LANES]
    return part


def _idx_topk_kernel(qi_ref, kj_ref, q_ref, k_ref, w_ref, sc_ref, thr_ref, jc_ref, keys_ref,
                     *, t, tk, ksel, seq):
    p = pl.program_id(1)
    i, j = qi_ref[p], kj_ref[p]
    k = k_ref[...]
    acc = jnp.zeros((t, tk), F32)
    for h in range(IDX_HEADS):
        acc = acc + jnp.maximum(_dot_t(q_ref[h], k), 0.0) * w_ref[:, h:h + 1]
    sc_ref[...] = acc

    row = i * t + lax.broadcasted_iota(I32, (t, tk), 0)
    lane_col = lax.broadcasted_iota(I32, (t, tk), 1)
    bits = pltpu.bitcast(acc, I32)
    key = jnp.where(bits < 0, bits ^ jnp.int32(0x7FFFFFFF), bits)
    key = jnp.where(bits == jnp.int32(INT_MIN), 0, key)
    keys_ref[j] = jnp.where(j * tk + lane_col <= row, key, jnp.int32(INT_MIN))
    j_last = (i * t + t - 1) // tk

    @pl.when(j == j_last)
    def _():
        def count(pred):
            def body(c, cnt):
                return cnt + _fold_lanes(pred(keys_ref[c], c).astype(I32))
            cnt = lax.fori_loop(0, j_last + 1, body, jnp.zeros((t, LANES), I32))
            return jnp.sum(cnt.astype(F32), axis=-1, keepdims=True).astype(I32)

        ans = jnp.where(count(lambda k, c: k >= 0) >= ksel, 0, jnp.int32(INT_MIN))

        def bit_body(b, ans):
            cand = ans + lax.shift_left(jnp.int32(1), 30 - b)
            return jnp.where(count(lambda k, c: k >= cand) >= ksel, cand, ans)

        ans = lax.fori_loop(0, 31, bit_body, ans)
        need = ksel - count(lambda k, c: k > ans)
        ties = count(lambda k, c: k == ans)
        row1 = i * t + lax.broadcasted_iota(I32, (t, 1), 0)
        full = row1 >= ksel

        def tie_cut(_):
            nbits = seq.bit_length() - 1

            def bit_body(b, cut):
                cand = cut + lax.shift_left(jnp.int32(1), nbits - 1 - b)
                below = count(lambda k, c: jnp.logical_and(k == ans, c * tk + lane_col < cand))
                return jnp.where(below < need, cand, cut)
            return lax.fori_loop(0, nbits, bit_body, jnp.zeros((t, 1), I32))

        contested = jnp.max(jnp.where(jnp.logical_and(full, ties > need), 1.0, 0.0)) > 0.0
        cut = lax.cond(contested, tie_cut, lambda _: jnp.full((t, 1), seq - 1, I32), 0)
        tbits = jnp.where(ans < 0, ans ^ jnp.int32(0x7FFFFFFF), ans)
        thr = jnp.where(full, pltpu.bitcast(tbits, F32), -jnp.inf)
        thr_ref[...] = jnp.broadcast_to(thr, thr_ref.shape)
        jc_ref[...] = jnp.broadcast_to(jnp.where(full, cut, seq - 1), jc_ref.shape)


def _causal_pairs(seq, t, tk):
    pairs = [(i, j) for i in range(seq // t) for j in range((i * t + t - 1) // tk + 1)]
    return jnp.asarray([p[0] for p in pairs], I32), jnp.asarray([p[1] for p in pairs], I32)


def _idx_topk(qh, kidx, widx, *, batch, seq, ksel, name):
    t, tk = ATTN_TILE, min(DSA_KEY_TILE, seq)
    nq, nk = seq // t, seq // tk
    assert seq & (seq - 1) == 0 and tk >= ksel
    qi, kj = _causal_pairs(seq, t, tk)
    npairs = qi.shape[0]
    qrow = lambda b, p, qi, kj: (b * nq + qi[p], 0)
    return pl.pallas_call(
        functools.partial(_idx_topk_kernel, t=t, tk=tk, ksel=ksel, seq=seq),
        out_shape=(jax.ShapeDtypeStruct((batch, npairs, t, tk), F32),
                   jax.ShapeDtypeStruct((batch * seq, LANES), F32),
                   jax.ShapeDtypeStruct((batch * seq, LANES), I32)),
        grid_spec=pltpu.PrefetchScalarGridSpec(
            num_scalar_prefetch=2,
            grid=(batch, npairs),
            in_specs=[pl.BlockSpec((IDX_HEADS, t, LANES), lambda b, p, qi, kj: (1, b * nq + qi[p], 0)),
                      pl.BlockSpec((tk, LANES), lambda b, p, qi, kj: (b * nk + kj[p], 0)),
                      pl.BlockSpec((t, LANES), qrow)],
            out_specs=(pl.BlockSpec((None, None, t, tk), lambda b, p, qi, kj: (b, p, 0, 0)),
                       pl.BlockSpec((t, LANES), qrow), pl.BlockSpec((t, LANES), qrow)),
            scratch_shapes=[pltpu.VMEM((nk, t, tk), I32)]),
        compiler_params=_cparams(("parallel", "arbitrary")),
        name=name,
    )(qi, kj, qh, kidx, widx)


def _dsa_attn_kernel(qi_ref, kj_ref, q_ref, k_ref, v_ref, sc_ref, thr_ref, jc_ref, bias_ref,
                     o_ref, m_ref, l_ref, acc_ref, *, t, tk):
    p = pl.program_id(1)
    i, j = qi_ref[p], kj_ref[p]

    @pl.when(j == 0)
    def _():
        m_ref[...] = jnp.full(m_ref.shape, NEG, F32)
        l_ref[...] = jnp.zeros_like(l_ref)
        acc_ref[...] = jnp.zeros_like(acc_ref)

    sc = sc_ref[...]
    thr = thr_ref[:, 0:1]
    col = j * tk + lax.broadcasted_iota(I32, (t, tk), 1)
    row = i * t + lax.broadcasted_iota(I32, (t, tk), 0)
    tie = jnp.logical_and(sc == thr, col <= jc_ref[:, 0:1])
    keep = jnp.logical_and(jnp.logical_or(sc > thr, tie), col <= row)
    offs = [i * t - (j * tk + u * t) for u in range(tk // t)]

    def run(with_bias):
        def head_group(g, _):
            heads = [g * DSA_HEAD_GROUP + u for u in range(DSA_HEAD_GROUP)]
            logits = [_dot_t(q_ref[h], k_ref[h]) for h in heads]
            for h, s in zip(heads, logits):
                if with_bias:
                    parts = [jnp.where(off == 0, bias_ref[0, h],
                                       jnp.where(off == t, bias_ref[1, h], 0.0)) for off in offs]
                    s = s + (parts[0] if len(parts) == 1 else jnp.concatenate(parts, axis=1))
                s = jnp.where(keep, s, NEG)
                m_prev = m_ref[h]
                m_next = jnp.maximum(m_prev, jnp.max(s, axis=-1, keepdims=True))
                alpha = jnp.exp2(m_prev - m_next)
                p = jnp.exp2(s - jnp.concatenate([m_next] * (tk // HEAD_DIM), axis=1))
                l_ref[h] = alpha * l_ref[h] + jnp.sum(p, axis=-1, keepdims=True)
                acc_ref[h] = alpha * acc_ref[h] + _dot(p.astype(BF16), v_ref[h])
                m_ref[h] = m_next
            return 0

        lax.fori_loop(0, N_DSA_HEADS // DSA_HEAD_GROUP, head_group, 0)

    near = offs[-1] <= t

    @pl.when(near)
    def _():
        run(True)

    @pl.when(jnp.logical_not(near))
    def _():
        run(False)

    @pl.when(j == (i * t + t - 1) // tk)
    def _():
        for h in range(N_DSA_HEADS):
            o_ref[:, h * HEAD_DIM:(h + 1) * HEAD_DIM] = (acc_ref[h] / l_ref[h]).astype(o_ref.dtype)


def _dsa_attention(qh, kvh, scores, thr, jcut, bias, *, batch, seq, name):
    t, tk = ATTN_TILE, min(DSA_KEY_TILE, seq)
    nq, nk = seq // t, seq // tk
    h = N_DSA_HEADS
    qi, kj = _causal_pairs(seq, t, tk)
    qrow = lambda b, p, qi, kj: (b * nq + qi[p], 0)
    return pl.pallas_call(
        functools.partial(_dsa_attn_kernel, t=t, tk=tk),
        out_shape=jax.ShapeDtypeStruct((batch * seq, h * HEAD_DIM), BF16),
        grid_spec=pltpu.PrefetchScalarGridSpec(
            num_scalar_prefetch=2,
            grid=(batch, qi.shape[0]),
            in_specs=[pl.BlockSpec((h, t, HEAD_DIM), lambda b, p, qi, kj: (0, b * nq + qi[p], 0)),
                      pl.BlockSpec((h, tk, HEAD_DIM), lambda b, p, qi, kj: (0, b * nk + kj[p], 0)),
                      pl.BlockSpec((h, tk, HEAD_DIM), lambda b, p, qi, kj: (1, b * nk + kj[p], 0)),
                      pl.BlockSpec((None, None, t, tk), lambda b, p, qi, kj: (b, p, 0, 0)),
                      pl.BlockSpec((t, LANES), qrow),
                      pl.BlockSpec((t, LANES), qrow),
                      pl.BlockSpec((2, h, t, t), lambda b, p, qi, kj: (0, 0, 0, 0))],
            out_specs=pl.BlockSpec((t, h * HEAD_DIM), qrow),
            scratch_shapes=[pltpu.VMEM((h, t, HEAD_DIM), F32), pltpu.VMEM((h, t, HEAD_DIM), F32),
                            pltpu.VMEM((h, t, HEAD_DIM), F32)]),
        compiler_params=_cparams(("parallel", "arbitrary")),
        name=name,
    )(qi, kj, qh, kvh, kvh, scores, thr, jcut, bias)


def _route_kernel(lg_ref, info_ref, gate_ref, cnt_ref, carry_ref, *, tm):
    i = pl.program_id(0)

    @pl.when(i == 0)
    def _():
        carry_ref[...] = jnp.zeros_like(carry_ref)

    lane = lax.broadcasted_iota(I32, (tm, LANES), 1)
    lane_f = lane.astype(F32)
    lg = jnp.where(lane < N_EXPERTS, lg_ref[...], -jnp.inf)
    m1 = jnp.max(lg, axis=-1, keepdims=True)
    e1 = jnp.min(jnp.where(lg == m1, lane_f, float(LANES)), axis=-1, keepdims=True).astype(I32)
    lg2 = jnp.where(lane == e1, -jnp.inf, lg)
    m2 = jnp.max(lg2, axis=-1, keepdims=True)
    e2 = jnp.min(jnp.where(lg2 == m2, lane_f, float(LANES)), axis=-1, keepdims=True).astype(I32)
    ex = jnp.exp(m2 - m1)
    g1 = 1.0 / (1.0 + ex)
    g2 = ex / (1.0 + ex)
    onehot = jnp.where(jnp.logical_or(lane == e1, lane == e2), 1.0, 0.0)
    r = lax.broadcasted_iota(I32, (tm, tm), 0)
    c = lax.broadcasted_iota(I32, (tm, tm), 1)
    before = _dot(jnp.where(c < r, 1.0, 0.0).astype(BF16), onehot.astype(BF16)) + carry_ref[0:1, :]
    r1 = jnp.sum(jnp.where(lane == e1, before, 0.0), axis=-1, keepdims=True).astype(I32)
    r2 = jnp.sum(jnp.where(lane == e2, before, 0.0), axis=-1, keepdims=True).astype(I32)
    info = jnp.where(lane == 0, e1, jnp.where(lane == 1, e2, jnp.where(lane == 2, r1, r2)))
    info_ref[...] = info
    gate_ref[...] = jnp.where(lane == 0, g1, g2)
    total = carry_ref[0:1, :] + jnp.sum(onehot, axis=0, keepdims=True)
    carry_ref[...] = jnp.broadcast_to(total, carry_ref.shape)
    cnt_ref[...] = jnp.broadcast_to(total, cnt_ref.shape)


def _route(logits, *, tm=256, name):
    n = logits.shape[0]
    tm = min(tm, n)
    return pl.pallas_call(
        functools.partial(_route_kernel, tm=tm),
        out_shape=(jax.ShapeDtypeStruct((n, LANES), I32), jax.ShapeDtypeStruct((n, LANES), F32),
                   jax.ShapeDtypeStruct((8, LANES), F32)),
        grid=(n // tm,),
        in_specs=[pl.BlockSpec((tm, LANES), lambda i: (i, 0))],
        out_specs=(pl.BlockSpec((tm, LANES), lambda i: (i, 0)),
                   pl.BlockSpec((tm, LANES), lambda i: (i, 0)),
                   pl.BlockSpec((8, LANES), lambda i: (0, 0))),
        scratch_shapes=[pltpu.VMEM((8, LANES), F32)],
        compiler_params=_cparams(("arbitrary",)),
        name=name,
    )(logits)


def _combine_ln_kernel(d0_ref, d1_ref, h_ref, ys_ref, gate_ref, g_ref, b_ref, o_ref,
                       ya0_ref, ya1_ref, yb0_ref, yb1_ref, sems_a, sems_b, *, tm):
    i = pl.program_id(0)
    nt = pl.num_programs(0)
    bufs_a, bufs_b = (ya0_ref, ya1_ref), (yb0_ref, yb1_ref)

    def gather(tile, wait):
        for idx_ref, bufs, sems in ((d0_ref, bufs_a, sems_a), (d1_ref, bufs_b, sems_b)):
            _row_gather(ys_ref, idx_ref, tile * tm, bufs, sems, tile % 2, tm, wait=wait)

    @pl.when(i == 0)
    def _():
        gather(0, wait=False)

    gather(i, wait=True)

    @pl.when(i + 1 < nt)
    def _():
        gather(i + 1, wait=False)

    for s in range(2):
        @pl.when(i % 2 == s)
        def _(s=s):
            ff = (gate_ref[:, 0:1] * _slabs_to_rows(bufs_a[s])
                  + gate_ref[:, 1:2] * _slabs_to_rows(bufs_b[s]))
            o_ref[...] = _layer_norm(DEEPNORM_ALPHA * h_ref[...] + ff, g_ref[...], b_ref[...])


def _combine_ln(dest0, dest1, h, ys, gates, g, b, *, tm=256, name):
    n, d = h.shape
    slabs = ys.shape[1]
    tm = min(tm, n)
    row = lambda i, d0, d1: (i, 0)
    const = lambda i, d0, d1: (0, 0)
    slab_buf = pltpu.VMEM((tm, slabs, LANES), F32)
    return pl.pallas_call(
        functools.partial(_combine_ln_kernel, tm=tm),
        out_shape=jax.ShapeDtypeStruct((n, d), F32),
        grid_spec=pltpu.PrefetchScalarGridSpec(
            num_scalar_prefetch=2,
            grid=(n // tm,),
            in_specs=[pl.BlockSpec((tm, d), row),
                      pl.BlockSpec(memory_space=pl.ANY),
                      pl.BlockSpec((tm, LANES), row),
                      pl.BlockSpec((1, d), const),
                      pl.BlockSpec((1, d), const)],
            out_specs=pl.BlockSpec((tm, d), row),
            scratch_shapes=[slab_buf, slab_buf, slab_buf, slab_buf,
                            pltpu.SemaphoreType.DMA((2,)), pltpu.SemaphoreType.DMA((2,))]),
        compiler_params=_cparams(("arbitrary",)),
        name=name,
    )(dest0, dest1, h, ys, gates, g, b)


def _rel_bucket(dist):
    n = jnp.maximum(dist, 0)
    exact = REL_BUCKETS // 2
    nf = jnp.maximum(n, 1).astype(F32)
    large = exact + (jnp.log(nf / exact) / math.log(REL_MAX_DIST / exact) * (REL_BUCKETS - exact)).astype(I32)
    large = jnp.minimum(large, REL_BUCKETS - 1)
    return jnp.where(n < exact, n, large)


def _bias_tile_kernel(tab_ref, bucket_ref, o_ref):
    h = pl.program_id(1)
    bucket = bucket_ref[...]
    far = tab_ref[REL_BUCKETS - 1, h]
    acc = jnp.zeros(o_ref.shape, F32)
    for b in range(REL_BUCKETS - 1):
        acc = jnp.where(bucket == b, tab_ref[b, h] - far, acc)
    o_ref[...] = acc * LOG2E


def _bias_tiles(rel_table, n_heads, t, *, name):
    assert t >= REL_MAX_DIST
    r = jnp.arange(t)[:, None]
    c = jnp.arange(t)[None, :]
    buckets = jnp.stack([_rel_bucket(r - c + off) for off in (0, t)]).astype(I32)
    return pl.pallas_call(
        _bias_tile_kernel,
        out_shape=jax.ShapeDtypeStruct((2, n_heads, t, t), F32),
        grid=(2, n_heads),
        in_specs=[pl.BlockSpec(memory_space=pltpu.SMEM),
                  pl.BlockSpec((None, t, t), lambda k, h: (k, 0, 0))],
        out_specs=pl.BlockSpec((None, None, t, t), lambda k, h: (k, h, 0, 0)),
        compiler_params=_cparams(("parallel", "parallel")),
        name=name,
    )(rel_table.astype(F32), buckets)


def _pad_cols(w, n):
    return jnp.pad(w, ((0, 0), (0, n - w.shape[1])))


def _even_layer(h, rel_table, w_in, b_forget, w_out, ln1_g, ln1_b, w1, w3, w2, ln2_g, ln2_b,
                *, batch, seq):
    d = h.shape[1]
    wa = N_MOBA_HEADS * HEAD_DIM
    wb = N_FOX_HEADS * HEAD_DIM
    n_qkv = 3 * wa + 3 * wb
    scale = HEAD_DIM ** -0.5 * LOG2E
    ones, scl = jnp.ones((wa,), F32), jnp.full((wa,), scale, F32)
    colscale = jnp.concatenate([scl, ones, ones, scl, ones, ones])[None, :]
    qkv = _mm(h, w_in[:, :n_qkv].astype(BF16), colscale, out_dtype=BF16, name="ev_qkv_proj")
    fb = _mm(h, _pad_cols(w_in[:, n_qkv:], LANES).astype(BF16), jnp.ones((1, LANES), F32),
             out_dtype=F32, tn=LANES, name="ev_forget_proj")
    log_f = jax.nn.log_sigmoid(fb[:, :N_FOX_HEADS] + b_forget.astype(F32))
    csum = jnp.cumsum(log_f.reshape(batch, seq, N_FOX_HEADS).transpose(0, 2, 1), axis=-1) * LOG2E
    csum = csum.reshape(batch, N_FOX_HEADS, seq // ATTN_TILE, 1, ATTN_TILE)
    oa = _moba_attention(qkv, _bias_tiles(rel_table, N_MOBA_HEADS, MOBA_BLOCK, name="ev_bias_tiles"),
                         batch=batch, seq=seq, name="ev_moba_attn")
    ob = _fox_attention(qkv, csum, batch=batch, seq=seq, name="ev_fox_attn")
    attn = jnp.concatenate([oa, ob], axis=-1)
    h = _mm_ln(attn, w_out.astype(BF16), h, ln1_g[None, :], ln1_b[None, :], name="ev_out_proj_ln")
    return _ffn_ln(h, w1.astype(BF16), w3.astype(BF16), w2.astype(BF16),
                   ln2_g[None, :], ln2_b[None, :], name="ev_swiglu_ln")


def _moe(h, router, w1, w3, w2, ln_g, ln_b):
    n, d = h.shape
    tm = EXPERT_TILE
    logits = _mm(h, _pad_cols(router, LANES).astype(BF16), jnp.ones((1, LANES), F32),
                 out_dtype=F32, tn=LANES, name="od_router")
    info, gates, cnt = _route(logits, name="od_route")
    counts = cnt[0, :N_EXPERTS].astype(I32)
    padded = (counts + tm - 1) // tm * tm
    pend = jnp.cumsum(padded)
    pstart = pend - padded
    e = info[:, :MOE_TOPK]
    dest = (pstart[e] + info[:, MOE_TOPK:2 * MOE_TOPK]).astype(I32)
    n_rows = -(-(n * MOE_TOPK + N_EXPERTS * (tm - 1)) // tm) * tm
    n_tiles = n_rows // tm
    tile_start = jnp.arange(n_tiles, dtype=I32) * tm
    tile_valid = (tile_start < pend[-1]).astype(I32)
    last = jnp.maximum(pend[-1] - 1, 0)
    tile_e = jnp.minimum(jnp.searchsorted(pend, jnp.minimum(tile_start, last), side="right"),
                         N_EXPERTS - 1).astype(I32)
    tok = jnp.arange(n, dtype=I32)
    row_tok = jnp.zeros((n_rows,), I32).at[dest.T.reshape(-1)].set(jnp.concatenate([tok, tok]))
    ys = _moe_ffn(tile_e, tile_valid, row_tok, h.reshape(n, d // LANES, LANES), w1.astype(BF16),
                  w3.astype(BF16), w2.astype(BF16), tm=tm, name="od_moe_swiglu")
    return _combine_ln(dest[:, 0], dest[:, 1], h, ys, gates, ln_g[None, :], ln_b[None, :],
                       name="od_moe_combine_ln")


def _odd_layer(h, rel_table, w_in, q_norm_g, kv_norm_g, w_uq, w_qidx, w_uk, w_uv, w_out,
               ln1_g, ln1_b, router, w1, w3, w2, ln2_g, ln2_b, *, batch, seq):
    nh = N_DSA_HEADS
    rq, rkv = DSA_Q_RANK, DSA_KV_RANK
    scale = HEAD_DIM ** -0.5 * LOG2E
    w_in_p = jnp.concatenate([w_in[:, :rq + rkv],
                              _pad_cols(w_in[:, rq + rkv:rq + rkv + IDX_DIM], LANES),
                              _pad_cols(w_in[:, rq + rkv + IDX_DIM:], LANES)], axis=1)
    cq, ckv, kidx, widx = _dsa_in_proj(h, w_in_p.astype(BF16), q_norm_g[None, :], kv_norm_g[None, :],
                                       name="od_in_proj_rms")
    w_qidx_p = jnp.pad(w_qidx.reshape(rq, IDX_HEADS, IDX_DIM), ((0, 0), (0, 0), (0, LANES - IDX_DIM)))
    wq = jnp.concatenate([w_uq, w_qidx_p.reshape(rq, IDX_HEADS * LANES)], axis=1)
    qscale = jnp.concatenate([jnp.full((nh * HEAD_DIM,), scale, F32),
                              jnp.ones((IDX_HEADS * LANES,), F32)])[None, :]
    qh = _mm(cq, wq.astype(BF16), qscale, out_dtype=BF16, head_major=True, tm=1024, name="od_q_proj")
    wkv = jnp.concatenate([w_uk.transpose(1, 0, 2).reshape(rkv, nh * HEAD_DIM),
                           w_uv.transpose(1, 0, 2).reshape(rkv, nh * HEAD_DIM)], axis=1)
    kvh = _mm(ckv, wkv.astype(BF16), jnp.ones((1, 2 * nh * HEAD_DIM), F32), out_dtype=BF16,
              head_major=True, tm=1024, name="od_kv_proj")
    scores, thr, jcut = _idx_topk(qh, kidx, widx, batch=batch, seq=seq,
                                  ksel=min(DSA_TOPK_MAX, seq // 4), name="od_idx_topk")
    attn = _dsa_attention(qh, kvh, scores, thr, jcut,
                          _bias_tiles(rel_table, nh, ATTN_TILE, name="od_bias_tiles"),
                          batch=batch, seq=seq, name="od_dsa_attn")
    h = _mm_ln(attn, w_out.astype(BF16), h, ln1_g[None, :], ln1_b[None, :], name="od_out_proj_ln")
    return _moe(h, router, w1, w3, w2, ln2_g, ln2_b)


def kernel(x, rel_table, ev_w_in, ev_b_forget, ev_w_out, ev_ln1_g, ev_ln1_b, ev_ffn_w1, ev_ffn_w3, ev_ffn_w2, ev_ln2_g, ev_ln2_b, od_w_in, od_q_norm_g, od_kv_norm_g, od_w_uq, od_w_qidx, od_w_uk, od_w_uv, od_w_out, od_ln1_g, od_ln1_b, od_router, od_exp_w1, od_exp_w3, od_exp_w2, od_ln2_g, od_ln2_b):
    batch, seq, d = x.shape
    h = x.reshape(batch * seq, d)
    for layer in range(DEPTH):
        i = layer // 2
        if layer % 2 == 0:
            h = _even_layer(h, rel_table, ev_w_in[i], ev_b_forget[i], ev_w_out[i], ev_ln1_g[i],
                            ev_ln1_b[i], ev_ffn_w1[i], ev_ffn_w3[i], ev_ffn_w2[i], ev_ln2_g[i],
                            ev_ln2_b[i], batch=batch, seq=seq)
        else:
            h = _odd_layer(h, rel_table, od_w_in[i], od_q_norm_g[i], od_kv_norm_g[i], od_w_uq[i],
                           od_w_qidx[i], od_w_uk[i], od_w_uv[i], od_w_out[i], od_ln1_g[i],
                           od_ln1_b[i], od_router[i], od_exp_w1[i], od_exp_w3[i], od_exp_w2[i],
                           od_ln2_g[i], od_ln2_b[i], batch=batch, seq=seq)
    return h.reshape(batch, seq, d)
```

```python
import functools
import math

import jax
import jax.numpy as jnp
from jax import lax
from jax.experimental import pallas as pl
from jax.experimental.pallas import tpu as pltpu

F32 = jnp.float32
BF16 = jnp.bfloat16
I32 = jnp.int32

HEAD_DIM = 128
N_MOBA_HEADS = 8
N_FOX_HEADS = 8
MOBA_BLOCK = 256
MOBA_TOPK = 3
N_DSA_HEADS = 16
DSA_Q_RANK = 512
DSA_KV_RANK = 512
IDX_HEADS = 16
IDX_DIM = 64
DSA_TOPK_MAX = 256
REL_BUCKETS = 32
REL_MAX_DIST = 128
N_EXPERTS = 8
MOE_TOPK = 2
LN_EPS = 1e-5
RMS_EPS = 1e-6
DEPTH = 2
DEEPNORM_ALPHA = (2 * DEPTH) ** 0.25

LANES = 128
ATTN_TILE = 256
WIDE_CHUNK = 1024
DSA_KEY_TILE = 512
DSA_HEAD_GROUP = 4
EXPERT_TILE = 512
NEG = -1e30
LOG2E = math.log2(math.e)
INT_MIN = -(2 ** 31)
VMEM_LIMIT = 56 * 1024 * 1024


def _cparams(sem, vmem=VMEM_LIMIT, flags=None):
    return pltpu.CompilerParams(dimension_semantics=sem, vmem_limit_bytes=vmem, flags=flags)


INTERLEAVE_CHAINS = None


def _dot(a, b):
    return jnp.dot(a, b, preferred_element_type=F32)


def _dot_t(a, b):
    return lax.dot_general(a, b, (((1,), (1,)), ((), ())), preferred_element_type=F32)


def _layer_norm(y, g, b):
    mu = jnp.mean(y, axis=-1, keepdims=True)
    d = y - mu
    var = jnp.mean(d * d, axis=-1, keepdims=True)
    return d * lax.rsqrt(var + LN_EPS) * g + b


def _mm_kernel(x_ref, w_ref, cs_ref, o_ref, *, head_major):
    acc = _dot(x_ref[...].astype(BF16), w_ref[...]) * cs_ref[...]
    if head_major:
        for c in range(o_ref.shape[0]):
            o_ref[c] = acc[:, c * LANES:(c + 1) * LANES].astype(o_ref.dtype)
    else:
        o_ref[...] = acc.astype(o_ref.dtype)


def _mm(x, w, colscale, *, out_dtype, head_major=False, tm=512, tn=512, name):
    m, k = x.shape
    n = w.shape[1]
    tm, tn = min(tm, m), min(tn, n)
    assert m % tm == 0 and n % tn == 0 and tn % LANES == 0
    if head_major:
        out_shape = jax.ShapeDtypeStruct((n // LANES, m, LANES), out_dtype)
        out_spec = pl.BlockSpec((tn // LANES, tm, LANES), lambda i, j: (j, i, 0))
    else:
        out_shape = jax.ShapeDtypeStruct((m, n), out_dtype)
        out_spec = pl.BlockSpec((tm, tn), lambda i, j: (i, j))
    return pl.pallas_call(
        functools.partial(_mm_kernel, head_major=head_major),
        out_shape=out_shape,
        grid=(m // tm, n // tn),
        in_specs=[pl.BlockSpec((tm, k), lambda i, j: (i, 0)),
                  pl.BlockSpec((k, tn), lambda i, j: (0, j)),
                  pl.BlockSpec((1, tn), lambda i, j: (0, j))],
        out_specs=out_spec,
        compiler_params=_cparams(("parallel", "arbitrary")),
        name=name,
    )(x, w, colscale)


def _mm_ln_kernel(x_ref, w_ref, res_ref, g_ref, b_ref, o_ref):
    y = DEEPNORM_ALPHA * res_ref[...] + _dot(x_ref[...], w_ref[...])
    o_ref[...] = _layer_norm(y, g_ref[...], b_ref[...])


def _mm_ln(x, w, res, g, b, *, tm=256, name):
    m, k = x.shape
    d = w.shape[1]
    tm = min(tm, m)
    return pl.pallas_call(
        _mm_ln_kernel,
        out_shape=jax.ShapeDtypeStruct((m, d), F32),
        grid=(m // tm,),
        in_specs=[pl.BlockSpec((tm, k), lambda i: (i, 0)),
                  pl.BlockSpec((k, d), lambda i: (0, 0)),
                  pl.BlockSpec((tm, d), lambda i: (i, 0)),
                  pl.BlockSpec((1, d), lambda i: (0, 0)),
                  pl.BlockSpec((1, d), lambda i: (0, 0))],
        out_specs=pl.BlockSpec((tm, d), lambda i: (i, 0)),
        compiler_params=_cparams(("parallel",)),
        name=name,
    )(x, w, res, g, b)


def _swiglu_step(xb, w1_ref, w3_ref, w2_ref):
    a = _dot(xb, w1_ref[...])
    c = _dot(xb, w3_ref[...])
    hmid = a / (1.0 + jnp.exp(-a)) * c
    return _dot(hmid.astype(BF16), w2_ref[...])


def _ffn_ln_kernel(x_ref, w1_ref, w3_ref, w2_ref, g_ref, b_ref, o_ref, acc_ref, xb_ref):
    f = pl.program_id(1)

    @pl.when(f == 0)
    def _():
        acc_ref[...] = jnp.zeros_like(acc_ref)
        xb_ref[...] = x_ref[...].astype(BF16)

    acc_ref[...] += _swiglu_step(xb_ref[...], w1_ref, w3_ref, w2_ref)

    @pl.when(f == pl.num_programs(1) - 1)
    def _():
        y = DEEPNORM_ALPHA * x_ref[...] + acc_ref[...]
        o_ref[...] = _layer_norm(y, g_ref[...], b_ref[...])


def _ffn_ln(x, w1, w3, w2, g, b, *, tm=512, tf=512, name):
    m, d = x.shape
    dff = w1.shape[1]
    tm, tf = min(tm, m), min(tf, dff)
    assert m % tm == 0 and dff % tf == 0
    return pl.pallas_call(
        _ffn_ln_kernel,
        out_shape=jax.ShapeDtypeStruct((m, d), F32),
        grid=(m // tm, dff // tf),
        in_specs=[pl.BlockSpec((tm, d), lambda i, f: (i, 0)),
                  pl.BlockSpec((d, tf), lambda i, f: (0, f)),
                  pl.BlockSpec((d, tf), lambda i, f: (0, f)),
                  pl.BlockSpec((tf, d), lambda i, f: (f, 0)),
                  pl.BlockSpec((1, d), lambda i, f: (0, 0)),
                  pl.BlockSpec((1, d), lambda i, f: (0, 0))],
        out_specs=pl.BlockSpec((tm, d), lambda i, f: (i, 0)),
        scratch_shapes=[pltpu.VMEM((tm, d), F32), pltpu.VMEM((tm, d), BF16)],
        compiler_params=_cparams(("parallel", "arbitrary")),
        name=name,
    )(x, w1, w3, w2, g, b)


def _row_gather(src_hbm, idx_ref, base, bufs, sems, slot, n_rows, slabs, *, wait):
    for s, buf in enumerate(bufs):
        @pl.when(slot == s)
        def _(s=s, buf=buf):
            def body(r, _):
                src = src_hbm.at[pl.ds(pl.multiple_of(idx_ref[base + r] * slabs, slabs), slabs)]
                dst = buf.at[pl.ds(pl.multiple_of(r * slabs, slabs), slabs)]
                cp = pltpu.make_async_copy(src, dst, sems.at[s])
                if wait:
                    cp.wait()
                else:
                    cp.start()
                return 0
            lax.fori_loop(0, n_rows, body, 0, unroll=8)


def _slabs_to_rows(buf, slabs):
    rows = buf.shape[0] // slabs
    return jnp.concatenate([buf[pl.ds(c, rows, stride=slabs), :] for c in range(slabs)], axis=1)


def _moe_ffn_kernel(te_ref, tv_ref, tok_ref, h_ref, w1_ref, w3_ref, w2_ref, o_ref,
                    acc_ref, xb_ref, xg0_ref, xg1_ref, sems, *, tm, slabs):
    i, f = pl.program_id(0), pl.program_id(1)
    nt = pl.num_programs(0)
    bufs = (xg0_ref, xg1_ref)
    gather = functools.partial(_row_gather, h_ref, tok_ref, bufs=bufs, sems=sems, n_rows=tm,
                               slabs=slabs)

    @pl.when(tv_ref[i] > 0)
    def _():
        @pl.when(f == 0)
        def _():
            @pl.when(i == 0)
            def _():
                gather(base=0, slot=0, wait=False)

            gather(base=i * tm, slot=i % 2, wait=True)
            nxt = jnp.minimum(i + 1, nt - 1)

            @pl.when(jnp.logical_and(i + 1 < nt, tv_ref[nxt] > 0))
            def _():
                gather(base=nxt * tm, slot=nxt % 2, wait=False)

            for s, buf in enumerate(bufs):
                @pl.when(i % 2 == s)
                def _(buf=buf):
                    xb_ref[...] = _slabs_to_rows(buf, slabs).astype(BF16)

            acc_ref[...] = jnp.zeros_like(acc_ref)

        acc_ref[...] += _swiglu_step(xb_ref[...], w1_ref, w3_ref, w2_ref)

        @pl.when(f == pl.num_programs(1) - 1)
        def _():
            for c in range(slabs):
                o_ref[pl.ds(c, tm, stride=slabs), :] = acc_ref[:, c * LANES:(c + 1) * LANES]

    @pl.when(tv_ref[i] == 0)
    def _():
        o_ref[...] = jnp.zeros_like(o_ref)


def _moe_ffn(tile_e, tile_valid, row_tok, h_slabs, w1, w3, w2, *, tm, tf=512, name):
    n_rows = row_tok.shape[0]
    d = w1.shape[1]
    slabs = d // LANES
    dff = w1.shape[2]
    tf = min(tf, dff)
    nf = dff // tf
    assert n_rows % tm == 0 and dff % tf == 0

    def fidx(i, f, tv):
        return jnp.where(tv[i] > 0, f, nf - 1)

    return pl.pallas_call(
        functools.partial(_moe_ffn_kernel, tm=tm, slabs=slabs),
        out_shape=jax.ShapeDtypeStruct((n_rows * slabs, LANES), F32),
        grid_spec=pltpu.PrefetchScalarGridSpec(
            num_scalar_prefetch=3,
            grid=(n_rows // tm, nf),
            in_specs=[pl.BlockSpec(memory_space=pl.ANY),
                      pl.BlockSpec((None, d, tf), lambda i, f, te, tv, tok: (te[i], 0, fidx(i, f, tv))),
                      pl.BlockSpec((None, d, tf), lambda i, f, te, tv, tok: (te[i], 0, fidx(i, f, tv))),
                      pl.BlockSpec((None, tf, d), lambda i, f, te, tv, tok: (te[i], fidx(i, f, tv), 0))],
            out_specs=pl.BlockSpec((tm * slabs, LANES), lambda i, f, te, tv, tok: (i, 0)),
            scratch_shapes=[pltpu.VMEM((tm, d), F32), pltpu.VMEM((tm, d), BF16),
                            pltpu.VMEM((tm * slabs, LANES), F32), pltpu.VMEM((tm * slabs, LANES), F32),
                            pltpu.SemaphoreType.DMA((2,))]),
        compiler_params=_cparams(("arbitrary", "arbitrary")),
        name=name,
    )(tile_e, tile_valid, row_tok, h_slabs, w1, w3, w2)


def _online_softmax_step(s, v, carry):
    m, acc = carry
    m_new = jnp.maximum(m, jnp.max(s, axis=-1, keepdims=True))
    alpha = jnp.exp2(m - m_new)
    p = jnp.exp2((s - m_new).astype(BF16))
    return m_new, alpha * acc + _dot(p, _with_ones(v))


def _with_ones(v):
    return jnp.concatenate([v, jnp.ones_like(v)], axis=1)


def _softmax_init(tq):
    return jnp.full((tq, 1), NEG, F32), jnp.zeros((tq, 2 * HEAD_DIM), F32)


def _softmax_finish(acc):
    return acc[:, :HEAD_DIM] / acc[:, HEAD_DIM:]


def _pairwise_loop(n, body, carry):
    carry = lax.fori_loop(0, n // 2, lambda k, c: body(2 * k + 1, body(2 * k, c)), carry)
    return lax.fori_loop(n // 2 * 2, n, body, carry)


def _head_cols(h):
    return slice(h * HEAD_DIM, (h + 1) * HEAD_DIM)


def _kv_block(k_ref, v_ref, n, width, h):
    rows = pl.ds(pl.multiple_of(n * width, width), width)
    return k_ref[rows, _head_cols(h)], v_ref[rows, _head_cols(h)]


def _causal_mask(blk):
    r = lax.broadcasted_iota(I32, (blk, blk), 0)
    c = lax.broadcasted_iota(I32, (blk, blk), 1)
    return c <= r


def _moba_select(q, km, i, blk):
    nbp = km.shape[0]
    gate = _dot_t(q, km.astype(BF16))
    lane = lax.broadcasted_iota(I32, (blk, nbp), 1)
    lane_f = lane.astype(F32)
    g = jnp.where(lane < i, gate, -jnp.inf)
    sel = jnp.zeros((blk, nbp), F32)
    for _ in range(MOBA_TOPK):
        mx = jnp.max(g, axis=-1, keepdims=True)
        first = jnp.min(jnp.where(g == mx, lane_f, float(nbp)), axis=-1, keepdims=True)
        pick = jnp.logical_and(lane_f == first, mx > -jnp.inf)
        sel = jnp.where(pick, 1.0, sel)
        g = jnp.where(pick, -jnp.inf, g)
    return sel, lane


def _moba_kernel(q_ref, k_ref, v_ref, bias_ref, o_ref, km_ref, *, blk, nb, wide, hp):
    i = pl.program_id(2)
    per = wide // blk

    @pl.when(i == 0)
    def _():
        km_ref[...] = jnp.zeros_like(km_ref)

        def mean_body(n, _):
            for h in range(hp):
                kblk, _ = _kv_block(k_ref, v_ref, n, blk, h)
                km_ref[h, pl.ds(n, 1), :] = jnp.mean(kblk.astype(F32), axis=0, keepdims=True)
            return 0

        lax.fori_loop(0, nb, mean_body, 0)

    qs = [q_ref[:, _head_cols(h)] for h in range(hp)]
    sels = [_moba_select(qs[h], km_ref[h], i, blk) for h in range(hp)]

    def past_step(n, carries, width, bias_idx):
        kv = [_kv_block(k_ref, v_ref, n, width, h) for h in range(hp)]
        logits = [_dot_t(qs[h], kv[h][0]) for h in range(hp)]
        out = []
        for h in range(hp):
            s, vblk = logits[h], kv[h][1]
            if bias_idx is not None:
                s = s + bias_ref[bias_idx, h]
            sel, lane = sels[h]
            parts = []
            for u in range(width // blk):
                blk_id = n * (width // blk) + u
                chosen = jnp.sum(jnp.where(lane == blk_id, sel, 0.0), axis=-1, keepdims=True)
                parts.append(jnp.where(chosen > 0.0, s[:, u * blk:(u + 1) * blk], NEG))
            s = parts[0] if len(parts) == 1 else jnp.concatenate(parts, axis=1)
            out.append(_online_softmax_step(s, vblk, carries[h]))
        return tuple(out)

    n_far = jnp.maximum(i - 1, 0)
    n_wide = n_far // per
    carries = tuple(_softmax_init(blk) for _ in range(hp))
    carries = _pairwise_loop(n_wide, lambda n, c: past_step(n, c, wide, None), carries)
    carries = lax.fori_loop(n_wide * per, n_far, lambda n, c: past_step(n, c, blk, None), carries)
    carries = past_step(n_far, carries, blk, 1)
    kv = [_kv_block(k_ref, v_ref, i, blk, h) for h in range(hp)]
    logits = [_dot_t(qs[h], kv[h][0]) for h in range(hp)]
    for h in range(hp):
        s = jnp.where(_causal_mask(blk), logits[h] + bias_ref[0, h], NEG)
        _, acc = _online_softmax_step(s, kv[h][1], carries[h])
        o_ref[:, _head_cols(h)] = _softmax_finish(acc).astype(o_ref.dtype)


def _moba_attention(qkv, bias, *, batch, seq, hp=2, name):
    blk = MOBA_BLOCK
    nb = seq // blk
    nbp = -(-nb // LANES) * LANES
    h = N_MOBA_HEADS
    nq = seq // blk
    hg = h // hp
    w = hp * HEAD_DIM
    return pl.pallas_call(
        functools.partial(_moba_kernel, blk=blk, nb=nb, wide=min(WIDE_CHUNK, seq), hp=hp),
        out_shape=jax.ShapeDtypeStruct((batch * seq, h * HEAD_DIM), BF16),
        grid=(batch, hg, nq),
        in_specs=[pl.BlockSpec((blk, w), lambda b, g, i: (b * nq + i, g)),
                  pl.BlockSpec((seq, w), lambda b, g, i: (b, hg + g)),
                  pl.BlockSpec((seq, w), lambda b, g, i: (b, 2 * hg + g)),
                  pl.BlockSpec((2, hp, blk, blk), lambda b, g, i: (0, g, 0, 0))],
        out_specs=pl.BlockSpec((blk, w), lambda b, g, i: (b * nq + i, g)),
        scratch_shapes=[pltpu.VMEM((hp, nbp, HEAD_DIM), F32)],
        compiler_params=_cparams(("parallel", "parallel", "arbitrary"), flags=INTERLEAVE_CHAINS),
        name=name,
    )(qkv, qkv, qkv, bias)


def _fox_kernel(q_ref, k_ref, v_ref, ck_ref, o_ref, *, blk, wide, hp):
    i = pl.program_id(2)
    per = wide // blk
    qs = [q_ref[:, _head_cols(h)] for h in range(hp)]

    def step(n, carries, causal):
        kv = [_kv_block(k_ref, v_ref, n, wide, h) for h in range(hp)]
        logits = [_dot_t(qs[h], kv[h][0]) for h in range(hp)]
        if causal:
            row = i * blk + lax.broadcasted_iota(I32, (blk, wide), 0)
            col = n * wide + lax.broadcasted_iota(I32, (blk, wide), 1)
            visible = col <= row
        out = []
        for h in range(hp):
            ck = [ck_ref[h, n * per + u] for u in range(per)]
            s = logits[h] - (ck[0] if per == 1 else jnp.concatenate(ck, axis=1))
            if causal:
                s = jnp.where(visible, s, NEG)
            out.append(_online_softmax_step(s, kv[h][1], carries[h]))
        return tuple(out)

    n_wide = i // per
    carries = tuple(_softmax_init(blk) for _ in range(hp))
    carries = _pairwise_loop(n_wide, lambda n, c: step(n, c, False), carries)
    carries = step(n_wide, carries, True)
    for h in range(hp):
        o_ref[:, _head_cols(h)] = _softmax_finish(carries[h][1]).astype(o_ref.dtype)


def _fox_attention(qkv, csum, *, batch, seq, hp=2, name):
    blk = ATTN_TILE
    h = N_FOX_HEADS
    nq = seq // blk
    hg = h // hp
    base = 3 * N_MOBA_HEADS // hp
    w = hp * HEAD_DIM
    return pl.pallas_call(
        functools.partial(_fox_kernel, blk=blk, wide=min(WIDE_CHUNK, seq), hp=hp),
        out_shape=jax.ShapeDtypeStruct((batch * seq, h * HEAD_DIM), BF16),
        grid=(batch, hg, nq),
        in_specs=[pl.BlockSpec((blk, w), lambda b, g, i: (b * nq + i, base + g)),
                  pl.BlockSpec((seq, w), lambda b, g, i: (b, base + hg + g)),
                  pl.BlockSpec((seq, w), lambda b, g, i: (b, base + 2 * hg + g)),
                  pl.BlockSpec((None, hp, nq, 1, blk), lambda b, g, i: (b, g, 0, 0, 0))],
        out_specs=pl.BlockSpec((blk, w), lambda b, g, i: (b * nq + i, g)),
        compiler_params=_cparams(("parallel", "parallel", "arbitrary"), flags=INTERLEAVE_CHAINS),
        name=name,
    )(qkv, qkv, qkv, csum)


def _dsa_in_kernel(x_ref, w_ref, gq_ref, gkv_ref, cq_ref, ckv_ref, kidx_ref, widx_ref):
    acc = _dot(x_ref[...].astype(BF16), w_ref[...])
    rq, rkv = DSA_Q_RANK, DSA_KV_RANK

    def rms(z, g):
        return z * lax.rsqrt(jnp.mean(z * z, axis=-1, keepdims=True) + RMS_EPS) * g

    cq_ref[...] = rms(acc[:, :rq], gq_ref[...]).astype(cq_ref.dtype)
    ckv_ref[...] = rms(acc[:, rq:rq + rkv], gkv_ref[...]).astype(ckv_ref.dtype)
    kidx_ref[...] = acc[:, rq + rkv:rq + rkv + LANES].astype(kidx_ref.dtype)
    widx_ref[...] = acc[:, rq + rkv + LANES:] * (IDX_HEADS ** -0.5 * IDX_DIM ** -0.5)


def _dsa_in_proj(x, w, gq, gkv, *, tm=512, name):
    m, k = x.shape
    n = w.shape[1]
    tm = min(tm, m)
    rq, rkv = DSA_Q_RANK, DSA_KV_RANK
    row = lambda i: (i, 0)
    const = lambda i: (0, 0)
    return pl.pallas_call(
        _dsa_in_kernel,
        out_shape=(jax.ShapeDtypeStruct((m, rq), BF16), jax.ShapeDtypeStruct((m, rkv), BF16),
                   jax.ShapeDtypeStruct((m, LANES), BF16), jax.ShapeDtypeStruct((m, LANES), F32)),
        grid=(m // tm,),
        in_specs=[pl.BlockSpec((tm, k), row), pl.BlockSpec((k, n), const),
                  pl.BlockSpec((1, rq), const), pl.BlockSpec((1, rkv), const)],
        out_specs=(pl.BlockSpec((tm, rq), row), pl.BlockSpec((tm, rkv), row),
                   pl.BlockSpec((tm, LANES), row), pl.BlockSpec((tm, LANES), row)),
        compiler_params=_cparams(("parallel",)),
        name=name,
    )(x, w, gq, gkv)


def _fold_lanes(x):
    part = x[:, :LANES]
    for g in range(1, x.shape[1] // LANES):
        part = part + x[:, g * LANES:(g + 1) * LANES]
    return part


def _idx_topk_kernel(qi_ref, kj_ref, q_ref, k_ref, w_ref, sc_ref, thr_ref, jc_ref, keys_ref,
                     *, t, tk, ksel, seq):
    p = pl.program_id(1)
    i, j = qi_ref[p], kj_ref[p]
    k = k_ref[...]
    acc = jnp.zeros((t, tk), F32)
    for h in range(IDX_HEADS):
        acc = acc + jnp.maximum(_dot_t(q_ref[h], k), 0.0) * w_ref[:, h:h + 1]
    sc_ref[...] = acc

    row = i * t + lax.broadcasted_iota(I32, (t, tk), 0)
    lane_col = lax.broadcasted_iota(I32, (t, tk), 1)
    bits = pltpu.bitcast(acc, I32)
    key = jnp.where(bits < 0, bits ^ jnp.int32(0x7FFFFFFF), bits)
    key = jnp.where(bits == jnp.int32(INT_MIN), 0, key)
    keys_ref[j] = jnp.where(j * tk + lane_col <= row, key, jnp.int32(INT_MIN))
    j_last = (i * t + t - 1) // tk

    @pl.when(j == j_last)
    def _():
        def count(pred):
            def body(c, cnt):
                return cnt + _fold_lanes(pred(keys_ref[c], c).astype(I32))
            cnt = lax.fori_loop(0, j_last + 1, body, jnp.zeros((t, LANES), I32))
            return jnp.sum(cnt.astype(F32), axis=-1, keepdims=True).astype(I32)

        ans = jnp.where(count(lambda k, c: k >= 0) >= ksel, 0, jnp.int32(INT_MIN))

        def bit_body(b, ans):
            cand = ans + lax.shift_left(jnp.int32(1), 30 - b)
            return jnp.where(count(lambda k, c: k >= cand) >= ksel, cand, ans)

        ans = lax.fori_loop(0, 31, bit_body, ans)
        need = ksel - count(lambda k, c: k > ans)
        ties = count(lambda k, c: k == ans)
        row1 = i * t + lax.broadcasted_iota(I32, (t, 1), 0)
        full = row1 >= ksel

        def tie_cut(_):
            nbits = seq.bit_length() - 1

            def bit_body(b, cut):
                cand = cut + lax.shift_left(jnp.int32(1), nbits - 1 - b)
                below = count(lambda k, c: jnp.logical_and(k == ans, c * tk + lane_col < cand))
                return jnp.where(below < need, cand, cut)
            return lax.fori_loop(0, nbits, bit_body, jnp.zeros((t, 1), I32))

        contested = jnp.max(jnp.where(jnp.logical_and(full, ties > need), 1.0, 0.0)) > 0.0
        cut = lax.cond(contested, tie_cut, lambda _: jnp.full((t, 1), seq - 1, I32), 0)
        tbits = jnp.where(ans < 0, ans ^ jnp.int32(0x7FFFFFFF), ans)
        thr = jnp.where(full, pltpu.bitcast(tbits, F32), -jnp.inf)
        thr_ref[...] = jnp.broadcast_to(thr, thr_ref.shape)
        jc_ref[...] = jnp.broadcast_to(jnp.where(full, cut, seq - 1), jc_ref.shape)


def _causal_pairs(seq, t, tk):
    pairs = [(i, j) for i in range(seq // t) for j in range((i * t + t - 1) // tk + 1)]
    return jnp.asarray([p[0] for p in pairs], I32), jnp.asarray([p[1] for p in pairs], I32)


def _idx_topk(qh, kidx, widx, *, batch, seq, ksel, name):
    t, tk = ATTN_TILE, min(DSA_KEY_TILE, seq)
    nq, nk = seq // t, seq // tk
    assert seq & (seq - 1) == 0 and tk >= ksel
    qi, kj = _causal_pairs(seq, t, tk)
    npairs = qi.shape[0]
    qrow = lambda b, p, qi, kj: (b * nq + qi[p], 0)
    return pl.pallas_call(
        functools.partial(_idx_topk_kernel, t=t, tk=tk, ksel=ksel, seq=seq),
        out_shape=(jax.ShapeDtypeStruct((batch, npairs, t, tk), F32),
                   jax.ShapeDtypeStruct((batch * seq, LANES), F32),
                   jax.ShapeDtypeStruct((batch * seq, LANES), I32)),
        grid_spec=pltpu.PrefetchScalarGridSpec(
            num_scalar_prefetch=2,
            grid=(batch, npairs),
            in_specs=[pl.BlockSpec((IDX_HEADS, t, LANES), lambda b, p, qi, kj: (1, b * nq + qi[p], 0)),
                      pl.BlockSpec((tk, LANES), lambda b, p, qi, kj: (b * nk + kj[p], 0)),
                      pl.BlockSpec((t, LANES), qrow)],
            out_specs=(pl.BlockSpec((None, None, t, tk), lambda b, p, qi, kj: (b, p, 0, 0)),
                       pl.BlockSpec((t, LANES), qrow), pl.BlockSpec((t, LANES), qrow)),
            scratch_shapes=[pltpu.VMEM((nk, t, tk), I32)]),
        compiler_params=_cparams(("parallel", "arbitrary")),
        name=name,
    )(qi, kj, qh, kidx, widx)


def _dsa_attn_kernel(qi_ref, kj_ref, q_ref, k_ref, v_ref, sc_ref, thr_ref, jc_ref, bias_ref,
                     o_ref, m_ref, acc_ref, *, t, tk):
    p = pl.program_id(1)
    i, j = qi_ref[p], kj_ref[p]

    @pl.when(j == 0)
    def _():
        m_ref[...] = jnp.full(m_ref.shape, NEG, F32)
        acc_ref[...] = jnp.zeros_like(acc_ref)

    sc = sc_ref[...]
    thr = thr_ref[:, 0:1]
    col = j * tk + lax.broadcasted_iota(I32, (t, tk), 1)
    row = i * t + lax.broadcasted_iota(I32, (t, tk), 0)
    tie = jnp.logical_and(sc == thr, col <= jc_ref[:, 0:1])
    keep = jnp.logical_and(jnp.logical_or(sc > thr, tie), col <= row)
    offs = [i * t - (j * tk + u * t) for u in range(tk // t)]

    def run(with_bias):
        for g in range(N_DSA_HEADS // DSA_HEAD_GROUP):
            heads = [g * DSA_HEAD_GROUP + u for u in range(DSA_HEAD_GROUP)]
            logits = [_dot_t(q_ref[h], k_ref[h]) for h in heads]
            for h, s in zip(heads, logits):
                if with_bias:
                    parts = [jnp.where(off == 0, bias_ref[0, h],
                                       jnp.where(off == t, bias_ref[1, h], 0.0)) for off in offs]
                    s = s + (parts[0] if len(parts) == 1 else jnp.concatenate(parts, axis=1))
                s = jnp.where(keep, s, NEG)
                m_prev = m_ref[h]
                m_next = jnp.maximum(m_prev, jnp.max(s, axis=-1, keepdims=True))
                alpha = jnp.exp2(m_prev - m_next)
                p = jnp.exp2((s - jnp.concatenate([m_next] * (tk // HEAD_DIM), axis=1)).astype(BF16))
                acc_ref[h] = (jnp.concatenate([alpha, alpha], axis=1) * acc_ref[h]
                              + _dot(p, _with_ones(v_ref[h])))
                m_ref[h] = m_next

    near = offs[-1] <= t

    @pl.when(near)
    def _():
        run(True)

    @pl.when(jnp.logical_not(near))
    def _():
        run(False)

    @pl.when(j == (i * t + t - 1) // tk)
    def _():
        for h in range(N_DSA_HEADS):
            o_ref[:, _head_cols(h)] = _softmax_finish(acc_ref[h]).astype(o_ref.dtype)


def _dsa_attention(qh, kvh, scores, thr, jcut, bias, *, batch, seq, name):
    t, tk = ATTN_TILE, min(DSA_KEY_TILE, seq)
    nq, nk = seq // t, seq // tk
    h = N_DSA_HEADS
    qi, kj = _causal_pairs(seq, t, tk)
    qrow = lambda b, p, qi, kj: (b * nq + qi[p], 0)
    return pl.pallas_call(
        functools.partial(_dsa_attn_kernel, t=t, tk=tk),
        out_shape=jax.ShapeDtypeStruct((batch * seq, h * HEAD_DIM), BF16),
        grid_spec=pltpu.PrefetchScalarGridSpec(
            num_scalar_prefetch=2,
            grid=(batch, qi.shape[0]),
            in_specs=[pl.BlockSpec((h, t, HEAD_DIM), lambda b, p, qi, kj: (0, b * nq + qi[p], 0)),
                      pl.BlockSpec((h, tk, HEAD_DIM), lambda b, p, qi, kj: (0, b * nk + kj[p], 0)),
                      pl.BlockSpec((h, tk, HEAD_DIM), lambda b, p, qi, kj: (1, b * nk + kj[p], 0)),
                      pl.BlockSpec((None, None, t, tk), lambda b, p, qi, kj: (b, p, 0, 0)),
                      pl.BlockSpec((t, LANES), qrow),
                      pl.BlockSpec((t, LANES), qrow),
                      pl.BlockSpec((2, h, t, t), lambda b, p, qi, kj: (0, 0, 0, 0))],
            out_specs=pl.BlockSpec((t, h * HEAD_DIM), qrow),
            scratch_shapes=[pltpu.VMEM((h, t, HEAD_DIM), F32),
                            pltpu.VMEM((h, t, 2 * HEAD_DIM), F32)]),
        compiler_params=_cparams(("parallel", "arbitrary")),
        name=name,
    )(qi, kj, qh, kvh, kvh, scores, thr, jcut, bias)


def _route_kernel(lg_ref, info_ref, gate_ref, cnt_ref, carry_ref, *, tm):
    i = pl.program_id(0)

    @pl.when(i == 0)
    def _():
        carry_ref[...] = jnp.zeros_like(carry_ref)

    lane = lax.broadcasted_iota(I32, (tm, LANES), 1)
    lane_f = lane.astype(F32)
    lg = jnp.where(lane < N_EXPERTS, lg_ref[...], -jnp.inf)
    m1 = jnp.max(lg, axis=-1, keepdims=True)
    e1 = jnp.min(jnp.where(lg == m1, lane_f, float(LANES)), axis=-1, keepdims=True).astype(I32)
    lg2 = jnp.where(lane == e1, -jnp.inf, lg)
    m2 = jnp.max(lg2, axis=-1, keepdims=True)
    e2 = jnp.min(jnp.where(lg2 == m2, lane_f, float(LANES)), axis=-1, keepdims=True).astype(I32)
    ex = jnp.exp(m2 - m1)
    g1 = 1.0 / (1.0 + ex)
    g2 = ex / (1.0 + ex)
    onehot = jnp.where(jnp.logical_or(lane == e1, lane == e2), 1.0, 0.0)
    r = lax.broadcasted_iota(I32, (tm, tm), 0)
    c = lax.broadcasted_iota(I32, (tm, tm), 1)
    before = _dot(jnp.where(c < r, 1.0, 0.0).astype(BF16), onehot.astype(BF16)) + carry_ref[0:1, :]
    r1 = jnp.sum(jnp.where(lane == e1, before, 0.0), axis=-1, keepdims=True).astype(I32)
    r2 = jnp.sum(jnp.where(lane == e2, before, 0.0), axis=-1, keepdims=True).astype(I32)
    info = jnp.where(lane == 0, e1, jnp.where(lane == 1, e2, jnp.where(lane == 2, r1, r2)))
    info_ref[...] = info
    gate_ref[...] = jnp.where(lane == 0, g1, g2)
    total = carry_ref[0:1, :] + jnp.sum(onehot, axis=0, keepdims=True)
    carry_ref[...] = jnp.broadcast_to(total, carry_ref.shape)
    cnt_ref[...] = jnp.broadcast_to(total, cnt_ref.shape)


def _route(logits, *, tm=256, name):
    n = logits.shape[0]
    tm = min(tm, n)
    return pl.pallas_call(
        functools.partial(_route_kernel, tm=tm),
        out_shape=(jax.ShapeDtypeStruct((n, LANES), I32), jax.ShapeDtypeStruct((n, LANES), F32),
                   jax.ShapeDtypeStruct((8, LANES), F32)),
        grid=(n // tm,),
        in_specs=[pl.BlockSpec((tm, LANES), lambda i: (i, 0))],
        out_specs=(pl.BlockSpec((tm, LANES), lambda i: (i, 0)),
                   pl.BlockSpec((tm, LANES), lambda i: (i, 0)),
                   pl.BlockSpec((8, LANES), lambda i: (0, 0))),
        scratch_shapes=[pltpu.VMEM((8, LANES), F32)],
        compiler_params=_cparams(("arbitrary",)),
        name=name,
    )(logits)


def _combine_ln_kernel(d0_ref, d1_ref, h_ref, ys_ref, gate_ref, g_ref, b_ref, o_ref,
                       ya0_ref, ya1_ref, yb0_ref, yb1_ref, sems_a, sems_b, *, tm, slabs):
    i = pl.program_id(0)
    nt = pl.num_programs(0)
    bufs_a, bufs_b = (ya0_ref, ya1_ref), (yb0_ref, yb1_ref)

    def gather(tile, wait):
        for idx_ref, bufs, sems in ((d0_ref, bufs_a, sems_a), (d1_ref, bufs_b, sems_b)):
            _row_gather(ys_ref, idx_ref, tile * tm, bufs, sems, tile % 2, tm, slabs, wait=wait)

    @pl.when(i == 0)
    def _():
        gather(0, wait=False)

    gather(i, wait=True)

    @pl.when(i + 1 < nt)
    def _():
        gather(i + 1, wait=False)

    for s in range(2):
        @pl.when(i % 2 == s)
        def _(s=s):
            ff = (gate_ref[:, 0:1] * _slabs_to_rows(bufs_a[s], slabs)
                  + gate_ref[:, 1:2] * _slabs_to_rows(bufs_b[s], slabs))
            o_ref[...] = _layer_norm(DEEPNORM_ALPHA * h_ref[...] + ff, g_ref[...], b_ref[...])


def _combine_ln(dest0, dest1, h, ys, gates, g, b, *, tm=256, name):
    n, d = h.shape
    slabs = d // LANES
    tm = min(tm, n)
    row = lambda i, d0, d1: (i, 0)
    const = lambda i, d0, d1: (0, 0)
    slab_buf = pltpu.VMEM((tm * slabs, LANES), F32)
    return pl.pallas_call(
        functools.partial(_combine_ln_kernel, tm=tm, slabs=slabs),
        out_shape=jax.ShapeDtypeStruct((n, d), F32),
        grid_spec=pltpu.PrefetchScalarGridSpec(
            num_scalar_prefetch=2,
            grid=(n // tm,),
            in_specs=[pl.BlockSpec((tm, d), row),
                      pl.BlockSpec(memory_space=pl.ANY),
                      pl.BlockSpec((tm, LANES), row),
                      pl.BlockSpec((1, d), const),
                      pl.BlockSpec((1, d), const)],
            out_specs=pl.BlockSpec((tm, d), row),
            scratch_shapes=[slab_buf, slab_buf, slab_buf, slab_buf,
                            pltpu.SemaphoreType.DMA((2,)), pltpu.SemaphoreType.DMA((2,))]),
        compiler_params=_cparams(("arbitrary",)),
        name=name,
    )(dest0, dest1, h, ys, gates, g, b)


def _rel_bucket(dist):
    n = jnp.maximum(dist, 0)
    exact = REL_BUCKETS // 2
    nf = jnp.maximum(n, 1).astype(F32)
    large = exact + (jnp.log(nf / exact) / math.log(REL_MAX_DIST / exact) * (REL_BUCKETS - exact)).astype(I32)
    large = jnp.minimum(large, REL_BUCKETS - 1)
    return jnp.where(n < exact, n, large)


def _bias_tile_kernel(tab_ref, bucket_ref, o_ref):
    h = pl.program_id(1)
    bucket = bucket_ref[...]
    far = tab_ref[REL_BUCKETS - 1, h]
    acc = jnp.zeros(o_ref.shape, F32)
    for b in range(REL_BUCKETS - 1):
        acc = jnp.where(bucket == b, tab_ref[b, h] - far, acc)
    o_ref[...] = acc * LOG2E


def _bias_tiles(rel_table, n_heads, t, *, name):
    assert t >= REL_MAX_DIST
    r = jnp.arange(t)[:, None]
    c = jnp.arange(t)[None, :]
    buckets = jnp.stack([_rel_bucket(r - c + off) for off in (0, t)]).astype(I32)
    return pl.pallas_call(
        _bias_tile_kernel,
        out_shape=jax.ShapeDtypeStruct((2, n_heads, t, t), F32),
        grid=(2, n_heads),
        in_specs=[pl.BlockSpec(memory_space=pltpu.SMEM),
                  pl.BlockSpec((None, t, t), lambda k, h: (k, 0, 0))],
        out_specs=pl.BlockSpec((None, None, t, t), lambda k, h: (k, h, 0, 0)),
        compiler_params=_cparams(("parallel", "parallel")),
        name=name,
    )(rel_table.astype(F32), buckets)


def _pad_cols(w, n):
    return jnp.pad(w, ((0, 0), (0, n - w.shape[1])))


def _even_layer(h, rel_table, w_in, b_forget, w_out, ln1_g, ln1_b, w1, w3, w2, ln2_g, ln2_b,
                *, batch, seq):
    d = h.shape[1]
    wa = N_MOBA_HEADS * HEAD_DIM
    wb = N_FOX_HEADS * HEAD_DIM
    n_qkv = 3 * wa + 3 * wb
    scale = HEAD_DIM ** -0.5 * LOG2E
    ones, scl = jnp.ones((wa,), F32), jnp.full((wa,), scale, F32)
    colscale = jnp.concatenate([scl, ones, ones, scl, ones, ones])[None, :]
    qkv = _mm(h, w_in[:, :n_qkv].astype(BF16), colscale, out_dtype=BF16, tm=1024, tn=768,
              name="ev_qkv_proj")
    fb = _mm(h, _pad_cols(w_in[:, n_qkv:], LANES).astype(BF16), jnp.ones((1, LANES), F32),
             out_dtype=F32, tn=LANES, name="ev_forget_proj")
    log_f = jax.nn.log_sigmoid(fb[:, :N_FOX_HEADS] + b_forget.astype(F32))
    csum = jnp.cumsum(log_f.reshape(batch, seq, N_FOX_HEADS).transpose(0, 2, 1), axis=-1) * LOG2E
    csum = csum.reshape(batch, N_FOX_HEADS, seq // ATTN_TILE, 1, ATTN_TILE)
    oa = _moba_attention(qkv, _bias_tiles(rel_table, N_MOBA_HEADS, MOBA_BLOCK, name="ev_bias_tiles"),
                         batch=batch, seq=seq, name="ev_moba_attn")
    ob = _fox_attention(qkv, csum, batch=batch, seq=seq, name="ev_fox_attn")
    attn = jnp.concatenate([oa, ob], axis=-1)
    h = _mm_ln(attn, w_out.astype(BF16), h, ln1_g[None, :], ln1_b[None, :], name="ev_out_proj_ln")
    return _ffn_ln(h, w1.astype(BF16), w3.astype(BF16), w2.astype(BF16),
                   ln2_g[None, :], ln2_b[None, :], name="ev_swiglu_ln")


def _moe(h, router, w1, w3, w2, ln_g, ln_b):
    n, d = h.shape
    tm = EXPERT_TILE
    logits = _mm(h, _pad_cols(router, LANES).astype(BF16), jnp.ones((1, LANES), F32),
                 out_dtype=F32, tn=LANES, name="od_router")
    info, gates, cnt = _route(logits, name="od_route")
    counts = cnt[0, :N_EXPERTS].astype(I32)
    padded = (counts + tm - 1) // tm * tm
    pend = jnp.cumsum(padded)
    pstart = pend - padded
    e = info[:, :MOE_TOPK]
    dest = (pstart[e] + info[:, MOE_TOPK:2 * MOE_TOPK]).astype(I32)
    n_rows = -(-(n * MOE_TOPK + N_EXPERTS * (tm - 1)) // tm) * tm
    n_tiles = n_rows // tm
    tile_start = jnp.arange(n_tiles, dtype=I32) * tm
    tile_valid = (tile_start < pend[-1]).astype(I32)
    last = jnp.maximum(pend[-1] - 1, 0)
    tile_e = jnp.minimum(jnp.searchsorted(pend, jnp.minimum(tile_start, last), side="right"),
                         N_EXPERTS - 1).astype(I32)
    tok = jnp.arange(n, dtype=I32)
    row_tok = jnp.zeros((n_rows,), I32).at[dest.T.reshape(-1)].set(jnp.concatenate([tok, tok]))
    ys = _moe_ffn(tile_e, tile_valid, row_tok, h.reshape(n * d // LANES, LANES), w1.astype(BF16),
                  w3.astype(BF16), w2.astype(BF16), tm=tm, name="od_moe_swiglu")
    return _combine_ln(dest[:, 0], dest[:, 1], h, ys, gates, ln_g[None, :], ln_b[None, :],
                       name="od_moe_combine_ln")


def _odd_layer(h, rel_table, w_in, q_norm_g, kv_norm_g, w_uq, w_qidx, w_uk, w_uv, w_out,
               ln1_g, ln1_b, router, w1, w3, w2, ln2_g, ln2_b, *, batch, seq):
    nh = N_DSA_HEADS
    rq, rkv = DSA_Q_RANK, DSA_KV_RANK
    scale = HEAD_DIM ** -0.5 * LOG2E
    w_in_p = jnp.concatenate([w_in[:, :rq + rkv],
                              _pad_cols(w_in[:, rq + rkv:rq + rkv + IDX_DIM], LANES),
                              _pad_cols(w_in[:, rq + rkv + IDX_DIM:], LANES)], axis=1)
    cq, ckv, kidx, widx = _dsa_in_proj(h, w_in_p.astype(BF16), q_norm_g[None, :], kv_norm_g[None, :],
                                       name="od_in_proj_rms")
    w_qidx_p = jnp.pad(w_qidx.reshape(rq, IDX_HEADS, IDX_DIM), ((0, 0), (0, 0), (0, LANES - IDX_DIM)))
    wq = jnp.concatenate([w_uq, w_qidx_p.reshape(rq, IDX_HEADS * LANES)], axis=1)
    qscale = jnp.concatenate([jnp.full((nh * HEAD_DIM,), scale, F32),
                              jnp.ones((IDX_HEADS * LANES,), F32)])[None, :]
    qh = _mm(cq, wq.astype(BF16), qscale, out_dtype=BF16, head_major=True, tm=1024, name="od_q_proj")
    wkv = jnp.concatenate([w_uk.transpose(1, 0, 2).reshape(rkv, nh * HEAD_DIM),
                           w_uv.transpose(1, 0, 2).reshape(rkv, nh * HEAD_DIM)], axis=1)
    kvh = _mm(ckv, wkv.astype(BF16), jnp.ones((1, 2 * nh * HEAD_DIM), F32), out_dtype=BF16,
              head_major=True, tm=1024, name="od_kv_proj")
    scores, thr, jcut = _idx_topk(qh, kidx, widx, batch=batch, seq=seq,
                                  ksel=min(DSA_TOPK_MAX, seq // 4), name="od_idx_topk")
    attn = _dsa_attention(qh, kvh, scores, thr, jcut,
                          _bias_tiles(rel_table, nh, ATTN_TILE, name="od_bias_tiles"),
                          batch=batch, seq=seq, name="od_dsa_attn")
    h = _mm_ln(attn, w_out.astype(BF16), h, ln1_g[None, :], ln1_b[None, :], name="od_out_proj_ln")
    return _moe(h, router, w1, w3, w2, ln2_g, ln2_b)


def kernel(x, rel_table, ev_w_in, ev_b_forget, ev_w_out, ev_ln1_g, ev_ln1_b, ev_ffn_w1, ev_ffn_w3, ev_ffn_w2, ev_ln2_g, ev_ln2_b, od_w_in, od_q_norm_g, od_kv_norm_g, od_w_uq, od_w_qidx, od_w_uk, od_w_uv, od_w_out, od_ln1_g, od_ln1_b, od_router, od_exp_w1, od_exp_w3, od_exp_w2, od_ln2_g, od_ln2_b):
    batch, seq, d = x.shape
    h = x.reshape(batch * seq, d)
    for layer in range(DEPTH):
        i = layer // 2
        if layer % 2 == 0:
            h = _even_layer(h, rel_table, ev_w_in[i], ev_b_forget[i], ev_w_out[i], ev_ln1_g[i],
                            ev_ln1_b[i], ev_ffn_w1[i], ev_ffn_w3[i], ev_ffn_w2[i], ev_ln2_g[i],
                            ev_ln2_b[i], batch=batch, seq=seq)
        else:
            h = _odd_layer(h, rel_table, od_w_in[i], od_q_norm_g[i], od_kv_norm_g[i], od_w_uq[i],
                           od_w_qidx[i], od_w_uk[i], od_w_uv[i], od_w_out[i], od_ln1_g[i],
                           od_ln1_b[i], od_router[i], od_exp_w1[i], od_exp_w3[i], od_exp_w2[i],
                           od_ln2_g[i], od_ln2_b[i], batch=batch, seq=seq)
    return h.reshape(batch, seq, d)
```

```python
import functools
import math

import jax
import jax.numpy as jnp
from jax import lax
from jax.experimental import pallas as pl
from jax.experimental.pallas import tpu as pltpu

F32 = jnp.float32
BF16 = jnp.bfloat16
I32 = jnp.int32

HEAD_DIM = 128
N_MOBA_HEADS = 8
N_FOX_HEADS = 8
MOBA_BLOCK = 256
MOBA_TOPK = 3
N_DSA_HEADS = 16
DSA_Q_RANK = 512
DSA_KV_RANK = 512
IDX_HEADS = 16
IDX_DIM = 64
DSA_TOPK_MAX = 256
REL_BUCKETS = 32
REL_MAX_DIST = 128
N_EXPERTS = 8
MOE_TOPK = 2
LN_EPS = 1e-5
RMS_EPS = 1e-6
DEPTH = 2
DEEPNORM_ALPHA = (2 * DEPTH) ** 0.25

LANES = 128
ATTN_TILE = 256
WIDE_CHUNK = 1024
DSA_KEY_TILE = 512
DSA_HEAD_GROUP = 4
COUNT_STRIP = 128
EXPERT_TILE = 512
NEG = -1e30
LOG2E = math.log2(math.e)
INT_MIN = -(2 ** 31)
VMEM_LIMIT = 56 * 1024 * 1024


def _cparams(sem, vmem=VMEM_LIMIT, flags=None):
    return pltpu.CompilerParams(dimension_semantics=sem, vmem_limit_bytes=vmem, flags=flags)


INTERLEAVE_CHAINS = None


def _dot(a, b):
    return jnp.dot(a, b, preferred_element_type=F32)


def _dot_t(a, b):
    return lax.dot_general(a, b, (((1,), (1,)), ((), ())), preferred_element_type=F32)


def _layer_norm(y, g, b):
    mu = jnp.mean(y, axis=-1, keepdims=True)
    d = y - mu
    var = jnp.mean(d * d, axis=-1, keepdims=True)
    return d * lax.rsqrt(var + LN_EPS) * g + b


def _mm_kernel(x_ref, w_ref, cs_ref, o_ref, *, head_major):
    acc = _dot(x_ref[...].astype(BF16), w_ref[...]) * cs_ref[...]
    if head_major:
        for c in range(o_ref.shape[0]):
            o_ref[c] = acc[:, c * LANES:(c + 1) * LANES].astype(o_ref.dtype)
    else:
        o_ref[...] = acc.astype(o_ref.dtype)


def _mm(x, w, colscale, *, out_dtype, head_major=False, tm=512, tn=512, name):
    m, k = x.shape
    n = w.shape[1]
    tm, tn = min(tm, m), min(tn, n)
    assert m % tm == 0 and n % tn == 0 and tn % LANES == 0
    if head_major:
        out_shape = jax.ShapeDtypeStruct((n // LANES, m, LANES), out_dtype)
        out_spec = pl.BlockSpec((tn // LANES, tm, LANES), lambda i, j: (j, i, 0))
    else:
        out_shape = jax.ShapeDtypeStruct((m, n), out_dtype)
        out_spec = pl.BlockSpec((tm, tn), lambda i, j: (i, j))
    return pl.pallas_call(
        functools.partial(_mm_kernel, head_major=head_major),
        out_shape=out_shape,
        grid=(m // tm, n // tn),
        in_specs=[pl.BlockSpec((tm, k), lambda i, j: (i, 0)),
                  pl.BlockSpec((k, tn), lambda i, j: (0, j)),
                  pl.BlockSpec((1, tn), lambda i, j: (0, j))],
        out_specs=out_spec,
        compiler_params=_cparams(("parallel", "arbitrary")),
        name=name,
    )(x, w, colscale)


def _mm_ln_kernel(x_ref, w_ref, res_ref, g_ref, b_ref, o_ref):
    y = DEEPNORM_ALPHA * res_ref[...] + _dot(x_ref[...], w_ref[...])
    o_ref[...] = _layer_norm(y, g_ref[...], b_ref[...])


def _mm_ln(x, w, res, g, b, *, tm=256, name):
    m, k = x.shape
    d = w.shape[1]
    tm = min(tm, m)
    return pl.pallas_call(
        _mm_ln_kernel,
        out_shape=jax.ShapeDtypeStruct((m, d), F32),
        grid=(m // tm,),
        in_specs=[pl.BlockSpec((tm, k), lambda i: (i, 0)),
                  pl.BlockSpec((k, d), lambda i: (0, 0)),
                  pl.BlockSpec((tm, d), lambda i: (i, 0)),
                  pl.BlockSpec((1, d), lambda i: (0, 0)),
                  pl.BlockSpec((1, d), lambda i: (0, 0))],
        out_specs=pl.BlockSpec((tm, d), lambda i: (i, 0)),
        compiler_params=_cparams(("parallel",)),
        name=name,
    )(x, w, res, g, b)


def _mm_ln_route_kernel(x_ref, w_ref, res_ref, g_ref, b_ref, wr_ref, o_ref, slab_ref, lg_ref,
                        *, slabs):
    y = _layer_norm(DEEPNORM_ALPHA * res_ref[...] + _dot(x_ref[...], w_ref[...]),
                    g_ref[...], b_ref[...])
    o_ref[...] = y
    tm = y.shape[0]
    for c in range(slabs):
        slab_ref[pl.ds(c, tm, stride=slabs), :] = y[:, c * LANES:(c + 1) * LANES]
    lg_ref[...] = _dot(y.astype(BF16), wr_ref[...])


def _mm_ln_route(x, w, res, g, b, w_router, *, tm=256, name):
    m, k = x.shape
    d = w.shape[1]
    tm = min(tm, m)
    slabs = d // LANES
    row = lambda i: (i, 0)
    const = lambda i: (0, 0)
    return pl.pallas_call(
        functools.partial(_mm_ln_route_kernel, slabs=slabs),
        out_shape=(jax.ShapeDtypeStruct((m, d), F32), jax.ShapeDtypeStruct((m * slabs, LANES), F32),
                   jax.ShapeDtypeStruct((m, LANES), F32)),
        grid=(m // tm,),
        in_specs=[pl.BlockSpec((tm, k), row), pl.BlockSpec((k, d), const), pl.BlockSpec((tm, d), row),
                  pl.BlockSpec((1, d), const), pl.BlockSpec((1, d), const),
                  pl.BlockSpec((d, LANES), const)],
        out_specs=(pl.BlockSpec((tm, d), row), pl.BlockSpec((tm * slabs, LANES), row),
                   pl.BlockSpec((tm, LANES), row)),
        compiler_params=_cparams(("parallel",)),
        name=name,
    )(x, w, res, g, b, w_router)


def _swiglu_step(xb, w1_ref, w3_ref, w2_ref):
    a = _dot(xb, w1_ref[...])
    c = _dot(xb, w3_ref[...])
    hmid = a / (1.0 + jnp.exp(-a)) * c
    return _dot(hmid.astype(BF16), w2_ref[...])


def _ffn_ln_kernel(x_ref, w1_ref, w3_ref, w2_ref, g_ref, b_ref, o_ref, acc_ref, xb_ref):
    f = pl.program_id(1)

    @pl.when(f == 0)
    def _():
        acc_ref[...] = jnp.zeros_like(acc_ref)
        xb_ref[...] = x_ref[...].astype(BF16)

    acc_ref[...] += _swiglu_step(xb_ref[...], w1_ref, w3_ref, w2_ref)

    @pl.when(f == pl.num_programs(1) - 1)
    def _():
        y = DEEPNORM_ALPHA * x_ref[...] + acc_ref[...]
        o_ref[...] = _layer_norm(y, g_ref[...], b_ref[...])


def _ffn_ln(x, w1, w3, w2, g, b, *, tm=512, tf=512, name):
    m, d = x.shape
    dff = w1.shape[1]
    tm, tf = min(tm, m), min(tf, dff)
    assert m % tm == 0 and dff % tf == 0
    return pl.pallas_call(
        _ffn_ln_kernel,
        out_shape=jax.ShapeDtypeStruct((m, d), F32),
        grid=(m // tm, dff // tf),
        in_specs=[pl.BlockSpec((tm, d), lambda i, f: (i, 0)),
                  pl.BlockSpec((d, tf), lambda i, f: (0, f)),
                  pl.BlockSpec((d, tf), lambda i, f: (0, f)),
                  pl.BlockSpec((tf, d), lambda i, f: (f, 0)),
                  pl.BlockSpec((1, d), lambda i, f: (0, 0)),
                  pl.BlockSpec((1, d), lambda i, f: (0, 0))],
        out_specs=pl.BlockSpec((tm, d), lambda i, f: (i, 0)),
        scratch_shapes=[pltpu.VMEM((tm, d), F32), pltpu.VMEM((tm, d), BF16)],
        compiler_params=_cparams(("parallel", "arbitrary")),
        name=name,
    )(x, w1, w3, w2, g, b)


def _row_gather(src_hbm, idx_ref, base, bufs, sems, slot, n_rows, slabs, *, wait):
    for s, buf in enumerate(bufs):
        @pl.when(slot == s)
        def _(s=s, buf=buf):
            def body(r, _):
                src = src_hbm.at[pl.ds(pl.multiple_of(idx_ref[base + r] * slabs, slabs), slabs)]
                dst = buf.at[pl.ds(pl.multiple_of(r * slabs, slabs), slabs)]
                cp = pltpu.make_async_copy(src, dst, sems.at[s])
                if wait:
                    cp.wait()
                else:
                    cp.start()
                return 0
            lax.fori_loop(0, n_rows, body, 0, unroll=8)


def _slabs_to_rows(buf, slabs):
    rows = buf.shape[0] // slabs
    return jnp.concatenate([buf[pl.ds(c, rows, stride=slabs), :] for c in range(slabs)], axis=1)


def _moe_ffn_kernel(te_ref, tv_ref, tok_ref, h_ref, w1_ref, w3_ref, w2_ref, o_ref,
                    acc_ref, xb_ref, xg0_ref, xg1_ref, sems, *, tm, slabs):
    i, f = pl.program_id(0), pl.program_id(1)
    nt = pl.num_programs(0)
    bufs = (xg0_ref, xg1_ref)
    gather = functools.partial(_row_gather, h_ref, tok_ref, bufs=bufs, sems=sems, n_rows=tm,
                               slabs=slabs)

    @pl.when(tv_ref[i] > 0)
    def _():
        @pl.when(f == 0)
        def _():
            @pl.when(i == 0)
            def _():
                gather(base=0, slot=0, wait=False)

            gather(base=i * tm, slot=i % 2, wait=True)
            nxt = jnp.minimum(i + 1, nt - 1)

            @pl.when(jnp.logical_and(i + 1 < nt, tv_ref[nxt] > 0))
            def _():
                gather(base=nxt * tm, slot=nxt % 2, wait=False)

            for s, buf in enumerate(bufs):
                @pl.when(i % 2 == s)
                def _(buf=buf):
                    xb_ref[...] = _slabs_to_rows(buf, slabs).astype(BF16)

            acc_ref[...] = jnp.zeros_like(acc_ref)

        acc_ref[...] += _swiglu_step(xb_ref[...], w1_ref, w3_ref, w2_ref)

        @pl.when(f == pl.num_programs(1) - 1)
        def _():
            for c in range(slabs):
                o_ref[pl.ds(c, tm, stride=slabs), :] = acc_ref[:, c * LANES:(c + 1) * LANES]

    @pl.when(tv_ref[i] == 0)
    def _():
        o_ref[...] = jnp.zeros_like(o_ref)


def _moe_ffn(tile_e, tile_valid, row_tok, h_slabs, w1, w3, w2, *, tm, tf=512, name):
    n_rows = row_tok.shape[0]
    d = w1.shape[1]
    slabs = d // LANES
    dff = w1.shape[2]
    tf = min(tf, dff)
    nf = dff // tf
    assert n_rows % tm == 0 and dff % tf == 0

    def fidx(i, f, tv):
        return jnp.where(tv[i] > 0, f, nf - 1)

    return pl.pallas_call(
        functools.partial(_moe_ffn_kernel, tm=tm, slabs=slabs),
        out_shape=jax.ShapeDtypeStruct((n_rows * slabs, LANES), F32),
        grid_spec=pltpu.PrefetchScalarGridSpec(
            num_scalar_prefetch=3,
            grid=(n_rows // tm, nf),
            in_specs=[pl.BlockSpec(memory_space=pl.ANY),
                      pl.BlockSpec((None, d, tf), lambda i, f, te, tv, tok: (te[i], 0, fidx(i, f, tv))),
                      pl.BlockSpec((None, d, tf), lambda i, f, te, tv, tok: (te[i], 0, fidx(i, f, tv))),
                      pl.BlockSpec((None, tf, d), lambda i, f, te, tv, tok: (te[i], fidx(i, f, tv), 0))],
            out_specs=pl.BlockSpec((tm * slabs, LANES), lambda i, f, te, tv, tok: (i, 0)),
            scratch_shapes=[pltpu.VMEM((tm, d), F32), pltpu.VMEM((tm, d), BF16),
                            pltpu.VMEM((tm * slabs, LANES), F32), pltpu.VMEM((tm * slabs, LANES), F32),
                            pltpu.SemaphoreType.DMA((2,))]),
        compiler_params=_cparams(("arbitrary", "arbitrary")),
        name=name,
    )(tile_e, tile_valid, row_tok, h_slabs, w1, w3, w2)


def _online_softmax_step(s, v, carry):
    m, acc = carry
    m_new = jnp.maximum(m, jnp.max(s, axis=-1, keepdims=True))
    alpha = jnp.exp2(m - m_new)
    p = jnp.exp2((s - m_new).astype(BF16))
    return m_new, alpha * acc + _dot(p, _with_ones(v))


def _with_ones(v):
    return jnp.concatenate([v, jnp.ones_like(v)], axis=1)


def _softmax_init(tq):
    return jnp.full((tq, 1), NEG, F32), jnp.zeros((tq, 2 * HEAD_DIM), F32)


def _softmax_finish(acc):
    return acc[:, :HEAD_DIM] / acc[:, HEAD_DIM:]


def _pairwise_loop(n, body, carry):
    carry = lax.fori_loop(0, n // 2, lambda k, c: body(2 * k + 1, body(2 * k, c)), carry)
    return lax.fori_loop(n // 2 * 2, n, body, carry)


def _head_cols(h):
    return slice(h * HEAD_DIM, (h + 1) * HEAD_DIM)


def _kv_block(k_ref, v_ref, n, width, h):
    rows = pl.ds(pl.multiple_of(n * width, width), width)
    return k_ref[rows, _head_cols(h)], v_ref[rows, _head_cols(h)]


def _causal_mask(blk):
    r = lax.broadcasted_iota(I32, (blk, blk), 0)
    c = lax.broadcasted_iota(I32, (blk, blk), 1)
    return c <= r


def _moba_select(q, km, i, blk):
    nbp = km.shape[0]
    gate = _dot_t(q, km.astype(BF16))
    lane = lax.broadcasted_iota(I32, (blk, nbp), 1)
    lane_f = lane.astype(F32)
    g = jnp.where(lane < i, gate, -jnp.inf)
    sel = jnp.zeros((blk, nbp), F32)
    for _ in range(MOBA_TOPK):
        mx = jnp.max(g, axis=-1, keepdims=True)
        first = jnp.min(jnp.where(g == mx, lane_f, float(nbp)), axis=-1, keepdims=True)
        pick = jnp.logical_and(lane_f == first, mx > -jnp.inf)
        sel = jnp.where(pick, 1.0, sel)
        g = jnp.where(pick, -jnp.inf, g)
    return sel, lane


def _moba_kernel(q_ref, k_ref, v_ref, bias_ref, o_ref, km_ref, *, blk, nb, wide, hp):
    i = pl.program_id(2)
    per = wide // blk

    @pl.when(i == 0)
    def _():
        km_ref[...] = jnp.zeros_like(km_ref)

        def mean_body(n, _):
            for h in range(hp):
                kblk, _ = _kv_block(k_ref, v_ref, n, blk, h)
                km_ref[h, pl.ds(n, 1), :] = jnp.mean(kblk.astype(F32), axis=0, keepdims=True)
            return 0

        lax.fori_loop(0, nb, mean_body, 0)

    qs = [q_ref[:, _head_cols(h)] for h in range(hp)]
    sels = [_moba_select(qs[h], km_ref[h], i, blk) for h in range(hp)]

    def chunk_step(n, carries, near):
        kv = [_kv_block(k_ref, v_ref, n, wide, h) for h in range(hp)]
        logits = [_dot_t(qs[h], kv[h][0]) for h in range(hp)]
        causal = _causal_mask(blk)
        out = []
        for h in range(hp):
            s, vblk = logits[h], kv[h][1]
            sel, lane = sels[h]
            parts = []
            for u in range(per):
                blk_id = n * per + u
                part = s[:, u * blk:(u + 1) * blk]
                chosen = jnp.sum(jnp.where(lane == blk_id, sel, 0.0), axis=-1, keepdims=True) > 0.0
                if near:
                    part = part + jnp.where(blk_id == i, bias_ref[0, h],
                                            jnp.where(blk_id == i - 1, bias_ref[1, h], 0.0))
                    parts.append(jnp.where(blk_id == i, jnp.where(causal, part, NEG),
                                           jnp.where(chosen, part, NEG)))
                else:
                    parts.append(jnp.where(chosen, part, NEG))
            s = parts[0] if per == 1 else jnp.concatenate(parts, axis=1)
            out.append(_online_softmax_step(s, vblk, carries[h]))
        return tuple(out)

    first_near = jnp.maximum(i - 1, 0) // per
    carries = tuple(_softmax_init(blk) for _ in range(hp))
    carries = _pairwise_loop(first_near, lambda n, c: chunk_step(n, c, False), carries)
    carries = lax.fori_loop(first_near, i // per + 1, lambda n, c: chunk_step(n, c, True), carries)
    for h in range(hp):
        o_ref[:, _head_cols(h)] = _softmax_finish(carries[h][1]).astype(o_ref.dtype)


def _moba_attention(qkv, bias, *, batch, seq, hp=2, name):
    blk = MOBA_BLOCK
    nb = seq // blk
    nbp = -(-nb // LANES) * LANES
    h = N_MOBA_HEADS
    nq = seq // blk
    hg = h // hp
    w = hp * HEAD_DIM
    return pl.pallas_call(
        functools.partial(_moba_kernel, blk=blk, nb=nb, wide=min(WIDE_CHUNK, seq), hp=hp),
        out_shape=jax.ShapeDtypeStruct((batch * seq, h * HEAD_DIM), BF16),
        grid=(batch, hg, nq),
        in_specs=[pl.BlockSpec((blk, w), lambda b, g, i: (b * nq + i, g)),
                  pl.BlockSpec((seq, w), lambda b, g, i: (b, hg + g)),
                  pl.BlockSpec((seq, w), lambda b, g, i: (b, 2 * hg + g)),
                  pl.BlockSpec((2, hp, blk, blk), lambda b, g, i: (0, g, 0, 0))],
        out_specs=pl.BlockSpec((blk, w), lambda b, g, i: (b * nq + i, g)),
        scratch_shapes=[pltpu.VMEM((hp, nbp, HEAD_DIM), F32)],
        compiler_params=_cparams(("parallel", "parallel", "arbitrary"), flags=INTERLEAVE_CHAINS),
        name=name,
    )(qkv, qkv, qkv, bias)


def _fox_kernel(q_ref, k_ref, v_ref, ck_ref, o_ref, *, blk, wide, hp):
    i = pl.program_id(2)
    per = wide // blk
    qs = [q_ref[:, _head_cols(h)] for h in range(hp)]

    def step(n, carries, causal):
        kv = [_kv_block(k_ref, v_ref, n, wide, h) for h in range(hp)]
        logits = [_dot_t(qs[h], kv[h][0]) for h in range(hp)]
        if causal:
            row = i * blk + lax.broadcasted_iota(I32, (blk, wide), 0)
            col = n * wide + lax.broadcasted_iota(I32, (blk, wide), 1)
            visible = col <= row
        out = []
        for h in range(hp):
            ck = [ck_ref[h, n * per + u] for u in range(per)]
            s = logits[h] - (ck[0] if per == 1 else jnp.concatenate(ck, axis=1))
            if causal:
                s = jnp.where(visible, s, NEG)
            out.append(_online_softmax_step(s, kv[h][1], carries[h]))
        return tuple(out)

    n_wide = i // per
    carries = tuple(_softmax_init(blk) for _ in range(hp))
    carries = _pairwise_loop(n_wide, lambda n, c: step(n, c, False), carries)
    carries = step(n_wide, carries, True)
    for h in range(hp):
        o_ref[:, _head_cols(h)] = _softmax_finish(carries[h][1]).astype(o_ref.dtype)


def _fox_attention(qkv, csum, *, batch, seq, hp=2, name):
    blk = ATTN_TILE
    h = N_FOX_HEADS
    nq = seq // blk
    hg = h // hp
    base = 3 * N_MOBA_HEADS // hp
    w = hp * HEAD_DIM
    return pl.pallas_call(
        functools.partial(_fox_kernel, blk=blk, wide=min(WIDE_CHUNK, seq), hp=hp),
        out_shape=jax.ShapeDtypeStruct((batch * seq, h * HEAD_DIM), BF16),
        grid=(batch, hg, nq),
        in_specs=[pl.BlockSpec((blk, w), lambda b, g, i: (b * nq + i, base + g)),
                  pl.BlockSpec((seq, w), lambda b, g, i: (b, base + hg + g)),
                  pl.BlockSpec((seq, w), lambda b, g, i: (b, base + 2 * hg + g)),
                  pl.BlockSpec((None, hp, nq, 1, blk), lambda b, g, i: (b, g, 0, 0, 0))],
        out_specs=pl.BlockSpec((blk, w), lambda b, g, i: (b * nq + i, g)),
        compiler_params=_cparams(("parallel", "parallel", "arbitrary"), flags=INTERLEAVE_CHAINS),
        name=name,
    )(qkv, qkv, qkv, csum)


def _dsa_in_kernel(x_ref, w_ref, gq_ref, gkv_ref, cq_ref, ckv_ref, kidx_ref, widx_ref):
    acc = _dot(x_ref[...].astype(BF16), w_ref[...])
    rq, rkv = DSA_Q_RANK, DSA_KV_RANK

    def rms(z, g):
        return z * lax.rsqrt(jnp.mean(z * z, axis=-1, keepdims=True) + RMS_EPS) * g

    cq_ref[...] = rms(acc[:, :rq], gq_ref[...]).astype(cq_ref.dtype)
    ckv_ref[...] = rms(acc[:, rq:rq + rkv], gkv_ref[...]).astype(ckv_ref.dtype)
    kidx_ref[...] = acc[:, rq + rkv:rq + rkv + LANES].astype(kidx_ref.dtype)
    widx_ref[...] = acc[:, rq + rkv + LANES:] * (IDX_HEADS ** -0.5 * IDX_DIM ** -0.5)


def _dsa_in_proj(x, w, gq, gkv, *, tm=512, name):
    m, k = x.shape
    n = w.shape[1]
    tm = min(tm, m)
    rq, rkv = DSA_Q_RANK, DSA_KV_RANK
    row = lambda i: (i, 0)
    const = lambda i: (0, 0)
    return pl.pallas_call(
        _dsa_in_kernel,
        out_shape=(jax.ShapeDtypeStruct((m, rq), BF16), jax.ShapeDtypeStruct((m, rkv), BF16),
                   jax.ShapeDtypeStruct((m, LANES), BF16), jax.ShapeDtypeStruct((m, LANES), F32)),
        grid=(m // tm,),
        in_specs=[pl.BlockSpec((tm, k), row), pl.BlockSpec((k, n), const),
                  pl.BlockSpec((1, rq), const), pl.BlockSpec((1, rkv), const)],
        out_specs=(pl.BlockSpec((tm, rq), row), pl.BlockSpec((tm, rkv), row),
                   pl.BlockSpec((tm, LANES), row), pl.BlockSpec((tm, LANES), row)),
        compiler_params=_cparams(("parallel",)),
        name=name,
    )(x, w, gq, gkv)


def _fold_lanes(x):
    part = x[:, :LANES]
    for g in range(1, x.shape[1] // LANES):
        part = part + x[:, g * LANES:(g + 1) * LANES]
    return part


def _idx_topk_kernel(qi_ref, kj_ref, q_ref, k_ref, w_ref, sc_ref, thr_ref, jc_ref, keys_ref,
                     *, t, tk, ksel, seq):
    p = pl.program_id(1)
    i, j = qi_ref[p], kj_ref[p]
    k = k_ref[...]
    acc = jnp.zeros((t, tk), F32)
    for h in range(IDX_HEADS):
        acc = acc + jnp.maximum(_dot_t(q_ref[h], k), 0.0) * w_ref[:, h:h + 1]
    sc_ref[...] = acc

    row = i * t + lax.broadcasted_iota(I32, (t, tk), 0)
    lane_col = lax.broadcasted_iota(I32, (t, tk), 1)
    bits = pltpu.bitcast(acc, I32)
    key = jnp.where(bits < 0, bits ^ jnp.int32(0x7FFFFFFF), bits)
    key = jnp.where(bits == jnp.int32(INT_MIN), 0, key)
    keys_ref[j] = jnp.where(j * tk + lane_col <= row, key, jnp.int32(INT_MIN))
    j_last = (i * t + t - 1) // tk

    @pl.when(j == j_last)
    def _():
        lane = lax.broadcasted_iota(I32, (COUNT_STRIP, LANES), 1)

        def count(pred, *row_args):
            strips = [slice(s0, s0 + COUNT_STRIP) for s0 in range(0, t, COUNT_STRIP)]
            reps = [[jnp.broadcast_to(a[rows], (COUNT_STRIP, LANES)) for a in row_args]
                    for rows in strips]
            cnts = []
            for rows, args in zip(strips, reps):
                def body(c, cnt, rows=rows, args=args):
                    for g in range(tk // LANES):
                        keys = keys_ref[c, rows, g * LANES:(g + 1) * LANES]
                        cnt = cnt + pred(keys, c * tk + g * LANES, *args).astype(I32)
                    return cnt

                cnts.append(_pairwise_loop(j_last + 1, body, jnp.zeros((COUNT_STRIP, LANES), I32)))
            cnt = jnp.concatenate(cnts, axis=0)
            return jnp.sum(cnt.astype(F32), axis=-1, keepdims=True).astype(I32)

        ans = jnp.where(count(lambda k, c0: k >= 0) >= ksel, 0, jnp.int32(INT_MIN))

        def bit_body(b, ans):
            cand = ans + lax.shift_left(jnp.int32(1), 30 - b)
            return jnp.where(count(lambda k, c0, cd: k >= cd, cand) >= ksel, cand, ans)

        ans = lax.fori_loop(0, 31, bit_body, ans)
        need = ksel - count(lambda k, c0, a: k > a, ans)
        ties = count(lambda k, c0, a: k == a, ans)
        row1 = i * t + lax.broadcasted_iota(I32, (t, 1), 0)
        full = row1 >= ksel

        def tie_cut(_):
            nbits = seq.bit_length() - 1

            def bit_body(b, cut):
                cand = cut + lax.shift_left(jnp.int32(1), nbits - 1 - b)
                below = count(lambda k, c0, a, cd: jnp.logical_and(k == a, c0 + lane < cd), ans, cand)
                return jnp.where(below < need, cand, cut)
            return lax.fori_loop(0, nbits, bit_body, jnp.zeros((t, 1), I32))

        contested = jnp.max(jnp.where(jnp.logical_and(full, ties > need), 1.0, 0.0)) > 0.0
        cut = lax.cond(contested, tie_cut, lambda _: jnp.full((t, 1), seq - 1, I32), 0)
        tbits = jnp.where(ans < 0, ans ^ jnp.int32(0x7FFFFFFF), ans)
        thr = jnp.where(full, pltpu.bitcast(tbits, F32), -jnp.inf)
        thr_ref[...] = jnp.broadcast_to(thr, thr_ref.shape)
        jc_ref[...] = jnp.broadcast_to(jnp.where(full, cut, seq - 1), jc_ref.shape)


def _causal_pairs(seq, t, tk):
    pairs = [(i, j) for i in range(seq // t) for j in range((i * t + t - 1) // tk + 1)]
    return jnp.asarray([p[0] for p in pairs], I32), jnp.asarray([p[1] for p in pairs], I32)


def _idx_topk(qh, kidx, widx, *, batch, seq, ksel, name):
    t, tk = ATTN_TILE, min(DSA_KEY_TILE, seq)
    nq, nk = seq // t, seq // tk
    assert seq & (seq - 1) == 0 and tk >= ksel
    qi, kj = _causal_pairs(seq, t, tk)
    npairs = qi.shape[0]
    qrow = lambda b, p, qi, kj: (b * nq + qi[p], 0)
    return pl.pallas_call(
        functools.partial(_idx_topk_kernel, t=t, tk=tk, ksel=ksel, seq=seq),
        out_shape=(jax.ShapeDtypeStruct((batch, npairs, t, tk), F32),
                   jax.ShapeDtypeStruct((batch * seq, LANES), F32),
                   jax.ShapeDtypeStruct((batch * seq, LANES), I32)),
        grid_spec=pltpu.PrefetchScalarGridSpec(
            num_scalar_prefetch=2,
            grid=(batch, npairs),
            in_specs=[pl.BlockSpec((IDX_HEADS, t, LANES), lambda b, p, qi, kj: (1, b * nq + qi[p], 0)),
                      pl.BlockSpec((tk, LANES), lambda b, p, qi, kj: (b * nk + kj[p], 0)),
                      pl.BlockSpec((t, LANES), qrow)],
            out_specs=(pl.BlockSpec((None, None, t, tk), lambda b, p, qi, kj: (b, p, 0, 0)),
                       pl.BlockSpec((t, LANES), qrow), pl.BlockSpec((t, LANES), qrow)),
            scratch_shapes=[pltpu.VMEM((nk, t, tk), I32)]),
        compiler_params=_cparams(("parallel", "arbitrary")),
        name=name,
    )(qi, kj, qh, kidx, widx)


def _dsa_attn_kernel(qi_ref, kj_ref, q_ref, k_ref, v_ref, sc_ref, thr_ref, jc_ref, bias_ref,
                     o_ref, m_ref, acc_ref, *, t, tk):
    p = pl.program_id(1)
    i, j = qi_ref[p], kj_ref[p]

    @pl.when(j == 0)
    def _():
        m_ref[...] = jnp.full(m_ref.shape, NEG, F32)
        acc_ref[...] = jnp.zeros_like(acc_ref)

    sc = sc_ref[...]
    thr = thr_ref[:, 0:1]
    col = j * tk + lax.broadcasted_iota(I32, (t, tk), 1)
    row = i * t + lax.broadcasted_iota(I32, (t, tk), 0)
    tie = jnp.logical_and(sc == thr, col <= jc_ref[:, 0:1])
    keep = jnp.logical_and(jnp.logical_or(sc > thr, tie), col <= row)
    offs = [i * t - (j * tk + u * t) for u in range(tk // t)]

    def run(with_bias):
        for g in range(N_DSA_HEADS // DSA_HEAD_GROUP):
            heads = [g * DSA_HEAD_GROUP + u for u in range(DSA_HEAD_GROUP)]
            logits = [_dot_t(q_ref[h], k_ref[h]) for h in heads]
            for h, s in zip(heads, logits):
                if with_bias:
                    parts = [jnp.where(off == 0, bias_ref[0, h],
                                       jnp.where(off == t, bias_ref[1, h], 0.0)) for off in offs]
                    s = s + (parts[0] if len(parts) == 1 else jnp.concatenate(parts, axis=1))
                s = jnp.where(keep, s, NEG)
                m_prev = m_ref[h]
                m_next = jnp.maximum(m_prev, jnp.max(s, axis=-1, keepdims=True))
                alpha = jnp.exp2(m_prev - m_next)
                p = jnp.exp2((s - jnp.concatenate([m_next] * (tk // HEAD_DIM), axis=1)).astype(BF16))
                acc_ref[h] = (jnp.concatenate([alpha, alpha], axis=1) * acc_ref[h]
                              + _dot(p, _with_ones(v_ref[h])))
                m_ref[h] = m_next

    near = offs[-1] <= t

    @pl.when(near)
    def _():
        run(True)

    @pl.when(jnp.logical_not(near))
    def _():
        run(False)

    @pl.when(j == (i * t + t - 1) // tk)
    def _():
        for h in range(N_DSA_HEADS):
            o_ref[:, _head_cols(h)] = _softmax_finish(acc_ref[h]).astype(o_ref.dtype)


def _dsa_attention(qh, kvh, scores, thr, jcut, bias, *, batch, seq, name):
    t, tk = ATTN_TILE, min(DSA_KEY_TILE, seq)
    nq, nk = seq // t, seq // tk
    h = N_DSA_HEADS
    qi, kj = _causal_pairs(seq, t, tk)
    qrow = lambda b, p, qi, kj: (b * nq + qi[p], 0)
    return pl.pallas_call(
        functools.partial(_dsa_attn_kernel, t=t, tk=tk),
        out_shape=jax.ShapeDtypeStruct((batch * seq, h * HEAD_DIM), BF16),
        grid_spec=pltpu.PrefetchScalarGridSpec(
            num_scalar_prefetch=2,
            grid=(batch, qi.shape[0]),
            in_specs=[pl.BlockSpec((h, t, HEAD_DIM), lambda b, p, qi, kj: (0, b * nq + qi[p], 0)),
                      pl.BlockSpec((h, tk, HEAD_DIM), lambda b, p, qi, kj: (0, b * nk + kj[p], 0)),
                      pl.BlockSpec((h, tk, HEAD_DIM), lambda b, p, qi, kj: (1, b * nk + kj[p], 0)),
                      pl.BlockSpec((None, None, t, tk), lambda b, p, qi, kj: (b, p, 0, 0)),
                      pl.BlockSpec((t, LANES), qrow),
                      pl.BlockSpec((t, LANES), qrow),
                      pl.BlockSpec((2, h, t, t), lambda b, p, qi, kj: (0, 0, 0, 0))],
            out_specs=pl.BlockSpec((t, h * HEAD_DIM), qrow),
            scratch_shapes=[pltpu.VMEM((h, t, HEAD_DIM), F32),
                            pltpu.VMEM((h, t, 2 * HEAD_DIM), F32)]),
        compiler_params=_cparams(("parallel", "arbitrary")),
        name=name,
    )(qi, kj, qh, kvh, kvh, scores, thr, jcut, bias)


def _route_kernel(lg_ref, info_ref, gate_ref, cnt_ref, carry_ref, *, tm):
    i = pl.program_id(0)

    @pl.when(i == 0)
    def _():
        carry_ref[...] = jnp.zeros_like(carry_ref)

    lane = lax.broadcasted_iota(I32, (tm, LANES), 1)
    lane_f = lane.astype(F32)
    lg = jnp.where(lane < N_EXPERTS, lg_ref[...], -jnp.inf)
    m1 = jnp.max(lg, axis=-1, keepdims=True)
    e1 = jnp.min(jnp.where(lg == m1, lane_f, float(LANES)), axis=-1, keepdims=True).astype(I32)
    lg2 = jnp.where(lane == e1, -jnp.inf, lg)
    m2 = jnp.max(lg2, axis=-1, keepdims=True)
    e2 = jnp.min(jnp.where(lg2 == m2, lane_f, float(LANES)), axis=-1, keepdims=True).astype(I32)
    ex = jnp.exp(m2 - m1)
    g1 = 1.0 / (1.0 + ex)
    g2 = ex / (1.0 + ex)
    onehot = jnp.where(jnp.logical_or(lane == e1, lane == e2), 1.0, 0.0)
    r = lax.broadcasted_iota(I32, (tm, tm), 0)
    c = lax.broadcasted_iota(I32, (tm, tm), 1)
    before = _dot(jnp.where(c < r, 1.0, 0.0).astype(BF16), onehot.astype(BF16)) + carry_ref[0:1, :]
    r1 = jnp.sum(jnp.where(lane == e1, before, 0.0), axis=-1, keepdims=True).astype(I32)
    r2 = jnp.sum(jnp.where(lane == e2, before, 0.0), axis=-1, keepdims=True).astype(I32)
    info = jnp.where(lane == 0, e1, jnp.where(lane == 1, e2, jnp.where(lane == 2, r1, r2)))
    info_ref[...] = info
    gate_ref[...] = jnp.where(lane == 0, g1, g2)
    total = carry_ref[0:1, :] + jnp.sum(onehot, axis=0, keepdims=True)
    carry_ref[...] = jnp.broadcast_to(total, carry_ref.shape)
    cnt_ref[...] = jnp.broadcast_to(total, cnt_ref.shape)


def _route(logits, *, tm=256, name):
    n = logits.shape[0]
    tm = min(tm, n)
    return pl.pallas_call(
        functools.partial(_route_kernel, tm=tm),
        out_shape=(jax.ShapeDtypeStruct((n, LANES), I32), jax.ShapeDtypeStruct((n, LANES), F32),
                   jax.ShapeDtypeStruct((8, LANES), F32)),
        grid=(n // tm,),
        in_specs=[pl.BlockSpec((tm, LANES), lambda i: (i, 0))],
        out_specs=(pl.BlockSpec((tm, LANES), lambda i: (i, 0)),
                   pl.BlockSpec((tm, LANES), lambda i: (i, 0)),
                   pl.BlockSpec((8, LANES), lambda i: (0, 0))),
        scratch_shapes=[pltpu.VMEM((8, LANES), F32)],
        compiler_params=_cparams(("arbitrary",)),
        name=name,
    )(logits)


def _combine_ln_kernel(d0_ref, d1_ref, h_ref, ys_ref, gate_ref, g_ref, b_ref, o_ref,
                       ya0_ref, ya1_ref, yb0_ref, yb1_ref, sems_a, sems_b, *, tm, slabs):
    i = pl.program_id(0)
    nt = pl.num_programs(0)
    bufs_a, bufs_b = (ya0_ref, ya1_ref), (yb0_ref, yb1_ref)

    def gather(tile, wait):
        for idx_ref, bufs, sems in ((d0_ref, bufs_a, sems_a), (d1_ref, bufs_b, sems_b)):
            _row_gather(ys_ref, idx_ref, tile * tm, bufs, sems, tile % 2, tm, slabs, wait=wait)

    @pl.when(i == 0)
    def _():
        gather(0, wait=False)

    gather(i, wait=True)

    @pl.when(i + 1 < nt)
    def _():
        gather(i + 1, wait=False)

    for s in range(2):
        @pl.when(i % 2 == s)
        def _(s=s):
            ff = (gate_ref[:, 0:1] * _slabs_to_rows(bufs_a[s], slabs)
                  + gate_ref[:, 1:2] * _slabs_to_rows(bufs_b[s], slabs))
            o_ref[...] = _layer_norm(DEEPNORM_ALPHA * h_ref[...] + ff, g_ref[...], b_ref[...])


def _combine_ln(dest0, dest1, h, ys, gates, g, b, *, tm=256, name):
    n, d = h.shape
    slabs = d // LANES
    tm = min(tm, n)
    row = lambda i, d0, d1: (i, 0)
    const = lambda i, d0, d1: (0, 0)
    slab_buf = pltpu.VMEM((tm * slabs, LANES), F32)
    return pl.pallas_call(
        functools.partial(_combine_ln_kernel, tm=tm, slabs=slabs),
        out_shape=jax.ShapeDtypeStruct((n, d), F32),
        grid_spec=pltpu.PrefetchScalarGridSpec(
            num_scalar_prefetch=2,
            grid=(n // tm,),
            in_specs=[pl.BlockSpec((tm, d), row),
                      pl.BlockSpec(memory_space=pl.ANY),
                      pl.BlockSpec((tm, LANES), row),
                      pl.BlockSpec((1, d), const),
                      pl.BlockSpec((1, d), const)],
            out_specs=pl.BlockSpec((tm, d), row),
            scratch_shapes=[slab_buf, slab_buf, slab_buf, slab_buf,
                            pltpu.SemaphoreType.DMA((2,)), pltpu.SemaphoreType.DMA((2,))]),
        compiler_params=_cparams(("arbitrary",)),
        name=name,
    )(dest0, dest1, h, ys, gates, g, b)


def _rel_bucket(dist):
    n = jnp.maximum(dist, 0)
    exact = REL_BUCKETS // 2
    nf = jnp.maximum(n, 1).astype(F32)
    large = exact + (jnp.log(nf / exact) / math.log(REL_MAX_DIST / exact) * (REL_BUCKETS - exact)).astype(I32)
    large = jnp.minimum(large, REL_BUCKETS - 1)
    return jnp.where(n < exact, n, large)


def _bias_tile_kernel(tab_ref, bucket_ref, o_ref):
    h = pl.program_id(1)
    bucket = bucket_ref[...]
    far = tab_ref[REL_BUCKETS - 1, h]
    acc = jnp.zeros(o_ref.shape, F32)
    for b in range(REL_BUCKETS - 1):
        acc = jnp.where(bucket == b, tab_ref[b, h] - far, acc)
    o_ref[...] = acc * LOG2E


def _bias_tiles(rel_table, n_heads, t, *, name):
    assert t >= REL_MAX_DIST
    r = jnp.arange(t)[:, None]
    c = jnp.arange(t)[None, :]
    buckets = jnp.stack([_rel_bucket(r - c + off) for off in (0, t)]).astype(I32)
    return pl.pallas_call(
        _bias_tile_kernel,
        out_shape=jax.ShapeDtypeStruct((2, n_heads, t, t), F32),
        grid=(2, n_heads),
        in_specs=[pl.BlockSpec(memory_space=pltpu.SMEM),
                  pl.BlockSpec((None, t, t), lambda k, h: (k, 0, 0))],
        out_specs=pl.BlockSpec((None, None, t, t), lambda k, h: (k, h, 0, 0)),
        compiler_params=_cparams(("parallel", "parallel")),
        name=name,
    )(rel_table.astype(F32), buckets)


def _pad_cols(w, n):
    return jnp.pad(w, ((0, 0), (0, n - w.shape[1])))


def _even_layer(h, rel_table, w_in, b_forget, w_out, ln1_g, ln1_b, w1, w3, w2, ln2_g, ln2_b,
                *, batch, seq):
    d = h.shape[1]
    wa = N_MOBA_HEADS * HEAD_DIM
    wb = N_FOX_HEADS * HEAD_DIM
    n_qkv = 3 * wa + 3 * wb
    scale = HEAD_DIM ** -0.5 * LOG2E
    ones, scl = jnp.ones((wa,), F32), jnp.full((wa,), scale, F32)
    colscale = jnp.concatenate([scl, ones, ones, scl, ones, ones])[None, :]
    qkv = _mm(h, w_in[:, :n_qkv].astype(BF16), colscale, out_dtype=BF16, tm=1024, tn=768,
              name="ev_qkv_proj")
    fb = _mm(h, _pad_cols(w_in[:, n_qkv:], LANES).astype(BF16), jnp.ones((1, LANES), F32),
             out_dtype=F32, tn=LANES, name="ev_forget_proj")
    log_f = jax.nn.log_sigmoid(fb[:, :N_FOX_HEADS] + b_forget.astype(F32))
    csum = jnp.cumsum(log_f.reshape(batch, seq, N_FOX_HEADS).transpose(0, 2, 1), axis=-1) * LOG2E
    csum = csum.reshape(batch, N_FOX_HEADS, seq // ATTN_TILE, 1, ATTN_TILE)
    oa = _moba_attention(qkv, _bias_tiles(rel_table, N_MOBA_HEADS, MOBA_BLOCK, name="ev_bias_tiles"),
                         batch=batch, seq=seq, name="ev_moba_attn")
    ob = _fox_attention(qkv, csum, batch=batch, seq=seq, name="ev_fox_attn")
    attn = jnp.concatenate([oa, ob], axis=-1)
    h = _mm_ln(attn, w_out.astype(BF16), h, ln1_g[None, :], ln1_b[None, :], name="ev_out_proj_ln")
    return _ffn_ln(h, w1.astype(BF16), w3.astype(BF16), w2.astype(BF16),
                   ln2_g[None, :], ln2_b[None, :], name="ev_swiglu_ln")


def _moe(h, h_slabs, logits, w1, w3, w2, ln_g, ln_b):
    n, d = h.shape
    tm = EXPERT_TILE
    info, gates, cnt = _route(logits, name="od_route")
    counts = cnt[0, :N_EXPERTS].astype(I32)
    padded = (counts + tm - 1) // tm * tm
    pend = jnp.cumsum(padded)
    pstart = pend - padded
    e = info[:, :MOE_TOPK]
    dest = (pstart[e] + info[:, MOE_TOPK:2 * MOE_TOPK]).astype(I32)
    n_rows = -(-(n * MOE_TOPK + N_EXPERTS * (tm - 1)) // tm) * tm
    n_tiles = n_rows // tm
    tile_start = jnp.arange(n_tiles, dtype=I32) * tm
    tile_valid = (tile_start < pend[-1]).astype(I32)
    last = jnp.maximum(pend[-1] - 1, 0)
    tile_e = jnp.minimum(jnp.searchsorted(pend, jnp.minimum(tile_start, last), side="right"),
                         N_EXPERTS - 1).astype(I32)
    tok = jnp.arange(n, dtype=I32)
    row_tok = jnp.zeros((n_rows,), I32).at[dest.T.reshape(-1)].set(jnp.concatenate([tok, tok]))
    ys = _moe_ffn(tile_e, tile_valid, row_tok, h_slabs, w1.astype(BF16),
                  w3.astype(BF16), w2.astype(BF16), tm=tm, name="od_moe_swiglu")
    return _combine_ln(dest[:, 0], dest[:, 1], h, ys, gates, ln_g[None, :], ln_b[None, :],
                       name="od_moe_combine_ln")


def _odd_layer(h, rel_table, w_in, q_norm_g, kv_norm_g, w_uq, w_qidx, w_uk, w_uv, w_out,
               ln1_g, ln1_b, router, w1, w3, w2, ln2_g, ln2_b, *, batch, seq):
    nh = N_DSA_HEADS
    rq, rkv = DSA_Q_RANK, DSA_KV_RANK
    scale = HEAD_DIM ** -0.5 * LOG2E
    w_in_p = jnp.concatenate([w_in[:, :rq + rkv],
                              _pad_cols(w_in[:, rq + rkv:rq + rkv + IDX_DIM], LANES),
                              _pad_cols(w_in[:, rq + rkv + IDX_DIM:], LANES)], axis=1)
    cq, ckv, kidx, widx = _dsa_in_proj(h, w_in_p.astype(BF16), q_norm_g[None, :], kv_norm_g[None, :],
                                       name="od_in_proj_rms")
    w_qidx_p = jnp.pad(w_qidx.reshape(rq, IDX_HEADS, IDX_DIM), ((0, 0), (0, 0), (0, LANES - IDX_DIM)))
    wq = jnp.concatenate([w_uq, w_qidx_p.reshape(rq, IDX_HEADS * LANES)], axis=1)
    qscale = jnp.concatenate([jnp.full((nh * HEAD_DIM,), scale, F32),
                              jnp.ones((IDX_HEADS * LANES,), F32)])[None, :]
    qh = _mm(cq, wq.astype(BF16), qscale, out_dtype=BF16, head_major=True, tm=512, tn=2048,
             name="od_q_proj")
    wkv = jnp.concatenate([w_uk.transpose(1, 0, 2).reshape(rkv, nh * HEAD_DIM),
                           w_uv.transpose(1, 0, 2).reshape(rkv, nh * HEAD_DIM)], axis=1)
    kvh = _mm(ckv, wkv.astype(BF16), jnp.ones((1, 2 * nh * HEAD_DIM), F32), out_dtype=BF16,
              head_major=True, tm=512, tn=2048, name="od_kv_proj")
    scores, thr, jcut = _idx_topk(qh, kidx, widx, batch=batch, seq=seq,
                                  ksel=min(DSA_TOPK_MAX, seq // 4), name="od_idx_topk")
    attn = _dsa_attention(qh, kvh, scores, thr, jcut,
                          _bias_tiles(rel_table, nh, ATTN_TILE, name="od_bias_tiles"),
                          batch=batch, seq=seq, name="od_dsa_attn")
    h, h_slabs, logits = _mm_ln_route(attn, w_out.astype(BF16), h, ln1_g[None, :], ln1_b[None, :],
                                      _pad_cols(router, LANES).astype(BF16), name="od_out_proj_ln")
    return _moe(h, h_slabs, logits, w1, w3, w2, ln2_g, ln2_b)


def kernel(x, rel_table, ev_w_in, ev_b_forget, ev_w_out, ev_ln1_g, ev_ln1_b, ev_ffn_w1, ev_ffn_w3, ev_ffn_w2, ev_ln2_g, ev_ln2_b, od_w_in, od_q_norm_g, od_kv_norm_g, od_w_uq, od_w_qidx, od_w_uk, od_w_uv, od_w_out, od_ln1_g, od_ln1_b, od_router, od_exp_w1, od_exp_w3, od_exp_w2, od_ln2_g, od_ln2_b):
    batch, seq, d = x.shape
    h = x.reshape(batch * seq, d)
    for layer in range(DEPTH):
        i = layer // 2
        if layer % 2 == 0:
            h = _even_layer(h, rel_table, ev_w_in[i], ev_b_forget[i], ev_w_out[i], ev_ln1_g[i],
                            ev_ln1_b[i], ev_ffn_w1[i], ev_ffn_w3[i], ev_ffn_w2[i], ev_ln2_g[i],
                            ev_ln2_b[i], batch=batch, seq=seq)
        else:
            h = _odd_layer(h, rel_table, od_w_in[i], od_q_norm_g[i], od_kv_norm_g[i], od_w_uq[i],
                           od_w_qidx[i], od_w_uk[i], od_w_uv[i], od_w_out[i], od_ln1_g[i],
                           od_ln1_b[i], od_router[i], od_exp_w1[i], od_exp_w3[i], od_exp_w2[i],
                           od_ln2_g[i], od_ln2_b[i], batch=batch, seq=seq)
    return h.reshape(batch, seq, d)
```

```python
import functools
import math

import jax
import jax.numpy as jnp
from jax import lax
from jax.experimental import pallas as pl
from jax.experimental.pallas import tpu as pltpu

F32 = jnp.float32
BF16 = jnp.bfloat16
I32 = jnp.int32
I16 = jnp.int16

HEAD_DIM = 128
N_MOBA_HEADS = 8
N_FOX_HEADS = 8
MOBA_BLOCK = 256
MOBA_TOPK = 3
N_DSA_HEADS = 16
DSA_Q_RANK = 512
DSA_KV_RANK = 512
IDX_HEADS = 16
IDX_DIM = 64
DSA_TOPK_MAX = 256
REL_BUCKETS = 32
REL_MAX_DIST = 128
N_EXPERTS = 8
MOE_TOPK = 2
LN_EPS = 1e-5
RMS_EPS = 1e-6
DEPTH = 2
DEEPNORM_ALPHA = (2 * DEPTH) ** 0.25

LANES = 128
ATTN_TILE = 256
WIDE_CHUNK = 1024
DSA_KEY_TILE = 512
DSA_HEAD_GROUP = 4
COUNT_STRIP = 128
EXPERT_TILE = 512
NEG = -1e30
LOG2E = math.log2(math.e)
INT_MIN = -(2 ** 31)
HALF_RANGE = 2 ** 15
VMEM_LIMIT = 56 * 1024 * 1024


def _cparams(sem, vmem=VMEM_LIMIT, flags=None):
    return pltpu.CompilerParams(dimension_semantics=sem, vmem_limit_bytes=vmem, flags=flags)


INTERLEAVE_CHAINS = None


def _dot(a, b):
    return jnp.dot(a, b, preferred_element_type=F32)


def _dot_t(a, b):
    return lax.dot_general(a, b, (((1,), (1,)), ((), ())), preferred_element_type=F32)


def _layer_norm(y, g, b):
    mu = jnp.mean(y, axis=-1, keepdims=True)
    d = y - mu
    var = jnp.mean(d * d, axis=-1, keepdims=True)
    return d * lax.rsqrt(var + LN_EPS) * g + b


def _mm_kernel(x_ref, w_ref, cs_ref, o_ref, *, head_major):
    acc = _dot(x_ref[...].astype(BF16), w_ref[...]) * cs_ref[...]
    if head_major:
        for c in range(o_ref.shape[0]):
            o_ref[c] = acc[:, c * LANES:(c + 1) * LANES].astype(o_ref.dtype)
    else:
        o_ref[...] = acc.astype(o_ref.dtype)


def _mm(x, w, colscale, *, out_dtype, head_major=False, tm=512, tn=512, name):
    m, k = x.shape
    n = w.shape[1]
    tm, tn = min(tm, m), min(tn, n)
    assert m % tm == 0 and n % tn == 0 and tn % LANES == 0
    if head_major:
        out_shape = jax.ShapeDtypeStruct((n // LANES, m, LANES), out_dtype)
        out_spec = pl.BlockSpec((tn // LANES, tm, LANES), lambda i, j: (j, i, 0))
    else:
        out_shape = jax.ShapeDtypeStruct((m, n), out_dtype)
        out_spec = pl.BlockSpec((tm, tn), lambda i, j: (i, j))
    return pl.pallas_call(
        functools.partial(_mm_kernel, head_major=head_major),
        out_shape=out_shape,
        grid=(m // tm, n // tn),
        in_specs=[pl.BlockSpec((tm, k), lambda i, j: (i, 0)),
                  pl.BlockSpec((k, tn), lambda i, j: (0, j)),
                  pl.BlockSpec((1, tn), lambda i, j: (0, j))],
        out_specs=out_spec,
        compiler_params=_cparams(("parallel", "arbitrary")),
        name=name,
    )(x, w, colscale)


def _mm_ln_kernel(x_ref, w_ref, res_ref, g_ref, b_ref, o_ref):
    y = DEEPNORM_ALPHA * res_ref[...] + _dot(x_ref[...], w_ref[...])
    o_ref[...] = _layer_norm(y, g_ref[...], b_ref[...])


def _mm_ln(x, w, res, g, b, *, tm=256, name):
    m, k = x.shape
    d = w.shape[1]
    tm = min(tm, m)
    return pl.pallas_call(
        _mm_ln_kernel,
        out_shape=jax.ShapeDtypeStruct((m, d), F32),
        grid=(m // tm,),
        in_specs=[pl.BlockSpec((tm, k), lambda i: (i, 0)),
                  pl.BlockSpec((k, d), lambda i: (0, 0)),
                  pl.BlockSpec((tm, d), lambda i: (i, 0)),
                  pl.BlockSpec((1, d), lambda i: (0, 0)),
                  pl.BlockSpec((1, d), lambda i: (0, 0))],
        out_specs=pl.BlockSpec((tm, d), lambda i: (i, 0)),
        compiler_params=_cparams(("parallel",)),
        name=name,
    )(x, w, res, g, b)


def _mm_ln_route_kernel(x_ref, w_ref, res_ref, g_ref, b_ref, wr_ref, o_ref, slab_ref, lg_ref,
                        *, slabs):
    y = _layer_norm(DEEPNORM_ALPHA * res_ref[...] + _dot(x_ref[...], w_ref[...]),
                    g_ref[...], b_ref[...])
    o_ref[...] = y
    tm = y.shape[0]
    for c in range(slabs):
        slab_ref[pl.ds(c, tm, stride=slabs), :] = y[:, c * LANES:(c + 1) * LANES]
    lg_ref[...] = _dot(y.astype(BF16), wr_ref[...])


def _mm_ln_route(x, w, res, g, b, w_router, *, tm=256, name):
    m, k = x.shape
    d = w.shape[1]
    tm = min(tm, m)
    slabs = d // LANES
    row = lambda i: (i, 0)
    const = lambda i: (0, 0)
    return pl.pallas_call(
        functools.partial(_mm_ln_route_kernel, slabs=slabs),
        out_shape=(jax.ShapeDtypeStruct((m, d), F32), jax.ShapeDtypeStruct((m * slabs, LANES), F32),
                   jax.ShapeDtypeStruct((m, LANES), F32)),
        grid=(m // tm,),
        in_specs=[pl.BlockSpec((tm, k), row), pl.BlockSpec((k, d), const), pl.BlockSpec((tm, d), row),
                  pl.BlockSpec((1, d), const), pl.BlockSpec((1, d), const),
                  pl.BlockSpec((d, LANES), const)],
        out_specs=(pl.BlockSpec((tm, d), row), pl.BlockSpec((tm * slabs, LANES), row),
                   pl.BlockSpec((tm, LANES), row)),
        compiler_params=_cparams(("parallel",)),
        name=name,
    )(x, w, res, g, b, w_router)


def _swiglu_step(xb, w1_ref, w3_ref, w2_ref):
    a = _dot(xb, w1_ref[...])
    c = _dot(xb, w3_ref[...])
    hmid = a / (1.0 + jnp.exp(-a)) * c
    return _dot(hmid.astype(BF16), w2_ref[...])


def _ffn_ln_kernel(x_ref, w1_ref, w3_ref, w2_ref, g_ref, b_ref, o_ref, acc_ref, xb_ref):
    f = pl.program_id(1)

    @pl.when(f == 0)
    def _():
        acc_ref[...] = jnp.zeros_like(acc_ref)
        xb_ref[...] = x_ref[...].astype(BF16)

    acc_ref[...] += _swiglu_step(xb_ref[...], w1_ref, w3_ref, w2_ref)

    @pl.when(f == pl.num_programs(1) - 1)
    def _():
        y = DEEPNORM_ALPHA * x_ref[...] + acc_ref[...]
        o_ref[...] = _layer_norm(y, g_ref[...], b_ref[...])


def _ffn_ln(x, w1, w3, w2, g, b, *, tm=512, tf=512, name):
    m, d = x.shape
    dff = w1.shape[1]
    tm, tf = min(tm, m), min(tf, dff)
    assert m % tm == 0 and dff % tf == 0
    return pl.pallas_call(
        _ffn_ln_kernel,
        out_shape=jax.ShapeDtypeStruct((m, d), F32),
        grid=(m // tm, dff // tf),
        in_specs=[pl.BlockSpec((tm, d), lambda i, f: (i, 0)),
                  pl.BlockSpec((d, tf), lambda i, f: (0, f)),
                  pl.BlockSpec((d, tf), lambda i, f: (0, f)),
                  pl.BlockSpec((tf, d), lambda i, f: (f, 0)),
                  pl.BlockSpec((1, d), lambda i, f: (0, 0)),
                  pl.BlockSpec((1, d), lambda i, f: (0, 0))],
        out_specs=pl.BlockSpec((tm, d), lambda i, f: (i, 0)),
        scratch_shapes=[pltpu.VMEM((tm, d), F32), pltpu.VMEM((tm, d), BF16)],
        compiler_params=_cparams(("parallel", "arbitrary")),
        name=name,
    )(x, w1, w3, w2, g, b)


def _row_gather(src_hbm, idx_ref, base, bufs, sems, slot, n_rows, slabs, *, wait):
    for s, buf in enumerate(bufs):
        @pl.when(slot == s)
        def _(s=s, buf=buf):
            def body(r, _):
                src = src_hbm.at[pl.ds(pl.multiple_of(idx_ref[base + r] * slabs, slabs), slabs)]
                dst = buf.at[pl.ds(pl.multiple_of(r * slabs, slabs), slabs)]
                cp = pltpu.make_async_copy(src, dst, sems.at[s])
                if wait:
                    cp.wait()
                else:
                    cp.start()
                return 0
            lax.fori_loop(0, n_rows, body, 0, unroll=8)


def _slabs_to_rows(buf, slabs):
    rows = buf.shape[0] // slabs
    return jnp.concatenate([buf[pl.ds(c, rows, stride=slabs), :] for c in range(slabs)], axis=1)


def _moe_ffn_kernel(te_ref, tv_ref, tok_ref, h_ref, w1_ref, w3_ref, w2_ref, o_ref,
                    acc_ref, xb_ref, xg0_ref, xg1_ref, sems, *, tm, slabs):
    i, f = pl.program_id(0), pl.program_id(1)
    nt = pl.num_programs(0)
    bufs = (xg0_ref, xg1_ref)
    gather = functools.partial(_row_gather, h_ref, tok_ref, bufs=bufs, sems=sems, n_rows=tm,
                               slabs=slabs)

    @pl.when(tv_ref[i] > 0)
    def _():
        @pl.when(f == 0)
        def _():
            @pl.when(i == 0)
            def _():
                gather(base=0, slot=0, wait=False)

            gather(base=i * tm, slot=i % 2, wait=True)
            nxt = jnp.minimum(i + 1, nt - 1)

            @pl.when(jnp.logical_and(i + 1 < nt, tv_ref[nxt] > 0))
            def _():
                gather(base=nxt * tm, slot=nxt % 2, wait=False)

            for s, buf in enumerate(bufs):
                @pl.when(i % 2 == s)
                def _(buf=buf):
                    xb_ref[...] = _slabs_to_rows(buf, slabs).astype(BF16)

            acc_ref[...] = jnp.zeros_like(acc_ref)

        acc_ref[...] += _swiglu_step(xb_ref[...], w1_ref, w3_ref, w2_ref)

        @pl.when(f == pl.num_programs(1) - 1)
        def _():
            for c in range(slabs):
                o_ref[pl.ds(c, tm, stride=slabs), :] = acc_ref[:, c * LANES:(c + 1) * LANES]

    @pl.when(tv_ref[i] == 0)
    def _():
        o_ref[...] = jnp.zeros_like(o_ref)


def _moe_ffn(tile_e, tile_valid, row_tok, h_slabs, w1, w3, w2, *, tm, tf=512, name):
    n_rows = row_tok.shape[0]
    d = w1.shape[1]
    slabs = d // LANES
    dff = w1.shape[2]
    tf = min(tf, dff)
    nf = dff // tf
    assert n_rows % tm == 0 and dff % tf == 0

    def fidx(i, f, tv):
        return jnp.where(tv[i] > 0, f, nf - 1)

    return pl.pallas_call(
        functools.partial(_moe_ffn_kernel, tm=tm, slabs=slabs),
        out_shape=jax.ShapeDtypeStruct((n_rows * slabs, LANES), F32),
        grid_spec=pltpu.PrefetchScalarGridSpec(
            num_scalar_prefetch=3,
            grid=(n_rows // tm, nf),
            in_specs=[pl.BlockSpec(memory_space=pl.ANY),
                      pl.BlockSpec((None, d, tf), lambda i, f, te, tv, tok: (te[i], 0, fidx(i, f, tv))),
                      pl.BlockSpec((None, d, tf), lambda i, f, te, tv, tok: (te[i], 0, fidx(i, f, tv))),
                      pl.BlockSpec((None, tf, d), lambda i, f, te, tv, tok: (te[i], fidx(i, f, tv), 0))],
            out_specs=pl.BlockSpec((tm * slabs, LANES), lambda i, f, te, tv, tok: (i, 0)),
            scratch_shapes=[pltpu.VMEM((tm, d), F32), pltpu.VMEM((tm, d), BF16),
                            pltpu.VMEM((tm * slabs, LANES), F32), pltpu.VMEM((tm * slabs, LANES), F32),
                            pltpu.SemaphoreType.DMA((2,))]),
        compiler_params=_cparams(("arbitrary", "arbitrary")),
        name=name,
    )(tile_e, tile_valid, row_tok, h_slabs, w1, w3, w2)


def _online_softmax_step(s, v, carry):
    m, acc = carry
    m_new = jnp.maximum(m, jnp.max(s, axis=-1, keepdims=True))
    alpha = jnp.exp2(m - m_new)
    p = jnp.exp2((s - m_new).astype(BF16))
    return m_new, alpha * acc + _dot(p, _with_ones(v))


def _with_ones(v):
    return jnp.concatenate([v, jnp.ones_like(v)], axis=1)


def _softmax_init(tq):
    return jnp.full((tq, 1), NEG, F32), jnp.zeros((tq, 2 * HEAD_DIM), F32)


def _softmax_finish(acc):
    return acc[:, :HEAD_DIM] / acc[:, HEAD_DIM:]


def _pairwise_loop(n, body, carry):
    carry = lax.fori_loop(0, n // 2, lambda k, c: body(2 * k + 1, body(2 * k, c)), carry)
    return lax.fori_loop(n // 2 * 2, n, body, carry)


def _head_cols(h):
    return slice(h * HEAD_DIM, (h + 1) * HEAD_DIM)


def _kv_block(k_ref, v_ref, n, width, h):
    rows = pl.ds(pl.multiple_of(n * width, width), width)
    return k_ref[rows, _head_cols(h)], v_ref[rows, _head_cols(h)]


def _causal_mask(blk):
    r = lax.broadcasted_iota(I32, (blk, blk), 0)
    c = lax.broadcasted_iota(I32, (blk, blk), 1)
    return c <= r


def _moba_select(q, km, i, blk):
    nbp = km.shape[0]
    gate = _dot_t(q, km.astype(BF16))
    lane = lax.broadcasted_iota(I32, (blk, nbp), 1)
    lane_f = lane.astype(F32)
    g = jnp.where(lane < i, gate, -jnp.inf)
    sel = jnp.zeros((blk, nbp), F32)
    for _ in range(MOBA_TOPK):
        mx = jnp.max(g, axis=-1, keepdims=True)
        first = jnp.min(jnp.where(g == mx, lane_f, float(nbp)), axis=-1, keepdims=True)
        pick = jnp.logical_and(lane_f == first, mx > -jnp.inf)
        sel = jnp.where(pick, 1.0, sel)
        g = jnp.where(pick, -jnp.inf, g)
    return sel, lane


def _moba_kernel(q_ref, k_ref, v_ref, bias_ref, o_ref, km_ref, *, blk, nb, wide, hp):
    i = pl.program_id(2)
    per = wide // blk

    @pl.when(i == 0)
    def _():
        km_ref[...] = jnp.zeros_like(km_ref)

        def mean_body(n, _):
            for h in range(hp):
                kblk, _ = _kv_block(k_ref, v_ref, n, blk, h)
                km_ref[h, pl.ds(n, 1), :] = jnp.mean(kblk.astype(F32), axis=0, keepdims=True)
            return 0

        lax.fori_loop(0, nb, mean_body, 0)

    qs = [q_ref[:, _head_cols(h)] for h in range(hp)]
    sels = [_moba_select(qs[h], km_ref[h], i, blk) for h in range(hp)]

    def chunk_step(n, carries, near):
        kv = [_kv_block(k_ref, v_ref, n, wide, h) for h in range(hp)]
        logits = [_dot_t(qs[h], kv[h][0]) for h in range(hp)]
        causal = _causal_mask(blk)
        out = []
        for h in range(hp):
            s, vblk = logits[h], kv[h][1]
            sel, lane = sels[h]
            parts = []
            for u in range(per):
                blk_id = n * per + u
                part = s[:, u * blk:(u + 1) * blk]
                chosen = jnp.sum(jnp.where(lane == blk_id, sel, 0.0), axis=-1, keepdims=True) > 0.0
                if near:
                    part = part + jnp.where(blk_id == i, bias_ref[0, h],
                                            jnp.where(blk_id == i - 1, bias_ref[1, h], 0.0))
                    parts.append(jnp.where(blk_id == i, jnp.where(causal, part, NEG),
                                           jnp.where(chosen, part, NEG)))
                else:
                    parts.append(jnp.where(chosen, part, NEG))
            s = parts[0] if per == 1 else jnp.concatenate(parts, axis=1)
            out.append(_online_softmax_step(s, vblk, carries[h]))
        return tuple(out)

    first_near = jnp.maximum(i - 1, 0) // per
    carries = tuple(_softmax_init(blk) for _ in range(hp))
    carries = _pairwise_loop(first_near, lambda n, c: chunk_step(n, c, False), carries)
    carries = lax.fori_loop(first_near, i // per + 1, lambda n, c: chunk_step(n, c, True), carries)
    for h in range(hp):
        o_ref[:, _head_cols(h)] = _softmax_finish(carries[h][1]).astype(o_ref.dtype)


def _moba_attention(qkv, bias, *, batch, seq, hp=4, name):
    blk = MOBA_BLOCK
    nb = seq // blk
    nbp = -(-nb // LANES) * LANES
    h = N_MOBA_HEADS
    nq = seq // blk
    hg = h // hp
    w = hp * HEAD_DIM
    return pl.pallas_call(
        functools.partial(_moba_kernel, blk=blk, nb=nb, wide=min(WIDE_CHUNK, seq), hp=hp),
        out_shape=jax.ShapeDtypeStruct((batch * seq, h * HEAD_DIM), BF16),
        grid=(batch, hg, nq),
        in_specs=[pl.BlockSpec((blk, w), lambda b, g, i: (b * nq + i, g)),
                  pl.BlockSpec((seq, w), lambda b, g, i: (b, hg + g)),
                  pl.BlockSpec((seq, w), lambda b, g, i: (b, 2 * hg + g)),
                  pl.BlockSpec((2, hp, blk, blk), lambda b, g, i: (0, g, 0, 0))],
        out_specs=pl.BlockSpec((blk, w), lambda b, g, i: (b * nq + i, g)),
        scratch_shapes=[pltpu.VMEM((hp, nbp, HEAD_DIM), F32)],
        compiler_params=_cparams(("parallel", "parallel", "arbitrary"), flags=INTERLEAVE_CHAINS),
        name=name,
    )(qkv, qkv, qkv, bias)


def _fox_kernel(q_ref, k_ref, v_ref, ck_ref, o_ref, *, blk, wide, hp):
    i = pl.program_id(2)
    per = wide // blk
    qs = [q_ref[:, _head_cols(h)] for h in range(hp)]

    def step(n, carries, causal):
        kv = [_kv_block(k_ref, v_ref, n, wide, h) for h in range(hp)]
        logits = [_dot_t(qs[h], kv[h][0]) for h in range(hp)]
        if causal:
            row = i * blk + lax.broadcasted_iota(I32, (blk, wide), 0)
            col = n * wide + lax.broadcasted_iota(I32, (blk, wide), 1)
            visible = col <= row
        out = []
        for h in range(hp):
            ck = [ck_ref[h, n * per + u] for u in range(per)]
            s = logits[h] - (ck[0] if per == 1 else jnp.concatenate(ck, axis=1))
            if causal:
                s = jnp.where(visible, s, NEG)
            out.append(_online_softmax_step(s, kv[h][1], carries[h]))
        return tuple(out)

    n_wide = i // per
    carries = tuple(_softmax_init(blk) for _ in range(hp))
    carries = _pairwise_loop(n_wide, lambda n, c: step(n, c, False), carries)
    carries = step(n_wide, carries, True)
    for h in range(hp):
        o_ref[:, _head_cols(h)] = _softmax_finish(carries[h][1]).astype(o_ref.dtype)


def _fox_attention(qkv, csum, *, batch, seq, hp=4, name):
    blk = ATTN_TILE
    h = N_FOX_HEADS
    nq = seq // blk
    hg = h // hp
    base = 3 * N_MOBA_HEADS // hp
    w = hp * HEAD_DIM
    return pl.pallas_call(
        functools.partial(_fox_kernel, blk=blk, wide=min(WIDE_CHUNK, seq), hp=hp),
        out_shape=jax.ShapeDtypeStruct((batch * seq, h * HEAD_DIM), BF16),
        grid=(batch, hg, nq),
        in_specs=[pl.BlockSpec((blk, w), lambda b, g, i: (b * nq + i, base + g)),
                  pl.BlockSpec((seq, w), lambda b, g, i: (b, base + hg + g)),
                  pl.BlockSpec((seq, w), lambda b, g, i: (b, base + 2 * hg + g)),
                  pl.BlockSpec((None, hp, nq, 1, blk), lambda b, g, i: (b, g, 0, 0, 0))],
        out_specs=pl.BlockSpec((blk, w), lambda b, g, i: (b * nq + i, g)),
        compiler_params=_cparams(("parallel", "parallel", "arbitrary"), flags=INTERLEAVE_CHAINS),
        name=name,
    )(qkv, qkv, qkv, csum)


def _dsa_in_kernel(x_ref, w_ref, gq_ref, gkv_ref, cq_ref, ckv_ref, kidx_ref, widx_ref):
    acc = _dot(x_ref[...].astype(BF16), w_ref[...])
    rq, rkv = DSA_Q_RANK, DSA_KV_RANK

    def rms(z, g):
        return z * lax.rsqrt(jnp.mean(z * z, axis=-1, keepdims=True) + RMS_EPS) * g

    cq_ref[...] = rms(acc[:, :rq], gq_ref[...]).astype(cq_ref.dtype)
    ckv_ref[...] = rms(acc[:, rq:rq + rkv], gkv_ref[...]).astype(ckv_ref.dtype)
    kidx_ref[...] = acc[:, rq + rkv:rq + rkv + LANES].astype(kidx_ref.dtype)
    widx_ref[...] = acc[:, rq + rkv + LANES:] * (IDX_HEADS ** -0.5 * IDX_DIM ** -0.5)


def _dsa_in_proj(x, w, gq, gkv, *, tm=512, name):
    m, k = x.shape
    n = w.shape[1]
    tm = min(tm, m)
    rq, rkv = DSA_Q_RANK, DSA_KV_RANK
    row = lambda i: (i, 0)
    const = lambda i: (0, 0)
    return pl.pallas_call(
        _dsa_in_kernel,
        out_shape=(jax.ShapeDtypeStruct((m, rq), BF16), jax.ShapeDtypeStruct((m, rkv), BF16),
                   jax.ShapeDtypeStruct((m, LANES), BF16), jax.ShapeDtypeStruct((m, LANES), F32)),
        grid=(m // tm,),
        in_specs=[pl.BlockSpec((tm, k), row), pl.BlockSpec((k, n), const),
                  pl.BlockSpec((1, rq), const), pl.BlockSpec((1, rkv), const)],
        out_specs=(pl.BlockSpec((tm, rq), row), pl.BlockSpec((tm, rkv), row),
                   pl.BlockSpec((tm, LANES), row), pl.BlockSpec((tm, LANES), row)),
        compiler_params=_cparams(("parallel",)),
        name=name,
    )(x, w, gq, gkv)


def _fold_lanes(x):
    part = x[:, :LANES]
    for g in range(1, x.shape[1] // LANES):
        part = part + x[:, g * LANES:(g + 1) * LANES]
    return part


def _idx_topk_kernel(qi_ref, kj_ref, q_ref, k_ref, w_ref, sc_ref, thr_ref, jc_ref, keys_ref,
                     hi_ref, lo_ref, *, t, tk, ksel, seq):
    p = pl.program_id(1)
    i, j = qi_ref[p], kj_ref[p]
    k = k_ref[...]
    acc = jnp.zeros((t, tk), F32)
    for h in range(IDX_HEADS):
        acc = acc + jnp.maximum(_dot_t(q_ref[h], k), 0.0) * w_ref[:, h:h + 1]
    sc_ref[...] = acc

    row = i * t + lax.broadcasted_iota(I32, (t, tk), 0)
    lane_col = lax.broadcasted_iota(I32, (t, tk), 1)
    bits = pltpu.bitcast(acc, I32)
    key = jnp.where(bits < 0, bits ^ jnp.int32(0x7FFFFFFF), bits)
    key = jnp.where(bits == jnp.int32(INT_MIN), 0, key)
    key = jnp.where(j * tk + lane_col <= row, key, jnp.int32(INT_MIN))
    keys_ref[j] = key
    hi_ref[j] = lax.shift_right_arithmetic(key, 16).astype(I16)
    lo_ref[j] = ((key & 0xFFFF) - HALF_RANGE).astype(I16)
    j_last = (i * t + t - 1) // tk

    @pl.when(j == j_last)
    def _():
        lane = lax.broadcasted_iota(I32, (COUNT_STRIP, LANES), 1)
        strips = [slice(s0, s0 + COUNT_STRIP) for s0 in range(0, t, COUNT_STRIP)]

        def count(ref, pred, *row_args):
            dt = ref.dtype
            reps = [[jnp.broadcast_to(a[rows], (COUNT_STRIP, LANES)).astype(dt) for a in row_args]
                    for rows in strips]
            cnts = []
            for rows, args in zip(strips, reps):
                def body(c, cnt, rows=rows, args=args):
                    for g in range(tk // LANES):
                        keys = ref[c, rows, g * LANES:(g + 1) * LANES]
                        cnt = cnt + pred(keys, c * tk + g * LANES, *args).astype(dt)
                    return cnt

                cnts.append(_pairwise_loop(j_last + 1, body, jnp.zeros((COUNT_STRIP, LANES), dt)))
            cnt = jnp.concatenate(cnts, axis=0)
            return jnp.sum(cnt.astype(F32), axis=-1, keepdims=True).astype(I32)

        def kth_largest_16(ref, k_rows):
            ans = jnp.where(count(ref, lambda k, c0: k >= 0) >= k_rows, 0, -HALF_RANGE)

            def bit_body(b, ans):
                cand = ans + lax.shift_left(jnp.int32(1), 14 - b)
                return jnp.where(count(ref, lambda k, c0, cd: k >= cd, cand) >= k_rows, cand, ans)

            return lax.fori_loop(0, 15, bit_body, ans)

        top = kth_largest_16(hi_ref, ksel)
        left = ksel - count(hi_ref, lambda k, c0, a: k > a, top)

        def stage_bucket(c, _):
            for rows in strips:
                keep = hi_ref[c, rows, :] == jnp.broadcast_to(top[rows], (COUNT_STRIP, tk)).astype(I16)
                lo_ref[c, rows, :] = jnp.where(keep, lo_ref[c, rows, :], jnp.int16(-HALF_RANGE))
            return 0

        lax.fori_loop(0, j_last + 1, stage_bucket, 0)
        low = kth_largest_16(lo_ref, left)
        ans = top * (2 * HALF_RANGE) + (low + HALF_RANGE)
        need = ksel - count(keys_ref, lambda k, c0, a: k > a, ans)
        ties = count(keys_ref, lambda k, c0, a: k == a, ans)
        row1 = i * t + lax.broadcasted_iota(I32, (t, 1), 0)
        full = row1 >= ksel

        def tie_cut(_):
            nbits = seq.bit_length() - 1

            def bit_body(b, cut):
                cand = cut + lax.shift_left(jnp.int32(1), nbits - 1 - b)
                below = count(keys_ref, lambda k, c0, a, cd: jnp.logical_and(k == a, c0 + lane < cd),
                              ans, cand)
                return jnp.where(below < need, cand, cut)
            return lax.fori_loop(0, nbits, bit_body, jnp.zeros((t, 1), I32))

        contested = jnp.max(jnp.where(jnp.logical_and(full, ties > need), 1.0, 0.0)) > 0.0
        cut = lax.cond(contested, tie_cut, lambda _: jnp.full((t, 1), seq - 1, I32), 0)
        tbits = jnp.where(ans < 0, ans ^ jnp.int32(0x7FFFFFFF), ans)
        thr = jnp.where(full, pltpu.bitcast(tbits, F32), -jnp.inf)
        thr_ref[...] = jnp.broadcast_to(thr, thr_ref.shape)
        jc_ref[...] = jnp.broadcast_to(jnp.where(full, cut, seq - 1), jc_ref.shape)


def _causal_pairs(seq, t, tk):
    pairs = [(i, j) for i in range(seq // t) for j in range((i * t + t - 1) // tk + 1)]
    return jnp.asarray([p[0] for p in pairs], I32), jnp.asarray([p[1] for p in pairs], I32)


def _idx_topk(qh, kidx, widx, *, batch, seq, ksel, name):
    t, tk = ATTN_TILE, min(DSA_KEY_TILE, seq)
    nq, nk = seq // t, seq // tk
    assert seq & (seq - 1) == 0 and tk >= ksel
    qi, kj = _causal_pairs(seq, t, tk)
    npairs = qi.shape[0]
    qrow = lambda b, p, qi, kj: (b * nq + qi[p], 0)
    return pl.pallas_call(
        functools.partial(_idx_topk_kernel, t=t, tk=tk, ksel=ksel, seq=seq),
        out_shape=(jax.ShapeDtypeStruct((batch, npairs, t, tk), F32),
                   jax.ShapeDtypeStruct((batch * seq, LANES), F32),
                   jax.ShapeDtypeStruct((batch * seq, LANES), I32)),
        grid_spec=pltpu.PrefetchScalarGridSpec(
            num_scalar_prefetch=2,
            grid=(batch, npairs),
            in_specs=[pl.BlockSpec((IDX_HEADS, t, LANES), lambda b, p, qi, kj: (1, b * nq + qi[p], 0)),
                      pl.BlockSpec((tk, LANES), lambda b, p, qi, kj: (b * nk + kj[p], 0)),
                      pl.BlockSpec((t, LANES), qrow)],
            out_specs=(pl.BlockSpec((None, None, t, tk), lambda b, p, qi, kj: (b, p, 0, 0)),
                       pl.BlockSpec((t, LANES), qrow), pl.BlockSpec((t, LANES), qrow)),
            scratch_shapes=[pltpu.VMEM((nk, t, tk), I32), pltpu.VMEM((nk, t, tk), I16),
                            pltpu.VMEM((nk, t, tk), I16)]),
        compiler_params=_cparams(("parallel", "arbitrary")),
        name=name,
    )(qi, kj, qh, kidx, widx)


def _dsa_attn_kernel(qi_ref, kj_ref, q_ref, k_ref, v_ref, sc_ref, thr_ref, jc_ref, bias_ref,
                     o_ref, m_ref, acc_ref, *, t, tk):
    p = pl.program_id(1)
    i, j = qi_ref[p], kj_ref[p]

    @pl.when(j == 0)
    def _():
        m_ref[...] = jnp.full(m_ref.shape, NEG, F32)
        acc_ref[...] = jnp.zeros_like(acc_ref)

    sc = sc_ref[...]
    thr = thr_ref[:, 0:1]
    col = j * tk + lax.broadcasted_iota(I32, (t, tk), 1)
    row = i * t + lax.broadcasted_iota(I32, (t, tk), 0)
    tie = jnp.logical_and(sc == thr, col <= jc_ref[:, 0:1])
    keep = jnp.logical_and(jnp.logical_or(sc > thr, tie), col <= row)
    offs = [i * t - (j * tk + u * t) for u in range(tk // t)]

    def run(with_bias):
        for g in range(N_DSA_HEADS // DSA_HEAD_GROUP):
            heads = [g * DSA_HEAD_GROUP + u for u in range(DSA_HEAD_GROUP)]
            logits = [_dot_t(q_ref[h], k_ref[h]) for h in heads]
            for h, s in zip(heads, logits):
                if with_bias:
                    parts = [jnp.where(off == 0, bias_ref[0, h],
                                       jnp.where(off == t, bias_ref[1, h], 0.0)) for off in offs]
                    s = s + (parts[0] if len(parts) == 1 else jnp.concatenate(parts, axis=1))
                s = jnp.where(keep, s, NEG)
                m_prev = m_ref[h]
                m_next = jnp.maximum(m_prev, jnp.max(s, axis=-1, keepdims=True))
                alpha = jnp.exp2(m_prev - m_next)
                p = jnp.exp2((s - jnp.concatenate([m_next] * (tk // HEAD_DIM), axis=1)).astype(BF16))
                acc_ref[h] = (jnp.concatenate([alpha, alpha], axis=1) * acc_ref[h]
                              + _dot(p, _with_ones(v_ref[h])))
                m_ref[h] = m_next

    near = offs[-1] <= t

    @pl.when(near)
    def _():
        run(True)

    @pl.when(jnp.logical_not(near))
    def _():
        run(False)

    @pl.when(j == (i * t + t - 1) // tk)
    def _():
        for h in range(N_DSA_HEADS):
            o_ref[:, _head_cols(h)] = _softmax_finish(acc_ref[h]).astype(o_ref.dtype)


def _dsa_attention(qh, kvh, scores, thr, jcut, bias, *, batch, seq, name):
    t, tk = ATTN_TILE, min(DSA_KEY_TILE, seq)
    nq, nk = seq // t, seq // tk
    h = N_DSA_HEADS
    qi, kj = _causal_pairs(seq, t, tk)
    qrow = lambda b, p, qi, kj: (b * nq + qi[p], 0)
    return pl.pallas_call(
        functools.partial(_dsa_attn_kernel, t=t, tk=tk),
        out_shape=jax.ShapeDtypeStruct((batch * seq, h * HEAD_DIM), BF16),
        grid_spec=pltpu.PrefetchScalarGridSpec(
            num_scalar_prefetch=2,
            grid=(batch, qi.shape[0]),
            in_specs=[pl.BlockSpec((h, t, HEAD_DIM), lambda b, p, qi, kj: (0, b * nq + qi[p], 0)),
                      pl.BlockSpec((h, tk, HEAD_DIM), lambda b, p, qi, kj: (0, b * nk + kj[p], 0)),
                      pl.BlockSpec((h, tk, HEAD_DIM), lambda b, p, qi, kj: (1, b * nk + kj[p], 0)),
                      pl.BlockSpec((None, None, t, tk), lambda b, p, qi, kj: (b, p, 0, 0)),
                      pl.BlockSpec((t, LANES), qrow),
                      pl.BlockSpec((t, LANES), qrow),
                      pl.BlockSpec((2, h, t, t), lambda b, p, qi, kj: (0, 0, 0, 0))],
            out_specs=pl.BlockSpec((t, h * HEAD_DIM), qrow),
            scratch_shapes=[pltpu.VMEM((h, t, HEAD_DIM), F32),
                            pltpu.VMEM((h, t, 2 * HEAD_DIM), F32)]),
        compiler_params=_cparams(("parallel", "arbitrary")),
        name=name,
    )(qi, kj, qh, kvh, kvh, scores, thr, jcut, bias)


def _route_kernel(lg_ref, info_ref, gate_ref, cnt_ref, carry_ref, *, tm):
    i = pl.program_id(0)

    @pl.when(i == 0)
    def _():
        carry_ref[...] = jnp.zeros_like(carry_ref)

    lane = lax.broadcasted_iota(I32, (tm, LANES), 1)
    lane_f = lane.astype(F32)
    lg = jnp.where(lane < N_EXPERTS, lg_ref[...], -jnp.inf)
    m1 = jnp.max(lg, axis=-1, keepdims=True)
    e1 = jnp.min(jnp.where(lg == m1, lane_f, float(LANES)), axis=-1, keepdims=True).astype(I32)
    lg2 = jnp.where(lane == e1, -jnp.inf, lg)
    m2 = jnp.max(lg2, axis=-1, keepdims=True)
    e2 = jnp.min(jnp.where(lg2 == m2, lane_f, float(LANES)), axis=-1, keepdims=True).astype(I32)
    ex = jnp.exp(m2 - m1)
    g1 = 1.0 / (1.0 + ex)
    g2 = ex / (1.0 + ex)
    onehot = jnp.where(jnp.logical_or(lane == e1, lane == e2), 1.0, 0.0)
    r = lax.broadcasted_iota(I32, (tm, tm), 0)
    c = lax.broadcasted_iota(I32, (tm, tm), 1)
    before = _dot(jnp.where(c < r, 1.0, 0.0).astype(BF16), onehot.astype(BF16)) + carry_ref[0:1, :]
    r1 = jnp.sum(jnp.where(lane == e1, before, 0.0), axis=-1, keepdims=True).astype(I32)
    r2 = jnp.sum(jnp.where(lane == e2, before, 0.0), axis=-1, keepdims=True).astype(I32)
    info = jnp.where(lane == 0, e1, jnp.where(lane == 1, e2, jnp.where(lane == 2, r1, r2)))
    info_ref[...] = info
    gate_ref[...] = jnp.where(lane == 0, g1, g2)
    total = carry_ref[0:1, :] + jnp.sum(onehot, axis=0, keepdims=True)
    carry_ref[...] = jnp.broadcast_to(total, carry_ref.shape)
    cnt_ref[...] = jnp.broadcast_to(total, cnt_ref.shape)


def _route(logits, *, tm=256, name):
    n = logits.shape[0]
    tm = min(tm, n)
    return pl.pallas_call(
        functools.partial(_route_kernel, tm=tm),
        out_shape=(jax.ShapeDtypeStruct((n, LANES), I32), jax.ShapeDtypeStruct((n, LANES), F32),
                   jax.ShapeDtypeStruct((8, LANES), F32)),
        grid=(n // tm,),
        in_specs=[pl.BlockSpec((tm, LANES), lambda i: (i, 0))],
        out_specs=(pl.BlockSpec((tm, LANES), lambda i: (i, 0)),
                   pl.BlockSpec((tm, LANES), lambda i: (i, 0)),
                   pl.BlockSpec((8, LANES), lambda i: (0, 0))),
        scratch_shapes=[pltpu.VMEM((8, LANES), F32)],
        compiler_params=_cparams(("arbitrary",)),
        name=name,
    )(logits)


def _combine_ln_kernel(d0_ref, d1_ref, h_ref, ys_ref, gate_ref, g_ref, b_ref, o_ref,
                       ya0_ref, ya1_ref, yb0_ref, yb1_ref, sems_a, sems_b, *, tm, slabs):
    i = pl.program_id(0)
    nt = pl.num_programs(0)
    bufs_a, bufs_b = (ya0_ref, ya1_ref), (yb0_ref, yb1_ref)

    def gather(tile, wait):
        for idx_ref, bufs, sems in ((d0_ref, bufs_a, sems_a), (d1_ref, bufs_b, sems_b)):
            _row_gather(ys_ref, idx_ref, tile * tm, bufs, sems, tile % 2, tm, slabs, wait=wait)

    @pl.when(i == 0)
    def _():
        gather(0, wait=False)

    gather(i, wait=True)

    @pl.when(i + 1 < nt)
    def _():
        gather(i + 1, wait=False)

    for s in range(2):
        @pl.when(i % 2 == s)
        def _(s=s):
            ff = (gate_ref[:, 0:1] * _slabs_to_rows(bufs_a[s], slabs)
                  + gate_ref[:, 1:2] * _slabs_to_rows(bufs_b[s], slabs))
            o_ref[...] = _layer_norm(DEEPNORM_ALPHA * h_ref[...] + ff, g_ref[...], b_ref[...])


def _combine_ln(dest0, dest1, h, ys, gates, g, b, *, tm=256, name):
    n, d = h.shape
    slabs = d // LANES
    tm = min(tm, n)
    row = lambda i, d0, d1: (i, 0)
    const = lambda i, d0, d1: (0, 0)
    slab_buf = pltpu.VMEM((tm * slabs, LANES), F32)
    return pl.pallas_call(
        functools.partial(_combine_ln_kernel, tm=tm, slabs=slabs),
        out_shape=jax.ShapeDtypeStruct((n, d), F32),
        grid_spec=pltpu.PrefetchScalarGridSpec(
            num_scalar_prefetch=2,
            grid=(n // tm,),
            in_specs=[pl.BlockSpec((tm, d), row),
                      pl.BlockSpec(memory_space=pl.ANY),
                      pl.BlockSpec((tm, LANES), row),
                      pl.BlockSpec((1, d), const),
                      pl.BlockSpec((1, d), const)],
            out_specs=pl.BlockSpec((tm, d), row),
            scratch_shapes=[slab_buf, slab_buf, slab_buf, slab_buf,
                            pltpu.SemaphoreType.DMA((2,)), pltpu.SemaphoreType.DMA((2,))]),
        compiler_params=_cparams(("arbitrary",)),
        name=name,
    )(dest0, dest1, h, ys, gates, g, b)


def _rel_bucket(dist):
    n = jnp.maximum(dist, 0)
    exact = REL_BUCKETS // 2
    nf = jnp.maximum(n, 1).astype(F32)
    large = exact + (jnp.log(nf / exact) / math.log(REL_MAX_DIST / exact) * (REL_BUCKETS - exact)).astype(I32)
    large = jnp.minimum(large, REL_BUCKETS - 1)
    return jnp.where(n < exact, n, large)


def _bias_tile_kernel(tab_ref, bucket_ref, o_ref):
    h = pl.program_id(1)
    bucket = bucket_ref[...]
    far = tab_ref[REL_BUCKETS - 1, h]
    acc = jnp.zeros(o_ref.shape, F32)
    for b in range(REL_BUCKETS - 1):
        acc = jnp.where(bucket == b, tab_ref[b, h] - far, acc)
    o_ref[...] = acc * LOG2E


def _bias_tiles(rel_table, n_heads, t, *, name):
    assert t >= REL_MAX_DIST
    r = jnp.arange(t)[:, None]
    c = jnp.arange(t)[None, :]
    buckets = jnp.stack([_rel_bucket(r - c + off) for off in (0, t)]).astype(I32)
    return pl.pallas_call(
        _bias_tile_kernel,
        out_shape=jax.ShapeDtypeStruct((2, n_heads, t, t), F32),
        grid=(2, n_heads),
        in_specs=[pl.BlockSpec(memory_space=pltpu.SMEM),
                  pl.BlockSpec((None, t, t), lambda k, h: (k, 0, 0))],
        out_specs=pl.BlockSpec((None, None, t, t), lambda k, h: (k, h, 0, 0)),
        compiler_params=_cparams(("parallel", "parallel")),
        name=name,
    )(rel_table.astype(F32), buckets)


def _pad_cols(w, n):
    return jnp.pad(w, ((0, 0), (0, n - w.shape[1])))


def _even_layer(h, rel_table, w_in, b_forget, w_out, ln1_g, ln1_b, w1, w3, w2, ln2_g, ln2_b,
                *, batch, seq):
    d = h.shape[1]
    wa = N_MOBA_HEADS * HEAD_DIM
    wb = N_FOX_HEADS * HEAD_DIM
    n_qkv = 3 * wa + 3 * wb
    scale = HEAD_DIM ** -0.5 * LOG2E
    ones, scl = jnp.ones((wa,), F32), jnp.full((wa,), scale, F32)
    colscale = jnp.concatenate([scl, ones, ones, scl, ones, ones])[None, :]
    qkv = _mm(h, w_in[:, :n_qkv].astype(BF16), colscale, out_dtype=BF16, tm=1024, tn=768,
              name="ev_qkv_proj")
    fb = _mm(h, _pad_cols(w_in[:, n_qkv:], LANES).astype(BF16), jnp.ones((1, LANES), F32),
             out_dtype=F32, tn=LANES, name="ev_forget_proj")
    log_f = jax.nn.log_sigmoid(fb[:, :N_FOX_HEADS] + b_forget.astype(F32))
    csum = jnp.cumsum(log_f.reshape(batch, seq, N_FOX_HEADS).transpose(0, 2, 1), axis=-1) * LOG2E
    csum = csum.reshape(batch, N_FOX_HEADS, seq // ATTN_TILE, 1, ATTN_TILE)
    oa = _moba_attention(qkv, _bias_tiles(rel_table, N_MOBA_HEADS, MOBA_BLOCK, name="ev_bias_tiles"),
                         batch=batch, seq=seq, name="ev_moba_attn")
    ob = _fox_attention(qkv, csum, batch=batch, seq=seq, name="ev_fox_attn")
    attn = jnp.concatenate([oa, ob], axis=-1)
    h = _mm_ln(attn, w_out.astype(BF16), h, ln1_g[None, :], ln1_b[None, :], name="ev_out_proj_ln")
    return _ffn_ln(h, w1.astype(BF16), w3.astype(BF16), w2.astype(BF16),
                   ln2_g[None, :], ln2_b[None, :], name="ev_swiglu_ln")


def _moe(h, h_slabs, logits, w1, w3, w2, ln_g, ln_b):
    n, d = h.shape
    tm = EXPERT_TILE
    info, gates, cnt = _route(logits, name="od_route")
    counts = cnt[0, :N_EXPERTS].astype(I32)
    padded = (counts + tm - 1) // tm * tm
    pend = jnp.cumsum(padded)
    pstart = pend - padded
    e = info[:, :MOE_TOPK]
    dest = (pstart[e] + info[:, MOE_TOPK:2 * MOE_TOPK]).astype(I32)
    n_rows = -(-(n * MOE_TOPK + N_EXPERTS * (tm - 1)) // tm) * tm
    n_tiles = n_rows // tm
    tile_start = jnp.arange(n_tiles, dtype=I32) * tm
    tile_valid = (tile_start < pend[-1]).astype(I32)
    last = jnp.maximum(pend[-1] - 1, 0)
    tile_e = jnp.minimum(jnp.searchsorted(pend, jnp.minimum(tile_start, last), side="right"),
                         N_EXPERTS - 1).astype(I32)
    tok = jnp.arange(n, dtype=I32)
    row_tok = jnp.zeros((n_rows,), I32).at[dest.T.reshape(-1)].set(jnp.concatenate([tok, tok]))
    ys = _moe_ffn(tile_e, tile_valid, row_tok, h_slabs, w1.astype(BF16),
                  w3.astype(BF16), w2.astype(BF16), tm=tm, name="od_moe_swiglu")
    return _combine_ln(dest[:, 0], dest[:, 1], h, ys, gates, ln_g[None, :], ln_b[None, :],
                       name="od_moe_combine_ln")


def _odd_layer(h, rel_table, w_in, q_norm_g, kv_norm_g, w_uq, w_qidx, w_uk, w_uv, w_out,
               ln1_g, ln1_b, router, w1, w3, w2, ln2_g, ln2_b, *, batch, seq):
    nh = N_DSA_HEADS
    rq, rkv = DSA_Q_RANK, DSA_KV_RANK
    scale = HEAD_DIM ** -0.5 * LOG2E
    w_in_p = jnp.concatenate([w_in[:, :rq + rkv],
                              _pad_cols(w_in[:, rq + rkv:rq + rkv + IDX_DIM], LANES),
                              _pad_cols(w_in[:, rq + rkv + IDX_DIM:], LANES)], axis=1)
    cq, ckv, kidx, widx = _dsa_in_proj(h, w_in_p.astype(BF16), q_norm_g[None, :], kv_norm_g[None, :],
                                       name="od_in_proj_rms")
    w_qidx_p = jnp.pad(w_qidx.reshape(rq, IDX_HEADS, IDX_DIM), ((0, 0), (0, 0), (0, LANES - IDX_DIM)))
    wq = jnp.concatenate([w_uq, w_qidx_p.reshape(rq, IDX_HEADS * LANES)], axis=1)
    qscale = jnp.concatenate([jnp.full((nh * HEAD_DIM,), scale, F32),
                              jnp.ones((IDX_HEADS * LANES,), F32)])[None, :]
    qh = _mm(cq, wq.astype(BF16), qscale, out_dtype=BF16, head_major=True, tm=512, tn=2048,
             name="od_q_proj")
    wkv = jnp.concatenate([w_uk.transpose(1, 0, 2).reshape(rkv, nh * HEAD_DIM),
                           w_uv.transpose(1, 0, 2).reshape(rkv, nh * HEAD_DIM)], axis=1)
    kvh = _mm(ckv, wkv.astype(BF16), jnp.ones((1, 2 * nh * HEAD_DIM), F32), out_dtype=BF16,
              head_major=True, tm=512, tn=2048, name="od_kv_proj")
    scores, thr, jcut = _idx_topk(qh, kidx, widx, batch=batch, seq=seq,
                                  ksel=min(DSA_TOPK_MAX, seq // 4), name="od_idx_topk")
    attn = _dsa_attention(qh, kvh, scores, thr, jcut,
                          _bias_tiles(rel_table, nh, ATTN_TILE, name="od_bias_tiles"),
                          batch=batch, seq=seq, name="od_dsa_attn")
    h, h_slabs, logits = _mm_ln_route(attn, w_out.astype(BF16), h, ln1_g[None, :], ln1_b[None, :],
                                      _pad_cols(router, LANES).astype(BF16), name="od_out_proj_ln")
    return _moe(h, h_slabs, logits, w1, w3, w2, ln2_g, ln2_b)


def kernel(x, rel_table, ev_w_in, ev_b_forget, ev_w_out, ev_ln1_g, ev_ln1_b, ev_ffn_w1, ev_ffn_w3, ev_ffn_w2, ev_ln2_g, ev_ln2_b, od_w_in, od_q_norm_g, od_kv_norm_g, od_w_uq, od_w_qidx, od_w_uk, od_w_uv, od_w_out, od_ln1_g, od_ln1_b, od_router, od_exp_w1, od_exp_w3, od_exp_w2, od_ln2_g, od_ln2_b):
    batch, seq, d = x.shape
    h = x.reshape(batch * seq, d)
    for layer in range(DEPTH):
        i = layer // 2
        if layer % 2 == 0:
            h = _even_layer(h, rel_table, ev_w_in[i], ev_b_forget[i], ev_w_out[i], ev_ln1_g[i],
                            ev_ln1_b[i], ev_ffn_w1[i], ev_ffn_w3[i], ev_ffn_w2[i], ev_ln2_g[i],
                            ev_ln2_b[i], batch=batch, seq=seq)
        else:
            h = _odd_layer(h, rel_table, od_w_in[i], od_q_norm_g[i], od_kv_norm_g[i], od_w_uq[i],
                           od_w_qidx[i], od_w_uk[i], od_w_uv[i], od_w_out[i], od_ln1_g[i],
                           od_ln1_b[i], od_router[i], od_exp_w1[i], od_exp_w3[i], od_exp_w2[i],
                           od_ln2_g[i], od_ln2_b[i], batch=batch, seq=seq)
    return h.reshape(batch, seq, d)
```

```python
import functools
import math

import jax
import jax.numpy as jnp
from jax import lax
from jax.experimental import pallas as pl
from jax.experimental.pallas import tpu as pltpu

F32 = jnp.float32
BF16 = jnp.bfloat16
I32 = jnp.int32

HEAD_DIM = 128
N_MOBA_HEADS = 8
N_FOX_HEADS = 8
MOBA_BLOCK = 256
MOBA_TOPK = 3
N_DSA_HEADS = 16
DSA_Q_RANK = 512
DSA_KV_RANK = 512
IDX_HEADS = 16
IDX_DIM = 64
DSA_TOPK_MAX = 256
REL_BUCKETS = 32
REL_MAX_DIST = 128
N_EXPERTS = 8
MOE_TOPK = 2
LN_EPS = 1e-5
RMS_EPS = 1e-6
DEPTH = 2
DEEPNORM_ALPHA = (2 * DEPTH) ** 0.25

LANES = 128
ATTN_TILE = 256
WIDE_CHUNK = 1024
DSA_KEY_TILE = 512
DSA_HEAD_GROUP = 4
COUNT_STRIP = 128
EXPERT_TILE = 512
NEG = -1e30
LOG2E = math.log2(math.e)
INT_MIN = -(2 ** 31)
VMEM_LIMIT = 56 * 1024 * 1024


def _cparams(sem, vmem=VMEM_LIMIT, flags=None):
    return pltpu.CompilerParams(dimension_semantics=sem, vmem_limit_bytes=vmem, flags=flags)


INTERLEAVE_CHAINS = None


def _dot(a, b):
    return jnp.dot(a, b, preferred_element_type=F32)


def _dot_t(a, b):
    return lax.dot_general(a, b, (((1,), (1,)), ((), ())), preferred_element_type=F32)


def _layer_norm(y, g, b):
    mu = jnp.mean(y, axis=-1, keepdims=True)
    d = y - mu
    var = jnp.mean(d * d, axis=-1, keepdims=True)
    return d * lax.rsqrt(var + LN_EPS) * g + b


def _mm_kernel(x_ref, w_ref, cs_ref, o_ref, *, head_major):
    acc = _dot(x_ref[...].astype(BF16), w_ref[...]) * cs_ref[...]
    if head_major:
        for c in range(o_ref.shape[0]):
            o_ref[c] = acc[:, c * LANES:(c + 1) * LANES].astype(o_ref.dtype)
    else:
        o_ref[...] = acc.astype(o_ref.dtype)


def _mm(x, w, colscale, *, out_dtype, head_major=False, tm=512, tn=512, name):
    m, k = x.shape
    n = w.shape[1]
    tm, tn = min(tm, m), min(tn, n)
    assert m % tm == 0 and n % tn == 0 and tn % LANES == 0
    if head_major:
        out_shape = jax.ShapeDtypeStruct((n // LANES, m, LANES), out_dtype)
        out_spec = pl.BlockSpec((tn // LANES, tm, LANES), lambda i, j: (j, i, 0))
    else:
        out_shape = jax.ShapeDtypeStruct((m, n), out_dtype)
        out_spec = pl.BlockSpec((tm, tn), lambda i, j: (i, j))
    return pl.pallas_call(
        functools.partial(_mm_kernel, head_major=head_major),
        out_shape=out_shape,
        grid=(m // tm, n // tn),
        in_specs=[pl.BlockSpec((tm, k), lambda i, j: (i, 0)),
                  pl.BlockSpec((k, tn), lambda i, j: (0, j)),
                  pl.BlockSpec((1, tn), lambda i, j: (0, j))],
        out_specs=out_spec,
        compiler_params=_cparams(("parallel", "arbitrary")),
        name=name,
    )(x, w, colscale)


def _mm_ln_kernel(x_ref, w_ref, res_ref, g_ref, b_ref, o_ref):
    y = DEEPNORM_ALPHA * res_ref[...] + _dot(x_ref[...], w_ref[...])
    o_ref[...] = _layer_norm(y, g_ref[...], b_ref[...])


def _mm_ln(x, w, res, g, b, *, tm=256, name):
    m, k = x.shape
    d = w.shape[1]
    tm = min(tm, m)
    return pl.pallas_call(
        _mm_ln_kernel,
        out_shape=jax.ShapeDtypeStruct((m, d), F32),
        grid=(m // tm,),
        in_specs=[pl.BlockSpec((tm, k), lambda i: (i, 0)),
                  pl.BlockSpec((k, d), lambda i: (0, 0)),
                  pl.BlockSpec((tm, d), lambda i: (i, 0)),
                  pl.BlockSpec((1, d), lambda i: (0, 0)),
                  pl.BlockSpec((1, d), lambda i: (0, 0))],
        out_specs=pl.BlockSpec((tm, d), lambda i: (i, 0)),
        compiler_params=_cparams(("parallel",)),
        name=name,
    )(x, w, res, g, b)


def _mm_ln_route_kernel(x_ref, w_ref, res_ref, g_ref, b_ref, wr_ref, o_ref, slab_ref, lg_ref,
                        *, slabs):
    y = _layer_norm(DEEPNORM_ALPHA * res_ref[...] + _dot(x_ref[...], w_ref[...]),
                    g_ref[...], b_ref[...])
    o_ref[...] = y
    tm = y.shape[0]
    for c in range(slabs):
        slab_ref[pl.ds(c, tm, stride=slabs), :] = y[:, c * LANES:(c + 1) * LANES]
    lg_ref[...] = _dot(y.astype(BF16), wr_ref[...])


def _mm_ln_route(x, w, res, g, b, w_router, *, tm=256, name):
    m, k = x.shape
    d = w.shape[1]
    tm = min(tm, m)
    slabs = d // LANES
    row = lambda i: (i, 0)
    const = lambda i: (0, 0)
    return pl.pallas_call(
        functools.partial(_mm_ln_route_kernel, slabs=slabs),
        out_shape=(jax.ShapeDtypeStruct((m, d), F32), jax.ShapeDtypeStruct((m * slabs, LANES), F32),
                   jax.ShapeDtypeStruct((m, LANES), F32)),
        grid=(m // tm,),
        in_specs=[pl.BlockSpec((tm, k), row), pl.BlockSpec((k, d), const), pl.BlockSpec((tm, d), row),
                  pl.BlockSpec((1, d), const), pl.BlockSpec((1, d), const),
                  pl.BlockSpec((d, LANES), const)],
        out_specs=(pl.BlockSpec((tm, d), row), pl.BlockSpec((tm * slabs, LANES), row),
                   pl.BlockSpec((tm, LANES), row)),
        compiler_params=_cparams(("parallel",)),
        name=name,
    )(x, w, res, g, b, w_router)


def _swiglu_step(xb, w1_ref, w3_ref, w2_ref):
    a = _dot(xb, w1_ref[...])
    c = _dot(xb, w3_ref[...])
    hmid = a / (1.0 + jnp.exp(-a)) * c
    return _dot(hmid.astype(BF16), w2_ref[...])


def _ffn_ln_kernel(x_ref, w1_ref, w3_ref, w2_ref, g_ref, b_ref, o_ref, acc_ref, xb_ref):
    f = pl.program_id(1)

    @pl.when(f == 0)
    def _():
        acc_ref[...] = jnp.zeros_like(acc_ref)
        xb_ref[...] = x_ref[...].astype(BF16)

    acc_ref[...] += _swiglu_step(xb_ref[...], w1_ref, w3_ref, w2_ref)

    @pl.when(f == pl.num_programs(1) - 1)
    def _():
        y = DEEPNORM_ALPHA * x_ref[...] + acc_ref[...]
        o_ref[...] = _layer_norm(y, g_ref[...], b_ref[...])


def _ffn_ln(x, w1, w3, w2, g, b, *, tm=512, tf=512, name):
    m, d = x.shape
    dff = w1.shape[1]
    tm, tf = min(tm, m), min(tf, dff)
    assert m % tm == 0 and dff % tf == 0
    return pl.pallas_call(
        _ffn_ln_kernel,
        out_shape=jax.ShapeDtypeStruct((m, d), F32),
        grid=(m // tm, dff // tf),
        in_specs=[pl.BlockSpec((tm, d), lambda i, f: (i, 0)),
                  pl.BlockSpec((d, tf), lambda i, f: (0, f)),
                  pl.BlockSpec((d, tf), lambda i, f: (0, f)),
                  pl.BlockSpec((tf, d), lambda i, f: (f, 0)),
                  pl.BlockSpec((1, d), lambda i, f: (0, 0)),
                  pl.BlockSpec((1, d), lambda i, f: (0, 0))],
        out_specs=pl.BlockSpec((tm, d), lambda i, f: (i, 0)),
        scratch_shapes=[pltpu.VMEM((tm, d), F32), pltpu.VMEM((tm, d), BF16)],
        compiler_params=_cparams(("parallel", "arbitrary")),
        name=name,
    )(x, w1, w3, w2, g, b)


def _row_gather(src_hbm, idx_ref, base, bufs, sems, slot, n_rows, slabs, *, wait):
    for s, buf in enumerate(bufs):
        @pl.when(slot == s)
        def _(s=s, buf=buf):
            def body(r, _):
                src = src_hbm.at[pl.ds(pl.multiple_of(idx_ref[base + r] * slabs, slabs), slabs)]
                dst = buf.at[pl.ds(pl.multiple_of(r * slabs, slabs), slabs)]
                cp = pltpu.make_async_copy(src, dst, sems.at[s])
                if wait:
                    cp.wait()
                else:
                    cp.start()
                return 0
            lax.fori_loop(0, n_rows, body, 0, unroll=8)


def _slabs_to_rows(buf, slabs):
    rows = buf.shape[0] // slabs
    return jnp.concatenate([buf[pl.ds(c, rows, stride=slabs), :] for c in range(slabs)], axis=1)


def _moe_ffn_kernel(te_ref, tv_ref, tok_ref, h_ref, w1_ref, w3_ref, w2_ref, o_ref,
                    acc_ref, xb_ref, xg0_ref, xg1_ref, sems, *, tm, slabs):
    i, f = pl.program_id(0), pl.program_id(1)
    nt = pl.num_programs(0)
    bufs = (xg0_ref, xg1_ref)
    gather = functools.partial(_row_gather, h_ref, tok_ref, bufs=bufs, sems=sems, n_rows=tm,
                               slabs=slabs)

    @pl.when(tv_ref[i] > 0)
    def _():
        @pl.when(f == 0)
        def _():
            @pl.when(i == 0)
            def _():
                gather(base=0, slot=0, wait=False)

            gather(base=i * tm, slot=i % 2, wait=True)
            nxt = jnp.minimum(i + 1, nt - 1)

            @pl.when(jnp.logical_and(i + 1 < nt, tv_ref[nxt] > 0))
            def _():
                gather(base=nxt * tm, slot=nxt % 2, wait=False)

            for s, buf in enumerate(bufs):
                @pl.when(i % 2 == s)
                def _(buf=buf):
                    xb_ref[...] = _slabs_to_rows(buf, slabs).astype(BF16)

            acc_ref[...] = jnp.zeros_like(acc_ref)

        acc_ref[...] += _swiglu_step(xb_ref[...], w1_ref, w3_ref, w2_ref)

        @pl.when(f == pl.num_programs(1) - 1)
        def _():
            for c in range(slabs):
                o_ref[pl.ds(c, tm, stride=slabs), :] = acc_ref[:, c * LANES:(c + 1) * LANES]

    @pl.when(tv_ref[i] == 0)
    def _():
        o_ref[...] = jnp.zeros_like(o_ref)


def _moe_ffn(tile_e, tile_valid, row_tok, h_slabs, w1, w3, w2, *, tm, tf=512, name):
    n_rows = row_tok.shape[0]
    d = w1.shape[1]
    slabs = d // LANES
    dff = w1.shape[2]
    tf = min(tf, dff)
    nf = dff // tf
    assert n_rows % tm == 0 and dff % tf == 0

    def fidx(i, f, tv):
        return jnp.where(tv[i] > 0, f, nf - 1)

    return pl.pallas_call(
        functools.partial(_moe_ffn_kernel, tm=tm, slabs=slabs),
        out_shape=jax.ShapeDtypeStruct((n_rows * slabs, LANES), F32),
        grid_spec=pltpu.PrefetchScalarGridSpec(
            num_scalar_prefetch=3,
            grid=(n_rows // tm, nf),
            in_specs=[pl.BlockSpec(memory_space=pl.ANY),
                      pl.BlockSpec((None, d, tf), lambda i, f, te, tv, tok: (te[i], 0, fidx(i, f, tv))),
                      pl.BlockSpec((None, d, tf), lambda i, f, te, tv, tok: (te[i], 0, fidx(i, f, tv))),
                      pl.BlockSpec((None, tf, d), lambda i, f, te, tv, tok: (te[i], fidx(i, f, tv), 0))],
            out_specs=pl.BlockSpec((tm * slabs, LANES), lambda i, f, te, tv, tok: (i, 0)),
            scratch_shapes=[pltpu.VMEM((tm, d), F32), pltpu.VMEM((tm, d), BF16),
                            pltpu.VMEM((tm * slabs, LANES), F32), pltpu.VMEM((tm * slabs, LANES), F32),
                            pltpu.SemaphoreType.DMA((2,))]),
        compiler_params=_cparams(("arbitrary", "arbitrary")),
        name=name,
    )(tile_e, tile_valid, row_tok, h_slabs, w1, w3, w2)


def _online_softmax_step(s, v, carry):
    m, acc = carry
    m_new = jnp.maximum(m, jnp.max(s, axis=-1, keepdims=True))
    alpha = jnp.exp2(m - m_new)
    p = jnp.exp2((s - m_new).astype(BF16))
    return m_new, alpha * acc + _dot(p, _with_ones(v))


def _with_ones(v):
    return jnp.concatenate([v, jnp.ones_like(v)], axis=1)


def _softmax_init(tq):
    return jnp.full((tq, 1), NEG, F32), jnp.zeros((tq, 2 * HEAD_DIM), F32)


def _softmax_finish(acc):
    return acc[:, :HEAD_DIM] / acc[:, HEAD_DIM:]


def _pairwise_loop(n, body, carry):
    carry = lax.fori_loop(0, n // 2, lambda k, c: body(2 * k + 1, body(2 * k, c)), carry)
    return lax.fori_loop(n // 2 * 2, n, body, carry)


def _head_cols(h):
    return slice(h * HEAD_DIM, (h + 1) * HEAD_DIM)


def _kv_block(k_ref, v_ref, n, width, h):
    rows = pl.ds(pl.multiple_of(n * width, width), width)
    return k_ref[rows, _head_cols(h)], v_ref[rows, _head_cols(h)]


def _causal_mask(blk):
    r = lax.broadcasted_iota(I32, (blk, blk), 0)
    c = lax.broadcasted_iota(I32, (blk, blk), 1)
    return c <= r


def _moba_select(q, km, i, blk):
    nbp = km.shape[0]
    gate = _dot_t(q, km.astype(BF16))
    lane = lax.broadcasted_iota(I32, (blk, nbp), 1)
    lane_f = lane.astype(F32)
    g = jnp.where(lane < i, gate, -jnp.inf)
    sel = jnp.zeros((blk, nbp), F32)
    for _ in range(MOBA_TOPK):
        mx = jnp.max(g, axis=-1, keepdims=True)
        first = jnp.min(jnp.where(g == mx, lane_f, float(nbp)), axis=-1, keepdims=True)
        pick = jnp.logical_and(lane_f == first, mx > -jnp.inf)
        sel = jnp.where(pick, 1.0, sel)
        g = jnp.where(pick, -jnp.inf, g)
    return sel, lane


def _moba_kernel(q_ref, k_ref, v_ref, bias_ref, o_ref, km_ref, *, blk, nb, wide, hp):
    i = pl.program_id(2)
    per = wide // blk

    @pl.when(i == 0)
    def _():
        km_ref[...] = jnp.zeros_like(km_ref)

        def mean_body(n, _):
            for h in range(hp):
                kblk, _ = _kv_block(k_ref, v_ref, n, blk, h)
                km_ref[h, pl.ds(n, 1), :] = jnp.mean(kblk.astype(F32), axis=0, keepdims=True)
            return 0

        lax.fori_loop(0, nb, mean_body, 0)

    qs = [q_ref[:, _head_cols(h)] for h in range(hp)]
    sels = [_moba_select(qs[h], km_ref[h], i, blk) for h in range(hp)]

    def chunk_step(n, carries, near):
        kv = [_kv_block(k_ref, v_ref, n, wide, h) for h in range(hp)]
        logits = [_dot_t(qs[h], kv[h][0]) for h in range(hp)]
        causal = _causal_mask(blk)
        out = []
        for h in range(hp):
            s, vblk = logits[h], kv[h][1]
            sel, lane = sels[h]
            parts = []
            for u in range(per):
                blk_id = n * per + u
                part = s[:, u * blk:(u + 1) * blk]
                chosen = jnp.sum(jnp.where(lane == blk_id, sel, 0.0), axis=-1, keepdims=True) > 0.0
                if near:
                    part = part + jnp.where(blk_id == i, bias_ref[0, h],
                                            jnp.where(blk_id == i - 1, bias_ref[1, h], 0.0))
                    parts.append(jnp.where(blk_id == i, jnp.where(causal, part, NEG),
                                           jnp.where(chosen, part, NEG)))
                else:
                    parts.append(jnp.where(chosen, part, NEG))
            s = parts[0] if per == 1 else jnp.concatenate(parts, axis=1)
            out.append(_online_softmax_step(s, vblk, carries[h]))
        return tuple(out)

    first_near = jnp.maximum(i - 1, 0) // per
    carries = tuple(_softmax_init(blk) for _ in range(hp))
    carries = _pairwise_loop(first_near, lambda n, c: chunk_step(n, c, False), carries)
    carries = lax.fori_loop(first_near, i // per + 1, lambda n, c: chunk_step(n, c, True), carries)
    for h in range(hp):
        o_ref[:, _head_cols(h)] = _softmax_finish(carries[h][1]).astype(o_ref.dtype)


def _moba_attention(qkv, bias, *, batch, seq, hp=4, name):
    blk = MOBA_BLOCK
    nb = seq // blk
    nbp = -(-nb // LANES) * LANES
    h = N_MOBA_HEADS
    nq = seq // blk
    hg = h // hp
    w = hp * HEAD_DIM
    return pl.pallas_call(
        functools.partial(_moba_kernel, blk=blk, nb=nb, wide=min(WIDE_CHUNK, seq), hp=hp),
        out_shape=jax.ShapeDtypeStruct((batch * seq, h * HEAD_DIM), BF16),
        grid=(batch, hg, nq),
        in_specs=[pl.BlockSpec((blk, w), lambda b, g, i: (b * nq + i, g)),
                  pl.BlockSpec((seq, w), lambda b, g, i: (b, hg + g)),
                  pl.BlockSpec((seq, w), lambda b, g, i: (b, 2 * hg + g)),
                  pl.BlockSpec((2, hp, blk, blk), lambda b, g, i: (0, g, 0, 0))],
        out_specs=pl.BlockSpec((blk, w), lambda b, g, i: (b * nq + i, g)),
        scratch_shapes=[pltpu.VMEM((hp, nbp, HEAD_DIM), F32)],
        compiler_params=_cparams(("parallel", "parallel", "arbitrary"), flags=INTERLEAVE_CHAINS),
        name=name,
    )(qkv, qkv, qkv, bias)


def _fox_kernel(q_ref, k_ref, v_ref, ck_ref, o_ref, *, blk, wide, hp):
    i = pl.program_id(2)
    per = wide // blk
    qs = [q_ref[:, _head_cols(h)] for h in range(hp)]

    def step(n, carries, causal):
        kv = [_kv_block(k_ref, v_ref, n, wide, h) for h in range(hp)]
        logits = [_dot_t(qs[h], kv[h][0]) for h in range(hp)]
        if causal:
            row = i * blk + lax.broadcasted_iota(I32, (blk, wide), 0)
            col = n * wide + lax.broadcasted_iota(I32, (blk, wide), 1)
            visible = col <= row
        out = []
        for h in range(hp):
            ck = [ck_ref[h, n * per + u] for u in range(per)]
            s = logits[h] - (ck[0] if per == 1 else jnp.concatenate(ck, axis=1))
            if causal:
                s = jnp.where(visible, s, NEG)
            out.append(_online_softmax_step(s, kv[h][1], carries[h]))
        return tuple(out)

    n_wide = i // per
    carries = tuple(_softmax_init(blk) for _ in range(hp))
    carries = _pairwise_loop(n_wide, lambda n, c: step(n, c, False), carries)
    carries = step(n_wide, carries, True)
    for h in range(hp):
        o_ref[:, _head_cols(h)] = _softmax_finish(carries[h][1]).astype(o_ref.dtype)


def _fox_attention(qkv, csum, *, batch, seq, hp=4, name):
    blk = ATTN_TILE
    h = N_FOX_HEADS
    nq = seq // blk
    hg = h // hp
    base = 3 * N_MOBA_HEADS // hp
    w = hp * HEAD_DIM
    return pl.pallas_call(
        functools.partial(_fox_kernel, blk=blk, wide=min(WIDE_CHUNK, seq), hp=hp),
        out_shape=jax.ShapeDtypeStruct((batch * seq, h * HEAD_DIM), BF16),
        grid=(batch, hg, nq),
        in_specs=[pl.BlockSpec((blk, w), lambda b, g, i: (b * nq + i, base + g)),
                  pl.BlockSpec((seq, w), lambda b, g, i: (b, base + hg + g)),
                  pl.BlockSpec((seq, w), lambda b, g, i: (b, base + 2 * hg + g)),
                  pl.BlockSpec((None, hp, nq, 1, blk), lambda b, g, i: (b, g, 0, 0, 0))],
        out_specs=pl.BlockSpec((blk, w), lambda b, g, i: (b * nq + i, g)),
        compiler_params=_cparams(("parallel", "parallel", "arbitrary"), flags=INTERLEAVE_CHAINS),
        name=name,
    )(qkv, qkv, qkv, csum)


def _forget_csum_kernel(fb_ref, b_ref, o_ref, carry_ref, *, tc):
    @pl.when(pl.program_id(1) == 0)
    def _():
        carry_ref[...] = jnp.zeros_like(carry_ref)

    z = fb_ref[...] + b_ref[...]
    run = jnp.minimum(z, 0.0) - jnp.log1p(jnp.exp(-jnp.abs(z)))
    row = lax.broadcasted_iota(I32, (tc, LANES), 0)
    shift = 1
    while shift < tc:
        run = run + jnp.where(row >= shift, pltpu.roll(run, shift, axis=0), 0.0)
        shift *= 2
    run = run + carry_ref[0:1, :]
    carry_ref[...] = jnp.broadcast_to(run[tc - 1:tc, :], carry_ref.shape)
    o_ref[...] = (run * LOG2E).T[:o_ref.shape[0], :]


def _forget_csum(fb, bias, *, batch, seq, n_heads, tc=512, name):
    tc = min(tc, seq)
    nt = seq // tc
    return pl.pallas_call(
        functools.partial(_forget_csum_kernel, tc=tc),
        out_shape=jax.ShapeDtypeStruct((batch, n_heads, seq), F32),
        grid=(batch, nt),
        in_specs=[pl.BlockSpec((tc, LANES), lambda b, j: (b * nt + j, 0)),
                  pl.BlockSpec((1, LANES), lambda b, j: (0, 0))],
        out_specs=pl.BlockSpec((None, n_heads, tc), lambda b, j: (b, 0, j)),
        scratch_shapes=[pltpu.VMEM((8, LANES), F32)],
        compiler_params=_cparams(("parallel", "arbitrary")),
        name=name,
    )(fb, bias)


def _dsa_in_kernel(x_ref, w_ref, gq_ref, gkv_ref, cq_ref, ckv_ref, kidx_ref, widx_ref):
    acc = _dot(x_ref[...].astype(BF16), w_ref[...])
    rq, rkv = DSA_Q_RANK, DSA_KV_RANK

    def rms(z, g):
        return z * lax.rsqrt(jnp.mean(z * z, axis=-1, keepdims=True) + RMS_EPS) * g

    cq_ref[...] = rms(acc[:, :rq], gq_ref[...]).astype(cq_ref.dtype)
    ckv_ref[...] = rms(acc[:, rq:rq + rkv], gkv_ref[...]).astype(ckv_ref.dtype)
    kidx_ref[...] = acc[:, rq + rkv:rq + rkv + LANES].astype(kidx_ref.dtype)
    widx_ref[...] = acc[:, rq + rkv + LANES:] * (IDX_HEADS ** -0.5 * IDX_DIM ** -0.5)


def _dsa_in_proj(x, w, gq, gkv, *, tm=512, name):
    m, k = x.shape
    n = w.shape[1]
    tm = min(tm, m)
    rq, rkv = DSA_Q_RANK, DSA_KV_RANK
    row = lambda i: (i, 0)
    const = lambda i: (0, 0)
    return pl.pallas_call(
        _dsa_in_kernel,
        out_shape=(jax.ShapeDtypeStruct((m, rq), BF16), jax.ShapeDtypeStruct((m, rkv), BF16),
                   jax.ShapeDtypeStruct((m, LANES), BF16), jax.ShapeDtypeStruct((m, LANES), F32)),
        grid=(m // tm,),
        in_specs=[pl.BlockSpec((tm, k), row), pl.BlockSpec((k, n), const),
                  pl.BlockSpec((1, rq), const), pl.BlockSpec((1, rkv), const)],
        out_specs=(pl.BlockSpec((tm, rq), row), pl.BlockSpec((tm, rkv), row),
                   pl.BlockSpec((tm, LANES), row), pl.BlockSpec((tm, LANES), row)),
        compiler_params=_cparams(("parallel",)),
        name=name,
    )(x, w, gq, gkv)


def _fold_lanes(x):
    part = x[:, :LANES]
    for g in range(1, x.shape[1] // LANES):
        part = part + x[:, g * LANES:(g + 1) * LANES]
    return part


def _idx_topk_kernel(qi_ref, kj_ref, q_ref, k_ref, w_ref, sc_ref, thr_ref, jc_ref, keys_ref,
                     *, t, tk, ksel, seq):
    p = pl.program_id(1)
    i, j = qi_ref[p], kj_ref[p]
    k = k_ref[...]
    acc = jnp.zeros((t, tk), F32)
    for h in range(IDX_HEADS):
        acc = acc + jnp.maximum(_dot_t(q_ref[h], k), 0.0) * w_ref[:, h:h + 1]
    sc_ref[...] = acc

    row = i * t + lax.broadcasted_iota(I32, (t, tk), 0)
    lane_col = lax.broadcasted_iota(I32, (t, tk), 1)
    bits = pltpu.bitcast(acc, I32)
    key = jnp.where(bits < 0, bits ^ jnp.int32(0x7FFFFFFF), bits)
    key = jnp.where(bits == jnp.int32(INT_MIN), 0, key)
    keys_ref[j] = jnp.where(j * tk + lane_col <= row, key, jnp.int32(INT_MIN))
    j_last = (i * t + t - 1) // tk

    @pl.when(j == j_last)
    def _():
        lane = lax.broadcasted_iota(I32, (COUNT_STRIP, LANES), 1)
        strips = [slice(s0, s0 + COUNT_STRIP) for s0 in range(0, t, COUNT_STRIP)]

        def count(ref, pred, *row_args):
            dt = ref.dtype
            reps = [[jnp.broadcast_to(a[rows], (COUNT_STRIP, LANES)).astype(dt) for a in row_args]
                    for rows in strips]
            cnts = []
            for rows, args in zip(strips, reps):
                def body(c, cnt, rows=rows, args=args):
                    for g in range(tk // LANES):
                        keys = ref[c, rows, g * LANES:(g + 1) * LANES]
                        cnt = cnt + pred(keys, c * tk + g * LANES, *args).astype(dt)
                    return cnt

                cnts.append(_pairwise_loop(j_last + 1, body, jnp.zeros((COUNT_STRIP, LANES), dt)))
            cnt = jnp.concatenate(cnts, axis=0)
            return jnp.sum(cnt.astype(F32), axis=-1, keepdims=True).astype(I32)

        ans = jnp.where(count(keys_ref, lambda k, c0: k >= 0) >= ksel, 0, jnp.int32(INT_MIN))

        def bit_body(b, ans):
            cand = ans + lax.shift_left(jnp.int32(1), 30 - b)
            return jnp.where(count(keys_ref, lambda k, c0, cd: k >= cd, cand) >= ksel, cand, ans)

        ans = lax.fori_loop(0, 31, bit_body, ans)
        need = ksel - count(keys_ref, lambda k, c0, a: k > a, ans)
        ties = count(keys_ref, lambda k, c0, a: k == a, ans)
        row1 = i * t + lax.broadcasted_iota(I32, (t, 1), 0)
        full = row1 >= ksel

        def tie_cut(_):
            nbits = seq.bit_length() - 1

            def bit_body(b, cut):
                cand = cut + lax.shift_left(jnp.int32(1), nbits - 1 - b)
                below = count(keys_ref, lambda k, c0, a, cd: jnp.logical_and(k == a, c0 + lane < cd),
                              ans, cand)
                return jnp.where(below < need, cand, cut)
            return lax.fori_loop(0, nbits, bit_body, jnp.zeros((t, 1), I32))

        contested = jnp.max(jnp.where(jnp.logical_and(full, ties > need), 1.0, 0.0)) > 0.0
        cut = lax.cond(contested, tie_cut, lambda _: jnp.full((t, 1), seq - 1, I32), 0)
        tbits = jnp.where(ans < 0, ans ^ jnp.int32(0x7FFFFFFF), ans)
        thr = jnp.where(full, pltpu.bitcast(tbits, F32), -jnp.inf)
        thr_ref[...] = jnp.broadcast_to(thr, thr_ref.shape)
        jc_ref[...] = jnp.broadcast_to(jnp.where(full, cut, seq - 1), jc_ref.shape)


def _causal_pairs(seq, t, tk):
    pairs = [(i, j) for i in range(seq // t) for j in range((i * t + t - 1) // tk + 1)]
    return jnp.asarray([p[0] for p in pairs], I32), jnp.asarray([p[1] for p in pairs], I32)


def _idx_topk(qh, kidx, widx, *, batch, seq, ksel, name):
    t, tk = ATTN_TILE, min(DSA_KEY_TILE, seq)
    nq, nk = seq // t, seq // tk
    assert seq & (seq - 1) == 0 and tk >= ksel
    qi, kj = _causal_pairs(seq, t, tk)
    npairs = qi.shape[0]
    qrow = lambda b, p, qi, kj: (b * nq + qi[p], 0)
    return pl.pallas_call(
        functools.partial(_idx_topk_kernel, t=t, tk=tk, ksel=ksel, seq=seq),
        out_shape=(jax.ShapeDtypeStruct((batch, npairs, t, tk), F32),
                   jax.ShapeDtypeStruct((batch * seq, LANES), F32),
                   jax.ShapeDtypeStruct((batch * seq, LANES), I32)),
        grid_spec=pltpu.PrefetchScalarGridSpec(
            num_scalar_prefetch=2,
            grid=(batch, npairs),
            in_specs=[pl.BlockSpec((IDX_HEADS, t, LANES), lambda b, p, qi, kj: (1, b * nq + qi[p], 0)),
                      pl.BlockSpec((tk, LANES), lambda b, p, qi, kj: (b * nk + kj[p], 0)),
                      pl.BlockSpec((t, LANES), qrow)],
            out_specs=(pl.BlockSpec((None, None, t, tk), lambda b, p, qi, kj: (b, p, 0, 0)),
                       pl.BlockSpec((t, LANES), qrow), pl.BlockSpec((t, LANES), qrow)),
            scratch_shapes=[pltpu.VMEM((nk, t, tk), I32)]),
        compiler_params=_cparams(("parallel", "arbitrary")),
        name=name,
    )(qi, kj, qh, kidx, widx)


def _dsa_attn_kernel(qi_ref, kj_ref, q_ref, k_ref, v_ref, sc_ref, thr_ref, jc_ref, bias_ref,
                     o_ref, m_ref, acc_ref, *, t, tk):
    p = pl.program_id(1)
    i, j = qi_ref[p], kj_ref[p]

    @pl.when(j == 0)
    def _():
        m_ref[...] = jnp.full(m_ref.shape, NEG, F32)
        acc_ref[...] = jnp.zeros_like(acc_ref)

    sc = sc_ref[...]
    thr = thr_ref[:, 0:1]
    col = j * tk + lax.broadcasted_iota(I32, (t, tk), 1)
    row = i * t + lax.broadcasted_iota(I32, (t, tk), 0)
    tie = jnp.logical_and(sc == thr, col <= jc_ref[:, 0:1])
    keep = jnp.logical_and(jnp.logical_or(sc > thr, tie), col <= row)
    offs = [i * t - (j * tk + u * t) for u in range(tk // t)]

    def run(with_bias):
        for g in range(N_DSA_HEADS // DSA_HEAD_GROUP):
            heads = [g * DSA_HEAD_GROUP + u for u in range(DSA_HEAD_GROUP)]
            logits = [_dot_t(q_ref[h], k_ref[h]) for h in heads]
            for h, s in zip(heads, logits):
                if with_bias:
                    parts = [jnp.where(off == 0, bias_ref[0, h],
                                       jnp.where(off == t, bias_ref[1, h], 0.0)) for off in offs]
                    s = s + (parts[0] if len(parts) == 1 else jnp.concatenate(parts, axis=1))
                s = jnp.where(keep, s, NEG)
                m_prev = m_ref[h]
                m_next = jnp.maximum(m_prev, jnp.max(s, axis=-1, keepdims=True))
                alpha = jnp.exp2(m_prev - m_next)
                p = jnp.exp2((s - jnp.concatenate([m_next] * (tk // HEAD_DIM), axis=1)).astype(BF16))
                acc_ref[h] = (jnp.concatenate([alpha, alpha], axis=1) * acc_ref[h]
                              + _dot(p, _with_ones(v_ref[h])))
                m_ref[h] = m_next

    near = offs[-1] <= t

    @pl.when(near)
    def _():
        run(True)

    @pl.when(jnp.logical_not(near))
    def _():
        run(False)

    @pl.when(j == (i * t + t - 1) // tk)
    def _():
        for h in range(N_DSA_HEADS):
            o_ref[:, _head_cols(h)] = _softmax_finish(acc_ref[h]).astype(o_ref.dtype)


def _dsa_attention(qh, kvh, scores, thr, jcut, bias, *, batch, seq, name):
    t, tk = ATTN_TILE, min(DSA_KEY_TILE, seq)
    nq, nk = seq // t, seq // tk
    h = N_DSA_HEADS
    qi, kj = _causal_pairs(seq, t, tk)
    qrow = lambda b, p, qi, kj: (b * nq + qi[p], 0)
    return pl.pallas_call(
        functools.partial(_dsa_attn_kernel, t=t, tk=tk),
        out_shape=jax.ShapeDtypeStruct((batch * seq, h * HEAD_DIM), BF16),
        grid_spec=pltpu.PrefetchScalarGridSpec(
            num_scalar_prefetch=2,
            grid=(batch, qi.shape[0]),
            in_specs=[pl.BlockSpec((h, t, HEAD_DIM), lambda b, p, qi, kj: (0, b * nq + qi[p], 0)),
                      pl.BlockSpec((h, tk, HEAD_DIM), lambda b, p, qi, kj: (0, b * nk + kj[p], 0)),
                      pl.BlockSpec((h, tk, HEAD_DIM), lambda b, p, qi, kj: (1, b * nk + kj[p], 0)),
                      pl.BlockSpec((None, None, t, tk), lambda b, p, qi, kj: (b, p, 0, 0)),
                      pl.BlockSpec((t, LANES), qrow),
                      pl.BlockSpec((t, LANES), qrow),
                      pl.BlockSpec((2, h, t, t), lambda b, p, qi, kj: (0, 0, 0, 0))],
            out_specs=pl.BlockSpec((t, h * HEAD_DIM), qrow),
            scratch_shapes=[pltpu.VMEM((h, t, HEAD_DIM), F32),
                            pltpu.VMEM((h, t, 2 * HEAD_DIM), F32)]),
        compiler_params=_cparams(("parallel", "arbitrary")),
        name=name,
    )(qi, kj, qh, kvh, kvh, scores, thr, jcut, bias)


def _route_kernel(lg_ref, info_ref, gate_ref, cnt_ref, carry_ref, *, tm):
    i = pl.program_id(0)

    @pl.when(i == 0)
    def _():
        carry_ref[...] = jnp.zeros_like(carry_ref)

    lane = lax.broadcasted_iota(I32, (tm, LANES), 1)
    lane_f = lane.astype(F32)
    lg = jnp.where(lane < N_EXPERTS, lg_ref[...], -jnp.inf)
    m1 = jnp.max(lg, axis=-1, keepdims=True)
    e1 = jnp.min(jnp.where(lg == m1, lane_f, float(LANES)), axis=-1, keepdims=True).astype(I32)
    lg2 = jnp.where(lane == e1, -jnp.inf, lg)
    m2 = jnp.max(lg2, axis=-1, keepdims=True)
    e2 = jnp.min(jnp.where(lg2 == m2, lane_f, float(LANES)), axis=-1, keepdims=True).astype(I32)
    ex = jnp.exp(m2 - m1)
    g1 = 1.0 / (1.0 + ex)
    g2 = ex / (1.0 + ex)
    onehot = jnp.where(jnp.logical_or(lane == e1, lane == e2), 1.0, 0.0)
    r = lax.broadcasted_iota(I32, (tm, tm), 0)
    c = lax.broadcasted_iota(I32, (tm, tm), 1)
    before = _dot(jnp.where(c < r, 1.0, 0.0).astype(BF16), onehot.astype(BF16)) + carry_ref[0:1, :]
    r1 = jnp.sum(jnp.where(lane == e1, before, 0.0), axis=-1, keepdims=True).astype(I32)
    r2 = jnp.sum(jnp.where(lane == e2, before, 0.0), axis=-1, keepdims=True).astype(I32)
    info = jnp.where(lane == 0, e1, jnp.where(lane == 1, e2, jnp.where(lane == 2, r1, r2)))
    info_ref[...] = info
    gate_ref[...] = jnp.where(lane == 0, g1, g2)
    total = carry_ref[0:1, :] + jnp.sum(onehot, axis=0, keepdims=True)
    carry_ref[...] = jnp.broadcast_to(total, carry_ref.shape)
    cnt_ref[...] = jnp.broadcast_to(total, cnt_ref.shape)


def _route(logits, *, tm=256, name):
    n = logits.shape[0]
    tm = min(tm, n)
    return pl.pallas_call(
        functools.partial(_route_kernel, tm=tm),
        out_shape=(jax.ShapeDtypeStruct((n, LANES), I32), jax.ShapeDtypeStruct((n, LANES), F32),
                   jax.ShapeDtypeStruct((8, LANES), F32)),
        grid=(n // tm,),
        in_specs=[pl.BlockSpec((tm, LANES), lambda i: (i, 0))],
        out_specs=(pl.BlockSpec((tm, LANES), lambda i: (i, 0)),
                   pl.BlockSpec((tm, LANES), lambda i: (i, 0)),
                   pl.BlockSpec((8, LANES), lambda i: (0, 0))),
        scratch_shapes=[pltpu.VMEM((8, LANES), F32)],
        compiler_params=_cparams(("arbitrary",)),
        name=name,
    )(logits)


def _combine_ln_kernel(d0_ref, d1_ref, h_ref, ys_ref, gate_ref, g_ref, b_ref, o_ref,
                       ya0_ref, ya1_ref, yb0_ref, yb1_ref, sems_a, sems_b, *, tm, slabs):
    i = pl.program_id(0)
    nt = pl.num_programs(0)
    bufs_a, bufs_b = (ya0_ref, ya1_ref), (yb0_ref, yb1_ref)

    def gather(tile, wait):
        for idx_ref, bufs, sems in ((d0_ref, bufs_a, sems_a), (d1_ref, bufs_b, sems_b)):
            _row_gather(ys_ref, idx_ref, tile * tm, bufs, sems, tile % 2, tm, slabs, wait=wait)

    @pl.when(i == 0)
    def _():
        gather(0, wait=False)

    gather(i, wait=True)

    @pl.when(i + 1 < nt)
    def _():
        gather(i + 1, wait=False)

    for s in range(2):
        @pl.when(i % 2 == s)
        def _(s=s):
            ff = (gate_ref[:, 0:1] * _slabs_to_rows(bufs_a[s], slabs)
                  + gate_ref[:, 1:2] * _slabs_to_rows(bufs_b[s], slabs))
            o_ref[...] = _layer_norm(DEEPNORM_ALPHA * h_ref[...] + ff, g_ref[...], b_ref[...])


def _combine_ln(dest0, dest1, h, ys, gates, g, b, *, tm=256, name):
    n, d = h.shape
    slabs = d // LANES
    tm = min(tm, n)
    row = lambda i, d0, d1: (i, 0)
    const = lambda i, d0, d1: (0, 0)
    slab_buf = pltpu.VMEM((tm * slabs, LANES), F32)
    return pl.pallas_call(
        functools.partial(_combine_ln_kernel, tm=tm, slabs=slabs),
        out_shape=jax.ShapeDtypeStruct((n, d), F32),
        grid_spec=pltpu.PrefetchScalarGridSpec(
            num_scalar_prefetch=2,
            grid=(n // tm,),
            in_specs=[pl.BlockSpec((tm, d), row),
                      pl.BlockSpec(memory_space=pl.ANY),
                      pl.BlockSpec((tm, LANES), row),
                      pl.BlockSpec((1, d), const),
                      pl.BlockSpec((1, d), const)],
            out_specs=pl.BlockSpec((tm, d), row),
            scratch_shapes=[slab_buf, slab_buf, slab_buf, slab_buf,
                            pltpu.SemaphoreType.DMA((2,)), pltpu.SemaphoreType.DMA((2,))]),
        compiler_params=_cparams(("arbitrary",)),
        name=name,
    )(dest0, dest1, h, ys, gates, g, b)


def _rel_bucket(dist):
    n = jnp.maximum(dist, 0)
    exact = REL_BUCKETS // 2
    nf = jnp.maximum(n, 1).astype(F32)
    large = exact + (jnp.log(nf / exact) / math.log(REL_MAX_DIST / exact) * (REL_BUCKETS - exact)).astype(I32)
    large = jnp.minimum(large, REL_BUCKETS - 1)
    return jnp.where(n < exact, n, large)


def _bias_tile_kernel(tab_ref, bucket_ref, o_ref):
    h = pl.program_id(1)
    bucket = bucket_ref[...]
    far = tab_ref[REL_BUCKETS - 1, h]
    acc = jnp.zeros(o_ref.shape, F32)
    for b in range(REL_BUCKETS - 1):
        acc = jnp.where(bucket == b, tab_ref[b, h] - far, acc)
    o_ref[...] = acc * LOG2E


def _bias_tiles(rel_table, n_heads, t, *, name):
    assert t >= REL_MAX_DIST
    r = jnp.arange(t)[:, None]
    c = jnp.arange(t)[None, :]
    buckets = jnp.stack([_rel_bucket(r - c + off) for off in (0, t)]).astype(I32)
    return pl.pallas_call(
        _bias_tile_kernel,
        out_shape=jax.ShapeDtypeStruct((2, n_heads, t, t), F32),
        grid=(2, n_heads),
        in_specs=[pl.BlockSpec(memory_space=pltpu.SMEM),
                  pl.BlockSpec((None, t, t), lambda k, h: (k, 0, 0))],
        out_specs=pl.BlockSpec((None, None, t, t), lambda k, h: (k, h, 0, 0)),
        compiler_params=_cparams(("parallel", "parallel")),
        name=name,
    )(rel_table.astype(F32), buckets)


def _pad_cols(w, n):
    return jnp.pad(w, ((0, 0), (0, n - w.shape[1])))


def _even_layer(h, rel_table, w_in, b_forget, w_out, ln1_g, ln1_b, w1, w3, w2, ln2_g, ln2_b,
                *, batch, seq):
    d = h.shape[1]
    wa = N_MOBA_HEADS * HEAD_DIM
    wb = N_FOX_HEADS * HEAD_DIM
    n_qkv = 3 * wa + 3 * wb
    scale = HEAD_DIM ** -0.5 * LOG2E
    ones, scl = jnp.ones((wa,), F32), jnp.full((wa,), scale, F32)
    colscale = jnp.concatenate([scl, ones, ones, scl, ones, ones])[None, :]
    qkv = _mm(h, w_in[:, :n_qkv].astype(BF16), colscale, out_dtype=BF16, tm=1024, tn=768,
              name="ev_qkv_proj")
    fb = _mm(h, _pad_cols(w_in[:, n_qkv:], LANES).astype(BF16), jnp.ones((1, LANES), F32),
             out_dtype=F32, tn=LANES, name="ev_forget_proj")
    csum = _forget_csum(fb, _pad_cols(b_forget.astype(F32)[None, :], LANES), batch=batch, seq=seq,
                        n_heads=N_FOX_HEADS, name="ev_forget_csum")
    csum = csum.reshape(batch, N_FOX_HEADS, seq // ATTN_TILE, 1, ATTN_TILE)
    oa = _moba_attention(qkv, _bias_tiles(rel_table, N_MOBA_HEADS, MOBA_BLOCK, name="ev_bias_tiles"),
                         batch=batch, seq=seq, name="ev_moba_attn")
    ob = _fox_attention(qkv, csum, batch=batch, seq=seq, name="ev_fox_attn")
    attn = jnp.concatenate([oa, ob], axis=-1)
    h = _mm_ln(attn, w_out.astype(BF16), h, ln1_g[None, :], ln1_b[None, :], name="ev_out_proj_ln")
    return _ffn_ln(h, w1.astype(BF16), w3.astype(BF16), w2.astype(BF16),
                   ln2_g[None, :], ln2_b[None, :], name="ev_swiglu_ln")


def _moe(h, h_slabs, logits, w1, w3, w2, ln_g, ln_b):
    n, d = h.shape
    tm = EXPERT_TILE
    info, gates, cnt = _route(logits, name="od_route")
    counts = cnt[0, :N_EXPERTS].astype(I32)
    padded = (counts + tm - 1) // tm * tm
    pend = jnp.cumsum(padded)
    pstart = pend - padded
    e = info[:, :MOE_TOPK]
    dest = (pstart[e] + info[:, MOE_TOPK:2 * MOE_TOPK]).astype(I32)
    n_rows = -(-(n * MOE_TOPK + N_EXPERTS * (tm - 1)) // tm) * tm
    n_tiles = n_rows // tm
    tile_start = jnp.arange(n_tiles, dtype=I32) * tm
    tile_valid = (tile_start < pend[-1]).astype(I32)
    last = jnp.maximum(pend[-1] - 1, 0)
    tile_e = jnp.minimum(jnp.searchsorted(pend, jnp.minimum(tile_start, last), side="right"),
                         N_EXPERTS - 1).astype(I32)
    tok = jnp.arange(n, dtype=I32)
    row_tok = jnp.zeros((n_rows,), I32).at[dest.T.reshape(-1)].set(jnp.concatenate([tok, tok]))
    ys = _moe_ffn(tile_e, tile_valid, row_tok, h_slabs, w1.astype(BF16),
                  w3.astype(BF16), w2.astype(BF16), tm=tm, name="od_moe_swiglu")
    return _combine_ln(dest[:, 0], dest[:, 1], h, ys, gates, ln_g[None, :], ln_b[None, :],
                       name="od_moe_combine_ln")


def _odd_layer(h, rel_table, w_in, q_norm_g, kv_norm_g, w_uq, w_qidx, w_uk, w_uv, w_out,
               ln1_g, ln1_b, router, w1, w3, w2, ln2_g, ln2_b, *, batch, seq):
    nh = N_DSA_HEADS
    rq, rkv = DSA_Q_RANK, DSA_KV_RANK
    scale = HEAD_DIM ** -0.5 * LOG2E
    w_in_p = jnp.concatenate([w_in[:, :rq + rkv],
                              _pad_cols(w_in[:, rq + rkv:rq + rkv + IDX_DIM], LANES),
                              _pad_cols(w_in[:, rq + rkv + IDX_DIM:], LANES)], axis=1)
    cq, ckv, kidx, widx = _dsa_in_proj(h, w_in_p.astype(BF16), q_norm_g[None, :], kv_norm_g[None, :],
                                       name="od_in_proj_rms")
    w_qidx_p = jnp.pad(w_qidx.reshape(rq, IDX_HEADS, IDX_DIM), ((0, 0), (0, 0), (0, LANES - IDX_DIM)))
    wq = jnp.concatenate([w_uq, w_qidx_p.reshape(rq, IDX_HEADS * LANES)], axis=1)
    qscale = jnp.concatenate([jnp.full((nh * HEAD_DIM,), scale, F32),
                              jnp.ones((IDX_HEADS * LANES,), F32)])[None, :]
    qh = _mm(cq, wq.astype(BF16), qscale, out_dtype=BF16, head_major=True, tm=512, tn=2048,
             name="od_q_proj")
    wkv = jnp.concatenate([w_uk.transpose(1, 0, 2).reshape(rkv, nh * HEAD_DIM),
                           w_uv.transpose(1, 0, 2).reshape(rkv, nh * HEAD_DIM)], axis=1)
    kvh = _mm(ckv, wkv.astype(BF16), jnp.ones((1, 2 * nh * HEAD_DIM), F32), out_dtype=BF16,
              head_major=True, tm=512, tn=2048, name="od_kv_proj")
    scores, thr, jcut = _idx_topk(qh, kidx, widx, batch=batch, seq=seq,
                                  ksel=min(DSA_TOPK_MAX, seq // 4), name="od_idx_topk")
    attn = _dsa_attention(qh, kvh, scores, thr, jcut,
                          _bias_tiles(rel_table, nh, ATTN_TILE, name="od_bias_tiles"),
                          batch=batch, seq=seq, name="od_dsa_attn")
    h, h_slabs, logits = _mm_ln_route(attn, w_out.astype(BF16), h, ln1_g[None, :], ln1_b[None, :],
                                      _pad_cols(router, LANES).astype(BF16), name="od_out_proj_ln")
    return _moe(h, h_slabs, logits, w1, w3, w2, ln2_g, ln2_b)


def kernel(x, rel_table, ev_w_in, ev_b_forget, ev_w_out, ev_ln1_g, ev_ln1_b, ev_ffn_w1, ev_ffn_w3, ev_ffn_w2, ev_ln2_g, ev_ln2_b, od_w_in, od_q_norm_g, od_kv_norm_g, od_w_uq, od_w_qidx, od_w_uk, od_w_uv, od_w_out, od_ln1_g, od_ln1_b, od_router, od_exp_w1, od_exp_w3, od_exp_w2, od_ln2_g, od_ln2_b):
    batch, seq, d = x.shape
    h = x.reshape(batch * seq, d)
    for layer in range(DEPTH):
        i = layer // 2
        if layer % 2 == 0:
            h = _even_layer(h, rel_table, ev_w_in[i], ev_b_forget[i], ev_w_out[i], ev_ln1_g[i],
                            ev_ln1_b[i], ev_ffn_w1[i], ev_ffn_w3[i], ev_ffn_w2[i], ev_ln2_g[i],
                            ev_ln2_b[i], batch=batch, seq=seq)
        else:
            h = _odd_layer(h, rel_table, od_w_in[i], od_q_norm_g[i], od_kv_norm_g[i], od_w_uq[i],
                           od_w_qidx[i], od_w_uk[i], od_w_uv[i], od_w_out[i], od_ln1_g[i],
                           od_ln1_b[i], od_router[i], od_exp_w1[i], od_exp_w3[i], od_exp_w2[i],
                           od_ln2_g[i], od_ln2_b[i], batch=batch, seq=seq)
    return h.reshape(batch, seq, d)
```

```python
import functools
import math

import jax
import jax.numpy as jnp
from jax import lax
from jax.experimental import pallas as pl
from jax.experimental.pallas import tpu as pltpu

F32 = jnp.float32
BF16 = jnp.bfloat16
I32 = jnp.int32

HEAD_DIM = 128
N_MOBA_HEADS = 8
N_FOX_HEADS = 8
MOBA_BLOCK = 256
MOBA_TOPK = 3
N_DSA_HEADS = 16
DSA_Q_RANK = 512
DSA_KV_RANK = 512
IDX_HEADS = 16
IDX_DIM = 64
DSA_TOPK_MAX = 256
REL_BUCKETS = 32
REL_MAX_DIST = 128
N_EXPERTS = 8
MOE_TOPK = 2
LN_EPS = 1e-5
RMS_EPS = 1e-6
DEPTH = 2
DEEPNORM_ALPHA = (2 * DEPTH) ** 0.25

LANES = 128
ATTN_TILE = 256
WIDE_CHUNK = 1024
DSA_KEY_TILE = 512
DSA_HEAD_GROUP = 4
COUNT_STRIP = 128
EXPERT_TILE = 512
ROW_DMA_UNROLL = 8
NEG = -1e30
LOG2E = math.log2(math.e)
INT_MIN = -(2 ** 31)
VMEM_LIMIT = 56 * 1024 * 1024


def _cparams(sem, vmem=VMEM_LIMIT):
    return pltpu.CompilerParams(dimension_semantics=sem, vmem_limit_bytes=vmem)


def _dot(a, b):
    return jnp.dot(a, b, preferred_element_type=F32)


def _dot_t(a, b):
    return lax.dot_general(a, b, (((1,), (1,)), ((), ())), preferred_element_type=F32)


def _layer_norm(y, g, b):
    mu = jnp.mean(y, axis=-1, keepdims=True)
    d = y - mu
    var = jnp.mean(d * d, axis=-1, keepdims=True)
    return d * lax.rsqrt(var + LN_EPS) * g + b


def _mm_kernel(x_ref, w_ref, cs_ref, o_ref, *, head_major):
    acc = _dot(x_ref[...].astype(BF16), w_ref[...]) * cs_ref[...]
    if head_major:
        for c in range(o_ref.shape[0]):
            o_ref[c] = acc[:, c * LANES:(c + 1) * LANES].astype(o_ref.dtype)
    else:
        o_ref[...] = acc.astype(o_ref.dtype)


def _mm(x, w, colscale, *, out_dtype, head_major=False, tm=512, tn=512, name):
    m, k = x.shape
    n = w.shape[1]
    tm, tn = min(tm, m), min(tn, n)
    assert m % tm == 0 and n % tn == 0 and tn % LANES == 0
    if head_major:
        out_shape = jax.ShapeDtypeStruct((n // LANES, m, LANES), out_dtype)
        out_spec = pl.BlockSpec((tn // LANES, tm, LANES), lambda i, j: (j, i, 0))
    else:
        out_shape = jax.ShapeDtypeStruct((m, n), out_dtype)
        out_spec = pl.BlockSpec((tm, tn), lambda i, j: (i, j))
    return pl.pallas_call(
        functools.partial(_mm_kernel, head_major=head_major),
        out_shape=out_shape,
        grid=(m // tm, n // tn),
        in_specs=[pl.BlockSpec((tm, k), lambda i, j: (i, 0)),
                  pl.BlockSpec((k, tn), lambda i, j: (0, j)),
                  pl.BlockSpec((1, tn), lambda i, j: (0, j))],
        out_specs=out_spec,
        compiler_params=_cparams(("parallel", "arbitrary")),
        name=name,
    )(x, w, colscale)


def _mm_ln_kernel(x_ref, w_ref, res_ref, g_ref, b_ref, o_ref):
    y = DEEPNORM_ALPHA * res_ref[...] + _dot(x_ref[...], w_ref[...])
    o_ref[...] = _layer_norm(y, g_ref[...], b_ref[...])


def _mm_ln(x, w, res, g, b, *, tm=256, name):
    m, k = x.shape
    d = w.shape[1]
    tm = min(tm, m)
    return pl.pallas_call(
        _mm_ln_kernel,
        out_shape=jax.ShapeDtypeStruct((m, d), F32),
        grid=(m // tm,),
        in_specs=[pl.BlockSpec((tm, k), lambda i: (i, 0)),
                  pl.BlockSpec((k, d), lambda i: (0, 0)),
                  pl.BlockSpec((tm, d), lambda i: (i, 0)),
                  pl.BlockSpec((1, d), lambda i: (0, 0)),
                  pl.BlockSpec((1, d), lambda i: (0, 0))],
        out_specs=pl.BlockSpec((tm, d), lambda i: (i, 0)),
        compiler_params=_cparams(("parallel",)),
        name=name,
    )(x, w, res, g, b)


def _mm_ln_route_kernel(x_ref, w_ref, res_ref, g_ref, b_ref, wr_ref, o_ref, slab_ref, lg_ref,
                        *, slabs):
    y = _layer_norm(DEEPNORM_ALPHA * res_ref[...] + _dot(x_ref[...], w_ref[...]),
                    g_ref[...], b_ref[...])
    o_ref[...] = y
    tm = y.shape[0]
    for c in range(slabs):
        slab_ref[pl.ds(c, tm, stride=slabs), :] = y[:, c * LANES:(c + 1) * LANES]
    lg_ref[...] = _dot(y.astype(BF16), wr_ref[...])


def _mm_ln_route(x, w, res, g, b, w_router, *, tm=256, name):
    m, k = x.shape
    d = w.shape[1]
    tm = min(tm, m)
    slabs = d // LANES
    row = lambda i: (i, 0)
    const = lambda i: (0, 0)
    return pl.pallas_call(
        functools.partial(_mm_ln_route_kernel, slabs=slabs),
        out_shape=(jax.ShapeDtypeStruct((m, d), F32), jax.ShapeDtypeStruct((m * slabs, LANES), F32),
                   jax.ShapeDtypeStruct((m, LANES), F32)),
        grid=(m // tm,),
        in_specs=[pl.BlockSpec((tm, k), row), pl.BlockSpec((k, d), const), pl.BlockSpec((tm, d), row),
                  pl.BlockSpec((1, d), const), pl.BlockSpec((1, d), const),
                  pl.BlockSpec((d, LANES), const)],
        out_specs=(pl.BlockSpec((tm, d), row), pl.BlockSpec((tm * slabs, LANES), row),
                   pl.BlockSpec((tm, LANES), row)),
        compiler_params=_cparams(("parallel",)),
        name=name,
    )(x, w, res, g, b, w_router)


def _swiglu_step(xb, w1_ref, w3_ref, w2_ref):
    a = _dot(xb, w1_ref[...])
    c = _dot(xb, w3_ref[...])
    hmid = a / (1.0 + jnp.exp(-a)) * c
    return _dot(hmid.astype(BF16), w2_ref[...])


def _ffn_ln_kernel(x_ref, w1_ref, w3_ref, w2_ref, g_ref, b_ref, o_ref, acc_ref, xb_ref):
    f = pl.program_id(1)

    @pl.when(f == 0)
    def _():
        acc_ref[...] = jnp.zeros_like(acc_ref)
        xb_ref[...] = x_ref[...].astype(BF16)

    acc_ref[...] += _swiglu_step(xb_ref[...], w1_ref, w3_ref, w2_ref)

    @pl.when(f == pl.num_programs(1) - 1)
    def _():
        y = DEEPNORM_ALPHA * x_ref[...] + acc_ref[...]
        o_ref[...] = _layer_norm(y, g_ref[...], b_ref[...])


def _ffn_ln(x, w1, w3, w2, g, b, *, tm=512, tf=512, name):
    m, d = x.shape
    dff = w1.shape[1]
    tm, tf = min(tm, m), min(tf, dff)
    assert m % tm == 0 and dff % tf == 0
    return pl.pallas_call(
        _ffn_ln_kernel,
        out_shape=jax.ShapeDtypeStruct((m, d), F32),
        grid=(m // tm, dff // tf),
        in_specs=[pl.BlockSpec((tm, d), lambda i, f: (i, 0)),
                  pl.BlockSpec((d, tf), lambda i, f: (0, f)),
                  pl.BlockSpec((d, tf), lambda i, f: (0, f)),
                  pl.BlockSpec((tf, d), lambda i, f: (f, 0)),
                  pl.BlockSpec((1, d), lambda i, f: (0, 0)),
                  pl.BlockSpec((1, d), lambda i, f: (0, 0))],
        out_specs=pl.BlockSpec((tm, d), lambda i, f: (i, 0)),
        scratch_shapes=[pltpu.VMEM((tm, d), F32), pltpu.VMEM((tm, d), BF16)],
        compiler_params=_cparams(("parallel", "arbitrary")),
        name=name,
    )(x, w1, w3, w2, g, b)


def _row_gather(src_hbm, idx_ref, base, bufs, sems, slot, n_rows, slabs, *, wait):
    for s, buf in enumerate(bufs):
        @pl.when(slot == s)
        def _(s=s, buf=buf):
            def body(r, _):
                src = src_hbm.at[pl.ds(pl.multiple_of(idx_ref[base + r] * slabs, slabs), slabs)]
                dst = buf.at[pl.ds(pl.multiple_of(r * slabs, slabs), slabs)]
                cp = pltpu.make_async_copy(src, dst, sems.at[s])
                if wait:
                    cp.wait()
                else:
                    cp.start()
                return 0
            lax.fori_loop(0, n_rows, body, 0, unroll=ROW_DMA_UNROLL)


def _slabs_to_rows(buf, slabs):
    rows = buf.shape[0] // slabs
    return jnp.concatenate([buf[pl.ds(c, rows, stride=slabs), :] for c in range(slabs)], axis=1)


def _moe_ffn_kernel(te_ref, tv_ref, tok_ref, h_ref, w1_ref, w3_ref, w2_ref, o_ref,
                    acc_ref, xb_ref, xg0_ref, xg1_ref, sems, *, tm, slabs):
    i, f = pl.program_id(0), pl.program_id(1)
    nt = pl.num_programs(0)
    bufs = (xg0_ref, xg1_ref)
    gather = functools.partial(_row_gather, h_ref, tok_ref, bufs=bufs, sems=sems, n_rows=tm,
                               slabs=slabs)

    @pl.when(tv_ref[i] > 0)
    def _():
        @pl.when(f == 0)
        def _():
            @pl.when(i == 0)
            def _():
                gather(base=0, slot=0, wait=False)

            gather(base=i * tm, slot=i % 2, wait=True)
            nxt = jnp.minimum(i + 1, nt - 1)

            @pl.when(jnp.logical_and(i + 1 < nt, tv_ref[nxt] > 0))
            def _():
                gather(base=nxt * tm, slot=nxt % 2, wait=False)

            for s, buf in enumerate(bufs):
                @pl.when(i % 2 == s)
                def _(buf=buf):
                    xb_ref[...] = _slabs_to_rows(buf, slabs).astype(BF16)

            acc_ref[...] = jnp.zeros_like(acc_ref)

        acc_ref[...] += _swiglu_step(xb_ref[...], w1_ref, w3_ref, w2_ref)

        @pl.when(f == pl.num_programs(1) - 1)
        def _():
            for c in range(slabs):
                o_ref[pl.ds(c, tm, stride=slabs), :] = acc_ref[:, c * LANES:(c + 1) * LANES]

    @pl.when(tv_ref[i] == 0)
    def _():
        o_ref[...] = jnp.zeros_like(o_ref)


def _moe_ffn(tile_e, tile_valid, row_tok, h_slabs, w1, w3, w2, *, tm, tf=1024, name):
    n_rows = row_tok.shape[0]
    d = w1.shape[1]
    slabs = d // LANES
    dff = w1.shape[2]
    tf = min(tf, dff)
    nf = dff // tf
    assert n_rows % tm == 0 and dff % tf == 0

    def fidx(i, f, tv):
        return jnp.where(tv[i] > 0, f, nf - 1)

    return pl.pallas_call(
        functools.partial(_moe_ffn_kernel, tm=tm, slabs=slabs),
        out_shape=jax.ShapeDtypeStruct((n_rows * slabs, LANES), F32),
        grid_spec=pltpu.PrefetchScalarGridSpec(
            num_scalar_prefetch=3,
            grid=(n_rows // tm, nf),
            in_specs=[pl.BlockSpec(memory_space=pl.ANY),
                      pl.BlockSpec((None, d, tf), lambda i, f, te, tv, tok: (te[i], 0, fidx(i, f, tv))),
                      pl.BlockSpec((None, d, tf), lambda i, f, te, tv, tok: (te[i], 0, fidx(i, f, tv))),
                      pl.BlockSpec((None, tf, d), lambda i, f, te, tv, tok: (te[i], fidx(i, f, tv), 0))],
            out_specs=pl.BlockSpec((tm * slabs, LANES), lambda i, f, te, tv, tok: (i, 0)),
            scratch_shapes=[pltpu.VMEM((tm, d), F32), pltpu.VMEM((tm, d), BF16),
                            pltpu.VMEM((tm * slabs, LANES), F32), pltpu.VMEM((tm * slabs, LANES), F32),
                            pltpu.SemaphoreType.DMA((2,))]),
        compiler_params=_cparams(("arbitrary", "arbitrary")),
        name=name,
    )(tile_e, tile_valid, row_tok, h_slabs, w1, w3, w2)


def _online_softmax_step(s, v, carry):
    m, acc = carry
    m_new = jnp.maximum(m, jnp.max(s, axis=-1, keepdims=True))
    alpha = jnp.exp2(m - m_new)
    p = jnp.exp2((s - m_new).astype(BF16))
    return m_new, alpha * acc + _dot(p, _with_ones(v))


def _with_ones(v):
    return jnp.concatenate([v, jnp.ones_like(v)], axis=1)


def _softmax_init(tq):
    return jnp.full((tq, 1), NEG, F32), jnp.zeros((tq, 2 * HEAD_DIM), F32)


def _softmax_finish(acc):
    return acc[:, :HEAD_DIM] / acc[:, HEAD_DIM:]


def _pairwise_loop(n, body, carry):
    carry = lax.fori_loop(0, n // 2, lambda k, c: body(2 * k + 1, body(2 * k, c)), carry)
    return lax.fori_loop(n // 2 * 2, n, body, carry)


def _head_cols(h):
    return slice(h * HEAD_DIM, (h + 1) * HEAD_DIM)


def _kv_block(k_ref, v_ref, n, width, h):
    rows = pl.ds(pl.multiple_of(n * width, width), width)
    return k_ref[rows, _head_cols(h)], v_ref[rows, _head_cols(h)]


def _causal_mask(blk):
    r = lax.broadcasted_iota(I32, (blk, blk), 0)
    c = lax.broadcasted_iota(I32, (blk, blk), 1)
    return c <= r


def _moba_select(q, km, i, blk):
    nbp = km.shape[0]
    gate = _dot_t(q, km.astype(BF16))
    lane = lax.broadcasted_iota(I32, (blk, nbp), 1)
    lane_f = lane.astype(F32)
    g = jnp.where(lane < i, gate, -jnp.inf)
    sel = jnp.zeros((blk, nbp), F32)
    for _ in range(MOBA_TOPK):
        mx = jnp.max(g, axis=-1, keepdims=True)
        first = jnp.min(jnp.where(g == mx, lane_f, float(nbp)), axis=-1, keepdims=True)
        pick = jnp.logical_and(lane_f == first, mx > -jnp.inf)
        sel = jnp.where(pick, 1.0, sel)
        g = jnp.where(pick, -jnp.inf, g)
    return sel, lane


def _moba_kernel(q_ref, k_ref, v_ref, bias_ref, o_ref, km_ref, *, blk, nb, wide, hp):
    i = pl.program_id(2)
    per = wide // blk

    @pl.when(i == 0)
    def _():
        km_ref[...] = jnp.zeros_like(km_ref)

        def mean_body(n, _):
            for h in range(hp):
                kblk, _ = _kv_block(k_ref, v_ref, n, blk, h)
                km_ref[h, pl.ds(n, 1), :] = jnp.mean(kblk.astype(F32), axis=0, keepdims=True)
            return 0

        lax.fori_loop(0, nb, mean_body, 0)

    qs = [q_ref[:, _head_cols(h)] for h in range(hp)]
    sels = [_moba_select(qs[h], km_ref[h], i, blk) for h in range(hp)]

    def scores(n):
        return [_dot_t(qs[h], _kv_block(k_ref, v_ref, n, wide, h)[0]) for h in range(hp)]

    def update(n, logits, carries, near):
        causal = _causal_mask(blk)
        out = []
        for h in range(hp):
            s, vblk = logits[h], _kv_block(k_ref, v_ref, n, wide, h)[1]
            sel, lane = sels[h]
            parts = []
            for u in range(per):
                blk_id = n * per + u
                part = s[:, u * blk:(u + 1) * blk]
                chosen = jnp.sum(jnp.where(lane == blk_id, sel, 0.0), axis=-1, keepdims=True) > 0.0
                if near:
                    part = part + jnp.where(blk_id == i, bias_ref[0, h],
                                            jnp.where(blk_id == i - 1, bias_ref[1, h], 0.0))
                    parts.append(jnp.where(blk_id == i, jnp.where(causal, part, NEG),
                                           jnp.where(chosen, part, NEG)))
                else:
                    parts.append(jnp.where(chosen, part, NEG))
            s = parts[0] if per == 1 else jnp.concatenate(parts, axis=1)
            out.append(_online_softmax_step(s, vblk, carries[h]))
        return tuple(out)

    first_near = jnp.maximum(i - 1, 0) // per
    carries = tuple(_softmax_init(blk) for _ in range(hp))
    carries = _pairwise_loop(first_near, lambda n, c: update(n, scores(n), c, False), carries)
    carries = lax.fori_loop(first_near, i // per + 1,
                            lambda n, c: update(n, scores(n), c, True), carries)
    for h in range(hp):
        o_ref[:, _head_cols(h)] = _softmax_finish(carries[h][1]).astype(o_ref.dtype)


def _moba_attention(qkv, bias, *, batch, seq, hp=4, name):
    blk = MOBA_BLOCK
    nb = seq // blk
    nbp = -(-nb // LANES) * LANES
    h = N_MOBA_HEADS
    nq = seq // blk
    hg = h // hp
    w = hp * HEAD_DIM
    return pl.pallas_call(
        functools.partial(_moba_kernel, blk=blk, nb=nb, wide=min(WIDE_CHUNK, seq), hp=hp),
        out_shape=jax.ShapeDtypeStruct((batch * seq, h * HEAD_DIM), BF16),
        grid=(batch, hg, nq),
        in_specs=[pl.BlockSpec((blk, w), lambda b, g, i: (b * nq + i, g)),
                  pl.BlockSpec((seq, w), lambda b, g, i: (b, hg + g)),
                  pl.BlockSpec((seq, w), lambda b, g, i: (b, 2 * hg + g)),
                  pl.BlockSpec((2, hp, blk, blk), lambda b, g, i: (0, g, 0, 0))],
        out_specs=pl.BlockSpec((blk, w), lambda b, g, i: (b * nq + i, g)),
        scratch_shapes=[pltpu.VMEM((hp, nbp, HEAD_DIM), F32)],
        compiler_params=_cparams(("parallel", "parallel", "arbitrary")),
        name=name,
    )(qkv, qkv, qkv, bias)


def _fox_kernel(q_ref, k_ref, v_ref, ck_ref, o_ref, *, blk, wide, hp):
    i = pl.program_id(2)
    per = wide // blk
    qs = [q_ref[:, _head_cols(h)] for h in range(hp)]

    def scores(n):
        return [_dot_t(qs[h], _kv_block(k_ref, v_ref, n, wide, h)[0]) for h in range(hp)]

    def update(n, logits, carries, causal):
        if causal:
            row = i * blk + lax.broadcasted_iota(I32, (blk, wide), 0)
            col = n * wide + lax.broadcasted_iota(I32, (blk, wide), 1)
            visible = col <= row
        out = []
        for h in range(hp):
            ck = [ck_ref[h, n * per + u] for u in range(per)]
            s = logits[h] - (ck[0] if per == 1 else jnp.concatenate(ck, axis=1))
            if causal:
                s = jnp.where(visible, s, NEG)
            out.append(_online_softmax_step(s, _kv_block(k_ref, v_ref, n, wide, h)[1], carries[h]))
        return tuple(out)

    n_wide = i // per
    carries = tuple(_softmax_init(blk) for _ in range(hp))
    carries = _pairwise_loop(n_wide, lambda n, c: update(n, scores(n), c, False), carries)
    carries = update(n_wide, scores(n_wide), carries, True)
    for h in range(hp):
        o_ref[:, _head_cols(h)] = _softmax_finish(carries[h][1]).astype(o_ref.dtype)


def _fox_attention(qkv, csum, *, batch, seq, hp=4, name):
    blk = ATTN_TILE
    h = N_FOX_HEADS
    nq = seq // blk
    hg = h // hp
    base = 3 * N_MOBA_HEADS // hp
    w = hp * HEAD_DIM
    return pl.pallas_call(
        functools.partial(_fox_kernel, blk=blk, wide=min(WIDE_CHUNK, seq), hp=hp),
        out_shape=jax.ShapeDtypeStruct((batch * seq, h * HEAD_DIM), BF16),
        grid=(batch, hg, nq),
        in_specs=[pl.BlockSpec((blk, w), lambda b, g, i: (b * nq + i, base + g)),
                  pl.BlockSpec((seq, w), lambda b, g, i: (b, base + hg + g)),
                  pl.BlockSpec((seq, w), lambda b, g, i: (b, base + 2 * hg + g)),
                  pl.BlockSpec((None, hp, nq, 1, blk), lambda b, g, i: (b, g, 0, 0, 0))],
        out_specs=pl.BlockSpec((blk, w), lambda b, g, i: (b * nq + i, g)),
        compiler_params=_cparams(("parallel", "parallel", "arbitrary")),
        name=name,
    )(qkv, qkv, qkv, csum)


def _forget_csum_kernel(fb_ref, b_ref, o_ref, carry_ref, *, tc):
    @pl.when(pl.program_id(1) == 0)
    def _():
        carry_ref[...] = jnp.zeros_like(carry_ref)

    z = fb_ref[...] + b_ref[...]
    run = jnp.minimum(z, 0.0) - jnp.log1p(jnp.exp(-jnp.abs(z)))
    row = lax.broadcasted_iota(I32, (tc, LANES), 0)
    shift = 1
    while shift < tc:
        run = run + jnp.where(row >= shift, pltpu.roll(run, shift, axis=0), 0.0)
        shift *= 2
    run = run + carry_ref[0:1, :]
    carry_ref[...] = jnp.broadcast_to(run[tc - 1:tc, :], carry_ref.shape)
    o_ref[...] = (run * LOG2E).T[:o_ref.shape[0], :]


def _forget_csum(fb, bias, *, batch, seq, n_heads, tc=512, name):
    tc = min(tc, seq)
    nt = seq // tc
    return pl.pallas_call(
        functools.partial(_forget_csum_kernel, tc=tc),
        out_shape=jax.ShapeDtypeStruct((batch, n_heads, seq), F32),
        grid=(batch, nt),
        in_specs=[pl.BlockSpec((tc, LANES), lambda b, j: (b * nt + j, 0)),
                  pl.BlockSpec((1, LANES), lambda b, j: (0, 0))],
        out_specs=pl.BlockSpec((None, n_heads, tc), lambda b, j: (b, 0, j)),
        scratch_shapes=[pltpu.VMEM((8, LANES), F32)],
        compiler_params=_cparams(("parallel", "arbitrary")),
        name=name,
    )(fb, bias)


def _dsa_in_kernel(x_ref, w_ref, gq_ref, gkv_ref, cq_ref, ckv_ref, kidx_ref, widx_ref):
    acc = _dot(x_ref[...].astype(BF16), w_ref[...])
    rq, rkv = DSA_Q_RANK, DSA_KV_RANK

    def rms(z, g):
        return z * lax.rsqrt(jnp.mean(z * z, axis=-1, keepdims=True) + RMS_EPS) * g

    cq_ref[...] = rms(acc[:, :rq], gq_ref[...]).astype(cq_ref.dtype)
    ckv_ref[...] = rms(acc[:, rq:rq + rkv], gkv_ref[...]).astype(ckv_ref.dtype)
    kidx_ref[...] = acc[:, rq + rkv:rq + rkv + LANES].astype(kidx_ref.dtype)
    widx_ref[...] = acc[:, rq + rkv + LANES:] * (IDX_HEADS ** -0.5 * IDX_DIM ** -0.5)


def _dsa_in_proj(x, w, gq, gkv, *, tm=512, name):
    m, k = x.shape
    n = w.shape[1]
    tm = min(tm, m)
    rq, rkv = DSA_Q_RANK, DSA_KV_RANK
    row = lambda i: (i, 0)
    const = lambda i: (0, 0)
    return pl.pallas_call(
        _dsa_in_kernel,
        out_shape=(jax.ShapeDtypeStruct((m, rq), BF16), jax.ShapeDtypeStruct((m, rkv), BF16),
                   jax.ShapeDtypeStruct((m, LANES), BF16), jax.ShapeDtypeStruct((m, LANES), F32)),
        grid=(m // tm,),
        in_specs=[pl.BlockSpec((tm, k), row), pl.BlockSpec((k, n), const),
                  pl.BlockSpec((1, rq), const), pl.BlockSpec((1, rkv), const)],
        out_specs=(pl.BlockSpec((tm, rq), row), pl.BlockSpec((tm, rkv), row),
                   pl.BlockSpec((tm, LANES), row), pl.BlockSpec((tm, LANES), row)),
        compiler_params=_cparams(("parallel",)),
        name=name,
    )(x, w, gq, gkv)


def _fold_lanes(x):
    part = x[:, :LANES]
    for g in range(1, x.shape[1] // LANES):
        part = part + x[:, g * LANES:(g + 1) * LANES]
    return part


def _idx_topk_kernel(qi_ref, kj_ref, q_ref, k_ref, w_ref, sc_ref, thr_ref, jc_ref, keys_ref,
                     *, t, tk, ksel, seq):
    p = pl.program_id(1)
    i, j = qi_ref[p], kj_ref[p]
    k = k_ref[...]
    acc = jnp.zeros((t, tk), F32)
    for h in range(IDX_HEADS):
        acc = acc + jnp.maximum(_dot_t(q_ref[h], k), 0.0) * w_ref[:, h:h + 1]
    sc_ref[...] = acc

    row = i * t + lax.broadcasted_iota(I32, (t, tk), 0)
    lane_col = lax.broadcasted_iota(I32, (t, tk), 1)
    bits = pltpu.bitcast(acc, I32)
    key = jnp.where(bits < 0, bits ^ jnp.int32(0x7FFFFFFF), bits)
    key = jnp.where(bits == jnp.int32(INT_MIN), 0, key)
    keys_ref[j] = jnp.where(j * tk + lane_col <= row, key, jnp.int32(INT_MIN))
    j_last = (i * t + t - 1) // tk

    @pl.when(j == j_last)
    def _():
        lane = lax.broadcasted_iota(I32, (COUNT_STRIP, LANES), 1)
        strips = [slice(s0, s0 + COUNT_STRIP) for s0 in range(0, t, COUNT_STRIP)]

        def count(ref, pred, *row_args):
            dt = ref.dtype
            reps = [[jnp.broadcast_to(a[rows], (COUNT_STRIP, LANES)).astype(dt) for a in row_args]
                    for rows in strips]
            cnts = []
            for rows, args in zip(strips, reps):
                def body(c, cnt, rows=rows, args=args):
                    for g in range(tk // LANES):
                        keys = ref[c, rows, g * LANES:(g + 1) * LANES]
                        cnt = cnt + pred(keys, c * tk + g * LANES, *args).astype(dt)
                    return cnt

                cnts.append(_pairwise_loop(j_last + 1, body, jnp.zeros((COUNT_STRIP, LANES), dt)))
            cnt = jnp.concatenate(cnts, axis=0)
            return jnp.sum(cnt.astype(F32), axis=-1, keepdims=True).astype(I32)

        ans = jnp.where(count(keys_ref, lambda k, c0: k >= 0) >= ksel, 0, jnp.int32(INT_MIN))

        def bit_body(b, ans):
            cand = ans + lax.shift_left(jnp.int32(1), 30 - b)
            return jnp.where(count(keys_ref, lambda k, c0, cd: k >= cd, cand) >= ksel, cand, ans)

        ans = lax.fori_loop(0, 31, bit_body, ans)
        need = ksel - count(keys_ref, lambda k, c0, a: k > a, ans)
        ties = count(keys_ref, lambda k, c0, a: k == a, ans)
        row1 = i * t + lax.broadcasted_iota(I32, (t, 1), 0)
        full = row1 >= ksel

        def tie_cut(_):
            nbits = seq.bit_length() - 1

            def bit_body(b, cut):
                cand = cut + lax.shift_left(jnp.int32(1), nbits - 1 - b)
                below = count(keys_ref, lambda k, c0, a, cd: jnp.logical_and(k == a, c0 + lane < cd),
                              ans, cand)
                return jnp.where(below < need, cand, cut)
            return lax.fori_loop(0, nbits, bit_body, jnp.zeros((t, 1), I32))

        contested = jnp.max(jnp.where(jnp.logical_and(full, ties > need), 1.0, 0.0)) > 0.0
        cut = lax.cond(contested, tie_cut, lambda _: jnp.full((t, 1), seq - 1, I32), 0)
        tbits = jnp.where(ans < 0, ans ^ jnp.int32(0x7FFFFFFF), ans)
        thr = jnp.where(full, pltpu.bitcast(tbits, F32), -jnp.inf)
        thr_ref[...] = jnp.broadcast_to(thr, thr_ref.shape)
        jc_ref[...] = jnp.broadcast_to(jnp.where(full, cut, seq - 1), jc_ref.shape)


def _causal_pairs(seq, t, tk):
    pairs = [(i, j) for i in range(seq // t) for j in range((i * t + t - 1) // tk + 1)]
    return jnp.asarray([p[0] for p in pairs], I32), jnp.asarray([p[1] for p in pairs], I32)


def _idx_topk(qh, kidx, widx, *, batch, seq, ksel, name):
    t, tk = ATTN_TILE, min(DSA_KEY_TILE, seq)
    nq, nk = seq // t, seq // tk
    assert seq & (seq - 1) == 0 and tk >= ksel
    qi, kj = _causal_pairs(seq, t, tk)
    npairs = qi.shape[0]
    qrow = lambda b, p, qi, kj: (b * nq + qi[p], 0)
    return pl.pallas_call(
        functools.partial(_idx_topk_kernel, t=t, tk=tk, ksel=ksel, seq=seq),
        out_shape=(jax.ShapeDtypeStruct((batch, npairs, t, tk), F32),
                   jax.ShapeDtypeStruct((batch * seq, LANES), F32),
                   jax.ShapeDtypeStruct((batch * seq, LANES), I32)),
        grid_spec=pltpu.PrefetchScalarGridSpec(
            num_scalar_prefetch=2,
            grid=(batch, npairs),
            in_specs=[pl.BlockSpec((IDX_HEADS, t, LANES), lambda b, p, qi, kj: (1, b * nq + qi[p], 0)),
                      pl.BlockSpec((tk, LANES), lambda b, p, qi, kj: (b * nk + kj[p], 0)),
                      pl.BlockSpec((t, LANES), qrow)],
            out_specs=(pl.BlockSpec((None, None, t, tk), lambda b, p, qi, kj: (b, p, 0, 0)),
                       pl.BlockSpec((t, LANES), qrow), pl.BlockSpec((t, LANES), qrow)),
            scratch_shapes=[pltpu.VMEM((nk, t, tk), I32)]),
        compiler_params=_cparams(("parallel", "arbitrary")),
        name=name,
    )(qi, kj, qh, kidx, widx)


def _dsa_attn_kernel(qi_ref, kj_ref, q_ref, k_ref, v_ref, sc_ref, thr_ref, jc_ref, bias_ref,
                     o_ref, m_ref, acc_ref, *, t, tk):
    p = pl.program_id(1)
    i, j = qi_ref[p], kj_ref[p]

    @pl.when(j == 0)
    def _():
        m_ref[...] = jnp.full(m_ref.shape, NEG, F32)
        acc_ref[...] = jnp.zeros_like(acc_ref)

    sc = sc_ref[...]
    thr = thr_ref[:, 0:1]
    col = j * tk + lax.broadcasted_iota(I32, (t, tk), 1)
    row = i * t + lax.broadcasted_iota(I32, (t, tk), 0)
    tie = jnp.logical_and(sc == thr, col <= jc_ref[:, 0:1])
    keep = jnp.logical_and(jnp.logical_or(sc > thr, tie), col <= row)
    offs = [i * t - (j * tk + u * t) for u in range(tk // t)]

    def run(with_bias):
        for g in range(N_DSA_HEADS // DSA_HEAD_GROUP):
            heads = [g * DSA_HEAD_GROUP + u for u in range(DSA_HEAD_GROUP)]
            logits = [_dot_t(q_ref[h], k_ref[h]) for h in heads]
            for h, s in zip(heads, logits):
                if with_bias:
                    parts = [jnp.where(off == 0, bias_ref[0, h],
                                       jnp.where(off == t, bias_ref[1, h], 0.0)) for off in offs]
                    s = s + (parts[0] if len(parts) == 1 else jnp.concatenate(parts, axis=1))
                s = jnp.where(keep, s, NEG)
                m_prev = m_ref[h]
                m_next = jnp.maximum(m_prev, jnp.max(s, axis=-1, keepdims=True))
                alpha = jnp.exp2(m_prev - m_next)
                p = jnp.exp2((s - jnp.concatenate([m_next] * (tk // HEAD_DIM), axis=1)).astype(BF16))
                acc_ref[h] = (jnp.concatenate([alpha, alpha], axis=1) * acc_ref[h]
                              + _dot(p, _with_ones(v_ref[h])))
                m_ref[h] = m_next

    near = offs[-1] <= t

    @pl.when(near)
    def _():
        run(True)

    @pl.when(jnp.logical_not(near))
    def _():
        run(False)

    @pl.when(j == (i * t + t - 1) // tk)
    def _():
        for h in range(N_DSA_HEADS):
            o_ref[:, _head_cols(h)] = _softmax_finish(acc_ref[h]).astype(o_ref.dtype)


def _dsa_attention(qh, kvh, scores, thr, jcut, bias, *, batch, seq, name):
    t, tk = ATTN_TILE, min(DSA_KEY_TILE, seq)
    nq, nk = seq // t, seq // tk
    h = N_DSA_HEADS
    qi, kj = _causal_pairs(seq, t, tk)
    qrow = lambda b, p, qi, kj: (b * nq + qi[p], 0)
    return pl.pallas_call(
        functools.partial(_dsa_attn_kernel, t=t, tk=tk),
        out_shape=jax.ShapeDtypeStruct((batch * seq, h * HEAD_DIM), BF16),
        grid_spec=pltpu.PrefetchScalarGridSpec(
            num_scalar_prefetch=2,
            grid=(batch, qi.shape[0]),
            in_specs=[pl.BlockSpec((h, t, HEAD_DIM), lambda b, p, qi, kj: (0, b * nq + qi[p], 0)),
                      pl.BlockSpec((h, tk, HEAD_DIM), lambda b, p, qi, kj: (0, b * nk + kj[p], 0)),
                      pl.BlockSpec((h, tk, HEAD_DIM), lambda b, p, qi, kj: (1, b * nk + kj[p], 0)),
                      pl.BlockSpec((None, None, t, tk), lambda b, p, qi, kj: (b, p, 0, 0)),
                      pl.BlockSpec((t, LANES), qrow),
                      pl.BlockSpec((t, LANES), qrow),
                      pl.BlockSpec((2, h, t, t), lambda b, p, qi, kj: (0, 0, 0, 0))],
            out_specs=pl.BlockSpec((t, h * HEAD_DIM), qrow),
            scratch_shapes=[pltpu.VMEM((h, t, HEAD_DIM), F32),
                            pltpu.VMEM((h, t, 2 * HEAD_DIM), F32)]),
        compiler_params=_cparams(("parallel", "arbitrary")),
        name=name,
    )(qi, kj, qh, kvh, kvh, scores, thr, jcut, bias)


def _route_kernel(lg_ref, info_ref, gate_ref, cnt_ref, carry_ref, *, tm):
    i = pl.program_id(0)

    @pl.when(i == 0)
    def _():
        carry_ref[...] = jnp.zeros_like(carry_ref)

    lane = lax.broadcasted_iota(I32, (tm, LANES), 1)
    lane_f = lane.astype(F32)
    lg = jnp.where(lane < N_EXPERTS, lg_ref[...], -jnp.inf)
    m1 = jnp.max(lg, axis=-1, keepdims=True)
    e1 = jnp.min(jnp.where(lg == m1, lane_f, float(LANES)), axis=-1, keepdims=True).astype(I32)
    lg2 = jnp.where(lane == e1, -jnp.inf, lg)
    m2 = jnp.max(lg2, axis=-1, keepdims=True)
    e2 = jnp.min(jnp.where(lg2 == m2, lane_f, float(LANES)), axis=-1, keepdims=True).astype(I32)
    ex = jnp.exp(m2 - m1)
    g1 = 1.0 / (1.0 + ex)
    g2 = ex / (1.0 + ex)
    onehot = jnp.where(jnp.logical_or(lane == e1, lane == e2), 1.0, 0.0)
    r = lax.broadcasted_iota(I32, (tm, tm), 0)
    c = lax.broadcasted_iota(I32, (tm, tm), 1)
    before = _dot(jnp.where(c < r, 1.0, 0.0).astype(BF16), onehot.astype(BF16)) + carry_ref[0:1, :]
    r1 = jnp.sum(jnp.where(lane == e1, before, 0.0), axis=-1, keepdims=True).astype(I32)
    r2 = jnp.sum(jnp.where(lane == e2, before, 0.0), axis=-1, keepdims=True).astype(I32)
    info = jnp.where(lane == 0, e1, jnp.where(lane == 1, e2, jnp.where(lane == 2, r1, r2)))
    info_ref[...] = info
    gate_ref[...] = jnp.where(lane == 0, g1, g2)
    total = carry_ref[0:1, :] + jnp.sum(onehot, axis=0, keepdims=True)
    carry_ref[...] = jnp.broadcast_to(total, carry_ref.shape)
    cnt_ref[...] = jnp.broadcast_to(total, cnt_ref.shape)


def _route(logits, *, tm=256, name):
    n = logits.shape[0]
    tm = min(tm, n)
    return pl.pallas_call(
        functools.partial(_route_kernel, tm=tm),
        out_shape=(jax.ShapeDtypeStruct((n, LANES), I32), jax.ShapeDtypeStruct((n, LANES), F32),
                   jax.ShapeDtypeStruct((8, LANES), F32)),
        grid=(n // tm,),
        in_specs=[pl.BlockSpec((tm, LANES), lambda i: (i, 0))],
        out_specs=(pl.BlockSpec((tm, LANES), lambda i: (i, 0)),
                   pl.BlockSpec((tm, LANES), lambda i: (i, 0)),
                   pl.BlockSpec((8, LANES), lambda i: (0, 0))),
        scratch_shapes=[pltpu.VMEM((8, LANES), F32)],
        compiler_params=_cparams(("arbitrary",)),
        name=name,
    )(logits)


def _combine_ln_kernel(d0_ref, d1_ref, h_ref, ys_ref, gate_ref, g_ref, b_ref, o_ref,
                       ya0_ref, ya1_ref, yb0_ref, yb1_ref, sems_a, sems_b, *, tm, slabs):
    i = pl.program_id(0)
    nt = pl.num_programs(0)
    bufs_a, bufs_b = (ya0_ref, ya1_ref), (yb0_ref, yb1_ref)

    def gather(tile, wait):
        for idx_ref, bufs, sems in ((d0_ref, bufs_a, sems_a), (d1_ref, bufs_b, sems_b)):
            _row_gather(ys_ref, idx_ref, tile * tm, bufs, sems, tile % 2, tm, slabs, wait=wait)

    @pl.when(i == 0)
    def _():
        gather(0, wait=False)

    gather(i, wait=True)

    @pl.when(i + 1 < nt)
    def _():
        gather(i + 1, wait=False)

    for s in range(2):
        @pl.when(i % 2 == s)
        def _(s=s):
            ff = (gate_ref[:, 0:1] * _slabs_to_rows(bufs_a[s], slabs)
                  + gate_ref[:, 1:2] * _slabs_to_rows(bufs_b[s], slabs))
            o_ref[...] = _layer_norm(DEEPNORM_ALPHA * h_ref[...] + ff, g_ref[...], b_ref[...])


def _combine_ln(dest0, dest1, h, ys, gates, g, b, *, tm=256, name):
    n, d = h.shape
    slabs = d // LANES
    tm = min(tm, n)
    row = lambda i, d0, d1: (i, 0)
    const = lambda i, d0, d1: (0, 0)
    slab_buf = pltpu.VMEM((tm * slabs, LANES), F32)
    return pl.pallas_call(
        functools.partial(_combine_ln_kernel, tm=tm, slabs=slabs),
        out_shape=jax.ShapeDtypeStruct((n, d), F32),
        grid_spec=pltpu.PrefetchScalarGridSpec(
            num_scalar_prefetch=2,
            grid=(n // tm,),
            in_specs=[pl.BlockSpec((tm, d), row),
                      pl.BlockSpec(memory_space=pl.ANY),
                      pl.BlockSpec((tm, LANES), row),
                      pl.BlockSpec((1, d), const),
                      pl.BlockSpec((1, d), const)],
            out_specs=pl.BlockSpec((tm, d), row),
            scratch_shapes=[slab_buf, slab_buf, slab_buf, slab_buf,
                            pltpu.SemaphoreType.DMA((2,)), pltpu.SemaphoreType.DMA((2,))]),
        compiler_params=_cparams(("arbitrary",)),
        name=name,
    )(dest0, dest1, h, ys, gates, g, b)


def _rel_bucket(dist):
    n = jnp.maximum(dist, 0)
    exact = REL_BUCKETS // 2
    nf = jnp.maximum(n, 1).astype(F32)
    large = exact + (jnp.log(nf / exact) / math.log(REL_MAX_DIST / exact) * (REL_BUCKETS - exact)).astype(I32)
    large = jnp.minimum(large, REL_BUCKETS - 1)
    return jnp.where(n < exact, n, large)


def _bias_tile_kernel(tab_ref, bucket_ref, o_ref):
    h = pl.program_id(1)
    bucket = bucket_ref[...]
    far = tab_ref[REL_BUCKETS - 1, h]
    acc = jnp.zeros(o_ref.shape, F32)
    for b in range(REL_BUCKETS - 1):
        acc = jnp.where(bucket == b, tab_ref[b, h] - far, acc)
    o_ref[...] = acc * LOG2E


def _bias_tiles(rel_table, n_heads, t, *, name):
    assert t >= REL_MAX_DIST
    r = jnp.arange(t)[:, None]
    c = jnp.arange(t)[None, :]
    buckets = jnp.stack([_rel_bucket(r - c + off) for off in (0, t)]).astype(I32)
    return pl.pallas_call(
        _bias_tile_kernel,
        out_shape=jax.ShapeDtypeStruct((2, n_heads, t, t), F32),
        grid=(2, n_heads),
        in_specs=[pl.BlockSpec(memory_space=pltpu.SMEM),
                  pl.BlockSpec((None, t, t), lambda k, h: (k, 0, 0))],
        out_specs=pl.BlockSpec((None, None, t, t), lambda k, h: (k, h, 0, 0)),
        compiler_params=_cparams(("parallel", "parallel")),
        name=name,
    )(rel_table.astype(F32), buckets)


def _pad_cols(w, n):
    return jnp.pad(w, ((0, 0), (0, n - w.shape[1])))


def _even_layer(h, rel_table, w_in, b_forget, w_out, ln1_g, ln1_b, w1, w3, w2, ln2_g, ln2_b,
                *, batch, seq):
    d = h.shape[1]
    wa = N_MOBA_HEADS * HEAD_DIM
    wb = N_FOX_HEADS * HEAD_DIM
    n_qkv = 3 * wa + 3 * wb
    scale = HEAD_DIM ** -0.5 * LOG2E
    ones, scl = jnp.ones((wa,), F32), jnp.full((wa,), scale, F32)
    colscale = jnp.concatenate([scl, ones, ones, scl, ones, ones])[None, :]
    qkv = _mm(h, w_in[:, :n_qkv].astype(BF16), colscale, out_dtype=BF16, tm=1024, tn=768,
              name="ev_qkv_proj")
    fb = _mm(h, _pad_cols(w_in[:, n_qkv:], LANES).astype(BF16), jnp.ones((1, LANES), F32),
             out_dtype=F32, tn=LANES, name="ev_forget_proj")
    csum = _forget_csum(fb, _pad_cols(b_forget.astype(F32)[None, :], LANES), batch=batch, seq=seq,
                        n_heads=N_FOX_HEADS, name="ev_forget_csum")
    csum = csum.reshape(batch, N_FOX_HEADS, seq // ATTN_TILE, 1, ATTN_TILE)
    oa = _moba_attention(qkv, _bias_tiles(rel_table, N_MOBA_HEADS, MOBA_BLOCK, name="ev_bias_tiles"),
                         batch=batch, seq=seq, name="ev_moba_attn")
    ob = _fox_attention(qkv, csum, batch=batch, seq=seq, name="ev_fox_attn")
    attn = jnp.concatenate([oa, ob], axis=-1)
    h = _mm_ln(attn, w_out.astype(BF16), h, ln1_g[None, :], ln1_b[None, :], name="ev_out_proj_ln")
    return _ffn_ln(h, w1.astype(BF16), w3.astype(BF16), w2.astype(BF16),
                   ln2_g[None, :], ln2_b[None, :], name="ev_swiglu_ln")


def _moe(h, h_slabs, logits, w1, w3, w2, ln_g, ln_b):
    n, d = h.shape
    tm = EXPERT_TILE
    info, gates, cnt = _route(logits, name="od_route")
    counts = cnt[0, :N_EXPERTS].astype(I32)
    padded = (counts + tm - 1) // tm * tm
    pend = jnp.cumsum(padded)
    pstart = pend - padded
    e = info[:, :MOE_TOPK]
    dest = (pstart[e] + info[:, MOE_TOPK:2 * MOE_TOPK]).astype(I32)
    n_rows = -(-(n * MOE_TOPK + N_EXPERTS * (tm - 1)) // tm) * tm
    n_tiles = n_rows // tm
    tile_start = jnp.arange(n_tiles, dtype=I32) * tm
    tile_valid = (tile_start < pend[-1]).astype(I32)
    last = jnp.maximum(pend[-1] - 1, 0)
    tile_e = jnp.minimum(jnp.searchsorted(pend, jnp.minimum(tile_start, last), side="right"),
                         N_EXPERTS - 1).astype(I32)
    tok = jnp.arange(n, dtype=I32)
    row_tok = jnp.zeros((n_rows,), I32).at[dest.T.reshape(-1)].set(jnp.concatenate([tok, tok]))
    ys = _moe_ffn(tile_e, tile_valid, row_tok, h_slabs, w1.astype(BF16),
                  w3.astype(BF16), w2.astype(BF16), tm=tm, name="od_moe_swiglu")
    return _combine_ln(dest[:, 0], dest[:, 1], h, ys, gates, ln_g[None, :], ln_b[None, :],
                       name="od_moe_combine_ln")


def _odd_layer(h, rel_table, w_in, q_norm_g, kv_norm_g, w_uq, w_qidx, w_uk, w_uv, w_out,
               ln1_g, ln1_b, router, w1, w3, w2, ln2_g, ln2_b, *, batch, seq):
    nh = N_DSA_HEADS
    rq, rkv = DSA_Q_RANK, DSA_KV_RANK
    scale = HEAD_DIM ** -0.5 * LOG2E
    w_in_p = jnp.concatenate([w_in[:, :rq + rkv],
                              _pad_cols(w_in[:, rq + rkv:rq + rkv + IDX_DIM], LANES),
                              _pad_cols(w_in[:, rq + rkv + IDX_DIM:], LANES)], axis=1)
    cq, ckv, kidx, widx = _dsa_in_proj(h, w_in_p.astype(BF16), q_norm_g[None, :], kv_norm_g[None, :],
                                       name="od_in_proj_rms")
    w_qidx_p = jnp.pad(w_qidx.reshape(rq, IDX_HEADS, IDX_DIM), ((0, 0), (0, 0), (0, LANES - IDX_DIM)))
    wq = jnp.concatenate([w_uq, w_qidx_p.reshape(rq, IDX_HEADS * LANES)], axis=1)
    qscale = jnp.concatenate([jnp.full((nh * HEAD_DIM,), scale, F32),
                              jnp.ones((IDX_HEADS * LANES,), F32)])[None, :]
    qh = _mm(cq, wq.astype(BF16), qscale, out_dtype=BF16, head_major=True, tm=512, tn=2048,
             name="od_q_proj")
    wkv = jnp.concatenate([w_uk.transpose(1, 0, 2).reshape(rkv, nh * HEAD_DIM),
                           w_uv.transpose(1, 0, 2).reshape(rkv, nh * HEAD_DIM)], axis=1)
    kvh = _mm(ckv, wkv.astype(BF16), jnp.ones((1, 2 * nh * HEAD_DIM), F32), out_dtype=BF16,
              head_major=True, tm=512, tn=2048, name="od_kv_proj")
    scores, thr, jcut = _idx_topk(qh, kidx, widx, batch=batch, seq=seq,
                                  ksel=min(DSA_TOPK_MAX, seq // 4), name="od_idx_topk")
    attn = _dsa_attention(qh, kvh, scores, thr, jcut,
                          _bias_tiles(rel_table, nh, ATTN_TILE, name="od_bias_tiles"),
                          batch=batch, seq=seq, name="od_dsa_attn")
    h, h_slabs, logits = _mm_ln_route(attn, w_out.astype(BF16), h, ln1_g[None, :], ln1_b[None, :],
                                      _pad_cols(router, LANES).astype(BF16), name="od_out_proj_ln")
    return _moe(h, h_slabs, logits, w1, w3, w2, ln2_g, ln2_b)


def kernel(x, rel_table, ev_w_in, ev_b_forget, ev_w_out, ev_ln1_g, ev_ln1_b, ev_ffn_w1, ev_ffn_w3, ev_ffn_w2, ev_ln2_g, ev_ln2_b, od_w_in, od_q_norm_g, od_kv_norm_g, od_w_uq, od_w_qidx, od_w_uk, od_w_uv, od_w_out, od_ln1_g, od_ln1_b, od_router, od_exp_w1, od_exp_w3, od_exp_w2, od_ln2_g, od_ln2_b):
    batch, seq, d = x.shape
    h = x.reshape(batch * seq, d)
    for layer in range(DEPTH):
        i = layer // 2
        if layer % 2 == 0:
            h = _even_layer(h, rel_table, ev_w_in[i], ev_b_forget[i], ev_w_out[i], ev_ln1_g[i],
                            ev_ln1_b[i], ev_ffn_w1[i], ev_ffn_w3[i], ev_ffn_w2[i], ev_ln2_g[i],
                            ev_ln2_b[i], batch=batch, seq=seq)
        else:
            h = _odd_layer(h, rel_table, od_w_in[i], od_q_norm_g[i], od_kv_norm_g[i], od_w_uq[i],
                           od_w_qidx[i], od_w_uk[i], od_w_uv[i], od_w_out[i], od_ln1_g[i],
                           od_ln1_b[i], od_router[i], od_exp_w1[i], od_exp_w3[i], od_exp_w2[i],
                           od_ln2_g[i], od_ln2_b[i], batch=batch, seq=seq)
    return h.reshape(batch, seq, d)
```

```python
import functools
import math

import jax
import jax.numpy as jnp
from jax import lax
from jax.experimental import pallas as pl
from jax.experimental.pallas import tpu as pltpu

F32 = jnp.float32
BF16 = jnp.bfloat16
I32 = jnp.int32

HEAD_DIM = 128
N_MOBA_HEADS = 8
N_FOX_HEADS = 8
MOBA_BLOCK = 256
MOBA_TOPK = 3
N_DSA_HEADS = 16
DSA_Q_RANK = 512
DSA_KV_RANK = 512
IDX_HEADS = 16
IDX_DIM = 64
DSA_TOPK_MAX = 256
REL_BUCKETS = 32
REL_MAX_DIST = 128
N_EXPERTS = 8
MOE_TOPK = 2
LN_EPS = 1e-5
RMS_EPS = 1e-6
DEPTH = 2
DEEPNORM_ALPHA = (2 * DEPTH) ** 0.25

LANES = 128
ATTN_TILE = 256
WIDE_CHUNK = 1024
DSA_KEY_TILE = 512
DSA_HEAD_GROUP = 4
SCORE_STRIP = 64
COUNT_STRIP = 128
EXPERT_TILE = 512
ROW_DMA_UNROLL = 8
NEG = -1e30
LOG2E = math.log2(math.e)
INT_MIN = -(2 ** 31)
VMEM_LIMIT = 56 * 1024 * 1024


def _cparams(sem, vmem=VMEM_LIMIT):
    return pltpu.CompilerParams(dimension_semantics=sem, vmem_limit_bytes=vmem)


def _dot(a, b):
    return jnp.dot(a, b, preferred_element_type=F32)


def _dot_t(a, b):
    return lax.dot_general(a, b, (((1,), (1,)), ((), ())), preferred_element_type=F32)


def _layer_norm(y, g, b):
    mu = jnp.mean(y, axis=-1, keepdims=True)
    d = y - mu
    var = jnp.mean(d * d, axis=-1, keepdims=True)
    return d * lax.rsqrt(var + LN_EPS) * g + b


def _mm_kernel(x_ref, w_ref, cs_ref, o_ref, *, head_major):
    acc = _dot(x_ref[...].astype(BF16), w_ref[...]) * cs_ref[...]
    if head_major:
        for c in range(o_ref.shape[0]):
            o_ref[c] = acc[:, c * LANES:(c + 1) * LANES].astype(o_ref.dtype)
    else:
        o_ref[...] = acc.astype(o_ref.dtype)


def _mm(x, w, colscale, *, out_dtype, head_major=False, tm=512, tn=512, name):
    m, k = x.shape
    n = w.shape[1]
    tm, tn = min(tm, m), min(tn, n)
    assert m % tm == 0 and n % tn == 0 and tn % LANES == 0
    if head_major:
        out_shape = jax.ShapeDtypeStruct((n // LANES, m, LANES), out_dtype)
        out_spec = pl.BlockSpec((tn // LANES, tm, LANES), lambda i, j: (j, i, 0))
    else:
        out_shape = jax.ShapeDtypeStruct((m, n), out_dtype)
        out_spec = pl.BlockSpec((tm, tn), lambda i, j: (i, j))
    return pl.pallas_call(
        functools.partial(_mm_kernel, head_major=head_major),
        out_shape=out_shape,
        grid=(m // tm, n // tn),
        in_specs=[pl.BlockSpec((tm, k), lambda i, j: (i, 0)),
                  pl.BlockSpec((k, tn), lambda i, j: (0, j)),
                  pl.BlockSpec((1, tn), lambda i, j: (0, j))],
        out_specs=out_spec,
        compiler_params=_cparams(("parallel", "arbitrary")),
        name=name,
    )(x, w, colscale)


def _mm_ln_kernel(x_ref, w_ref, res_ref, g_ref, b_ref, o_ref):
    y = DEEPNORM_ALPHA * res_ref[...] + _dot(x_ref[...], w_ref[...])
    o_ref[...] = _layer_norm(y, g_ref[...], b_ref[...])


def _mm_ln(x, w, res, g, b, *, tm=256, name):
    m, k = x.shape
    d = w.shape[1]
    tm = min(tm, m)
    return pl.pallas_call(
        _mm_ln_kernel,
        out_shape=jax.ShapeDtypeStruct((m, d), F32),
        grid=(m // tm,),
        in_specs=[pl.BlockSpec((tm, k), lambda i: (i, 0)),
                  pl.BlockSpec((k, d), lambda i: (0, 0)),
                  pl.BlockSpec((tm, d), lambda i: (i, 0)),
                  pl.BlockSpec((1, d), lambda i: (0, 0)),
                  pl.BlockSpec((1, d), lambda i: (0, 0))],
        out_specs=pl.BlockSpec((tm, d), lambda i: (i, 0)),
        compiler_params=_cparams(("parallel",)),
        name=name,
    )(x, w, res, g, b)


def _mm_ln_route_kernel(x_ref, w_ref, res_ref, g_ref, b_ref, wr_ref, o_ref, slab_ref, lg_ref,
                        *, slabs):
    y = _layer_norm(DEEPNORM_ALPHA * res_ref[...] + _dot(x_ref[...], w_ref[...]),
                    g_ref[...], b_ref[...])
    o_ref[...] = y
    tm = y.shape[0]
    for c in range(slabs):
        slab_ref[pl.ds(c, tm, stride=slabs), :] = y[:, c * LANES:(c + 1) * LANES]
    lg_ref[...] = _dot(y.astype(BF16), wr_ref[...])


def _mm_ln_route(x, w, res, g, b, w_router, *, tm=256, name):
    m, k = x.shape
    d = w.shape[1]
    tm = min(tm, m)
    slabs = d // LANES
    row = lambda i: (i, 0)
    const = lambda i: (0, 0)
    return pl.pallas_call(
        functools.partial(_mm_ln_route_kernel, slabs=slabs),
        out_shape=(jax.ShapeDtypeStruct((m, d), F32), jax.ShapeDtypeStruct((m * slabs, LANES), F32),
                   jax.ShapeDtypeStruct((m, LANES), F32)),
        grid=(m // tm,),
        in_specs=[pl.BlockSpec((tm, k), row), pl.BlockSpec((k, d), const), pl.BlockSpec((tm, d), row),
                  pl.BlockSpec((1, d), const), pl.BlockSpec((1, d), const),
                  pl.BlockSpec((d, LANES), const)],
        out_specs=(pl.BlockSpec((tm, d), row), pl.BlockSpec((tm * slabs, LANES), row),
                   pl.BlockSpec((tm, LANES), row)),
        compiler_params=_cparams(("parallel",)),
        name=name,
    )(x, w, res, g, b, w_router)


def _swiglu_step(xb, w1_ref, w3_ref, w2_ref):
    a = _dot(xb, w1_ref[...])
    c = _dot(xb, w3_ref[...])
    hmid = a / (1.0 + jnp.exp(-a)) * c
    return _dot(hmid.astype(BF16), w2_ref[...])


def _ffn_ln_kernel(x_ref, w1_ref, w3_ref, w2_ref, g_ref, b_ref, o_ref, acc_ref, xb_ref):
    f = pl.program_id(1)

    @pl.when(f == 0)
    def _():
        acc_ref[...] = jnp.zeros_like(acc_ref)
        xb_ref[...] = x_ref[...].astype(BF16)

    acc_ref[...] += _swiglu_step(xb_ref[...], w1_ref, w3_ref, w2_ref)

    @pl.when(f == pl.num_programs(1) - 1)
    def _():
        y = DEEPNORM_ALPHA * x_ref[...] + acc_ref[...]
        o_ref[...] = _layer_norm(y, g_ref[...], b_ref[...])


def _ffn_ln(x, w1, w3, w2, g, b, *, tm=512, tf=512, name):
    m, d = x.shape
    dff = w1.shape[1]
    tm, tf = min(tm, m), min(tf, dff)
    assert m % tm == 0 and dff % tf == 0
    return pl.pallas_call(
        _ffn_ln_kernel,
        out_shape=jax.ShapeDtypeStruct((m, d), F32),
        grid=(m // tm, dff // tf),
        in_specs=[pl.BlockSpec((tm, d), lambda i, f: (i, 0)),
                  pl.BlockSpec((d, tf), lambda i, f: (0, f)),
                  pl.BlockSpec((d, tf), lambda i, f: (0, f)),
                  pl.BlockSpec((tf, d), lambda i, f: (f, 0)),
                  pl.BlockSpec((1, d), lambda i, f: (0, 0)),
                  pl.BlockSpec((1, d), lambda i, f: (0, 0))],
        out_specs=pl.BlockSpec((tm, d), lambda i, f: (i, 0)),
        scratch_shapes=[pltpu.VMEM((tm, d), F32), pltpu.VMEM((tm, d), BF16)],
        compiler_params=_cparams(("parallel", "arbitrary")),
        name=name,
    )(x, w1, w3, w2, g, b)


def _row_gather(src_hbm, idx_ref, base, bufs, sems, slot, n_rows, slabs, *, wait):
    for s, buf in enumerate(bufs):
        @pl.when(slot == s)
        def _(s=s, buf=buf):
            def body(r, _):
                src = src_hbm.at[pl.ds(pl.multiple_of(idx_ref[base + r] * slabs, slabs), slabs)]
                dst = buf.at[pl.ds(pl.multiple_of(r * slabs, slabs), slabs)]
                cp = pltpu.make_async_copy(src, dst, sems.at[s])
                if wait:
                    cp.wait()
                else:
                    cp.start()
                return 0
            lax.fori_loop(0, n_rows, body, 0, unroll=ROW_DMA_UNROLL)


def _slabs_to_rows(buf, slabs):
    rows = buf.shape[0] // slabs
    return jnp.concatenate([buf[pl.ds(c, rows, stride=slabs), :] for c in range(slabs)], axis=1)


def _moe_ffn_kernel(te_ref, tv_ref, tok_ref, h_ref, w1_ref, w3_ref, w2_ref, o_ref,
                    acc_ref, xb_ref, xg0_ref, xg1_ref, sems, *, tm, slabs):
    i, f = pl.program_id(0), pl.program_id(1)
    nt = pl.num_programs(0)
    bufs = (xg0_ref, xg1_ref)
    gather = functools.partial(_row_gather, h_ref, tok_ref, bufs=bufs, sems=sems, n_rows=tm,
                               slabs=slabs)

    @pl.when(tv_ref[i] > 0)
    def _():
        @pl.when(f == 0)
        def _():
            @pl.when(i == 0)
            def _():
                gather(base=0, slot=0, wait=False)

            gather(base=i * tm, slot=i % 2, wait=True)
            nxt = jnp.minimum(i + 1, nt - 1)

            @pl.when(jnp.logical_and(i + 1 < nt, tv_ref[nxt] > 0))
            def _():
                gather(base=nxt * tm, slot=nxt % 2, wait=False)

            for s, buf in enumerate(bufs):
                @pl.when(i % 2 == s)
                def _(buf=buf):
                    xb_ref[...] = _slabs_to_rows(buf, slabs).astype(BF16)

            acc_ref[...] = jnp.zeros_like(acc_ref)

        acc_ref[...] += _swiglu_step(xb_ref[...], w1_ref, w3_ref, w2_ref)

        @pl.when(f == pl.num_programs(1) - 1)
        def _():
            for c in range(slabs):
                o_ref[pl.ds(c, tm, stride=slabs), :] = acc_ref[:, c * LANES:(c + 1) * LANES]

    @pl.when(tv_ref[i] == 0)
    def _():
        o_ref[...] = jnp.zeros_like(o_ref)


def _moe_ffn(tile_e, tile_valid, row_tok, h_slabs, w1, w3, w2, *, tm, tf=1024, name):
    n_rows = row_tok.shape[0]
    d = w1.shape[1]
    slabs = d // LANES
    dff = w1.shape[2]
    tf = min(tf, dff)
    nf = dff // tf
    assert n_rows % tm == 0 and dff % tf == 0

    def fidx(i, f, tv):
        return jnp.where(tv[i] > 0, f, nf - 1)

    return pl.pallas_call(
        functools.partial(_moe_ffn_kernel, tm=tm, slabs=slabs),
        out_shape=jax.ShapeDtypeStruct((n_rows * slabs, LANES), F32),
        grid_spec=pltpu.PrefetchScalarGridSpec(
            num_scalar_prefetch=3,
            grid=(n_rows // tm, nf),
            in_specs=[pl.BlockSpec(memory_space=pl.ANY),
                      pl.BlockSpec((None, d, tf), lambda i, f, te, tv, tok: (te[i], 0, fidx(i, f, tv))),
                      pl.BlockSpec((None, d, tf), lambda i, f, te, tv, tok: (te[i], 0, fidx(i, f, tv))),
                      pl.BlockSpec((None, tf, d), lambda i, f, te, tv, tok: (te[i], fidx(i, f, tv), 0))],
            out_specs=pl.BlockSpec((tm * slabs, LANES), lambda i, f, te, tv, tok: (i, 0)),
            scratch_shapes=[pltpu.VMEM((tm, d), F32), pltpu.VMEM((tm, d), BF16),
                            pltpu.VMEM((tm * slabs, LANES), F32), pltpu.VMEM((tm * slabs, LANES), F32),
                            pltpu.SemaphoreType.DMA((2,))]),
        compiler_params=_cparams(("arbitrary", "arbitrary")),
        name=name,
    )(tile_e, tile_valid, row_tok, h_slabs, w1, w3, w2)


def _online_softmax_step(s, v, carry):
    m, acc = carry
    m_new = jnp.maximum(m, jnp.max(s, axis=-1, keepdims=True))
    alpha = jnp.exp2(m - m_new)
    p = jnp.exp2((s - m_new).astype(BF16))
    return m_new, alpha * acc + _dot(p, _with_ones(v))


def _with_ones(v):
    return jnp.concatenate([v, jnp.ones_like(v)], axis=1)


def _softmax_init(tq):
    return jnp.full((tq, 1), NEG, F32), jnp.zeros((tq, 2 * HEAD_DIM), F32)


def _softmax_finish(acc):
    return acc[:, :HEAD_DIM] / acc[:, HEAD_DIM:]


def _pairwise_loop(n, body, carry):
    carry = lax.fori_loop(0, n // 2, lambda k, c: body(2 * k + 1, body(2 * k, c)), carry)
    return lax.fori_loop(n // 2 * 2, n, body, carry)


def _head_cols(h):
    return slice(h * HEAD_DIM, (h + 1) * HEAD_DIM)


def _kv_block(k_ref, v_ref, n, width, h):
    rows = pl.ds(pl.multiple_of(n * width, width), width)
    return k_ref[rows, _head_cols(h)], v_ref[rows, _head_cols(h)]


def _causal_mask(blk):
    r = lax.broadcasted_iota(I32, (blk, blk), 0)
    c = lax.broadcasted_iota(I32, (blk, blk), 1)
    return c <= r


def _moba_select(q, km, i, blk):
    nbp = km.shape[0]
    gate = _dot_t(q, km.astype(BF16))
    lane = lax.broadcasted_iota(I32, (blk, nbp), 1)
    lane_f = lane.astype(F32)
    g = jnp.where(lane < i, gate, -jnp.inf)
    sel = jnp.zeros((blk, nbp), F32)
    for _ in range(MOBA_TOPK):
        mx = jnp.max(g, axis=-1, keepdims=True)
        first = jnp.min(jnp.where(g == mx, lane_f, float(nbp)), axis=-1, keepdims=True)
        pick = jnp.logical_and(lane_f == first, mx > -jnp.inf)
        sel = jnp.where(pick, 1.0, sel)
        g = jnp.where(pick, -jnp.inf, g)
    return sel, lane


def _moba_kernel(q_ref, k_ref, v_ref, bias_ref, o_ref, km_ref, *, blk, nb, wide, hp):
    i = pl.program_id(2)
    per = wide // blk

    @pl.when(i == 0)
    def _():
        km_ref[...] = jnp.zeros_like(km_ref)

        def mean_body(n, _):
            for h in range(hp):
                kblk, _ = _kv_block(k_ref, v_ref, n, blk, h)
                km_ref[h, pl.ds(n, 1), :] = jnp.mean(kblk.astype(F32), axis=0, keepdims=True)
            return 0

        lax.fori_loop(0, nb, mean_body, 0)

    qs = [q_ref[:, _head_cols(h)] for h in range(hp)]
    sels = [_moba_select(qs[h], km_ref[h], i, blk) for h in range(hp)]

    def scores(n):
        return [_dot_t(qs[h], _kv_block(k_ref, v_ref, n, wide, h)[0]) for h in range(hp)]

    def update(n, logits, carries, near):
        causal = _causal_mask(blk)
        out = []
        for h in range(hp):
            s, vblk = logits[h], _kv_block(k_ref, v_ref, n, wide, h)[1]
            sel, lane = sels[h]
            parts = []
            for u in range(per):
                blk_id = n * per + u
                part = s[:, u * blk:(u + 1) * blk]
                chosen = jnp.sum(jnp.where(lane == blk_id, sel, 0.0), axis=-1, keepdims=True) > 0.0
                if near:
                    part = part + jnp.where(blk_id == i, bias_ref[0, h],
                                            jnp.where(blk_id == i - 1, bias_ref[1, h], 0.0))
                    parts.append(jnp.where(blk_id == i, jnp.where(causal, part, NEG),
                                           jnp.where(chosen, part, NEG)))
                else:
                    parts.append(jnp.where(chosen, part, NEG))
            s = parts[0] if per == 1 else jnp.concatenate(parts, axis=1)
            out.append(_online_softmax_step(s, vblk, carries[h]))
        return tuple(out)

    first_near = jnp.maximum(i - 1, 0) // per
    carries = tuple(_softmax_init(blk) for _ in range(hp))
    carries = _pairwise_loop(first_near, lambda n, c: update(n, scores(n), c, False), carries)
    carries = lax.fori_loop(first_near, i // per + 1,
                            lambda n, c: update(n, scores(n), c, True), carries)
    for h in range(hp):
        o_ref[:, _head_cols(h)] = _softmax_finish(carries[h][1]).astype(o_ref.dtype)


def _moba_attention(qkv, bias, *, batch, seq, hp=4, name):
    blk = MOBA_BLOCK
    nb = seq // blk
    nbp = -(-nb // LANES) * LANES
    h = N_MOBA_HEADS
    nq = seq // blk
    hg = h // hp
    w = hp * HEAD_DIM
    return pl.pallas_call(
        functools.partial(_moba_kernel, blk=blk, nb=nb, wide=min(WIDE_CHUNK, seq), hp=hp),
        out_shape=jax.ShapeDtypeStruct((batch * seq, h * HEAD_DIM), BF16),
        grid=(batch, hg, nq),
        in_specs=[pl.BlockSpec((blk, w), lambda b, g, i: (b * nq + i, g)),
                  pl.BlockSpec((seq, w), lambda b, g, i: (b, hg + g)),
                  pl.BlockSpec((seq, w), lambda b, g, i: (b, 2 * hg + g)),
                  pl.BlockSpec((2, hp, blk, blk), lambda b, g, i: (0, g, 0, 0))],
        out_specs=pl.BlockSpec((blk, w), lambda b, g, i: (b * nq + i, g)),
        scratch_shapes=[pltpu.VMEM((hp, nbp, HEAD_DIM), F32)],
        compiler_params=_cparams(("parallel", "parallel", "arbitrary")),
        name=name,
    )(qkv, qkv, qkv, bias)


def _fox_kernel(q_ref, k_ref, v_ref, ck_ref, o_ref, *, blk, wide, hp):
    i = pl.program_id(2)
    per = wide // blk
    qs = [q_ref[:, _head_cols(h)] for h in range(hp)]

    def scores(n):
        return [_dot_t(qs[h], _kv_block(k_ref, v_ref, n, wide, h)[0]) for h in range(hp)]

    def update(n, logits, carries, causal):
        if causal:
            row = i * blk + lax.broadcasted_iota(I32, (blk, wide), 0)
            col = n * wide + lax.broadcasted_iota(I32, (blk, wide), 1)
            visible = col <= row
        out = []
        for h in range(hp):
            ck = [ck_ref[h, n * per + u] for u in range(per)]
            s = logits[h] - (ck[0] if per == 1 else jnp.concatenate(ck, axis=1))
            if causal:
                s = jnp.where(visible, s, NEG)
            out.append(_online_softmax_step(s, _kv_block(k_ref, v_ref, n, wide, h)[1], carries[h]))
        return tuple(out)

    n_wide = i // per
    carries = tuple(_softmax_init(blk) for _ in range(hp))
    carries = _pairwise_loop(n_wide, lambda n, c: update(n, scores(n), c, False), carries)
    carries = update(n_wide, scores(n_wide), carries, True)
    for h in range(hp):
        o_ref[:, _head_cols(h)] = _softmax_finish(carries[h][1]).astype(o_ref.dtype)


def _fox_attention(qkv, csum, *, batch, seq, hp=4, name):
    blk = ATTN_TILE
    h = N_FOX_HEADS
    nq = seq // blk
    hg = h // hp
    base = 3 * N_MOBA_HEADS // hp
    w = hp * HEAD_DIM
    return pl.pallas_call(
        functools.partial(_fox_kernel, blk=blk, wide=min(WIDE_CHUNK, seq), hp=hp),
        out_shape=jax.ShapeDtypeStruct((batch * seq, h * HEAD_DIM), BF16),
        grid=(batch, hg, nq),
        in_specs=[pl.BlockSpec((blk, w), lambda b, g, i: (b * nq + i, base + g)),
                  pl.BlockSpec((seq, w), lambda b, g, i: (b, base + hg + g)),
                  pl.BlockSpec((seq, w), lambda b, g, i: (b, base + 2 * hg + g)),
                  pl.BlockSpec((None, hp, nq, 1, blk), lambda b, g, i: (b, g, 0, 0, 0))],
        out_specs=pl.BlockSpec((blk, w), lambda b, g, i: (b * nq + i, g)),
        compiler_params=_cparams(("parallel", "parallel", "arbitrary")),
        name=name,
    )(qkv, qkv, qkv, csum)


def _forget_csum_kernel(fb_ref, b_ref, o_ref, carry_ref, *, tc):
    @pl.when(pl.program_id(1) == 0)
    def _():
        carry_ref[...] = jnp.zeros_like(carry_ref)

    z = fb_ref[...] + b_ref[...]
    run = jnp.minimum(z, 0.0) - jnp.log1p(jnp.exp(-jnp.abs(z)))
    row = lax.broadcasted_iota(I32, (tc, LANES), 0)
    shift = 1
    while shift < tc:
        run = run + jnp.where(row >= shift, pltpu.roll(run, shift, axis=0), 0.0)
        shift *= 2
    run = run + carry_ref[0:1, :]
    carry_ref[...] = jnp.broadcast_to(run[tc - 1:tc, :], carry_ref.shape)
    o_ref[...] = (run * LOG2E).T[:o_ref.shape[0], :]


def _forget_csum(fb, bias, *, batch, seq, n_heads, tc=512, name):
    tc = min(tc, seq)
    nt = seq // tc
    return pl.pallas_call(
        functools.partial(_forget_csum_kernel, tc=tc),
        out_shape=jax.ShapeDtypeStruct((batch, n_heads, seq), F32),
        grid=(batch, nt),
        in_specs=[pl.BlockSpec((tc, LANES), lambda b, j: (b * nt + j, 0)),
                  pl.BlockSpec((1, LANES), lambda b, j: (0, 0))],
        out_specs=pl.BlockSpec((None, n_heads, tc), lambda b, j: (b, 0, j)),
        scratch_shapes=[pltpu.VMEM((8, LANES), F32)],
        compiler_params=_cparams(("parallel", "arbitrary")),
        name=name,
    )(fb, bias)


def _dsa_in_kernel(x_ref, w_ref, gq_ref, gkv_ref, cq_ref, ckv_ref, kidx_ref, widx_ref):
    acc = _dot(x_ref[...].astype(BF16), w_ref[...])
    rq, rkv = DSA_Q_RANK, DSA_KV_RANK

    def rms(z, g):
        return z * lax.rsqrt(jnp.mean(z * z, axis=-1, keepdims=True) + RMS_EPS) * g

    cq_ref[...] = rms(acc[:, :rq], gq_ref[...]).astype(cq_ref.dtype)
    ckv_ref[...] = rms(acc[:, rq:rq + rkv], gkv_ref[...]).astype(ckv_ref.dtype)
    kidx_ref[...] = acc[:, rq + rkv:rq + rkv + LANES].astype(kidx_ref.dtype)
    widx_ref[...] = acc[:, rq + rkv + LANES:] * (IDX_HEADS ** -0.5 * IDX_DIM ** -0.5)


def _dsa_in_proj(x, w, gq, gkv, *, tm=512, name):
    m, k = x.shape
    n = w.shape[1]
    tm = min(tm, m)
    rq, rkv = DSA_Q_RANK, DSA_KV_RANK
    row = lambda i: (i, 0)
    const = lambda i: (0, 0)
    return pl.pallas_call(
        _dsa_in_kernel,
        out_shape=(jax.ShapeDtypeStruct((m, rq), BF16), jax.ShapeDtypeStruct((m, rkv), BF16),
                   jax.ShapeDtypeStruct((m, LANES), BF16), jax.ShapeDtypeStruct((m, LANES), F32)),
        grid=(m // tm,),
        in_specs=[pl.BlockSpec((tm, k), row), pl.BlockSpec((k, n), const),
                  pl.BlockSpec((1, rq), const), pl.BlockSpec((1, rkv), const)],
        out_specs=(pl.BlockSpec((tm, rq), row), pl.BlockSpec((tm, rkv), row),
                   pl.BlockSpec((tm, LANES), row), pl.BlockSpec((tm, LANES), row)),
        compiler_params=_cparams(("parallel",)),
        name=name,
    )(x, w, gq, gkv)


def _fold_lanes(x):
    part = x[:, :LANES]
    for g in range(1, x.shape[1] // LANES):
        part = part + x[:, g * LANES:(g + 1) * LANES]
    return part


def _idx_topk_kernel(qi_ref, kj_ref, q_ref, k_ref, w_ref, sc_ref, thr_ref, jc_ref, keys_ref,
                     *, t, tk, ksel, seq):
    p = pl.program_id(1)
    i, j = qi_ref[p], kj_ref[p]
    k = k_ref[...]
    for s0 in range(0, t, SCORE_STRIP):
        rows = slice(s0, s0 + SCORE_STRIP)
        stacked = jnp.concatenate([q_ref[h, rows, :] for h in range(IDX_HEADS)], axis=0)
        d = _dot_t(stacked, k)
        acc = jnp.zeros((SCORE_STRIP, tk), F32)
        for h in range(IDX_HEADS):
            part = d[h * SCORE_STRIP:(h + 1) * SCORE_STRIP]
            acc = acc + jnp.maximum(part, 0.0) * w_ref[rows, h:h + 1]
        sc_ref[rows, :] = acc
        row = i * t + s0 + lax.broadcasted_iota(I32, (SCORE_STRIP, tk), 0)
        lane_col = lax.broadcasted_iota(I32, (SCORE_STRIP, tk), 1)
        bits = pltpu.bitcast(acc, I32)
        key = jnp.where(bits < 0, bits ^ jnp.int32(0x7FFFFFFF), bits)
        key = jnp.where(bits == jnp.int32(INT_MIN), 0, key)
        keys_ref[j, rows, :] = jnp.where(j * tk + lane_col <= row, key, jnp.int32(INT_MIN))
    j_last = (i * t + t - 1) // tk

    @pl.when(j == j_last)
    def _():
        lane = lax.broadcasted_iota(I32, (COUNT_STRIP, LANES), 1)
        strips = [slice(s0, s0 + COUNT_STRIP) for s0 in range(0, t, COUNT_STRIP)]

        def count(ref, pred, *row_args):
            dt = ref.dtype
            reps = [[jnp.broadcast_to(a[rows], (COUNT_STRIP, LANES)).astype(dt) for a in row_args]
                    for rows in strips]
            cnts = []
            for rows, args in zip(strips, reps):
                def body(c, cnt, rows=rows, args=args):
                    for g in range(tk // LANES):
                        keys = ref[c, rows, g * LANES:(g + 1) * LANES]
                        cnt = cnt + pred(keys, c * tk + g * LANES, *args).astype(dt)
                    return cnt

                cnts.append(_pairwise_loop(j_last + 1, body, jnp.zeros((COUNT_STRIP, LANES), dt)))
            cnt = jnp.concatenate(cnts, axis=0)
            return jnp.sum(cnt.astype(F32), axis=-1, keepdims=True).astype(I32)

        row1 = i * t + lax.broadcasted_iota(I32, (t, 1), 0)
        full = row1 >= ksel

        def unsettled(state):
            _, lo, hi, settled = state
            return jnp.logical_and(settled == 0, hi - lo != 1)

        def probe(state):
            step, lo, hi, settled = state
            cand = lo + lax.shift_right_logical(hi - lo, 1)
            n_ge = count(keys_ref, lambda k, c0, cd: k >= cd, cand)
            move = jnp.logical_and(settled == 0, n_ge >= ksel)
            stop = jnp.logical_and(settled == 0, n_ge < ksel)
            return (step + 1, jnp.where(move, cand, lo), jnp.where(stop, cand, hi),
                    jnp.where(n_ge == ksel, 1, settled))

        def searching(state):
            busy = jnp.max(jnp.where(unsettled(state), 1.0, 0.0)) > 0.0
            return jnp.logical_and(state[0] < 34, busy)

        start = (jnp.int32(0), jnp.full((t, 1), INT_MIN, I32), jnp.full((t, 1), 2 ** 31 - 1, I32),
                 jnp.where(full, 0, 1))
        _, ans, _, settled = lax.while_loop(searching, probe, start)

        def tie_counts(_):
            return (ksel - count(keys_ref, lambda k, c0, a: k > a, ans),
                    count(keys_ref, lambda k, c0, a: k == a, ans))

        exact = jnp.max(jnp.where(settled == 0, 1.0, 0.0)) > 0.0
        need, ties = lax.cond(exact, tie_counts,
                              lambda _: (jnp.zeros((t, 1), I32), jnp.zeros((t, 1), I32)), 0)

        def tie_cut(_):
            nbits = seq.bit_length() - 1

            def bit_body(b, cut):
                cand = cut + lax.shift_left(jnp.int32(1), nbits - 1 - b)
                below = count(keys_ref, lambda k, c0, a, cd: jnp.logical_and(k == a, c0 + lane < cd),
                              ans, cand)
                return jnp.where(below < need, cand, cut)
            return lax.fori_loop(0, nbits, bit_body, jnp.zeros((t, 1), I32))

        contested = jnp.max(jnp.where(jnp.logical_and(full, ties > need), 1.0, 0.0)) > 0.0
        cut = lax.cond(contested, tie_cut, lambda _: jnp.full((t, 1), seq - 1, I32), 0)
        tbits = jnp.where(ans < 0, ans ^ jnp.int32(0x7FFFFFFF), ans)
        thr = jnp.where(full, pltpu.bitcast(tbits, F32), -jnp.inf)
        thr_ref[...] = jnp.broadcast_to(thr, thr_ref.shape)
        jc_ref[...] = jnp.broadcast_to(jnp.where(full, cut, seq - 1), jc_ref.shape)


def _causal_pairs(seq, t, tk):
    pairs = [(i, j) for i in range(seq // t) for j in range((i * t + t - 1) // tk + 1)]
    return jnp.asarray([p[0] for p in pairs], I32), jnp.asarray([p[1] for p in pairs], I32)


def _idx_topk(qh, kidx, widx, *, batch, seq, ksel, name):
    t, tk = ATTN_TILE, min(DSA_KEY_TILE, seq)
    nq, nk = seq // t, seq // tk
    assert seq & (seq - 1) == 0 and tk >= ksel
    qi, kj = _causal_pairs(seq, t, tk)
    npairs = qi.shape[0]
    qrow = lambda b, p, qi, kj: (b * nq + qi[p], 0)
    return pl.pallas_call(
        functools.partial(_idx_topk_kernel, t=t, tk=tk, ksel=ksel, seq=seq),
        out_shape=(jax.ShapeDtypeStruct((batch, npairs, t, tk), F32),
                   jax.ShapeDtypeStruct((batch * seq, LANES), F32),
                   jax.ShapeDtypeStruct((batch * seq, LANES), I32)),
        grid_spec=pltpu.PrefetchScalarGridSpec(
            num_scalar_prefetch=2,
            grid=(batch, npairs),
            in_specs=[pl.BlockSpec((IDX_HEADS, t, LANES), lambda b, p, qi, kj: (1, b * nq + qi[p], 0)),
                      pl.BlockSpec((tk, LANES), lambda b, p, qi, kj: (b * nk + kj[p], 0)),
                      pl.BlockSpec((t, LANES), qrow)],
            out_specs=(pl.BlockSpec((None, None, t, tk), lambda b, p, qi, kj: (b, p, 0, 0)),
                       pl.BlockSpec((t, LANES), qrow), pl.BlockSpec((t, LANES), qrow)),
            scratch_shapes=[pltpu.VMEM((nk, t, tk), I32)]),
        compiler_params=_cparams(("parallel", "arbitrary")),
        name=name,
    )(qi, kj, qh, kidx, widx)


def _dsa_attn_kernel(qi_ref, kj_ref, q_ref, k_ref, v_ref, sc_ref, thr_ref, jc_ref, bias_ref,
                     o_ref, m_ref, acc_ref, *, t, tk):
    p = pl.program_id(1)
    i, j = qi_ref[p], kj_ref[p]

    @pl.when(j == 0)
    def _():
        m_ref[...] = jnp.full(m_ref.shape, NEG, F32)
        acc_ref[...] = jnp.zeros_like(acc_ref)

    sc = sc_ref[...]
    thr = thr_ref[:, 0:1]
    col = j * tk + lax.broadcasted_iota(I32, (t, tk), 1)
    row = i * t + lax.broadcasted_iota(I32, (t, tk), 0)
    tie = jnp.logical_and(sc == thr, col <= jc_ref[:, 0:1])
    keep = jnp.logical_and(jnp.logical_or(sc > thr, tie), col <= row)
    offs = [i * t - (j * tk + u * t) for u in range(tk // t)]

    def run(with_bias):
        for g in range(N_DSA_HEADS // DSA_HEAD_GROUP):
            heads = [g * DSA_HEAD_GROUP + u for u in range(DSA_HEAD_GROUP)]
            logits = [_dot_t(q_ref[h], k_ref[h]) for h in heads]
            for h, s in zip(heads, logits):
                if with_bias:
                    parts = [jnp.where(off == 0, bias_ref[0, h],
                                       jnp.where(off == t, bias_ref[1, h], 0.0)) for off in offs]
                    s = s + (parts[0] if len(parts) == 1 else jnp.concatenate(parts, axis=1))
                s = jnp.where(keep, s, NEG)
                m_prev = m_ref[h]
                m_next = jnp.maximum(m_prev, jnp.max(s, axis=-1, keepdims=True))
                alpha = jnp.exp2(m_prev - m_next)
                p = jnp.exp2((s - jnp.concatenate([m_next] * (tk // HEAD_DIM), axis=1)).astype(BF16))
                acc_ref[h] = (jnp.concatenate([alpha, alpha], axis=1) * acc_ref[h]
                              + _dot(p, _with_ones(v_ref[h])))
                m_ref[h] = m_next

    near = offs[-1] <= t

    @pl.when(near)
    def _():
        run(True)

    @pl.when(jnp.logical_not(near))
    def _():
        run(False)

    @pl.when(j == (i * t + t - 1) // tk)
    def _():
        for h in range(N_DSA_HEADS):
            o_ref[:, _head_cols(h)] = _softmax_finish(acc_ref[h]).astype(o_ref.dtype)


def _dsa_attention(qh, kvh, scores, thr, jcut, bias, *, batch, seq, name):
    t, tk = ATTN_TILE, min(DSA_KEY_TILE, seq)
    nq, nk = seq // t, seq // tk
    h = N_DSA_HEADS
    qi, kj = _causal_pairs(seq, t, tk)
    qrow = lambda b, p, qi, kj: (b * nq + qi[p], 0)
    return pl.pallas_call(
        functools.partial(_dsa_attn_kernel, t=t, tk=tk),
        out_shape=jax.ShapeDtypeStruct((batch * seq, h * HEAD_DIM), BF16),
        grid_spec=pltpu.PrefetchScalarGridSpec(
            num_scalar_prefetch=2,
            grid=(batch, qi.shape[0]),
            in_specs=[pl.BlockSpec((h, t, HEAD_DIM), lambda b, p, qi, kj: (0, b * nq + qi[p], 0)),
                      pl.BlockSpec((h, tk, HEAD_DIM), lambda b, p, qi, kj: (0, b * nk + kj[p], 0)),
                      pl.BlockSpec((h, tk, HEAD_DIM), lambda b, p, qi, kj: (1, b * nk + kj[p], 0)),
                      pl.BlockSpec((None, None, t, tk), lambda b, p, qi, kj: (b, p, 0, 0)),
                      pl.BlockSpec((t, LANES), qrow),
                      pl.BlockSpec((t, LANES), qrow),
                      pl.BlockSpec((2, h, t, t), lambda b, p, qi, kj: (0, 0, 0, 0))],
            out_specs=pl.BlockSpec((t, h * HEAD_DIM), qrow),
            scratch_shapes=[pltpu.VMEM((h, t, HEAD_DIM), F32),
                            pltpu.VMEM((h, t, 2 * HEAD_DIM), F32)]),
        compiler_params=_cparams(("parallel", "arbitrary")),
        name=name,
    )(qi, kj, qh, kvh, kvh, scores, thr, jcut, bias)


def _route_kernel(lg_ref, info_ref, gate_ref, cnt_ref, carry_ref, *, tm):
    i = pl.program_id(0)

    @pl.when(i == 0)
    def _():
        carry_ref[...] = jnp.zeros_like(carry_ref)

    lane = lax.broadcasted_iota(I32, (tm, LANES), 1)
    lane_f = lane.astype(F32)
    lg = jnp.where(lane < N_EXPERTS, lg_ref[...], -jnp.inf)
    m1 = jnp.max(lg, axis=-1, keepdims=True)
    e1 = jnp.min(jnp.where(lg == m1, lane_f, float(LANES)), axis=-1, keepdims=True).astype(I32)
    lg2 = jnp.where(lane == e1, -jnp.inf, lg)
    m2 = jnp.max(lg2, axis=-1, keepdims=True)
    e2 = jnp.min(jnp.where(lg2 == m2, lane_f, float(LANES)), axis=-1, keepdims=True).astype(I32)
    ex = jnp.exp(m2 - m1)
    g1 = 1.0 / (1.0 + ex)
    g2 = ex / (1.0 + ex)
    onehot = jnp.where(jnp.logical_or(lane == e1, lane == e2), 1.0, 0.0)
    r = lax.broadcasted_iota(I32, (tm, tm), 0)
    c = lax.broadcasted_iota(I32, (tm, tm), 1)
    before = _dot(jnp.where(c < r, 1.0, 0.0).astype(BF16), onehot.astype(BF16)) + carry_ref[0:1, :]
    r1 = jnp.sum(jnp.where(lane == e1, before, 0.0), axis=-1, keepdims=True).astype(I32)
    r2 = jnp.sum(jnp.where(lane == e2, before, 0.0), axis=-1, keepdims=True).astype(I32)
    info = jnp.where(lane == 0, e1, jnp.where(lane == 1, e2, jnp.where(lane == 2, r1, r2)))
    info_ref[...] = info
    gate_ref[...] = jnp.where(lane == 0, g1, g2)
    total = carry_ref[0:1, :] + jnp.sum(onehot, axis=0, keepdims=True)
    carry_ref[...] = jnp.broadcast_to(total, carry_ref.shape)
    cnt_ref[...] = jnp.broadcast_to(total, cnt_ref.shape)


def _route(logits, *, tm=256, name):
    n = logits.shape[0]
    tm = min(tm, n)
    return pl.pallas_call(
        functools.partial(_route_kernel, tm=tm),
        out_shape=(jax.ShapeDtypeStruct((n, LANES), I32), jax.ShapeDtypeStruct((n, LANES), F32),
                   jax.ShapeDtypeStruct((8, LANES), F32)),
        grid=(n // tm,),
        in_specs=[pl.BlockSpec((tm, LANES), lambda i: (i, 0))],
        out_specs=(pl.BlockSpec((tm, LANES), lambda i: (i, 0)),
                   pl.BlockSpec((tm, LANES), lambda i: (i, 0)),
                   pl.BlockSpec((8, LANES), lambda i: (0, 0))),
        scratch_shapes=[pltpu.VMEM((8, LANES), F32)],
        compiler_params=_cparams(("arbitrary",)),
        name=name,
    )(logits)


def _combine_ln_kernel(d0_ref, d1_ref, h_ref, ys_ref, gate_ref, g_ref, b_ref, o_ref,
                       ya0_ref, ya1_ref, yb0_ref, yb1_ref, sems_a, sems_b, *, tm, slabs):
    i = pl.program_id(0)
    nt = pl.num_programs(0)
    bufs_a, bufs_b = (ya0_ref, ya1_ref), (yb0_ref, yb1_ref)

    def gather(tile, wait):
        for idx_ref, bufs, sems in ((d0_ref, bufs_a, sems_a), (d1_ref, bufs_b, sems_b)):
            _row_gather(ys_ref, idx_ref, tile * tm, bufs, sems, tile % 2, tm, slabs, wait=wait)

    @pl.when(i == 0)
    def _():
        gather(0, wait=False)

    gather(i, wait=True)

    @pl.when(i + 1 < nt)
    def _():
        gather(i + 1, wait=False)

    for s in range(2):
        @pl.when(i % 2 == s)
        def _(s=s):
            ff = (gate_ref[:, 0:1] * _slabs_to_rows(bufs_a[s], slabs)
                  + gate_ref[:, 1:2] * _slabs_to_rows(bufs_b[s], slabs))
            o_ref[...] = _layer_norm(DEEPNORM_ALPHA * h_ref[...] + ff, g_ref[...], b_ref[...])


def _combine_ln(dest0, dest1, h, ys, gates, g, b, *, tm=256, name):
    n, d = h.shape
    slabs = d // LANES
    tm = min(tm, n)
    row = lambda i, d0, d1: (i, 0)
    const = lambda i, d0, d1: (0, 0)
    slab_buf = pltpu.VMEM((tm * slabs, LANES), F32)
    return pl.pallas_call(
        functools.partial(_combine_ln_kernel, tm=tm, slabs=slabs),
        out_shape=jax.ShapeDtypeStruct((n, d), F32),
        grid_spec=pltpu.PrefetchScalarGridSpec(
            num_scalar_prefetch=2,
            grid=(n // tm,),
            in_specs=[pl.BlockSpec((tm, d), row),
                      pl.BlockSpec(memory_space=pl.ANY),
                      pl.BlockSpec((tm, LANES), row),
                      pl.BlockSpec((1, d), const),
                      pl.BlockSpec((1, d), const)],
            out_specs=pl.BlockSpec((tm, d), row),
            scratch_shapes=[slab_buf, slab_buf, slab_buf, slab_buf,
                            pltpu.SemaphoreType.DMA((2,)), pltpu.SemaphoreType.DMA((2,))]),
        compiler_params=_cparams(("arbitrary",)),
        name=name,
    )(dest0, dest1, h, ys, gates, g, b)


def _rel_bucket(dist):
    n = jnp.maximum(dist, 0)
    exact = REL_BUCKETS // 2
    nf = jnp.maximum(n, 1).astype(F32)
    large = exact + (jnp.log(nf / exact) / math.log(REL_MAX_DIST / exact) * (REL_BUCKETS - exact)).astype(I32)
    large = jnp.minimum(large, REL_BUCKETS - 1)
    return jnp.where(n < exact, n, large)


def _bias_tile_kernel(tab_ref, bucket_ref, o_ref):
    h = pl.program_id(1)
    bucket = bucket_ref[...]
    far = tab_ref[REL_BUCKETS - 1, h]
    acc = jnp.zeros(o_ref.shape, F32)
    for b in range(REL_BUCKETS - 1):
        acc = jnp.where(bucket == b, tab_ref[b, h] - far, acc)
    o_ref[...] = acc * LOG2E


def _bias_tiles(rel_table, n_heads, t, *, name):
    assert t >= REL_MAX_DIST
    r = jnp.arange(t)[:, None]
    c = jnp.arange(t)[None, :]
    buckets = jnp.stack([_rel_bucket(r - c + off) for off in (0, t)]).astype(I32)
    return pl.pallas_call(
        _bias_tile_kernel,
        out_shape=jax.ShapeDtypeStruct((2, n_heads, t, t), F32),
        grid=(2, n_heads),
        in_specs=[pl.BlockSpec(memory_space=pltpu.SMEM),
                  pl.BlockSpec((None, t, t), lambda k, h: (k, 0, 0))],
        out_specs=pl.BlockSpec((None, None, t, t), lambda k, h: (k, h, 0, 0)),
        compiler_params=_cparams(("parallel", "parallel")),
        name=name,
    )(rel_table.astype(F32), buckets)


def _pad_cols(w, n):
    return jnp.pad(w, ((0, 0), (0, n - w.shape[1])))


def _even_layer(h, rel_table, w_in, b_forget, w_out, ln1_g, ln1_b, w1, w3, w2, ln2_g, ln2_b,
                *, batch, seq):
    d = h.shape[1]
    wa = N_MOBA_HEADS * HEAD_DIM
    wb = N_FOX_HEADS * HEAD_DIM
    n_qkv = 3 * wa + 3 * wb
    scale = HEAD_DIM ** -0.5 * LOG2E
    ones, scl = jnp.ones((wa,), F32), jnp.full((wa,), scale, F32)
    colscale = jnp.concatenate([scl, ones, ones, scl, ones, ones])[None, :]
    qkv = _mm(h, w_in[:, :n_qkv].astype(BF16), colscale, out_dtype=BF16, tm=1024, tn=768,
              name="ev_qkv_proj")
    fb = _mm(h, _pad_cols(w_in[:, n_qkv:], LANES).astype(BF16), jnp.ones((1, LANES), F32),
             out_dtype=F32, tn=LANES, name="ev_forget_proj")
    csum = _forget_csum(fb, _pad_cols(b_forget.astype(F32)[None, :], LANES), batch=batch, seq=seq,
                        n_heads=N_FOX_HEADS, name="ev_forget_csum")
    csum = csum.reshape(batch, N_FOX_HEADS, seq // ATTN_TILE, 1, ATTN_TILE)
    oa = _moba_attention(qkv, _bias_tiles(rel_table, N_MOBA_HEADS, MOBA_BLOCK, name="ev_bias_tiles"),
                         batch=batch, seq=seq, name="ev_moba_attn")
    ob = _fox_attention(qkv, csum, batch=batch, seq=seq, name="ev_fox_attn")
    attn = jnp.concatenate([oa, ob], axis=-1)
    h = _mm_ln(attn, w_out.astype(BF16), h, ln1_g[None, :], ln1_b[None, :], name="ev_out_proj_ln")
    return _ffn_ln(h, w1.astype(BF16), w3.astype(BF16), w2.astype(BF16),
                   ln2_g[None, :], ln2_b[None, :], name="ev_swiglu_ln")


def _moe(h, h_slabs, logits, w1, w3, w2, ln_g, ln_b):
    n, d = h.shape
    tm = EXPERT_TILE
    info, gates, cnt = _route(logits, name="od_route")
    counts = cnt[0, :N_EXPERTS].astype(I32)
    padded = (counts + tm - 1) // tm * tm
    pend = jnp.cumsum(padded)
    pstart = pend - padded
    e = info[:, :MOE_TOPK]
    dest = (pstart[e] + info[:, MOE_TOPK:2 * MOE_TOPK]).astype(I32)
    n_rows = -(-(n * MOE_TOPK + N_EXPERTS * (tm - 1)) // tm) * tm
    n_tiles = n_rows // tm
    tile_start = jnp.arange(n_tiles, dtype=I32) * tm
    tile_valid = (tile_start < pend[-1]).astype(I32)
    last = jnp.maximum(pend[-1] - 1, 0)
    tile_e = jnp.minimum(jnp.searchsorted(pend, jnp.minimum(tile_start, last), side="right"),
                         N_EXPERTS - 1).astype(I32)
    tok = jnp.arange(n, dtype=I32)
    row_tok = jnp.zeros((n_rows,), I32).at[dest.T.reshape(-1)].set(jnp.concatenate([tok, tok]))
    ys = _moe_ffn(tile_e, tile_valid, row_tok, h_slabs, w1.astype(BF16),
                  w3.astype(BF16), w2.astype(BF16), tm=tm, name="od_moe_swiglu")
    return _combine_ln(dest[:, 0], dest[:, 1], h, ys, gates, ln_g[None, :], ln_b[None, :],
                       name="od_moe_combine_ln")


def _odd_layer(h, rel_table, w_in, q_norm_g, kv_norm_g, w_uq, w_qidx, w_uk, w_uv, w_out,
               ln1_g, ln1_b, router, w1, w3, w2, ln2_g, ln2_b, *, batch, seq):
    nh = N_DSA_HEADS
    rq, rkv = DSA_Q_RANK, DSA_KV_RANK
    scale = HEAD_DIM ** -0.5 * LOG2E
    w_in_p = jnp.concatenate([w_in[:, :rq + rkv],
                              _pad_cols(w_in[:, rq + rkv:rq + rkv + IDX_DIM], LANES),
                              _pad_cols(w_in[:, rq + rkv + IDX_DIM:], LANES)], axis=1)
    cq, ckv, kidx, widx = _dsa_in_proj(h, w_in_p.astype(BF16), q_norm_g[None, :], kv_norm_g[None, :],
                                       name="od_in_proj_rms")
    w_qidx_p = jnp.pad(w_qidx.reshape(rq, IDX_HEADS, IDX_DIM), ((0, 0), (0, 0), (0, LANES - IDX_DIM)))
    wq = jnp.concatenate([w_uq, w_qidx_p.reshape(rq, IDX_HEADS * LANES)], axis=1)
    qscale = jnp.concatenate([jnp.full((nh * HEAD_DIM,), scale, F32),
                              jnp.ones((IDX_HEADS * LANES,), F32)])[None, :]
    qh = _mm(cq, wq.astype(BF16), qscale, out_dtype=BF16, head_major=True, tm=512, tn=2048,
             name="od_q_proj")
    wkv = jnp.concatenate([w_uk.transpose(1, 0, 2).reshape(rkv, nh * HEAD_DIM),
                           w_uv.transpose(1, 0, 2).reshape(rkv, nh * HEAD_DIM)], axis=1)
    kvh = _mm(ckv, wkv.astype(BF16), jnp.ones((1, 2 * nh * HEAD_DIM), F32), out_dtype=BF16,
              head_major=True, tm=512, tn=2048, name="od_kv_proj")
    scores, thr, jcut = _idx_topk(qh, kidx, widx, batch=batch, seq=seq,
                                  ksel=min(DSA_TOPK_MAX, seq // 4), name="od_idx_topk")
    attn = _dsa_attention(qh, kvh, scores, thr, jcut,
                          _bias_tiles(rel_table, nh, ATTN_TILE, name="od_bias_tiles"),
                          batch=batch, seq=seq, name="od_dsa_attn")
    h, h_slabs, logits = _mm_ln_route(attn, w_out.astype(BF16), h, ln1_g[None, :], ln1_b[None, :],
                                      _pad_cols(router, LANES).astype(BF16), name="od_out_proj_ln")
    return _moe(h, h_slabs, logits, w1, w3, w2, ln2_g, ln2_b)


def kernel(x, rel_table, ev_w_in, ev_b_forget, ev_w_out, ev_ln1_g, ev_ln1_b, ev_ffn_w1, ev_ffn_w3, ev_ffn_w2, ev_ln2_g, ev_ln2_b, od_w_in, od_q_norm_g, od_kv_norm_g, od_w_uq, od_w_qidx, od_w_uk, od_w_uv, od_w_out, od_ln1_g, od_ln1_b, od_router, od_exp_w1, od_exp_w3, od_exp_w2, od_ln2_g, od_ln2_b):
    batch, seq, d = x.shape
    h = x.reshape(batch * seq, d)
    for layer in range(DEPTH):
        i = layer // 2
        if layer % 2 == 0:
            h = _even_layer(h, rel_table, ev_w_in[i], ev_b_forget[i], ev_w_out[i], ev_ln1_g[i],
                            ev_ln1_b[i], ev_ffn_w1[i], ev_ffn_w3[i], ev_ffn_w2[i], ev_ln2_g[i],
                            ev_ln2_b[i], batch=batch, seq=seq)
        else:
            h = _odd_layer(h, rel_table, od_w_in[i], od_q_norm_g[i], od_kv_norm_g[i], od_w_uq[i],
                           od_w_qidx[i], od_w_uk[i], od_w_uv[i], od_w_out[i], od_ln1_g[i],
                           od_ln1_b[i], od_router[i], od_exp_w1[i], od_exp_w3[i], od_exp_w2[i],
                           od_ln2_g[i], od_ln2_b[i], batch=batch, seq=seq)
    return h.reshape(batch, seq, d)
```

```python
import functools
import math

import jax
import jax.numpy as jnp
from jax import lax
from jax.experimental import pallas as pl
from jax.experimental.pallas import tpu as pltpu

F32 = jnp.float32
BF16 = jnp.bfloat16
I32 = jnp.int32

HEAD_DIM = 128
N_MOBA_HEADS = 8
N_FOX_HEADS = 8
MOBA_BLOCK = 256
MOBA_TOPK = 3
N_DSA_HEADS = 16
DSA_Q_RANK = 512
DSA_KV_RANK = 512
IDX_HEADS = 16
IDX_DIM = 64
DSA_TOPK_MAX = 256
REL_BUCKETS = 32
REL_MAX_DIST = 128
N_EXPERTS = 8
MOE_TOPK = 2
LN_EPS = 1e-5
RMS_EPS = 1e-6
DEPTH = 2
DEEPNORM_ALPHA = (2 * DEPTH) ** 0.25

LANES = 128
ATTN_TILE = 256
WIDE_CHUNK = 1024
DSA_KEY_TILE = 512
DSA_HEAD_GROUP = 4
SCORE_STRIP = 64
COUNT_STRIP = 128
EXPERT_TILE = 512
ROW_DMA_UNROLL = 8
NEG = -1e30
LOG2E = math.log2(math.e)
INT_MIN = -(2 ** 31)
VMEM_LIMIT = 56 * 1024 * 1024


def _cparams(sem, vmem=VMEM_LIMIT):
    return pltpu.CompilerParams(dimension_semantics=sem, vmem_limit_bytes=vmem)


def _dot(a, b):
    return jnp.dot(a, b, preferred_element_type=F32)


def _dot_t(a, b):
    return lax.dot_general(a, b, (((1,), (1,)), ((), ())), preferred_element_type=F32)


def _layer_norm(y, g, b):
    mu = jnp.mean(y, axis=-1, keepdims=True)
    d = y - mu
    var = jnp.mean(d * d, axis=-1, keepdims=True)
    return d * lax.rsqrt(var + LN_EPS) * g + b


def _mm_kernel(x_ref, w_ref, cs_ref, o_ref, *, head_major):
    acc = _dot(x_ref[...].astype(BF16), w_ref[...]) * cs_ref[...]
    if head_major:
        for c in range(o_ref.shape[0]):
            o_ref[c] = acc[:, c * LANES:(c + 1) * LANES].astype(o_ref.dtype)
    else:
        o_ref[...] = acc.astype(o_ref.dtype)


def _mm(x, w, colscale, *, out_dtype, head_major=False, tm=512, tn=512, name):
    m, k = x.shape
    n = w.shape[1]
    tm, tn = min(tm, m), min(tn, n)
    assert m % tm == 0 and n % tn == 0 and tn % LANES == 0
    if head_major:
        out_shape = jax.ShapeDtypeStruct((n // LANES, m, LANES), out_dtype)
        out_spec = pl.BlockSpec((tn // LANES, tm, LANES), lambda i, j: (j, i, 0))
    else:
        out_shape = jax.ShapeDtypeStruct((m, n), out_dtype)
        out_spec = pl.BlockSpec((tm, tn), lambda i, j: (i, j))
    return pl.pallas_call(
        functools.partial(_mm_kernel, head_major=head_major),
        out_shape=out_shape,
        grid=(m // tm, n // tn),
        in_specs=[pl.BlockSpec((tm, k), lambda i, j: (i, 0)),
                  pl.BlockSpec((k, tn), lambda i, j: (0, j)),
                  pl.BlockSpec((1, tn), lambda i, j: (0, j))],
        out_specs=out_spec,
        compiler_params=_cparams(("parallel", "arbitrary")),
        name=name,
    )(x, w, colscale)


def _mm_ln_kernel(x_ref, w_ref, res_ref, g_ref, b_ref, o_ref):
    y = DEEPNORM_ALPHA * res_ref[...] + _dot(x_ref[...], w_ref[...])
    o_ref[...] = _layer_norm(y, g_ref[...], b_ref[...])


def _mm_ln(x, w, res, g, b, *, tm=512, name):
    m, k = x.shape
    d = w.shape[1]
    tm = min(tm, m)
    return pl.pallas_call(
        _mm_ln_kernel,
        out_shape=jax.ShapeDtypeStruct((m, d), F32),
        grid=(m // tm,),
        in_specs=[pl.BlockSpec((tm, k), lambda i: (i, 0)),
                  pl.BlockSpec((k, d), lambda i: (0, 0)),
                  pl.BlockSpec((tm, d), lambda i: (i, 0)),
                  pl.BlockSpec((1, d), lambda i: (0, 0)),
                  pl.BlockSpec((1, d), lambda i: (0, 0))],
        out_specs=pl.BlockSpec((tm, d), lambda i: (i, 0)),
        compiler_params=_cparams(("parallel",)),
        name=name,
    )(x, w, res, g, b)


def _mm_ln_route_kernel(x_ref, w_ref, res_ref, g_ref, b_ref, wr_ref, o_ref, slab_ref, lg_ref,
                        *, slabs):
    y = _layer_norm(DEEPNORM_ALPHA * res_ref[...] + _dot(x_ref[...], w_ref[...]),
                    g_ref[...], b_ref[...])
    o_ref[...] = y
    tm = y.shape[0]
    for c in range(slabs):
        slab_ref[pl.ds(c, tm, stride=slabs), :] = y[:, c * LANES:(c + 1) * LANES]
    lg_ref[...] = _dot(y.astype(BF16), wr_ref[...])


def _mm_ln_route(x, w, res, g, b, w_router, *, tm=512, name):
    m, k = x.shape
    d = w.shape[1]
    tm = min(tm, m)
    slabs = d // LANES
    row = lambda i: (i, 0)
    const = lambda i: (0, 0)
    return pl.pallas_call(
        functools.partial(_mm_ln_route_kernel, slabs=slabs),
        out_shape=(jax.ShapeDtypeStruct((m, d), F32), jax.ShapeDtypeStruct((m * slabs, LANES), F32),
                   jax.ShapeDtypeStruct((m, LANES), F32)),
        grid=(m // tm,),
        in_specs=[pl.BlockSpec((tm, k), row), pl.BlockSpec((k, d), const), pl.BlockSpec((tm, d), row),
                  pl.BlockSpec((1, d), const), pl.BlockSpec((1, d), const),
                  pl.BlockSpec((d, LANES), const)],
        out_specs=(pl.BlockSpec((tm, d), row), pl.BlockSpec((tm * slabs, LANES), row),
                   pl.BlockSpec((tm, LANES), row)),
        compiler_params=_cparams(("parallel",)),
        name=name,
    )(x, w, res, g, b, w_router)


def _swiglu_step(xb, w1_ref, w3_ref, w2_ref):
    a = _dot(xb, w1_ref[...])
    c = _dot(xb, w3_ref[...])
    hmid = a / (1.0 + jnp.exp(-a)) * c
    return _dot(hmid.astype(BF16), w2_ref[...])


def _ffn_ln_kernel(x_ref, w1_ref, w3_ref, w2_ref, g_ref, b_ref, o_ref, acc_ref, xb_ref):
    f = pl.program_id(1)

    @pl.when(f == 0)
    def _():
        acc_ref[...] = jnp.zeros_like(acc_ref)
        xb_ref[...] = x_ref[...].astype(BF16)

    acc_ref[...] += _swiglu_step(xb_ref[...], w1_ref, w3_ref, w2_ref)

    @pl.when(f == pl.num_programs(1) - 1)
    def _():
        y = DEEPNORM_ALPHA * x_ref[...] + acc_ref[...]
        o_ref[...] = _layer_norm(y, g_ref[...], b_ref[...])


def _ffn_ln(x, w1, w3, w2, g, b, *, tm=512, tf=512, name):
    m, d = x.shape
    dff = w1.shape[1]
    tm, tf = min(tm, m), min(tf, dff)
    assert m % tm == 0 and dff % tf == 0
    return pl.pallas_call(
        _ffn_ln_kernel,
        out_shape=jax.ShapeDtypeStruct((m, d), F32),
        grid=(m // tm, dff // tf),
        in_specs=[pl.BlockSpec((tm, d), lambda i, f: (i, 0)),
                  pl.BlockSpec((d, tf), lambda i, f: (0, f)),
                  pl.BlockSpec((d, tf), lambda i, f: (0, f)),
                  pl.BlockSpec((tf, d), lambda i, f: (f, 0)),
                  pl.BlockSpec((1, d), lambda i, f: (0, 0)),
                  pl.BlockSpec((1, d), lambda i, f: (0, 0))],
        out_specs=pl.BlockSpec((tm, d), lambda i, f: (i, 0)),
        scratch_shapes=[pltpu.VMEM((tm, d), F32), pltpu.VMEM((tm, d), BF16)],
        compiler_params=_cparams(("parallel", "arbitrary")),
        name=name,
    )(x, w1, w3, w2, g, b)


def _row_gather(src_hbm, idx_ref, base, bufs, sems, slot, n_rows, slabs, *, wait):
    for s, buf in enumerate(bufs):
        @pl.when(slot == s)
        def _(s=s, buf=buf):
            def body(r, _):
                first = 0 if wait else pl.multiple_of(idx_ref[base + r], slabs)
                src = src_hbm.at[pl.ds(first, slabs)]
                dst = buf.at[pl.ds(pl.multiple_of(r * slabs, slabs), slabs)]
                cp = pltpu.make_async_copy(src, dst, sems.at[s])
                if wait:
                    cp.wait()
                else:
                    cp.start()
                return 0
            lax.fori_loop(0, n_rows, body, 0, unroll=ROW_DMA_UNROLL)


def _slabs_to_rows(buf, slabs):
    rows = buf.shape[0] // slabs
    return jnp.concatenate([buf[pl.ds(c, rows, stride=slabs), :] for c in range(slabs)], axis=1)


def _moe_ffn_kernel(te_ref, tv_ref, tok_ref, h_ref, w1_ref, w3_ref, w2_ref, o_ref,
                    acc_ref, xb_ref, xg0_ref, xg1_ref, sems, *, tm, slabs):
    i, f = pl.program_id(0), pl.program_id(1)
    nt = pl.num_programs(0)
    bufs = (xg0_ref, xg1_ref)
    gather = functools.partial(_row_gather, h_ref, tok_ref, bufs=bufs, sems=sems, n_rows=tm,
                               slabs=slabs)

    @pl.when(tv_ref[i] > 0)
    def _():
        @pl.when(f == 0)
        def _():
            @pl.when(i == 0)
            def _():
                gather(base=0, slot=0, wait=False)

            gather(base=i * tm, slot=i % 2, wait=True)
            nxt = jnp.minimum(i + 1, nt - 1)

            @pl.when(jnp.logical_and(i + 1 < nt, tv_ref[nxt] > 0))
            def _():
                gather(base=nxt * tm, slot=nxt % 2, wait=False)

            for s, buf in enumerate(bufs):
                @pl.when(i % 2 == s)
                def _(buf=buf):
                    xb_ref[...] = _slabs_to_rows(buf, slabs).astype(BF16)

            acc_ref[...] = jnp.zeros_like(acc_ref)

        acc_ref[...] += _swiglu_step(xb_ref[...], w1_ref, w3_ref, w2_ref)

        @pl.when(f == pl.num_programs(1) - 1)
        def _():
            for c in range(slabs):
                o_ref[pl.ds(c, tm, stride=slabs), :] = acc_ref[:, c * LANES:(c + 1) * LANES]

    @pl.when(tv_ref[i] == 0)
    def _():
        o_ref[...] = jnp.zeros_like(o_ref)


def _moe_ffn(tile_e, tile_valid, row_tok, h_slabs, w1, w3, w2, *, tm, tf=1024, name):
    n_rows = row_tok.shape[0]
    d = w1.shape[1]
    slabs = d // LANES
    dff = w1.shape[2]
    tf = min(tf, dff)
    nf = dff // tf
    assert n_rows % tm == 0 and dff % tf == 0

    def fidx(i, f, tv):
        return jnp.where(tv[i] > 0, f, nf - 1)

    return pl.pallas_call(
        functools.partial(_moe_ffn_kernel, tm=tm, slabs=slabs),
        out_shape=jax.ShapeDtypeStruct((n_rows * slabs, LANES), F32),
        grid_spec=pltpu.PrefetchScalarGridSpec(
            num_scalar_prefetch=3,
            grid=(n_rows // tm, nf),
            in_specs=[pl.BlockSpec(memory_space=pl.ANY),
                      pl.BlockSpec((None, d, tf), lambda i, f, te, tv, tok: (te[i], 0, fidx(i, f, tv))),
                      pl.BlockSpec((None, d, tf), lambda i, f, te, tv, tok: (te[i], 0, fidx(i, f, tv))),
                      pl.BlockSpec((None, tf, d), lambda i, f, te, tv, tok: (te[i], fidx(i, f, tv), 0))],
            out_specs=pl.BlockSpec((tm * slabs, LANES), lambda i, f, te, tv, tok: (i, 0)),
            scratch_shapes=[pltpu.VMEM((tm, d), F32), pltpu.VMEM((tm, d), BF16),
                            pltpu.VMEM((tm * slabs, LANES), F32), pltpu.VMEM((tm * slabs, LANES), F32),
                            pltpu.SemaphoreType.DMA((2,))]),
        compiler_params=_cparams(("arbitrary", "arbitrary")),
        name=name,
    )(tile_e, tile_valid, row_tok, h_slabs, w1, w3, w2)


def _online_softmax_step(s, v, carry):
    m, acc = carry
    m_new = jnp.maximum(m, jnp.max(s, axis=-1, keepdims=True))
    alpha = jnp.exp2(m - m_new)
    p = jnp.exp2((s - m_new).astype(BF16))
    return m_new, alpha * acc + _dot(p, _with_ones(v))


def _with_ones(v):
    return jnp.concatenate([v, jnp.ones_like(v)], axis=1)


def _softmax_init(tq):
    return jnp.full((tq, 1), NEG, F32), jnp.zeros((tq, 2 * HEAD_DIM), F32)


def _softmax_finish(acc):
    return acc[:, :HEAD_DIM] / acc[:, HEAD_DIM:]


def _pairwise_loop(n, body, carry):
    carry = lax.fori_loop(0, n // 2, lambda k, c: body(2 * k + 1, body(2 * k, c)), carry)
    return lax.fori_loop(n // 2 * 2, n, body, carry)


def _head_cols(h):
    return slice(h * HEAD_DIM, (h + 1) * HEAD_DIM)


def _kv_block(k_ref, v_ref, n, width, h):
    rows = pl.ds(pl.multiple_of(n * width, width), width)
    return k_ref[rows, _head_cols(h)], v_ref[rows, _head_cols(h)]


def _causal_mask(blk):
    r = lax.broadcasted_iota(I32, (blk, blk), 0)
    c = lax.broadcasted_iota(I32, (blk, blk), 1)
    return c <= r


def _moba_select(q, km, i, blk):
    nbp = km.shape[0]
    gate = _dot_t(q, km.astype(BF16))
    lane = lax.broadcasted_iota(I32, (blk, nbp), 1)
    lane_f = lane.astype(F32)
    g = jnp.where(lane < i, gate, -jnp.inf)
    sel = jnp.zeros((blk, nbp), F32)
    for _ in range(MOBA_TOPK):
        mx = jnp.max(g, axis=-1, keepdims=True)
        first = jnp.min(jnp.where(g == mx, lane_f, float(nbp)), axis=-1, keepdims=True)
        pick = jnp.logical_and(lane_f == first, mx > -jnp.inf)
        sel = jnp.where(pick, 1.0, sel)
        g = jnp.where(pick, -jnp.inf, g)
    return sel, lane


def _moba_kernel(q_ref, k_ref, v_ref, bias_ref, o_ref, km_ref, *, blk, nb, wide, hp):
    i = pl.program_id(2)
    per = wide // blk

    @pl.when(i == 0)
    def _():
        km_ref[...] = jnp.zeros_like(km_ref)

        def mean_body(n, _):
            for h in range(hp):
                kblk, _ = _kv_block(k_ref, v_ref, n, blk, h)
                km_ref[h, pl.ds(n, 1), :] = jnp.mean(kblk.astype(F32), axis=0, keepdims=True)
            return 0

        lax.fori_loop(0, nb, mean_body, 0)

    qs = [q_ref[:, _head_cols(h)] for h in range(hp)]
    sels = [_moba_select(qs[h], km_ref[h], i, blk) for h in range(hp)]

    def scores(n):
        return [_dot_t(qs[h], _kv_block(k_ref, v_ref, n, wide, h)[0]) for h in range(hp)]

    def update(n, logits, carries, near):
        causal = _causal_mask(blk)
        out = []
        for h in range(hp):
            s, vblk = logits[h], _kv_block(k_ref, v_ref, n, wide, h)[1]
            sel, lane = sels[h]
            parts = []
            for u in range(per):
                blk_id = n * per + u
                part = s[:, u * blk:(u + 1) * blk]
                chosen = jnp.sum(jnp.where(lane == blk_id, sel, 0.0), axis=-1, keepdims=True) > 0.0
                if near:
                    part = part + jnp.where(blk_id == i, bias_ref[0, h],
                                            jnp.where(blk_id == i - 1, bias_ref[1, h], 0.0))
                    parts.append(jnp.where(blk_id == i, jnp.where(causal, part, NEG),
                                           jnp.where(chosen, part, NEG)))
                else:
                    parts.append(jnp.where(chosen, part, NEG))
            s = parts[0] if per == 1 else jnp.concatenate(parts, axis=1)
            out.append(_online_softmax_step(s, vblk, carries[h]))
        return tuple(out)

    first_near = jnp.maximum(i - 1, 0) // per
    carries = tuple(_softmax_init(blk) for _ in range(hp))
    carries = _pairwise_loop(first_near, lambda n, c: update(n, scores(n), c, False), carries)
    carries = lax.fori_loop(first_near, i // per + 1,
                            lambda n, c: update(n, scores(n), c, True), carries)
    for h in range(hp):
        o_ref[:, _head_cols(h)] = _softmax_finish(carries[h][1]).astype(o_ref.dtype)


def _moba_attention(qkv, bias, *, batch, seq, hp=4, name):
    blk = MOBA_BLOCK
    nb = seq // blk
    nbp = -(-nb // LANES) * LANES
    h = N_MOBA_HEADS
    nq = seq // blk
    hg = h // hp
    w = hp * HEAD_DIM
    return pl.pallas_call(
        functools.partial(_moba_kernel, blk=blk, nb=nb, wide=min(WIDE_CHUNK, seq), hp=hp),
        out_shape=jax.ShapeDtypeStruct((batch * seq, h * HEAD_DIM), BF16),
        grid=(batch, hg, nq),
        in_specs=[pl.BlockSpec((blk, w), lambda b, g, i: (b * nq + i, g)),
                  pl.BlockSpec((seq, w), lambda b, g, i: (b, hg + g)),
                  pl.BlockSpec((seq, w), lambda b, g, i: (b, 2 * hg + g)),
                  pl.BlockSpec((2, hp, blk, blk), lambda b, g, i: (0, g, 0, 0))],
        out_specs=pl.BlockSpec((blk, w), lambda b, g, i: (b * nq + i, g)),
        scratch_shapes=[pltpu.VMEM((hp, nbp, HEAD_DIM), F32)],
        compiler_params=_cparams(("parallel", "parallel", "arbitrary")),
        name=name,
    )(qkv, qkv, qkv, bias)


def _fox_kernel(q_ref, k_ref, v_ref, ck_ref, o_ref, *, blk, wide, hp):
    i = pl.program_id(2)
    per = wide // blk
    qs = [q_ref[:, _head_cols(h)] for h in range(hp)]

    def scores(n):
        return [_dot_t(qs[h], _kv_block(k_ref, v_ref, n, wide, h)[0]) for h in range(hp)]

    def update(n, logits, carries, causal):
        if causal:
            row = i * blk + lax.broadcasted_iota(I32, (blk, wide), 0)
            col = n * wide + lax.broadcasted_iota(I32, (blk, wide), 1)
            visible = col <= row
        out = []
        for h in range(hp):
            ck = [ck_ref[h, n * per + u] for u in range(per)]
            s = logits[h] - (ck[0] if per == 1 else jnp.concatenate(ck, axis=1))
            if causal:
                s = jnp.where(visible, s, NEG)
            out.append(_online_softmax_step(s, _kv_block(k_ref, v_ref, n, wide, h)[1], carries[h]))
        return tuple(out)

    n_wide = i // per
    carries = tuple(_softmax_init(blk) for _ in range(hp))
    carries = _pairwise_loop(n_wide, lambda n, c: update(n, scores(n), c, False), carries)
    carries = update(n_wide, scores(n_wide), carries, True)
    for h in range(hp):
        o_ref[:, _head_cols(h)] = _softmax_finish(carries[h][1]).astype(o_ref.dtype)


def _fox_attention(qkv, csum, *, batch, seq, hp=4, name):
    blk = ATTN_TILE
    h = N_FOX_HEADS
    nq = seq // blk
    hg = h // hp
    base = 3 * N_MOBA_HEADS // hp
    w = hp * HEAD_DIM
    return pl.pallas_call(
        functools.partial(_fox_kernel, blk=blk, wide=min(WIDE_CHUNK, seq), hp=hp),
        out_shape=jax.ShapeDtypeStruct((batch * seq, h * HEAD_DIM), BF16),
        grid=(batch, hg, nq),
        in_specs=[pl.BlockSpec((blk, w), lambda b, g, i: (b * nq + i, base + g)),
                  pl.BlockSpec((seq, w), lambda b, g, i: (b, base + hg + g)),
                  pl.BlockSpec((seq, w), lambda b, g, i: (b, base + 2 * hg + g)),
                  pl.BlockSpec((None, hp, nq, 1, blk), lambda b, g, i: (b, g, 0, 0, 0))],
        out_specs=pl.BlockSpec((blk, w), lambda b, g, i: (b * nq + i, g)),
        compiler_params=_cparams(("parallel", "parallel", "arbitrary")),
        name=name,
    )(qkv, qkv, qkv, csum)


def _forget_csum_kernel(fb_ref, b_ref, o_ref, carry_ref, *, tc):
    @pl.when(pl.program_id(1) == 0)
    def _():
        carry_ref[...] = jnp.zeros_like(carry_ref)

    z = fb_ref[...] + b_ref[...]
    run = jnp.minimum(z, 0.0) - jnp.log1p(jnp.exp(-jnp.abs(z)))
    row = lax.broadcasted_iota(I32, (tc, LANES), 0)
    shift = 1
    while shift < tc:
        run = run + jnp.where(row >= shift, pltpu.roll(run, shift, axis=0), 0.0)
        shift *= 2
    run = run + carry_ref[0:1, :]
    carry_ref[...] = jnp.broadcast_to(run[tc - 1:tc, :], carry_ref.shape)
    o_ref[...] = (run * LOG2E).T[:o_ref.shape[0], :]


def _forget_csum(fb, bias, *, batch, seq, n_heads, tc=512, name):
    tc = min(tc, seq)
    nt = seq // tc
    return pl.pallas_call(
        functools.partial(_forget_csum_kernel, tc=tc),
        out_shape=jax.ShapeDtypeStruct((batch, n_heads, seq), F32),
        grid=(batch, nt),
        in_specs=[pl.BlockSpec((tc, LANES), lambda b, j: (b * nt + j, 0)),
                  pl.BlockSpec((1, LANES), lambda b, j: (0, 0))],
        out_specs=pl.BlockSpec((None, n_heads, tc), lambda b, j: (b, 0, j)),
        scratch_shapes=[pltpu.VMEM((8, LANES), F32)],
        compiler_params=_cparams(("parallel", "arbitrary")),
        name=name,
    )(fb, bias)


def _dsa_in_kernel(x_ref, w_ref, gq_ref, gkv_ref, cq_ref, ckv_ref, kidx_ref, widx_ref):
    acc = _dot(x_ref[...].astype(BF16), w_ref[...])
    rq, rkv = DSA_Q_RANK, DSA_KV_RANK

    def rms(z, g):
        return z * lax.rsqrt(jnp.mean(z * z, axis=-1, keepdims=True) + RMS_EPS) * g

    cq_ref[...] = rms(acc[:, :rq], gq_ref[...]).astype(cq_ref.dtype)
    ckv_ref[...] = rms(acc[:, rq:rq + rkv], gkv_ref[...]).astype(ckv_ref.dtype)
    kidx_ref[...] = acc[:, rq + rkv:rq + rkv + LANES].astype(kidx_ref.dtype)
    widx_ref[...] = acc[:, rq + rkv + LANES:] * (IDX_HEADS ** -0.5 * IDX_DIM ** -0.5)


def _dsa_in_proj(x, w, gq, gkv, *, tm=512, name):
    m, k = x.shape
    n = w.shape[1]
    tm = min(tm, m)
    rq, rkv = DSA_Q_RANK, DSA_KV_RANK
    row = lambda i: (i, 0)
    const = lambda i: (0, 0)
    return pl.pallas_call(
        _dsa_in_kernel,
        out_shape=(jax.ShapeDtypeStruct((m, rq), BF16), jax.ShapeDtypeStruct((m, rkv), BF16),
                   jax.ShapeDtypeStruct((m, LANES), BF16), jax.ShapeDtypeStruct((m, LANES), F32)),
        grid=(m // tm,),
        in_specs=[pl.BlockSpec((tm, k), row), pl.BlockSpec((k, n), const),
                  pl.BlockSpec((1, rq), const), pl.BlockSpec((1, rkv), const)],
        out_specs=(pl.BlockSpec((tm, rq), row), pl.BlockSpec((tm, rkv), row),
                   pl.BlockSpec((tm, LANES), row), pl.BlockSpec((tm, LANES), row)),
        compiler_params=_cparams(("parallel",)),
        name=name,
    )(x, w, gq, gkv)


def _fold_lanes(x):
    part = x[:, :LANES]
    for g in range(1, x.shape[1] // LANES):
        part = part + x[:, g * LANES:(g + 1) * LANES]
    return part


def _idx_topk_kernel(qi_ref, kj_ref, q_ref, k_ref, w_ref, sc_ref, thr_ref, jc_ref, keys_ref,
                     *, t, tk, ksel, seq):
    p = pl.program_id(1)
    i, j = qi_ref[p], kj_ref[p]
    k = k_ref[...]
    for s0 in range(0, t, SCORE_STRIP):
        rows = slice(s0, s0 + SCORE_STRIP)
        stacked = jnp.concatenate([q_ref[h, rows, :] for h in range(IDX_HEADS)], axis=0)
        d = _dot_t(stacked, k)
        acc = jnp.zeros((SCORE_STRIP, tk), F32)
        for h in range(IDX_HEADS):
            part = d[h * SCORE_STRIP:(h + 1) * SCORE_STRIP]
            acc = acc + jnp.maximum(part, 0.0) * w_ref[rows, h:h + 1]
        sc_ref[rows, :] = acc
        row = i * t + s0 + lax.broadcasted_iota(I32, (SCORE_STRIP, tk), 0)
        lane_col = lax.broadcasted_iota(I32, (SCORE_STRIP, tk), 1)
        bits = pltpu.bitcast(acc, I32)
        key = jnp.where(bits < 0, bits ^ jnp.int32(0x7FFFFFFF), bits)
        key = jnp.where(bits == jnp.int32(INT_MIN), 0, key)
        keys_ref[j, rows, :] = jnp.where(j * tk + lane_col <= row, key, jnp.int32(INT_MIN))
    j_last = (i * t + t - 1) // tk

    @pl.when(j == j_last)
    def _():
        lane = lax.broadcasted_iota(I32, (COUNT_STRIP, LANES), 1)
        strips = [slice(s0, s0 + COUNT_STRIP) for s0 in range(0, t, COUNT_STRIP)]

        def count(ref, pred, *row_args):
            dt = ref.dtype
            reps = [[jnp.broadcast_to(a[rows], (COUNT_STRIP, LANES)).astype(dt) for a in row_args]
                    for rows in strips]
            cnts = []
            for rows, args in zip(strips, reps):
                def body(c, cnt, rows=rows, args=args):
                    for g in range(tk // LANES):
                        keys = ref[c, rows, g * LANES:(g + 1) * LANES]
                        cnt = cnt + pred(keys, c * tk + g * LANES, *args).astype(dt)
                    return cnt

                cnts.append(_pairwise_loop(j_last + 1, body, jnp.zeros((COUNT_STRIP, LANES), dt)))
            cnt = jnp.concatenate(cnts, axis=0)
            return jnp.sum(cnt.astype(F32), axis=-1, keepdims=True).astype(I32)

        row1 = i * t + lax.broadcasted_iota(I32, (t, 1), 0)
        full = row1 >= ksel

        def unsettled(state):
            _, lo, hi, settled = state
            return jnp.logical_and(settled == 0, hi - lo != 1)

        def probe(state):
            step, lo, hi, settled = state
            cand = lo + lax.shift_right_logical(hi - lo, 1)
            n_ge = count(keys_ref, lambda k, c0, cd: k >= cd, cand)
            move = jnp.logical_and(settled == 0, n_ge >= ksel)
            stop = jnp.logical_and(settled == 0, n_ge < ksel)
            return (step + 1, jnp.where(move, cand, lo), jnp.where(stop, cand, hi),
                    jnp.where(n_ge == ksel, 1, settled))

        def searching(state):
            busy = jnp.max(jnp.where(unsettled(state), 1.0, 0.0)) > 0.0
            return jnp.logical_and(state[0] < 34, busy)

        start = (jnp.int32(0), jnp.full((t, 1), INT_MIN, I32), jnp.full((t, 1), 2 ** 31 - 1, I32),
                 jnp.where(full, 0, 1))
        _, ans, _, settled = lax.while_loop(searching, probe, start)

        def tie_counts(_):
            return (ksel - count(keys_ref, lambda k, c0, a: k > a, ans),
                    count(keys_ref, lambda k, c0, a: k == a, ans))

        exact = jnp.max(jnp.where(settled == 0, 1.0, 0.0)) > 0.0
        need, ties = lax.cond(exact, tie_counts,
                              lambda _: (jnp.zeros((t, 1), I32), jnp.zeros((t, 1), I32)), 0)

        def tie_cut(_):
            nbits = seq.bit_length() - 1

            def bit_body(b, cut):
                cand = cut + lax.shift_left(jnp.int32(1), nbits - 1 - b)
                below = count(keys_ref, lambda k, c0, a, cd: jnp.logical_and(k == a, c0 + lane < cd),
                              ans, cand)
                return jnp.where(below < need, cand, cut)
            return lax.fori_loop(0, nbits, bit_body, jnp.zeros((t, 1), I32))

        contested = jnp.max(jnp.where(jnp.logical_and(full, ties > need), 1.0, 0.0)) > 0.0
        cut = lax.cond(contested, tie_cut, lambda _: jnp.full((t, 1), seq - 1, I32), 0)
        tbits = jnp.where(ans < 0, ans ^ jnp.int32(0x7FFFFFFF), ans)
        thr = jnp.where(full, pltpu.bitcast(tbits, F32), -jnp.inf)
        thr_ref[...] = jnp.broadcast_to(thr, thr_ref.shape)
        jc_ref[...] = jnp.broadcast_to(jnp.where(full, cut, seq - 1), jc_ref.shape)


def _causal_pairs(seq, t, tk):
    pairs = [(i, j) for i in range(seq // t) for j in range((i * t + t - 1) // tk + 1)]
    return jnp.asarray([p[0] for p in pairs], I32), jnp.asarray([p[1] for p in pairs], I32)


def _idx_topk(qh, kidx, widx, *, batch, seq, ksel, name):
    t, tk = ATTN_TILE, min(DSA_KEY_TILE, seq)
    nq, nk = seq // t, seq // tk
    assert seq & (seq - 1) == 0 and tk >= ksel
    qi, kj = _causal_pairs(seq, t, tk)
    npairs = qi.shape[0]
    qrow = lambda b, p, qi, kj: (b * nq + qi[p], 0)
    return pl.pallas_call(
        functools.partial(_idx_topk_kernel, t=t, tk=tk, ksel=ksel, seq=seq),
        out_shape=(jax.ShapeDtypeStruct((batch, npairs, t, tk), F32),
                   jax.ShapeDtypeStruct((batch * seq, LANES), F32),
                   jax.ShapeDtypeStruct((batch * seq, LANES), I32)),
        grid_spec=pltpu.PrefetchScalarGridSpec(
            num_scalar_prefetch=2,
            grid=(batch, npairs),
            in_specs=[pl.BlockSpec((IDX_HEADS, t, LANES), lambda b, p, qi, kj: (1, b * nq + qi[p], 0)),
                      pl.BlockSpec((tk, LANES), lambda b, p, qi, kj: (b * nk + kj[p], 0)),
                      pl.BlockSpec((t, LANES), qrow)],
            out_specs=(pl.BlockSpec((None, None, t, tk), lambda b, p, qi, kj: (b, p, 0, 0)),
                       pl.BlockSpec((t, LANES), qrow), pl.BlockSpec((t, LANES), qrow)),
            scratch_shapes=[pltpu.VMEM((nk, t, tk), I32)]),
        compiler_params=_cparams(("parallel", "arbitrary")),
        name=name,
    )(qi, kj, qh, kidx, widx)


def _dsa_attn_kernel(qi_ref, kj_ref, q_ref, k_ref, v_ref, sc_ref, thr_ref, jc_ref, bias_ref,
                     o_ref, m_ref, acc_ref, *, t, tk):
    p = pl.program_id(1)
    i, j = qi_ref[p], kj_ref[p]

    @pl.when(j == 0)
    def _():
        m_ref[...] = jnp.full(m_ref.shape, NEG, F32)
        acc_ref[...] = jnp.zeros_like(acc_ref)

    sc = sc_ref[...]
    thr = thr_ref[:, 0:1]
    col = j * tk + lax.broadcasted_iota(I32, (t, tk), 1)
    row = i * t + lax.broadcasted_iota(I32, (t, tk), 0)
    tie = jnp.logical_and(sc == thr, col <= jc_ref[:, 0:1])
    keep = jnp.logical_and(jnp.logical_or(sc > thr, tie), col <= row)
    offs = [i * t - (j * tk + u * t) for u in range(tk // t)]

    def run(with_bias):
        for g in range(N_DSA_HEADS // DSA_HEAD_GROUP):
            heads = [g * DSA_HEAD_GROUP + u for u in range(DSA_HEAD_GROUP)]
            logits = [_dot_t(q_ref[h], k_ref[h]) for h in heads]
            for h, s in zip(heads, logits):
                if with_bias:
                    parts = [jnp.where(off == 0, bias_ref[0, h],
                                       jnp.where(off == t, bias_ref[1, h], 0.0)) for off in offs]
                    s = s + (parts[0] if len(parts) == 1 else jnp.concatenate(parts, axis=1))
                s = jnp.where(keep, s, NEG)
                m_prev = m_ref[h]
                m_next = jnp.maximum(m_prev, jnp.max(s, axis=-1, keepdims=True))
                alpha = jnp.exp2(m_prev - m_next)
                p = jnp.exp2((s - jnp.concatenate([m_next] * (tk // HEAD_DIM), axis=1)).astype(BF16))
                acc_ref[h] = (jnp.concatenate([alpha, alpha], axis=1) * acc_ref[h]
                              + _dot(p, _with_ones(v_ref[h])))
                m_ref[h] = m_next

    near = offs[-1] <= t

    @pl.when(near)
    def _():
        run(True)

    @pl.when(jnp.logical_not(near))
    def _():
        run(False)

    @pl.when(j == (i * t + t - 1) // tk)
    def _():
        for h in range(N_DSA_HEADS):
            o_ref[:, _head_cols(h)] = _softmax_finish(acc_ref[h]).astype(o_ref.dtype)


def _dsa_attention(qh, kvh, scores, thr, jcut, bias, *, batch, seq, name):
    t, tk = ATTN_TILE, min(DSA_KEY_TILE, seq)
    nq, nk = seq // t, seq // tk
    h = N_DSA_HEADS
    qi, kj = _causal_pairs(seq, t, tk)
    qrow = lambda b, p, qi, kj: (b * nq + qi[p], 0)
    return pl.pallas_call(
        functools.partial(_dsa_attn_kernel, t=t, tk=tk),
        out_shape=jax.ShapeDtypeStruct((batch * seq, h * HEAD_DIM), BF16),
        grid_spec=pltpu.PrefetchScalarGridSpec(
            num_scalar_prefetch=2,
            grid=(batch, qi.shape[0]),
            in_specs=[pl.BlockSpec((h, t, HEAD_DIM), lambda b, p, qi, kj: (0, b * nq + qi[p], 0)),
                      pl.BlockSpec((h, tk, HEAD_DIM), lambda b, p, qi, kj: (0, b * nk + kj[p], 0)),
                      pl.BlockSpec((h, tk, HEAD_DIM), lambda b, p, qi, kj: (1, b * nk + kj[p], 0)),
                      pl.BlockSpec((None, None, t, tk), lambda b, p, qi, kj: (b, p, 0, 0)),
                      pl.BlockSpec((t, LANES), qrow),
                      pl.BlockSpec((t, LANES), qrow),
                      pl.BlockSpec((2, h, t, t), lambda b, p, qi, kj: (0, 0, 0, 0))],
            out_specs=pl.BlockSpec((t, h * HEAD_DIM), qrow),
            scratch_shapes=[pltpu.VMEM((h, t, HEAD_DIM), F32),
                            pltpu.VMEM((h, t, 2 * HEAD_DIM), F32)]),
        compiler_params=_cparams(("parallel", "arbitrary")),
        name=name,
    )(qi, kj, qh, kvh, kvh, scores, thr, jcut, bias)


def _route_kernel(lg_ref, info_ref, gate_ref, cnt_ref, carry_ref, *, tm):
    i = pl.program_id(0)

    @pl.when(i == 0)
    def _():
        carry_ref[...] = jnp.zeros_like(carry_ref)

    lane = lax.broadcasted_iota(I32, (tm, LANES), 1)
    lane_f = lane.astype(F32)
    lg = jnp.where(lane < N_EXPERTS, lg_ref[...], -jnp.inf)
    m1 = jnp.max(lg, axis=-1, keepdims=True)
    e1 = jnp.min(jnp.where(lg == m1, lane_f, float(LANES)), axis=-1, keepdims=True).astype(I32)
    lg2 = jnp.where(lane == e1, -jnp.inf, lg)
    m2 = jnp.max(lg2, axis=-1, keepdims=True)
    e2 = jnp.min(jnp.where(lg2 == m2, lane_f, float(LANES)), axis=-1, keepdims=True).astype(I32)
    ex = jnp.exp(m2 - m1)
    g1 = 1.0 / (1.0 + ex)
    g2 = ex / (1.0 + ex)
    onehot = jnp.where(jnp.logical_or(lane == e1, lane == e2), 1.0, 0.0)
    r = lax.broadcasted_iota(I32, (tm, tm), 0)
    c = lax.broadcasted_iota(I32, (tm, tm), 1)
    before = _dot(jnp.where(c < r, 1.0, 0.0).astype(BF16), onehot.astype(BF16)) + carry_ref[0:1, :]
    r1 = jnp.sum(jnp.where(lane == e1, before, 0.0), axis=-1, keepdims=True).astype(I32)
    r2 = jnp.sum(jnp.where(lane == e2, before, 0.0), axis=-1, keepdims=True).astype(I32)
    info = jnp.where(lane == 0, e1, jnp.where(lane == 1, e2, jnp.where(lane == 2, r1, r2)))
    info_ref[...] = info
    gate_ref[...] = jnp.where(lane == 0, g1, g2)
    total = carry_ref[0:1, :] + jnp.sum(onehot, axis=0, keepdims=True)
    carry_ref[...] = jnp.broadcast_to(total, carry_ref.shape)
    cnt_ref[...] = jnp.broadcast_to(total, cnt_ref.shape)


def _route(logits, *, tm=256, name):
    n = logits.shape[0]
    tm = min(tm, n)
    return pl.pallas_call(
        functools.partial(_route_kernel, tm=tm),
        out_shape=(jax.ShapeDtypeStruct((n, LANES), I32), jax.ShapeDtypeStruct((n, LANES), F32),
                   jax.ShapeDtypeStruct((8, LANES), F32)),
        grid=(n // tm,),
        in_specs=[pl.BlockSpec((tm, LANES), lambda i: (i, 0))],
        out_specs=(pl.BlockSpec((tm, LANES), lambda i: (i, 0)),
                   pl.BlockSpec((tm, LANES), lambda i: (i, 0)),
                   pl.BlockSpec((8, LANES), lambda i: (0, 0))),
        scratch_shapes=[pltpu.VMEM((8, LANES), F32)],
        compiler_params=_cparams(("arbitrary",)),
        name=name,
    )(logits)


def _combine_ln_kernel(d0_ref, d1_ref, h_ref, ys_ref, gate_ref, g_ref, b_ref, o_ref,
                       ya0_ref, ya1_ref, yb0_ref, yb1_ref, sems_a, sems_b, *, tm, slabs):
    i = pl.program_id(0)
    nt = pl.num_programs(0)
    bufs_a, bufs_b = (ya0_ref, ya1_ref), (yb0_ref, yb1_ref)

    def gather(tile, wait):
        for idx_ref, bufs, sems in ((d0_ref, bufs_a, sems_a), (d1_ref, bufs_b, sems_b)):
            _row_gather(ys_ref, idx_ref, tile * tm, bufs, sems, tile % 2, tm, slabs, wait=wait)

    @pl.when(i == 0)
    def _():
        gather(0, wait=False)

    gather(i, wait=True)

    @pl.when(i + 1 < nt)
    def _():
        gather(i + 1, wait=False)

    for s in range(2):
        @pl.when(i % 2 == s)
        def _(s=s):
            ff = (gate_ref[:, 0:1] * _slabs_to_rows(bufs_a[s], slabs)
                  + gate_ref[:, 1:2] * _slabs_to_rows(bufs_b[s], slabs))
            o_ref[...] = _layer_norm(DEEPNORM_ALPHA * h_ref[...] + ff, g_ref[...], b_ref[...])


def _combine_ln(dest0, dest1, h, ys, gates, g, b, *, tm=256, name):
    n, d = h.shape
    slabs = d // LANES
    tm = min(tm, n)
    row = lambda i, d0, d1: (i, 0)
    const = lambda i, d0, d1: (0, 0)
    slab_buf = pltpu.VMEM((tm * slabs, LANES), F32)
    return pl.pallas_call(
        functools.partial(_combine_ln_kernel, tm=tm, slabs=slabs),
        out_shape=jax.ShapeDtypeStruct((n, d), F32),
        grid_spec=pltpu.PrefetchScalarGridSpec(
            num_scalar_prefetch=2,
            grid=(n // tm,),
            in_specs=[pl.BlockSpec((tm, d), row),
                      pl.BlockSpec(memory_space=pl.ANY),
                      pl.BlockSpec((tm, LANES), row),
                      pl.BlockSpec((1, d), const),
                      pl.BlockSpec((1, d), const)],
            out_specs=pl.BlockSpec((tm, d), row),
            scratch_shapes=[slab_buf, slab_buf, slab_buf, slab_buf,
                            pltpu.SemaphoreType.DMA((2,)), pltpu.SemaphoreType.DMA((2,))]),
        compiler_params=_cparams(("arbitrary",)),
        name=name,
    )(dest0, dest1, h, ys, gates, g, b)


def _rel_bucket(dist):
    n = jnp.maximum(dist, 0)
    exact = REL_BUCKETS // 2
    nf = jnp.maximum(n, 1).astype(F32)
    large = exact + (jnp.log(nf / exact) / math.log(REL_MAX_DIST / exact) * (REL_BUCKETS - exact)).astype(I32)
    large = jnp.minimum(large, REL_BUCKETS - 1)
    return jnp.where(n < exact, n, large)


def _bias_tile_kernel(tab_ref, bucket_ref, o_ref):
    h = pl.program_id(1)
    bucket = bucket_ref[...]
    far = tab_ref[REL_BUCKETS - 1, h]
    acc = jnp.zeros(o_ref.shape, F32)
    for b in range(REL_BUCKETS - 1):
        acc = jnp.where(bucket == b, tab_ref[b, h] - far, acc)
    o_ref[...] = acc * LOG2E


def _bias_tiles(rel_table, n_heads, t, *, name):
    assert t >= REL_MAX_DIST
    r = jnp.arange(t)[:, None]
    c = jnp.arange(t)[None, :]
    buckets = jnp.stack([_rel_bucket(r - c + off) for off in (0, t)]).astype(I32)
    return pl.pallas_call(
        _bias_tile_kernel,
        out_shape=jax.ShapeDtypeStruct((2, n_heads, t, t), F32),
        grid=(2, n_heads),
        in_specs=[pl.BlockSpec(memory_space=pltpu.SMEM),
                  pl.BlockSpec((None, t, t), lambda k, h: (k, 0, 0))],
        out_specs=pl.BlockSpec((None, None, t, t), lambda k, h: (k, h, 0, 0)),
        compiler_params=_cparams(("parallel", "parallel")),
        name=name,
    )(rel_table.astype(F32), buckets)


def _pad_cols(w, n):
    return jnp.pad(w, ((0, 0), (0, n - w.shape[1])))


def _even_layer(h, rel_table, w_in, b_forget, w_out, ln1_g, ln1_b, w1, w3, w2, ln2_g, ln2_b,
                *, batch, seq):
    d = h.shape[1]
    wa = N_MOBA_HEADS * HEAD_DIM
    wb = N_FOX_HEADS * HEAD_DIM
    n_qkv = 3 * wa + 3 * wb
    scale = HEAD_DIM ** -0.5 * LOG2E
    ones, scl = jnp.ones((wa,), F32), jnp.full((wa,), scale, F32)
    colscale = jnp.concatenate([scl, ones, ones, scl, ones, ones])[None, :]
    qkv = _mm(h, w_in[:, :n_qkv].astype(BF16), colscale, out_dtype=BF16, tm=1024, tn=768,
              name="ev_qkv_proj")
    fb = _mm(h, _pad_cols(w_in[:, n_qkv:], LANES).astype(BF16), jnp.ones((1, LANES), F32),
             out_dtype=F32, tn=LANES, name="ev_forget_proj")
    csum = _forget_csum(fb, _pad_cols(b_forget.astype(F32)[None, :], LANES), batch=batch, seq=seq,
                        n_heads=N_FOX_HEADS, name="ev_forget_csum")
    csum = csum.reshape(batch, N_FOX_HEADS, seq // ATTN_TILE, 1, ATTN_TILE)
    oa = _moba_attention(qkv, _bias_tiles(rel_table, N_MOBA_HEADS, MOBA_BLOCK, name="ev_bias_tiles"),
                         batch=batch, seq=seq, name="ev_moba_attn")
    ob = _fox_attention(qkv, csum, batch=batch, seq=seq, name="ev_fox_attn")
    attn = jnp.concatenate([oa, ob], axis=-1)
    h = _mm_ln(attn, w_out.astype(BF16), h, ln1_g[None, :], ln1_b[None, :], name="ev_out_proj_ln")
    return _ffn_ln(h, w1.astype(BF16), w3.astype(BF16), w2.astype(BF16),
                   ln2_g[None, :], ln2_b[None, :], name="ev_swiglu_ln")


def _moe(h, h_slabs, logits, w1, w3, w2, ln_g, ln_b):
    n, d = h.shape
    tm = EXPERT_TILE
    info, gates, cnt = _route(logits, name="od_route")
    counts = cnt[0, :N_EXPERTS].astype(I32)
    padded = (counts + tm - 1) // tm * tm
    pend = jnp.cumsum(padded)
    pstart = pend - padded
    e = info[:, :MOE_TOPK]
    dest = (pstart[e] + info[:, MOE_TOPK:2 * MOE_TOPK]).astype(I32)
    n_rows = -(-(n * MOE_TOPK + N_EXPERTS * (tm - 1)) // tm) * tm
    n_tiles = n_rows // tm
    tile_start = jnp.arange(n_tiles, dtype=I32) * tm
    tile_valid = (tile_start < pend[-1]).astype(I32)
    last = jnp.maximum(pend[-1] - 1, 0)
    tile_e = jnp.minimum(jnp.searchsorted(pend, jnp.minimum(tile_start, last), side="right"),
                         N_EXPERTS - 1).astype(I32)
    slabs = d // LANES
    tok = jnp.arange(n, dtype=I32) * slabs
    row_tok = jnp.zeros((n_rows,), I32).at[dest.T.reshape(-1)].set(jnp.concatenate([tok, tok]))
    ys = _moe_ffn(tile_e, tile_valid, row_tok, h_slabs, w1.astype(BF16),
                  w3.astype(BF16), w2.astype(BF16), tm=tm, name="od_moe_swiglu")
    return _combine_ln(dest[:, 0] * slabs, dest[:, 1] * slabs, h, ys, gates, ln_g[None, :],
                       ln_b[None, :], name="od_moe_combine_ln")


def _odd_layer(h, rel_table, w_in, q_norm_g, kv_norm_g, w_uq, w_qidx, w_uk, w_uv, w_out,
               ln1_g, ln1_b, router, w1, w3, w2, ln2_g, ln2_b, *, batch, seq):
    nh = N_DSA_HEADS
    rq, rkv = DSA_Q_RANK, DSA_KV_RANK
    scale = HEAD_DIM ** -0.5 * LOG2E
    w_in_p = jnp.concatenate([w_in[:, :rq + rkv],
                              _pad_cols(w_in[:, rq + rkv:rq + rkv + IDX_DIM], LANES),
                              _pad_cols(w_in[:, rq + rkv + IDX_DIM:], LANES)], axis=1)
    cq, ckv, kidx, widx = _dsa_in_proj(h, w_in_p.astype(BF16), q_norm_g[None, :], kv_norm_g[None, :],
                                       name="od_in_proj_rms")
    w_qidx_p = jnp.pad(w_qidx.reshape(rq, IDX_HEADS, IDX_DIM), ((0, 0), (0, 0), (0, LANES - IDX_DIM)))
    wq = jnp.concatenate([w_uq, w_qidx_p.reshape(rq, IDX_HEADS * LANES)], axis=1)
    qscale = jnp.concatenate([jnp.full((nh * HEAD_DIM,), scale, F32),
                              jnp.ones((IDX_HEADS * LANES,), F32)])[None, :]
    qh = _mm(cq, wq.astype(BF16), qscale, out_dtype=BF16, head_major=True, tm=512, tn=2048,
             name="od_q_proj")
    wkv = jnp.concatenate([w_uk.transpose(1, 0, 2).reshape(rkv, nh * HEAD_DIM),
                           w_uv.transpose(1, 0, 2).reshape(rkv, nh * HEAD_DIM)], axis=1)
    kvh = _mm(ckv, wkv.astype(BF16), jnp.ones((1, 2 * nh * HEAD_DIM), F32), out_dtype=BF16,
              head_major=True, tm=512, tn=2048, name="od_kv_proj")
    scores, thr, jcut = _idx_topk(qh, kidx, widx, batch=batch, seq=seq,
                                  ksel=min(DSA_TOPK_MAX, seq // 4), name="od_idx_topk")
    attn = _dsa_attention(qh, kvh, scores, thr, jcut,
                          _bias_tiles(rel_table, nh, ATTN_TILE, name="od_bias_tiles"),
                          batch=batch, seq=seq, name="od_dsa_attn")
    h, h_slabs, logits = _mm_ln_route(attn, w_out.astype(BF16), h, ln1_g[None, :], ln1_b[None, :],
                                      _pad_cols(router, LANES).astype(BF16), name="od_out_proj_ln")
    return _moe(h, h_slabs, logits, w1, w3, w2, ln2_g, ln2_b)


def kernel(x, rel_table, ev_w_in, ev_b_forget, ev_w_out, ev_ln1_g, ev_ln1_b, ev_ffn_w1, ev_ffn_w3, ev_ffn_w2, ev_ln2_g, ev_ln2_b, od_w_in, od_q_norm_g, od_kv_norm_g, od_w_uq, od_w_qidx, od_w_uk, od_w_uv, od_w_out, od_ln1_g, od_ln1_b, od_router, od_exp_w1, od_exp_w3, od_exp_w2, od_ln2_g, od_ln2_b):
    batch, seq, d = x.shape
    h = x.reshape(batch * seq, d)
    for layer in range(DEPTH):
        i = layer // 2
        if layer % 2 == 0:
            h = _even_layer(h, rel_table, ev_w_in[i], ev_b_forget[i], ev_w_out[i], ev_ln1_g[i],
                            ev_ln1_b[i], ev_ffn_w1[i], ev_ffn_w3[i], ev_ffn_w2[i], ev_ln2_g[i],
                            ev_ln2_b[i], batch=batch, seq=seq)
        else:
            h = _odd_layer(h, rel_table, od_w_in[i], od_q_norm_g[i], od_kv_norm_g[i], od_w_uq[i],
                           od_w_qidx[i], od_w_uk[i], od_w_uv[i], od_w_out[i], od_ln1_g[i],
                           od_ln1_b[i], od_router[i], od_exp_w1[i], od_exp_w3[i], od_exp_w2[i],
                           od_ln2_g[i], od_ln2_b[i], batch=batch, seq=seq)
    return h.reshape(batch, seq, d)
```

```python
import functools
import math

import jax
import jax.numpy as jnp
from jax import lax
from jax.experimental import pallas as pl
from jax.experimental.pallas import tpu as pltpu

F32 = jnp.float32
BF16 = jnp.bfloat16
I32 = jnp.int32

HEAD_DIM = 128
N_MOBA_HEADS = 8
N_FOX_HEADS = 8
MOBA_BLOCK = 256
MOBA_TOPK = 3
N_DSA_HEADS = 16
DSA_Q_RANK = 512
DSA_KV_RANK = 512
IDX_HEADS = 16
IDX_DIM = 64
DSA_TOPK_MAX = 256
REL_BUCKETS = 32
REL_MAX_DIST = 128
N_EXPERTS = 8
MOE_TOPK = 2
LN_EPS = 1e-5
RMS_EPS = 1e-6
DEPTH = 2
DEEPNORM_ALPHA = (2 * DEPTH) ** 0.25

LANES = 128
ATTN_TILE = 256
WIDE_CHUNK = 1024
DSA_KEY_TILE = 512
DSA_HEAD_GROUP = 4
SCORE_STRIP = 64
COUNT_STRIP = 128
EXPERT_TILE = 512
ROW_DMA_UNROLL = 8
NEG = -1e30
LOG2E = math.log2(math.e)
INT_MIN = -(2 ** 31)
VMEM_LIMIT = 56 * 1024 * 1024


def _cparams(sem, vmem=VMEM_LIMIT):
    return pltpu.CompilerParams(dimension_semantics=sem, vmem_limit_bytes=vmem)


def _dot(a, b):
    return jnp.dot(a, b, preferred_element_type=F32)


def _dot_t(a, b):
    return lax.dot_general(a, b, (((1,), (1,)), ((), ())), preferred_element_type=F32)


def _layer_norm(y, g, b):
    mu = jnp.mean(y, axis=-1, keepdims=True)
    d = y - mu
    var = jnp.mean(d * d, axis=-1, keepdims=True)
    return d * lax.rsqrt(var + LN_EPS) * g + b


def _mm_kernel(x_ref, w_ref, cs_ref, o_ref, *, head_major):
    acc = _dot(x_ref[...].astype(BF16), w_ref[...]) * cs_ref[...]
    if head_major:
        for c in range(o_ref.shape[0]):
            o_ref[c] = acc[:, c * LANES:(c + 1) * LANES].astype(o_ref.dtype)
    else:
        o_ref[...] = acc.astype(o_ref.dtype)


def _mm(x, w, colscale, *, out_dtype, head_major=False, tm=512, tn=512, name):
    m, k = x.shape
    n = w.shape[1]
    tm, tn = min(tm, m), min(tn, n)
    assert m % tm == 0 and n % tn == 0 and tn % LANES == 0
    if head_major:
        out_shape = jax.ShapeDtypeStruct((n // LANES, m, LANES), out_dtype)
        out_spec = pl.BlockSpec((tn // LANES, tm, LANES), lambda i, j: (j, i, 0))
    else:
        out_shape = jax.ShapeDtypeStruct((m, n), out_dtype)
        out_spec = pl.BlockSpec((tm, tn), lambda i, j: (i, j))
    return pl.pallas_call(
        functools.partial(_mm_kernel, head_major=head_major),
        out_shape=out_shape,
        grid=(m // tm, n // tn),
        in_specs=[pl.BlockSpec((tm, k), lambda i, j: (i, 0)),
                  pl.BlockSpec((k, tn), lambda i, j: (0, j)),
                  pl.BlockSpec((1, tn), lambda i, j: (0, j))],
        out_specs=out_spec,
        compiler_params=_cparams(("parallel", "arbitrary")),
        name=name,
    )(x, w, colscale)


def _mm_ln_kernel(x_ref, w_ref, res_ref, g_ref, b_ref, o_ref):
    y = DEEPNORM_ALPHA * res_ref[...] + _dot(x_ref[...], w_ref[...])
    o_ref[...] = _layer_norm(y, g_ref[...], b_ref[...])


def _mm_ln(x, w, res, g, b, *, tm=512, name):
    m, k = x.shape
    d = w.shape[1]
    tm = min(tm, m)
    return pl.pallas_call(
        _mm_ln_kernel,
        out_shape=jax.ShapeDtypeStruct((m, d), F32),
        grid=(m // tm,),
        in_specs=[pl.BlockSpec((tm, k), lambda i: (i, 0)),
                  pl.BlockSpec((k, d), lambda i: (0, 0)),
                  pl.BlockSpec((tm, d), lambda i: (i, 0)),
                  pl.BlockSpec((1, d), lambda i: (0, 0)),
                  pl.BlockSpec((1, d), lambda i: (0, 0))],
        out_specs=pl.BlockSpec((tm, d), lambda i: (i, 0)),
        compiler_params=_cparams(("parallel",)),
        name=name,
    )(x, w, res, g, b)


def _mm_ln_route_kernel(x_ref, w_ref, res_ref, g_ref, b_ref, wr_ref, o_ref, slab_ref, lg_ref,
                        *, slabs):
    y = _layer_norm(DEEPNORM_ALPHA * res_ref[...] + _dot(x_ref[...], w_ref[...]),
                    g_ref[...], b_ref[...])
    o_ref[...] = y
    tm = y.shape[0]
    for c in range(slabs):
        slab_ref[pl.ds(c, tm, stride=slabs), :] = y[:, c * LANES:(c + 1) * LANES]
    lg_ref[...] = _dot(y.astype(BF16), wr_ref[...])


def _mm_ln_route(x, w, res, g, b, w_router, *, tm=512, name):
    m, k = x.shape
    d = w.shape[1]
    tm = min(tm, m)
    slabs = d // LANES
    row = lambda i: (i, 0)
    const = lambda i: (0, 0)
    return pl.pallas_call(
        functools.partial(_mm_ln_route_kernel, slabs=slabs),
        out_shape=(jax.ShapeDtypeStruct((m, d), F32), jax.ShapeDtypeStruct((m * slabs, LANES), F32),
                   jax.ShapeDtypeStruct((m, LANES), F32)),
        grid=(m // tm,),
        in_specs=[pl.BlockSpec((tm, k), row), pl.BlockSpec((k, d), const), pl.BlockSpec((tm, d), row),
                  pl.BlockSpec((1, d), const), pl.BlockSpec((1, d), const),
                  pl.BlockSpec((d, LANES), const)],
        out_specs=(pl.BlockSpec((tm, d), row), pl.BlockSpec((tm * slabs, LANES), row),
                   pl.BlockSpec((tm, LANES), row)),
        compiler_params=_cparams(("parallel",)),
        name=name,
    )(x, w, res, g, b, w_router)


def _swiglu_step(xb, w1_ref, w3_ref, w2_ref):
    a = _dot(xb, w1_ref[...])
    c = _dot(xb, w3_ref[...])
    hmid = a / (1.0 + jnp.exp(-a)) * c
    return _dot(hmid.astype(BF16), w2_ref[...])


def _ffn_ln_kernel(x_ref, w1_ref, w3_ref, w2_ref, g_ref, b_ref, o_ref, acc_ref, xb_ref):
    f = pl.program_id(1)

    @pl.when(f == 0)
    def _():
        acc_ref[...] = jnp.zeros_like(acc_ref)
        xb_ref[...] = x_ref[...].astype(BF16)

    acc_ref[...] += _swiglu_step(xb_ref[...], w1_ref, w3_ref, w2_ref)

    @pl.when(f == pl.num_programs(1) - 1)
    def _():
        y = DEEPNORM_ALPHA * x_ref[...] + acc_ref[...]
        o_ref[...] = _layer_norm(y, g_ref[...], b_ref[...])


def _ffn_ln(x, w1, w3, w2, g, b, *, tm=512, tf=512, name):
    m, d = x.shape
    dff = w1.shape[1]
    tm, tf = min(tm, m), min(tf, dff)
    assert m % tm == 0 and dff % tf == 0
    return pl.pallas_call(
        _ffn_ln_kernel,
        out_shape=jax.ShapeDtypeStruct((m, d), F32),
        grid=(m // tm, dff // tf),
        in_specs=[pl.BlockSpec((tm, d), lambda i, f: (i, 0)),
                  pl.BlockSpec((d, tf), lambda i, f: (0, f)),
                  pl.BlockSpec((d, tf), lambda i, f: (0, f)),
                  pl.BlockSpec((tf, d), lambda i, f: (f, 0)),
                  pl.BlockSpec((1, d), lambda i, f: (0, 0)),
                  pl.BlockSpec((1, d), lambda i, f: (0, 0))],
        out_specs=pl.BlockSpec((tm, d), lambda i, f: (i, 0)),
        scratch_shapes=[pltpu.VMEM((tm, d), F32), pltpu.VMEM((tm, d), BF16)],
        compiler_params=_cparams(("parallel", "arbitrary")),
        name=name,
    )(x, w1, w3, w2, g, b)


def _row_gather(src_hbm, idx_ref, base, bufs, sems, slot, n_rows, slabs, *, wait):
    for s, buf in enumerate(bufs):
        @pl.when(slot == s)
        def _(s=s, buf=buf):
            def body(r, _):
                first = 0 if wait else pl.multiple_of(idx_ref[base + r], slabs)
                src = src_hbm.at[pl.ds(first, slabs)]
                dst = buf.at[pl.ds(pl.multiple_of(r * slabs, slabs), slabs)]
                cp = pltpu.make_async_copy(src, dst, sems.at[s])
                if wait:
                    cp.wait()
                else:
                    cp.start()
                return 0
            lax.fori_loop(0, n_rows, body, 0, unroll=ROW_DMA_UNROLL)


def _slabs_to_rows(buf, slabs):
    rows = buf.shape[0] // slabs
    return jnp.concatenate([buf[pl.ds(c, rows, stride=slabs), :] for c in range(slabs)], axis=1)


def _moe_ffn_kernel(te_ref, tv_ref, tok_ref, h_ref, w1_ref, w3_ref, w2_ref, o_ref,
                    acc_ref, xb_ref, xg0_ref, xg1_ref, sems, *, tm, slabs):
    i, f = pl.program_id(0), pl.program_id(1)
    nt = pl.num_programs(0)
    bufs = (xg0_ref, xg1_ref)
    gather = functools.partial(_row_gather, h_ref, tok_ref, bufs=bufs, sems=sems, n_rows=tm,
                               slabs=slabs)

    @pl.when(tv_ref[i] > 0)
    def _():
        @pl.when(f == 0)
        def _():
            @pl.when(i == 0)
            def _():
                gather(base=0, slot=0, wait=False)

            gather(base=i * tm, slot=i % 2, wait=True)
            nxt = jnp.minimum(i + 1, nt - 1)

            @pl.when(jnp.logical_and(i + 1 < nt, tv_ref[nxt] > 0))
            def _():
                gather(base=nxt * tm, slot=nxt % 2, wait=False)

            for s, buf in enumerate(bufs):
                @pl.when(i % 2 == s)
                def _(buf=buf):
                    xb_ref[...] = _slabs_to_rows(buf, slabs).astype(BF16)

            acc_ref[...] = jnp.zeros_like(acc_ref)

        acc_ref[...] += _swiglu_step(xb_ref[...], w1_ref, w3_ref, w2_ref)

        @pl.when(f == pl.num_programs(1) - 1)
        def _():
            for c in range(slabs):
                o_ref[pl.ds(c, tm, stride=slabs), :] = acc_ref[:, c * LANES:(c + 1) * LANES]

    @pl.when(tv_ref[i] == 0)
    def _():
        o_ref[...] = jnp.zeros_like(o_ref)


def _moe_ffn(tile_e, tile_valid, row_tok, h_slabs, w1, w3, w2, *, tm, tf=1024, name):
    n_rows = row_tok.shape[0]
    d = w1.shape[1]
    slabs = d // LANES
    dff = w1.shape[2]
    tf = min(tf, dff)
    nf = dff // tf
    assert n_rows % tm == 0 and dff % tf == 0

    def fidx(i, f, tv):
        return jnp.where(tv[i] > 0, f, nf - 1)

    return pl.pallas_call(
        functools.partial(_moe_ffn_kernel, tm=tm, slabs=slabs),
        out_shape=jax.ShapeDtypeStruct((n_rows * slabs, LANES), F32),
        grid_spec=pltpu.PrefetchScalarGridSpec(
            num_scalar_prefetch=3,
            grid=(n_rows // tm, nf),
            in_specs=[pl.BlockSpec(memory_space=pl.ANY),
                      pl.BlockSpec((None, d, tf), lambda i, f, te, tv, tok: (te[i], 0, fidx(i, f, tv))),
                      pl.BlockSpec((None, d, tf), lambda i, f, te, tv, tok: (te[i], 0, fidx(i, f, tv))),
                      pl.BlockSpec((None, tf, d), lambda i, f, te, tv, tok: (te[i], fidx(i, f, tv), 0))],
            out_specs=pl.BlockSpec((tm * slabs, LANES), lambda i, f, te, tv, tok: (i, 0)),
            scratch_shapes=[pltpu.VMEM((tm, d), F32), pltpu.VMEM((tm, d), BF16),
                            pltpu.VMEM((tm * slabs, LANES), F32), pltpu.VMEM((tm * slabs, LANES), F32),
                            pltpu.SemaphoreType.DMA((2,))]),
        compiler_params=_cparams(("arbitrary", "arbitrary")),
        name=name,
    )(tile_e, tile_valid, row_tok, h_slabs, w1, w3, w2)


def _online_softmax_step(s, v, carry):
    m, acc = carry
    m_new = jnp.maximum(m, jnp.max(s, axis=-1, keepdims=True))
    alpha = jnp.exp2(m - m_new)
    p = jnp.exp2((s - m_new).astype(BF16))
    return m_new, alpha * acc + _dot(p, _with_ones(v))


def _with_ones(v):
    return jnp.concatenate([v, jnp.ones_like(v)], axis=1)


def _softmax_init(tq):
    return jnp.full((tq, 1), NEG, F32), jnp.zeros((tq, 2 * HEAD_DIM), F32)


def _softmax_finish(acc):
    return acc[:, :HEAD_DIM] / acc[:, HEAD_DIM:]


def _pairwise_loop(n, body, carry):
    carry = lax.fori_loop(0, n // 2, lambda k, c: body(2 * k + 1, body(2 * k, c)), carry)
    return lax.fori_loop(n // 2 * 2, n, body, carry)


def _head_cols(h):
    return slice(h * HEAD_DIM, (h + 1) * HEAD_DIM)


def _kv_block(k_ref, v_ref, n, width, h):
    rows = pl.ds(pl.multiple_of(n * width, width), width)
    return k_ref[rows, _head_cols(h)], v_ref[rows, _head_cols(h)]


def _causal_mask(blk):
    r = lax.broadcasted_iota(I32, (blk, blk), 0)
    c = lax.broadcasted_iota(I32, (blk, blk), 1)
    return c <= r


def _moba_select(q, km, i, blk):
    nbp = km.shape[0]
    gate = _dot_t(q, km.astype(BF16))
    lane = lax.broadcasted_iota(I32, (blk, nbp), 1)
    lane_f = lane.astype(F32)
    g = jnp.where(lane < i, gate, -jnp.inf)
    sel = jnp.zeros((blk, nbp), F32)
    for _ in range(MOBA_TOPK):
        mx = jnp.max(g, axis=-1, keepdims=True)
        first = jnp.min(jnp.where(g == mx, lane_f, float(nbp)), axis=-1, keepdims=True)
        pick = jnp.logical_and(lane_f == first, mx > -jnp.inf)
        sel = jnp.where(pick, 1.0, sel)
        g = jnp.where(pick, -jnp.inf, g)
    return sel, lane


def _moba_kernel(q_ref, k_ref, v_ref, bias_ref, o_ref, km_ref, *, blk, nb, wide, hp):
    i = pl.program_id(2)
    per = wide // blk

    @pl.when(i == 0)
    def _():
        km_ref[...] = jnp.zeros_like(km_ref)

        def mean_body(n, _):
            for h in range(hp):
                kblk, _ = _kv_block(k_ref, v_ref, n, blk, h)
                km_ref[h, pl.ds(n, 1), :] = jnp.mean(kblk.astype(F32), axis=0, keepdims=True)
            return 0

        lax.fori_loop(0, nb, mean_body, 0)

    qs = [q_ref[:, _head_cols(h)] for h in range(hp)]
    sels = [_moba_select(qs[h], km_ref[h], i, blk) for h in range(hp)]

    def chunk(n, carries, nblk, near):
        rows = pl.ds(pl.multiple_of(n * wide, wide), nblk * blk)
        logits = [_dot_t(qs[h], k_ref[rows, _head_cols(h)]) for h in range(hp)]
        causal = _causal_mask(blk)
        out = []
        for h in range(hp):
            s, vblk = logits[h], v_ref[rows, _head_cols(h)]
            sel, lane = sels[h]
            parts = []
            for u in range(nblk):
                blk_id = n * per + u
                part = s[:, u * blk:(u + 1) * blk]
                chosen = jnp.sum(jnp.where(lane == blk_id, sel, 0.0), axis=-1, keepdims=True) > 0.0
                if near:
                    part = part + jnp.where(blk_id == i, bias_ref[0, h],
                                            jnp.where(blk_id == i - 1, bias_ref[1, h], 0.0))
                    parts.append(jnp.where(blk_id == i, jnp.where(causal, part, NEG),
                                           jnp.where(chosen, part, NEG)))
                else:
                    parts.append(jnp.where(chosen, part, NEG))
            s = parts[0] if nblk == 1 else jnp.concatenate(parts, axis=1)
            out.append(_online_softmax_step(s, vblk, carries[h]))
        return tuple(out)

    first_near = jnp.maximum(i - 1, 0) // per
    last = i // per
    carries = tuple(_softmax_init(blk) for _ in range(hp))
    carries = _pairwise_loop(first_near, lambda n, c: chunk(n, c, per, False), carries)
    carries = lax.fori_loop(first_near, last, lambda n, c: chunk(n, c, per, True), carries)
    for r in range(per):
        @pl.when(i % per == r)
        def _(r=r):
            done = chunk(last, carries, r + 1, True)
            for h in range(hp):
                o_ref[:, _head_cols(h)] = _softmax_finish(done[h][1]).astype(o_ref.dtype)


def _moba_attention(qkv, bias, *, batch, seq, hp=4, name):
    blk = MOBA_BLOCK
    nb = seq // blk
    nbp = -(-nb // LANES) * LANES
    h = N_MOBA_HEADS
    nq = seq // blk
    hg = h // hp
    w = hp * HEAD_DIM
    return pl.pallas_call(
        functools.partial(_moba_kernel, blk=blk, nb=nb, wide=min(WIDE_CHUNK, seq), hp=hp),
        out_shape=jax.ShapeDtypeStruct((batch * seq, h * HEAD_DIM), BF16),
        grid=(batch, hg, nq),
        in_specs=[pl.BlockSpec((blk, w), lambda b, g, i: (b * nq + i, g)),
                  pl.BlockSpec((seq, w), lambda b, g, i: (b, hg + g)),
                  pl.BlockSpec((seq, w), lambda b, g, i: (b, 2 * hg + g)),
                  pl.BlockSpec((2, hp, blk, blk), lambda b, g, i: (0, g, 0, 0))],
        out_specs=pl.BlockSpec((blk, w), lambda b, g, i: (b * nq + i, g)),
        scratch_shapes=[pltpu.VMEM((hp, nbp, HEAD_DIM), F32)],
        compiler_params=_cparams(("parallel", "parallel", "arbitrary")),
        name=name,
    )(qkv, qkv, qkv, bias)


def _fox_kernel(q_ref, k_ref, v_ref, ck_ref, o_ref, *, blk, wide, hp):
    i = pl.program_id(2)
    per = wide // blk
    qs = [q_ref[:, _head_cols(h)] for h in range(hp)]

    def chunk(n, carries, nblk, causal):
        width = nblk * blk
        rows = pl.ds(pl.multiple_of(n * wide, wide), width)
        logits = [_dot_t(qs[h], k_ref[rows, _head_cols(h)]) for h in range(hp)]
        if causal:
            row = i * blk + lax.broadcasted_iota(I32, (blk, width), 0)
            col = n * wide + lax.broadcasted_iota(I32, (blk, width), 1)
            visible = col <= row
        out = []
        for h in range(hp):
            ck = [ck_ref[h, n * per + u] for u in range(nblk)]
            s = logits[h] - (ck[0] if nblk == 1 else jnp.concatenate(ck, axis=1))
            if causal:
                s = jnp.where(visible, s, NEG)
            out.append(_online_softmax_step(s, v_ref[rows, _head_cols(h)], carries[h]))
        return tuple(out)

    n_wide = i // per
    carries = tuple(_softmax_init(blk) for _ in range(hp))
    carries = _pairwise_loop(n_wide, lambda n, c: chunk(n, c, per, False), carries)
    for r in range(per):
        @pl.when(i % per == r)
        def _(r=r):
            done = chunk(n_wide, carries, r + 1, True)
            for h in range(hp):
                o_ref[:, _head_cols(h)] = _softmax_finish(done[h][1]).astype(o_ref.dtype)


def _fox_attention(qkv, csum, *, batch, seq, hp=4, name):
    blk = ATTN_TILE
    h = N_FOX_HEADS
    nq = seq // blk
    hg = h // hp
    base = 3 * N_MOBA_HEADS // hp
    w = hp * HEAD_DIM
    return pl.pallas_call(
        functools.partial(_fox_kernel, blk=blk, wide=min(WIDE_CHUNK, seq), hp=hp),
        out_shape=jax.ShapeDtypeStruct((batch * seq, h * HEAD_DIM), BF16),
        grid=(batch, hg, nq),
        in_specs=[pl.BlockSpec((blk, w), lambda b, g, i: (b * nq + i, base + g)),
                  pl.BlockSpec((seq, w), lambda b, g, i: (b, base + hg + g)),
                  pl.BlockSpec((seq, w), lambda b, g, i: (b, base + 2 * hg + g)),
                  pl.BlockSpec((None, hp, nq, 1, blk), lambda b, g, i: (b, g, 0, 0, 0))],
        out_specs=pl.BlockSpec((blk, w), lambda b, g, i: (b * nq + i, g)),
        compiler_params=_cparams(("parallel", "parallel", "arbitrary")),
        name=name,
    )(qkv, qkv, qkv, csum)


def _forget_csum_kernel(fb_ref, b_ref, o_ref, carry_ref, *, tc):
    @pl.when(pl.program_id(1) == 0)
    def _():
        carry_ref[...] = jnp.zeros_like(carry_ref)

    z = fb_ref[...] + b_ref[...]
    run = jnp.minimum(z, 0.0) - jnp.log1p(jnp.exp(-jnp.abs(z)))
    row = lax.broadcasted_iota(I32, (tc, LANES), 0)
    shift = 1
    while shift < tc:
        run = run + jnp.where(row >= shift, pltpu.roll(run, shift, axis=0), 0.0)
        shift *= 2
    run = run + carry_ref[0:1, :]
    carry_ref[...] = jnp.broadcast_to(run[tc - 1:tc, :], carry_ref.shape)
    o_ref[...] = (run * LOG2E).T[:o_ref.shape[0], :]


def _forget_csum(fb, bias, *, batch, seq, n_heads, tc=512, name):
    tc = min(tc, seq)
    nt = seq // tc
    return pl.pallas_call(
        functools.partial(_forget_csum_kernel, tc=tc),
        out_shape=jax.ShapeDtypeStruct((batch, n_heads, seq), F32),
        grid=(batch, nt),
        in_specs=[pl.BlockSpec((tc, LANES), lambda b, j: (b * nt + j, 0)),
                  pl.BlockSpec((1, LANES), lambda b, j: (0, 0))],
        out_specs=pl.BlockSpec((None, n_heads, tc), lambda b, j: (b, 0, j)),
        scratch_shapes=[pltpu.VMEM((8, LANES), F32)],
        compiler_params=_cparams(("parallel", "arbitrary")),
        name=name,
    )(fb, bias)


def _dsa_in_kernel(x_ref, w_ref, gq_ref, gkv_ref, cq_ref, ckv_ref, kidx_ref, widx_ref):
    acc = _dot(x_ref[...].astype(BF16), w_ref[...])
    rq, rkv = DSA_Q_RANK, DSA_KV_RANK

    def rms(z, g):
        return z * lax.rsqrt(jnp.mean(z * z, axis=-1, keepdims=True) + RMS_EPS) * g

    cq_ref[...] = rms(acc[:, :rq], gq_ref[...]).astype(cq_ref.dtype)
    ckv_ref[...] = rms(acc[:, rq:rq + rkv], gkv_ref[...]).astype(ckv_ref.dtype)
    kidx_ref[...] = acc[:, rq + rkv:rq + rkv + LANES].astype(kidx_ref.dtype)
    widx_ref[...] = acc[:, rq + rkv + LANES:] * (IDX_HEADS ** -0.5 * IDX_DIM ** -0.5)


def _dsa_in_proj(x, w, gq, gkv, *, tm=512, name):
    m, k = x.shape
    n = w.shape[1]
    tm = min(tm, m)
    rq, rkv = DSA_Q_RANK, DSA_KV_RANK
    row = lambda i: (i, 0)
    const = lambda i: (0, 0)
    return pl.pallas_call(
        _dsa_in_kernel,
        out_shape=(jax.ShapeDtypeStruct((m, rq), BF16), jax.ShapeDtypeStruct((m, rkv), BF16),
                   jax.ShapeDtypeStruct((m, LANES), BF16), jax.ShapeDtypeStruct((m, LANES), F32)),
        grid=(m // tm,),
        in_specs=[pl.BlockSpec((tm, k), row), pl.BlockSpec((k, n), const),
                  pl.BlockSpec((1, rq), const), pl.BlockSpec((1, rkv), const)],
        out_specs=(pl.BlockSpec((tm, rq), row), pl.BlockSpec((tm, rkv), row),
                   pl.BlockSpec((tm, LANES), row), pl.BlockSpec((tm, LANES), row)),
        compiler_params=_cparams(("parallel",)),
        name=name,
    )(x, w, gq, gkv)


def _fold_lanes(x):
    part = x[:, :LANES]
    for g in range(1, x.shape[1] // LANES):
        part = part + x[:, g * LANES:(g + 1) * LANES]
    return part


def _idx_topk_kernel(qi_ref, kj_ref, q_ref, k_ref, w_ref, sc_ref, thr_ref, jc_ref, keys_ref,
                     *, t, tk, ksel, seq):
    p = pl.program_id(1)
    i, j = qi_ref[p], kj_ref[p]
    k = k_ref[...]
    for s0 in range(0, t, SCORE_STRIP):
        rows = slice(s0, s0 + SCORE_STRIP)
        stacked = jnp.concatenate([q_ref[h, rows, :] for h in range(IDX_HEADS)], axis=0)
        d = _dot_t(stacked, k)
        acc = jnp.zeros((SCORE_STRIP, tk), F32)
        for h in range(IDX_HEADS):
            part = d[h * SCORE_STRIP:(h + 1) * SCORE_STRIP]
            acc = acc + jnp.maximum(part, 0.0) * w_ref[rows, h:h + 1]
        sc_ref[rows, :] = acc
        row = i * t + s0 + lax.broadcasted_iota(I32, (SCORE_STRIP, tk), 0)
        lane_col = lax.broadcasted_iota(I32, (SCORE_STRIP, tk), 1)
        bits = pltpu.bitcast(acc, I32)
        key = jnp.where(bits < 0, bits ^ jnp.int32(0x7FFFFFFF), bits)
        key = jnp.where(bits == jnp.int32(INT_MIN), 0, key)
        keys_ref[j, rows, :] = jnp.where(j * tk + lane_col <= row, key, jnp.int32(INT_MIN))
    j_last = (i * t + t - 1) // tk

    @pl.when(j == j_last)
    def _():
        lane = lax.broadcasted_iota(I32, (COUNT_STRIP, LANES), 1)
        strips = [slice(s0, s0 + COUNT_STRIP) for s0 in range(0, t, COUNT_STRIP)]

        def count(ref, pred, *row_args):
            dt = ref.dtype
            reps = [[jnp.broadcast_to(a[rows], (COUNT_STRIP, LANES)).astype(dt) for a in row_args]
                    for rows in strips]
            cnts = []
            for rows, args in zip(strips, reps):
                def body(c, cnt, rows=rows, args=args):
                    for g in range(tk // LANES):
                        keys = ref[c, rows, g * LANES:(g + 1) * LANES]
                        cnt = cnt + pred(keys, c * tk + g * LANES, *args).astype(dt)
                    return cnt

                cnts.append(_pairwise_loop(j_last + 1, body, jnp.zeros((COUNT_STRIP, LANES), dt)))
            cnt = jnp.concatenate(cnts, axis=0)
            return jnp.sum(cnt.astype(F32), axis=-1, keepdims=True).astype(I32)

        row1 = i * t + lax.broadcasted_iota(I32, (t, 1), 0)
        full = row1 >= ksel

        def unsettled(state):
            _, lo, hi, settled = state
            return jnp.logical_and(settled == 0, hi - lo != 1)

        def probe(state):
            step, lo, hi, settled = state
            cand = lo + lax.shift_right_logical(hi - lo, 1)
            n_ge = count(keys_ref, lambda k, c0, cd: k >= cd, cand)
            move = jnp.logical_and(settled == 0, n_ge >= ksel)
            stop = jnp.logical_and(settled == 0, n_ge < ksel)
            return (step + 1, jnp.where(move, cand, lo), jnp.where(stop, cand, hi),
                    jnp.where(n_ge == ksel, 1, settled))

        def searching(state):
            busy = jnp.max(jnp.where(unsettled(state), 1.0, 0.0)) > 0.0
            return jnp.logical_and(state[0] < 34, busy)

        start = (jnp.int32(0), jnp.full((t, 1), INT_MIN, I32), jnp.full((t, 1), 2 ** 31 - 1, I32),
                 jnp.where(full, 0, 1))
        _, ans, _, settled = lax.while_loop(searching, probe, start)

        def tie_counts(_):
            return (ksel - count(keys_ref, lambda k, c0, a: k > a, ans),
                    count(keys_ref, lambda k, c0, a: k == a, ans))

        exact = jnp.max(jnp.where(settled == 0, 1.0, 0.0)) > 0.0
        need, ties = lax.cond(exact, tie_counts,
                              lambda _: (jnp.zeros((t, 1), I32), jnp.zeros((t, 1), I32)), 0)

        def tie_cut(_):
            nbits = seq.bit_length() - 1

            def bit_body(b, cut):
                cand = cut + lax.shift_left(jnp.int32(1), nbits - 1 - b)
                below = count(keys_ref, lambda k, c0, a, cd: jnp.logical_and(k == a, c0 + lane < cd),
                              ans, cand)
                return jnp.where(below < need, cand, cut)
            return lax.fori_loop(0, nbits, bit_body, jnp.zeros((t, 1), I32))

        contested = jnp.max(jnp.where(jnp.logical_and(full, ties > need), 1.0, 0.0)) > 0.0
        cut = lax.cond(contested, tie_cut, lambda _: jnp.full((t, 1), seq - 1, I32), 0)
        tbits = jnp.where(ans < 0, ans ^ jnp.int32(0x7FFFFFFF), ans)
        thr = jnp.where(full, pltpu.bitcast(tbits, F32), -jnp.inf)
        thr_ref[...] = jnp.broadcast_to(thr, thr_ref.shape)
        jc_ref[...] = jnp.broadcast_to(jnp.where(full, cut, seq - 1), jc_ref.shape)


def _causal_pairs(seq, t, tk):
    pairs = [(i, j) for i in range(seq // t) for j in range((i * t + t - 1) // tk + 1)]
    return jnp.asarray([p[0] for p in pairs], I32), jnp.asarray([p[1] for p in pairs], I32)


def _idx_topk(qh, kidx, widx, *, batch, seq, ksel, name):
    t, tk = ATTN_TILE, min(DSA_KEY_TILE, seq)
    nq, nk = seq // t, seq // tk
    assert seq & (seq - 1) == 0 and tk >= ksel
    qi, kj = _causal_pairs(seq, t, tk)
    npairs = qi.shape[0]
    qrow = lambda b, p, qi, kj: (b * nq + qi[p], 0)
    return pl.pallas_call(
        functools.partial(_idx_topk_kernel, t=t, tk=tk, ksel=ksel, seq=seq),
        out_shape=(jax.ShapeDtypeStruct((batch, npairs, t, tk), F32),
                   jax.ShapeDtypeStruct((batch * seq, LANES), F32),
                   jax.ShapeDtypeStruct((batch * seq, LANES), I32)),
        grid_spec=pltpu.PrefetchScalarGridSpec(
            num_scalar_prefetch=2,
            grid=(batch, npairs),
            in_specs=[pl.BlockSpec((IDX_HEADS, t, LANES), lambda b, p, qi, kj: (1, b * nq + qi[p], 0)),
                      pl.BlockSpec((tk, LANES), lambda b, p, qi, kj: (b * nk + kj[p], 0)),
                      pl.BlockSpec((t, LANES), qrow)],
            out_specs=(pl.BlockSpec((None, None, t, tk), lambda b, p, qi, kj: (b, p, 0, 0)),
                       pl.BlockSpec((t, LANES), qrow), pl.BlockSpec((t, LANES), qrow)),
            scratch_shapes=[pltpu.VMEM((nk, t, tk), I32)]),
        compiler_params=_cparams(("parallel", "arbitrary")),
        name=name,
    )(qi, kj, qh, kidx, widx)


def _dsa_attn_kernel(qi_ref, kj_ref, q_ref, k_ref, v_ref, sc_ref, thr_ref, jc_ref, bias_ref,
                     o_ref, m_ref, acc_ref, *, t, tk):
    p = pl.program_id(1)
    i, j = qi_ref[p], kj_ref[p]

    @pl.when(j == 0)
    def _():
        m_ref[...] = jnp.full(m_ref.shape, NEG, F32)
        acc_ref[...] = jnp.zeros_like(acc_ref)

    sc = sc_ref[...]
    thr = thr_ref[:, 0:1]
    col = j * tk + lax.broadcasted_iota(I32, (t, tk), 1)
    row = i * t + lax.broadcasted_iota(I32, (t, tk), 0)
    tie = jnp.logical_and(sc == thr, col <= jc_ref[:, 0:1])
    keep = jnp.logical_and(jnp.logical_or(sc > thr, tie), col <= row)
    offs = [i * t - (j * tk + u * t) for u in range(tk // t)]

    def run(with_bias):
        for g in range(N_DSA_HEADS // DSA_HEAD_GROUP):
            heads = [g * DSA_HEAD_GROUP + u for u in range(DSA_HEAD_GROUP)]
            logits = [_dot_t(q_ref[h], k_ref[h]) for h in heads]
            for h, s in zip(heads, logits):
                if with_bias:
                    parts = [jnp.where(off == 0, bias_ref[0, h],
                                       jnp.where(off == t, bias_ref[1, h], 0.0)) for off in offs]
                    s = s + (parts[0] if len(parts) == 1 else jnp.concatenate(parts, axis=1))
                s = jnp.where(keep, s, NEG)
                m_prev = m_ref[h]
                m_next = jnp.maximum(m_prev, jnp.max(s, axis=-1, keepdims=True))
                alpha = jnp.exp2(m_prev - m_next)
                p = jnp.exp2((s - jnp.concatenate([m_next] * (tk // HEAD_DIM), axis=1)).astype(BF16))
                acc_ref[h] = (jnp.concatenate([alpha, alpha], axis=1) * acc_ref[h]
                              + _dot(p, _with_ones(v_ref[h])))
                m_ref[h] = m_next

    near = offs[-1] <= t

    @pl.when(near)
    def _():
        run(True)

    @pl.when(jnp.logical_not(near))
    def _():
        run(False)

    @pl.when(j == (i * t + t - 1) // tk)
    def _():
        for h in range(N_DSA_HEADS):
            o_ref[:, _head_cols(h)] = _softmax_finish(acc_ref[h]).astype(o_ref.dtype)


def _dsa_attention(qh, kvh, scores, thr, jcut, bias, *, batch, seq, name):
    t, tk = ATTN_TILE, min(DSA_KEY_TILE, seq)
    nq, nk = seq // t, seq // tk
    h = N_DSA_HEADS
    qi, kj = _causal_pairs(seq, t, tk)
    qrow = lambda b, p, qi, kj: (b * nq + qi[p], 0)
    return pl.pallas_call(
        functools.partial(_dsa_attn_kernel, t=t, tk=tk),
        out_shape=jax.ShapeDtypeStruct((batch * seq, h * HEAD_DIM), BF16),
        grid_spec=pltpu.PrefetchScalarGridSpec(
            num_scalar_prefetch=2,
            grid=(batch, qi.shape[0]),
            in_specs=[pl.BlockSpec((h, t, HEAD_DIM), lambda b, p, qi, kj: (0, b * nq + qi[p], 0)),
                      pl.BlockSpec((h, tk, HEAD_DIM), lambda b, p, qi, kj: (0, b * nk + kj[p], 0)),
                      pl.BlockSpec((h, tk, HEAD_DIM), lambda b, p, qi, kj: (1, b * nk + kj[p], 0)),
                      pl.BlockSpec((None, None, t, tk), lambda b, p, qi, kj: (b, p, 0, 0)),
                      pl.BlockSpec((t, LANES), qrow),
                      pl.BlockSpec((t, LANES), qrow),
                      pl.BlockSpec((2, h, t, t), lambda b, p, qi, kj: (0, 0, 0, 0))],
            out_specs=pl.BlockSpec((t, h * HEAD_DIM), qrow),
            scratch_shapes=[pltpu.VMEM((h, t, HEAD_DIM), F32),
                            pltpu.VMEM((h, t, 2 * HEAD_DIM), F32)]),
        compiler_params=_cparams(("parallel", "arbitrary")),
        name=name,
    )(qi, kj, qh, kvh, kvh, scores, thr, jcut, bias)


def _route_kernel(lg_ref, info_ref, gate_ref, cnt_ref, carry_ref, *, tm):
    i = pl.program_id(0)

    @pl.when(i == 0)
    def _():
        carry_ref[...] = jnp.zeros_like(carry_ref)

    lane = lax.broadcasted_iota(I32, (tm, LANES), 1)
    lane_f = lane.astype(F32)
    lg = jnp.where(lane < N_EXPERTS, lg_ref[...], -jnp.inf)
    m1 = jnp.max(lg, axis=-1, keepdims=True)
    e1 = jnp.min(jnp.where(lg == m1, lane_f, float(LANES)), axis=-1, keepdims=True).astype(I32)
    lg2 = jnp.where(lane == e1, -jnp.inf, lg)
    m2 = jnp.max(lg2, axis=-1, keepdims=True)
    e2 = jnp.min(jnp.where(lg2 == m2, lane_f, float(LANES)), axis=-1, keepdims=True).astype(I32)
    ex = jnp.exp(m2 - m1)
    g1 = 1.0 / (1.0 + ex)
    g2 = ex / (1.0 + ex)
    onehot = jnp.where(jnp.logical_or(lane == e1, lane == e2), 1.0, 0.0)
    r = lax.broadcasted_iota(I32, (tm, tm), 0)
    c = lax.broadcasted_iota(I32, (tm, tm), 1)
    before = _dot(jnp.where(c < r, 1.0, 0.0).astype(BF16), onehot.astype(BF16)) + carry_ref[0:1, :]
    r1 = jnp.sum(jnp.where(lane == e1, before, 0.0), axis=-1, keepdims=True).astype(I32)
    r2 = jnp.sum(jnp.where(lane == e2, before, 0.0), axis=-1, keepdims=True).astype(I32)
    info = jnp.where(lane == 0, e1, jnp.where(lane == 1, e2, jnp.where(lane == 2, r1, r2)))
    info_ref[...] = info
    gate_ref[...] = jnp.where(lane == 0, g1, g2)
    total = carry_ref[0:1, :] + jnp.sum(onehot, axis=0, keepdims=True)
    carry_ref[...] = jnp.broadcast_to(total, carry_ref.shape)
    cnt_ref[...] = jnp.broadcast_to(total, cnt_ref.shape)


def _route(logits, *, tm=256, name):
    n = logits.shape[0]
    tm = min(tm, n)
    return pl.pallas_call(
        functools.partial(_route_kernel, tm=tm),
        out_shape=(jax.ShapeDtypeStruct((n, LANES), I32), jax.ShapeDtypeStruct((n, LANES), F32),
                   jax.ShapeDtypeStruct((8, LANES), F32)),
        grid=(n // tm,),
        in_specs=[pl.BlockSpec((tm, LANES), lambda i: (i, 0))],
        out_specs=(pl.BlockSpec((tm, LANES), lambda i: (i, 0)),
                   pl.BlockSpec((tm, LANES), lambda i: (i, 0)),
                   pl.BlockSpec((8, LANES), lambda i: (0, 0))),
        scratch_shapes=[pltpu.VMEM((8, LANES), F32)],
        compiler_params=_cparams(("arbitrary",)),
        name=name,
    )(logits)


def _combine_ln_kernel(d0_ref, d1_ref, h_ref, ys_ref, gate_ref, g_ref, b_ref, o_ref,
                       ya0_ref, ya1_ref, yb0_ref, yb1_ref, sems_a, sems_b, *, tm, slabs):
    i = pl.program_id(0)
    nt = pl.num_programs(0)
    bufs_a, bufs_b = (ya0_ref, ya1_ref), (yb0_ref, yb1_ref)

    def gather(tile, wait):
        for idx_ref, bufs, sems in ((d0_ref, bufs_a, sems_a), (d1_ref, bufs_b, sems_b)):
            _row_gather(ys_ref, idx_ref, tile * tm, bufs, sems, tile % 2, tm, slabs, wait=wait)

    @pl.when(i == 0)
    def _():
        gather(0, wait=False)

    gather(i, wait=True)

    @pl.when(i + 1 < nt)
    def _():
        gather(i + 1, wait=False)

    for s in range(2):
        @pl.when(i % 2 == s)
        def _(s=s):
            ff = (gate_ref[:, 0:1] * _slabs_to_rows(bufs_a[s], slabs)
                  + gate_ref[:, 1:2] * _slabs_to_rows(bufs_b[s], slabs))
            o_ref[...] = _layer_norm(DEEPNORM_ALPHA * h_ref[...] + ff, g_ref[...], b_ref[...])


def _combine_ln(dest0, dest1, h, ys, gates, g, b, *, tm=256, name):
    n, d = h.shape
    slabs = d // LANES
    tm = min(tm, n)
    row = lambda i, d0, d1: (i, 0)
    const = lambda i, d0, d1: (0, 0)
    slab_buf = pltpu.VMEM((tm * slabs, LANES), F32)
    return pl.pallas_call(
        functools.partial(_combine_ln_kernel, tm=tm, slabs=slabs),
        out_shape=jax.ShapeDtypeStruct((n, d), F32),
        grid_spec=pltpu.PrefetchScalarGridSpec(
            num_scalar_prefetch=2,
            grid=(n // tm,),
            in_specs=[pl.BlockSpec((tm, d), row),
                      pl.BlockSpec(memory_space=pl.ANY),
                      pl.BlockSpec((tm, LANES), row),
                      pl.BlockSpec((1, d), const),
                      pl.BlockSpec((1, d), const)],
            out_specs=pl.BlockSpec((tm, d), row),
            scratch_shapes=[slab_buf, slab_buf, slab_buf, slab_buf,
                            pltpu.SemaphoreType.DMA((2,)), pltpu.SemaphoreType.DMA((2,))]),
        compiler_params=_cparams(("arbitrary",)),
        name=name,
    )(dest0, dest1, h, ys, gates, g, b)


def _rel_bucket(dist):
    n = jnp.maximum(dist, 0)
    exact = REL_BUCKETS // 2
    nf = jnp.maximum(n, 1).astype(F32)
    large = exact + (jnp.log(nf / exact) / math.log(REL_MAX_DIST / exact) * (REL_BUCKETS - exact)).astype(I32)
    large = jnp.minimum(large, REL_BUCKETS - 1)
    return jnp.where(n < exact, n, large)


def _bias_tile_kernel(tab_ref, bucket_ref, o_ref):
    h = pl.program_id(1)
    bucket = bucket_ref[...]
    far = tab_ref[REL_BUCKETS - 1, h]
    acc = jnp.zeros(o_ref.shape, F32)
    for b in range(REL_BUCKETS - 1):
        acc = jnp.where(bucket == b, tab_ref[b, h] - far, acc)
    o_ref[...] = acc * LOG2E


def _bias_tiles(rel_table, n_heads, t, *, name):
    assert t >= REL_MAX_DIST
    r = jnp.arange(t)[:, None]
    c = jnp.arange(t)[None, :]
    buckets = jnp.stack([_rel_bucket(r - c + off) for off in (0, t)]).astype(I32)
    return pl.pallas_call(
        _bias_tile_kernel,
        out_shape=jax.ShapeDtypeStruct((2, n_heads, t, t), F32),
        grid=(2, n_heads),
        in_specs=[pl.BlockSpec(memory_space=pltpu.SMEM),
                  pl.BlockSpec((None, t, t), lambda k, h: (k, 0, 0))],
        out_specs=pl.BlockSpec((None, None, t, t), lambda k, h: (k, h, 0, 0)),
        compiler_params=_cparams(("parallel", "parallel")),
        name=name,
    )(rel_table.astype(F32), buckets)


def _pad_cols(w, n):
    return jnp.pad(w, ((0, 0), (0, n - w.shape[1])))


def _even_layer(h, rel_table, w_in, b_forget, w_out, ln1_g, ln1_b, w1, w3, w2, ln2_g, ln2_b,
                *, batch, seq):
    d = h.shape[1]
    wa = N_MOBA_HEADS * HEAD_DIM
    wb = N_FOX_HEADS * HEAD_DIM
    n_qkv = 3 * wa + 3 * wb
    scale = HEAD_DIM ** -0.5 * LOG2E
    ones, scl = jnp.ones((wa,), F32), jnp.full((wa,), scale, F32)
    colscale = jnp.concatenate([scl, ones, ones, scl, ones, ones])[None, :]
    qkv = _mm(h, w_in[:, :n_qkv].astype(BF16), colscale, out_dtype=BF16, tm=1024, tn=768,
              name="ev_qkv_proj")
    fb = _mm(h, _pad_cols(w_in[:, n_qkv:], LANES).astype(BF16), jnp.ones((1, LANES), F32),
             out_dtype=F32, tn=LANES, name="ev_forget_proj")
    csum = _forget_csum(fb, _pad_cols(b_forget.astype(F32)[None, :], LANES), batch=batch, seq=seq,
                        n_heads=N_FOX_HEADS, name="ev_forget_csum")
    csum = csum.reshape(batch, N_FOX_HEADS, seq // ATTN_TILE, 1, ATTN_TILE)
    oa = _moba_attention(qkv, _bias_tiles(rel_table, N_MOBA_HEADS, MOBA_BLOCK, name="ev_bias_tiles"),
                         batch=batch, seq=seq, name="ev_moba_attn")
    ob = _fox_attention(qkv, csum, batch=batch, seq=seq, name="ev_fox_attn")
    attn = jnp.concatenate([oa, ob], axis=-1)
    h = _mm_ln(attn, w_out.astype(BF16), h, ln1_g[None, :], ln1_b[None, :], name="ev_out_proj_ln")
    return _ffn_ln(h, w1.astype(BF16), w3.astype(BF16), w2.astype(BF16),
                   ln2_g[None, :], ln2_b[None, :], name="ev_swiglu_ln")


def _moe(h, h_slabs, logits, w1, w3, w2, ln_g, ln_b):
    n, d = h.shape
    tm = EXPERT_TILE
    info, gates, cnt = _route(logits, name="od_route")
    counts = cnt[0, :N_EXPERTS].astype(I32)
    padded = (counts + tm - 1) // tm * tm
    pend = jnp.cumsum(padded)
    pstart = pend - padded
    e = info[:, :MOE_TOPK]
    dest = (pstart[e] + info[:, MOE_TOPK:2 * MOE_TOPK]).astype(I32)
    n_rows = -(-(n * MOE_TOPK + N_EXPERTS * (tm - 1)) // tm) * tm
    n_tiles = n_rows // tm
    tile_start = jnp.arange(n_tiles, dtype=I32) * tm
    tile_valid = (tile_start < pend[-1]).astype(I32)
    last = jnp.maximum(pend[-1] - 1, 0)
    tile_e = jnp.minimum(jnp.searchsorted(pend, jnp.minimum(tile_start, last), side="right"),
                         N_EXPERTS - 1).astype(I32)
    slabs = d // LANES
    tok = jnp.arange(n, dtype=I32) * slabs
    row_tok = jnp.zeros((n_rows,), I32).at[dest.T.reshape(-1)].set(jnp.concatenate([tok, tok]))
    ys = _moe_ffn(tile_e, tile_valid, row_tok, h_slabs, w1.astype(BF16),
                  w3.astype(BF16), w2.astype(BF16), tm=tm, name="od_moe_swiglu")
    return _combine_ln(dest[:, 0] * slabs, dest[:, 1] * slabs, h, ys, gates, ln_g[None, :],
                       ln_b[None, :], name="od_moe_combine_ln")


def _odd_layer(h, rel_table, w_in, q_norm_g, kv_norm_g, w_uq, w_qidx, w_uk, w_uv, w_out,
               ln1_g, ln1_b, router, w1, w3, w2, ln2_g, ln2_b, *, batch, seq):
    nh = N_DSA_HEADS
    rq, rkv = DSA_Q_RANK, DSA_KV_RANK
    scale = HEAD_DIM ** -0.5 * LOG2E
    w_in_p = jnp.concatenate([w_in[:, :rq + rkv],
                              _pad_cols(w_in[:, rq + rkv:rq + rkv + IDX_DIM], LANES),
                              _pad_cols(w_in[:, rq + rkv + IDX_DIM:], LANES)], axis=1)
    cq, ckv, kidx, widx = _dsa_in_proj(h, w_in_p.astype(BF16), q_norm_g[None, :], kv_norm_g[None, :],
                                       name="od_in_proj_rms")
    w_qidx_p = jnp.pad(w_qidx.reshape(rq, IDX_HEADS, IDX_DIM), ((0, 0), (0, 0), (0, LANES - IDX_DIM)))
    wq = jnp.concatenate([w_uq, w_qidx_p.reshape(rq, IDX_HEADS * LANES)], axis=1)
    qscale = jnp.concatenate([jnp.full((nh * HEAD_DIM,), scale, F32),
                              jnp.ones((IDX_HEADS * LANES,), F32)])[None, :]
    qh = _mm(cq, wq.astype(BF16), qscale, out_dtype=BF16, head_major=True, tm=512, tn=2048,
             name="od_q_proj")
    wkv = jnp.concatenate([w_uk.transpose(1, 0, 2).reshape(rkv, nh * HEAD_DIM),
                           w_uv.transpose(1, 0, 2).reshape(rkv, nh * HEAD_DIM)], axis=1)
    kvh = _mm(ckv, wkv.astype(BF16), jnp.ones((1, 2 * nh * HEAD_DIM), F32), out_dtype=BF16,
              head_major=True, tm=512, tn=2048, name="od_kv_proj")
    scores, thr, jcut = _idx_topk(qh, kidx, widx, batch=batch, seq=seq,
                                  ksel=min(DSA_TOPK_MAX, seq // 4), name="od_idx_topk")
    attn = _dsa_attention(qh, kvh, scores, thr, jcut,
                          _bias_tiles(rel_table, nh, ATTN_TILE, name="od_bias_tiles"),
                          batch=batch, seq=seq, name="od_dsa_attn")
    h, h_slabs, logits = _mm_ln_route(attn, w_out.astype(BF16), h, ln1_g[None, :], ln1_b[None, :],
                                      _pad_cols(router, LANES).astype(BF16), name="od_out_proj_ln")
    return _moe(h, h_slabs, logits, w1, w3, w2, ln2_g, ln2_b)


def kernel(x, rel_table, ev_w_in, ev_b_forget, ev_w_out, ev_ln1_g, ev_ln1_b, ev_ffn_w1, ev_ffn_w3, ev_ffn_w2, ev_ln2_g, ev_ln2_b, od_w_in, od_q_norm_g, od_kv_norm_g, od_w_uq, od_w_qidx, od_w_uk, od_w_uv, od_w_out, od_ln1_g, od_ln1_b, od_router, od_exp_w1, od_exp_w3, od_exp_w2, od_ln2_g, od_ln2_b):
    batch, seq, d = x.shape
    h = x.reshape(batch * seq, d)
    for layer in range(DEPTH):
        i = layer // 2
        if layer % 2 == 0:
            h = _even_layer(h, rel_table, ev_w_in[i], ev_b_forget[i], ev_w_out[i], ev_ln1_g[i],
                            ev_ln1_b[i], ev_ffn_w1[i], ev_ffn_w3[i], ev_ffn_w2[i], ev_ln2_g[i],
                            ev_ln2_b[i], batch=batch, seq=seq)
        else:
            h = _odd_layer(h, rel_table, od_w_in[i], od_q_norm_g[i], od_kv_norm_g[i], od_w_uq[i],
                           od_w_qidx[i], od_w_uk[i], od_w_uv[i], od_w_out[i], od_ln1_g[i],
                           od_ln1_b[i], od_router[i], od_exp_w1[i], od_exp_w3[i], od_exp_w2[i],
                           od_ln2_g[i], od_ln2_b[i], batch=batch, seq=seq)
    return h.reshape(batch, seq, d)
```

```python
import functools
import math

import jax
import jax.numpy as jnp
from jax import lax
from jax.experimental import pallas as pl
from jax.experimental.pallas import tpu as pltpu

F32 = jnp.float32
BF16 = jnp.bfloat16
I32 = jnp.int32

HEAD_DIM = 128
N_MOBA_HEADS = 8
N_FOX_HEADS = 8
MOBA_BLOCK = 256
MOBA_TOPK = 3
N_DSA_HEADS = 16
DSA_Q_RANK = 512
DSA_KV_RANK = 512
IDX_HEADS = 16
IDX_DIM = 64
DSA_TOPK_MAX = 256
REL_BUCKETS = 32
REL_MAX_DIST = 128
N_EXPERTS = 8
MOE_TOPK = 2
LN_EPS = 1e-5
RMS_EPS = 1e-6
DEPTH = 2
DEEPNORM_ALPHA = (2 * DEPTH) ** 0.25

LANES = 128
ATTN_TILE = 256
WIDE_CHUNK = 1024
DSA_KEY_TILE = 512
DSA_HEAD_GROUP = 4
SCORE_STRIP = 64
COUNT_STRIP = 128
EXPERT_TILE = 512
ROW_DMA_UNROLL = 8
NEG = -1e30
LOG2E = math.log2(math.e)
INT_MIN = -(2 ** 31)
VMEM_LIMIT = 56 * 1024 * 1024


def _cparams(sem, vmem=VMEM_LIMIT):
    return pltpu.CompilerParams(dimension_semantics=sem, vmem_limit_bytes=vmem)


def _dot(a, b):
    return jnp.dot(a, b, preferred_element_type=F32)


def _dot_t(a, b):
    return lax.dot_general(a, b, (((1,), (1,)), ((), ())), preferred_element_type=F32)


def _layer_norm(y, g, b):
    mu = jnp.mean(y, axis=-1, keepdims=True)
    d = y - mu
    var = jnp.mean(d * d, axis=-1, keepdims=True)
    return d * lax.rsqrt(var + LN_EPS) * g + b


def _mm_kernel(x_ref, w_ref, cs_ref, o_ref, *, head_major):
    acc = _dot(x_ref[...].astype(BF16), w_ref[...]) * cs_ref[...]
    if head_major:
        for c in range(o_ref.shape[0]):
            o_ref[c] = acc[:, c * LANES:(c + 1) * LANES].astype(o_ref.dtype)
    else:
        o_ref[...] = acc.astype(o_ref.dtype)


def _mm(x, w, colscale, *, out_dtype, head_major=False, tm=512, tn=512, name):
    m, k = x.shape
    n = w.shape[1]
    tm, tn = min(tm, m), min(tn, n)
    assert m % tm == 0 and n % tn == 0 and tn % LANES == 0
    if head_major:
        out_shape = jax.ShapeDtypeStruct((n // LANES, m, LANES), out_dtype)
        out_spec = pl.BlockSpec((tn // LANES, tm, LANES), lambda i, j: (j, i, 0))
    else:
        out_shape = jax.ShapeDtypeStruct((m, n), out_dtype)
        out_spec = pl.BlockSpec((tm, tn), lambda i, j: (i, j))
    return pl.pallas_call(
        functools.partial(_mm_kernel, head_major=head_major),
        out_shape=out_shape,
        grid=(m // tm, n // tn),
        in_specs=[pl.BlockSpec((tm, k), lambda i, j: (i, 0)),
                  pl.BlockSpec((k, tn), lambda i, j: (0, j)),
                  pl.BlockSpec((1, tn), lambda i, j: (0, j))],
        out_specs=out_spec,
        compiler_params=_cparams(("parallel", "arbitrary")),
        name=name,
    )(x, w, colscale)


def _mm_ln_kernel(x_ref, w_ref, res_ref, g_ref, b_ref, o_ref):
    y = DEEPNORM_ALPHA * res_ref[...] + _dot(x_ref[...], w_ref[...])
    o_ref[...] = _layer_norm(y, g_ref[...], b_ref[...])


def _mm_ln(x, w, res, g, b, *, tm=512, name):
    m, k = x.shape
    d = w.shape[1]
    tm = min(tm, m)
    return pl.pallas_call(
        _mm_ln_kernel,
        out_shape=jax.ShapeDtypeStruct((m, d), F32),
        grid=(m // tm,),
        in_specs=[pl.BlockSpec((tm, k), lambda i: (i, 0)),
                  pl.BlockSpec((k, d), lambda i: (0, 0)),
                  pl.BlockSpec((tm, d), lambda i: (i, 0)),
                  pl.BlockSpec((1, d), lambda i: (0, 0)),
                  pl.BlockSpec((1, d), lambda i: (0, 0))],
        out_specs=pl.BlockSpec((tm, d), lambda i: (i, 0)),
        compiler_params=_cparams(("parallel",)),
        name=name,
    )(x, w, res, g, b)


def _mm_ln_route_kernel(x_ref, w_ref, res_ref, g_ref, b_ref, wr_ref, o_ref, slab_ref, lg_ref,
                        *, slabs):
    y = _layer_norm(DEEPNORM_ALPHA * res_ref[...] + _dot(x_ref[...], w_ref[...]),
                    g_ref[...], b_ref[...])
    o_ref[...] = y
    tm = y.shape[0]
    for c in range(slabs):
        slab_ref[pl.ds(c, tm, stride=slabs), :] = y[:, c * LANES:(c + 1) * LANES]
    lg_ref[...] = _dot(y.astype(BF16), wr_ref[...])


def _mm_ln_route(x, w, res, g, b, w_router, *, tm=512, name):
    m, k = x.shape
    d = w.shape[1]
    tm = min(tm, m)
    slabs = d // LANES
    row = lambda i: (i, 0)
    const = lambda i: (0, 0)
    return pl.pallas_call(
        functools.partial(_mm_ln_route_kernel, slabs=slabs),
        out_shape=(jax.ShapeDtypeStruct((m, d), F32), jax.ShapeDtypeStruct((m * slabs, LANES), F32),
                   jax.ShapeDtypeStruct((m, LANES), F32)),
        grid=(m // tm,),
        in_specs=[pl.BlockSpec((tm, k), row), pl.BlockSpec((k, d), const), pl.BlockSpec((tm, d), row),
                  pl.BlockSpec((1, d), const), pl.BlockSpec((1, d), const),
                  pl.BlockSpec((d, LANES), const)],
        out_specs=(pl.BlockSpec((tm, d), row), pl.BlockSpec((tm * slabs, LANES), row),
                   pl.BlockSpec((tm, LANES), row)),
        compiler_params=_cparams(("parallel",)),
        name=name,
    )(x, w, res, g, b, w_router)


def _swiglu_step(xb, w1_ref, w3_ref, w2_ref):
    a = _dot(xb, w1_ref[...])
    c = _dot(xb, w3_ref[...])
    hmid = a / (1.0 + jnp.exp(-a)) * c
    return _dot(hmid.astype(BF16), w2_ref[...])


def _ffn_ln_kernel(x_ref, w1_ref, w3_ref, w2_ref, g_ref, b_ref, o_ref, acc_ref, xb_ref):
    f = pl.program_id(1)

    @pl.when(f == 0)
    def _():
        acc_ref[...] = jnp.zeros_like(acc_ref)
        xb_ref[...] = x_ref[...].astype(BF16)

    acc_ref[...] += _swiglu_step(xb_ref[...], w1_ref, w3_ref, w2_ref)

    @pl.when(f == pl.num_programs(1) - 1)
    def _():
        y = DEEPNORM_ALPHA * x_ref[...] + acc_ref[...]
        o_ref[...] = _layer_norm(y, g_ref[...], b_ref[...])


def _ffn_ln(x, w1, w3, w2, g, b, *, tm=512, tf=512, name):
    m, d = x.shape
    dff = w1.shape[1]
    tm, tf = min(tm, m), min(tf, dff)
    assert m % tm == 0 and dff % tf == 0
    return pl.pallas_call(
        _ffn_ln_kernel,
        out_shape=jax.ShapeDtypeStruct((m, d), F32),
        grid=(m // tm, dff // tf),
        in_specs=[pl.BlockSpec((tm, d), lambda i, f: (i, 0)),
                  pl.BlockSpec((d, tf), lambda i, f: (0, f)),
                  pl.BlockSpec((d, tf), lambda i, f: (0, f)),
                  pl.BlockSpec((tf, d), lambda i, f: (f, 0)),
                  pl.BlockSpec((1, d), lambda i, f: (0, 0)),
                  pl.BlockSpec((1, d), lambda i, f: (0, 0))],
        out_specs=pl.BlockSpec((tm, d), lambda i, f: (i, 0)),
        scratch_shapes=[pltpu.VMEM((tm, d), F32), pltpu.VMEM((tm, d), BF16)],
        compiler_params=_cparams(("parallel", "arbitrary")),
        name=name,
    )(x, w1, w3, w2, g, b)


def _row_gather(src_hbm, idx_ref, base, bufs, sems, slot, n_rows, slabs, *, wait):
    for s, buf in enumerate(bufs):
        @pl.when(slot == s)
        def _(s=s, buf=buf):
            def body(r, _):
                first = 0 if wait else pl.multiple_of(idx_ref[base + r], slabs)
                src = src_hbm.at[pl.ds(first, slabs)]
                dst = buf.at[pl.ds(pl.multiple_of(r * slabs, slabs), slabs)]
                cp = pltpu.make_async_copy(src, dst, sems.at[s])
                if wait:
                    cp.wait()
                else:
                    cp.start()
                return 0
            lax.fori_loop(0, n_rows, body, 0, unroll=ROW_DMA_UNROLL)


def _slabs_to_rows(buf, slabs):
    rows = buf.shape[0] // slabs
    return jnp.concatenate([buf[pl.ds(c, rows, stride=slabs), :] for c in range(slabs)], axis=1)


def _moe_ffn_kernel(te_ref, tv_ref, tok_ref, h_ref, w1_ref, w3_ref, w2_ref, o_ref,
                    acc_ref, xb_ref, xg0_ref, xg1_ref, sems, *, tm, slabs):
    i, f = pl.program_id(0), pl.program_id(1)
    nt = pl.num_programs(0)
    bufs = (xg0_ref, xg1_ref)
    gather = functools.partial(_row_gather, h_ref, tok_ref, bufs=bufs, sems=sems, n_rows=tm,
                               slabs=slabs)

    @pl.when(tv_ref[i] > 0)
    def _():
        @pl.when(f == 0)
        def _():
            @pl.when(i == 0)
            def _():
                gather(base=0, slot=0, wait=False)

            gather(base=i * tm, slot=i % 2, wait=True)
            nxt = jnp.minimum(i + 1, nt - 1)

            @pl.when(jnp.logical_and(i + 1 < nt, tv_ref[nxt] > 0))
            def _():
                gather(base=nxt * tm, slot=nxt % 2, wait=False)

            for s, buf in enumerate(bufs):
                @pl.when(i % 2 == s)
                def _(buf=buf):
                    xb_ref[...] = _slabs_to_rows(buf, slabs).astype(BF16)

            acc_ref[...] = jnp.zeros_like(acc_ref)

        acc_ref[...] += _swiglu_step(xb_ref[...], w1_ref, w3_ref, w2_ref)

        @pl.when(f == pl.num_programs(1) - 1)
        def _():
            for c in range(slabs):
                o_ref[pl.ds(c, tm, stride=slabs), :] = acc_ref[:, c * LANES:(c + 1) * LANES]

    @pl.when(tv_ref[i] == 0)
    def _():
        o_ref[...] = jnp.zeros_like(o_ref)


def _moe_ffn(tile_e, tile_valid, row_tok, h_slabs, w1, w3, w2, *, tm, tf=1024, name):
    n_rows = row_tok.shape[0]
    d = w1.shape[1]
    slabs = d // LANES
    dff = w1.shape[2]
    tf = min(tf, dff)
    nf = dff // tf
    assert n_rows % tm == 0 and dff % tf == 0

    def fidx(i, f, tv):
        return jnp.where(tv[i] > 0, f, nf - 1)

    return pl.pallas_call(
        functools.partial(_moe_ffn_kernel, tm=tm, slabs=slabs),
        out_shape=jax.ShapeDtypeStruct((n_rows * slabs, LANES), F32),
        grid_spec=pltpu.PrefetchScalarGridSpec(
            num_scalar_prefetch=3,
            grid=(n_rows // tm, nf),
            in_specs=[pl.BlockSpec(memory_space=pl.ANY),
                      pl.BlockSpec((None, d, tf), lambda i, f, te, tv, tok: (te[i], 0, fidx(i, f, tv))),
                      pl.BlockSpec((None, d, tf), lambda i, f, te, tv, tok: (te[i], 0, fidx(i, f, tv))),
                      pl.BlockSpec((None, tf, d), lambda i, f, te, tv, tok: (te[i], fidx(i, f, tv), 0))],
            out_specs=pl.BlockSpec((tm * slabs, LANES), lambda i, f, te, tv, tok: (i, 0)),
            scratch_shapes=[pltpu.VMEM((tm, d), F32), pltpu.VMEM((tm, d), BF16),
                            pltpu.VMEM((tm * slabs, LANES), F32), pltpu.VMEM((tm * slabs, LANES), F32),
                            pltpu.SemaphoreType.DMA((2,))]),
        compiler_params=_cparams(("arbitrary", "arbitrary")),
        name=name,
    )(tile_e, tile_valid, row_tok, h_slabs, w1, w3, w2)


def _online_softmax_step(s, v, carry):
    m, acc = carry
    m_new = jnp.maximum(m, jnp.max(s, axis=-1, keepdims=True))
    alpha = jnp.exp2(m - m_new)
    p = jnp.exp2((s - m_new).astype(BF16))
    return m_new, alpha * acc + _dot(p, _with_ones(v))


def _with_ones(v):
    return jnp.concatenate([v, jnp.ones_like(v)], axis=1)


def _softmax_init(tq):
    return jnp.full((tq, 1), NEG, F32), jnp.zeros((tq, 2 * HEAD_DIM), F32)


def _softmax_finish(acc):
    return acc[:, :HEAD_DIM] / acc[:, HEAD_DIM:]


def _pairwise_loop(n, body, carry):
    carry = lax.fori_loop(0, n // 2, lambda k, c: body(2 * k + 1, body(2 * k, c)), carry)
    return lax.fori_loop(n // 2 * 2, n, body, carry)


def _head_cols(h):
    return slice(h * HEAD_DIM, (h + 1) * HEAD_DIM)


def _kv_block(k_ref, v_ref, n, width, h):
    rows = pl.ds(pl.multiple_of(n * width, width), width)
    return k_ref[rows, _head_cols(h)], v_ref[rows, _head_cols(h)]


def _causal_mask(blk):
    r = lax.broadcasted_iota(I32, (blk, blk), 0)
    c = lax.broadcasted_iota(I32, (blk, blk), 1)
    return c <= r


def _moba_select(q, km, i, blk):
    nbp = km.shape[0]
    gate = _dot_t(q, km.astype(BF16))
    lane = lax.broadcasted_iota(I32, (blk, nbp), 1)
    lane_f = lane.astype(F32)
    g = jnp.where(lane < i, gate, -jnp.inf)
    sel = jnp.zeros((blk, nbp), F32)
    for _ in range(MOBA_TOPK):
        mx = jnp.max(g, axis=-1, keepdims=True)
        first = jnp.min(jnp.where(g == mx, lane_f, float(nbp)), axis=-1, keepdims=True)
        pick = jnp.logical_and(lane_f == first, mx > -jnp.inf)
        sel = jnp.where(pick, 1.0, sel)
        g = jnp.where(pick, -jnp.inf, g)
    return sel, lane


def _moba_kernel(q_ref, k_ref, v_ref, bias_ref, o_ref, km_ref, *, blk, nb, wide, hp):
    i = pl.program_id(2)
    per = wide // blk

    @pl.when(i == 0)
    def _():
        km_ref[...] = jnp.zeros_like(km_ref)

        def mean_body(n, _):
            for h in range(hp):
                kblk, _ = _kv_block(k_ref, v_ref, n, blk, h)
                km_ref[h, pl.ds(n, 1), :] = jnp.mean(kblk.astype(F32), axis=0, keepdims=True)
            return 0

        lax.fori_loop(0, nb, mean_body, 0)

    qs = [q_ref[:, _head_cols(h)] for h in range(hp)]
    sels = [_moba_select(qs[h], km_ref[h], i, blk) for h in range(hp)]

    def chunk(n, carries, nblk, near):
        rows = pl.ds(pl.multiple_of(n * wide, wide), nblk * blk)
        logits = [_dot_t(qs[h], k_ref[rows, _head_cols(h)]) for h in range(hp)]
        causal = _causal_mask(blk)
        out = []
        for h in range(hp):
            s, vblk = logits[h], v_ref[rows, _head_cols(h)]
            sel, lane = sels[h]
            parts = []
            for u in range(nblk):
                blk_id = n * per + u
                part = s[:, u * blk:(u + 1) * blk]
                chosen = jnp.sum(jnp.where(lane == blk_id, sel, 0.0), axis=-1, keepdims=True) > 0.0
                if near:
                    part = part + jnp.where(blk_id == i, bias_ref[0, h],
                                            jnp.where(blk_id == i - 1, bias_ref[1, h], 0.0))
                    parts.append(jnp.where(blk_id == i, jnp.where(causal, part, NEG),
                                           jnp.where(chosen, part, NEG)))
                else:
                    parts.append(jnp.where(chosen, part, NEG))
            s = parts[0] if nblk == 1 else jnp.concatenate(parts, axis=1)
            out.append(_online_softmax_step(s, vblk, carries[h]))
        return tuple(out)

    first_near = jnp.maximum(i - 1, 0) // per
    last = i // per
    carries = tuple(_softmax_init(blk) for _ in range(hp))
    carries = _pairwise_loop(first_near, lambda n, c: chunk(n, c, per, False), carries)
    carries = lax.fori_loop(first_near, last, lambda n, c: chunk(n, c, per, True), carries)
    for r in range(per):
        @pl.when(i % per == r)
        def _(r=r):
            done = chunk(last, carries, r + 1, True)
            for h in range(hp):
                o_ref[:, _head_cols(h)] = _softmax_finish(done[h][1]).astype(o_ref.dtype)


def _moba_attention(qkv, bias, *, batch, seq, hp=4, name):
    blk = MOBA_BLOCK
    nb = seq // blk
    nbp = -(-nb // LANES) * LANES
    h = N_MOBA_HEADS
    nq = seq // blk
    hg = h // hp
    w = hp * HEAD_DIM
    return pl.pallas_call(
        functools.partial(_moba_kernel, blk=blk, nb=nb, wide=min(WIDE_CHUNK, seq), hp=hp),
        out_shape=jax.ShapeDtypeStruct((batch * seq, h * HEAD_DIM), BF16),
        grid=(batch, hg, nq),
        in_specs=[pl.BlockSpec((blk, w), lambda b, g, i: (b * nq + i, g)),
                  pl.BlockSpec((seq, w), lambda b, g, i: (b, hg + g)),
                  pl.BlockSpec((seq, w), lambda b, g, i: (b, 2 * hg + g)),
                  pl.BlockSpec((2, hp, blk, blk), lambda b, g, i: (0, g, 0, 0))],
        out_specs=pl.BlockSpec((blk, w), lambda b, g, i: (b * nq + i, g)),
        scratch_shapes=[pltpu.VMEM((hp, nbp, HEAD_DIM), F32)],
        compiler_params=_cparams(("parallel", "parallel", "arbitrary")),
        name=name,
    )(qkv, qkv, qkv, bias)


def _fox_kernel(q_ref, k_ref, v_ref, ck_ref, o_ref, *, blk, wide, hp):
    i = pl.program_id(2)
    per = wide // blk
    qs = [q_ref[:, _head_cols(h)] for h in range(hp)]

    def chunk(n, carries, nblk, causal):
        width = nblk * blk
        rows = pl.ds(pl.multiple_of(n * wide, wide), width)
        logits = [_dot_t(qs[h], k_ref[rows, _head_cols(h)]) for h in range(hp)]
        if causal:
            row = i * blk + lax.broadcasted_iota(I32, (blk, width), 0)
            col = n * wide + lax.broadcasted_iota(I32, (blk, width), 1)
            visible = col <= row
        out = []
        for h in range(hp):
            ck = [ck_ref[h, n * per + u] for u in range(nblk)]
            s = logits[h] - (ck[0] if nblk == 1 else jnp.concatenate(ck, axis=1))
            if causal:
                s = jnp.where(visible, s, NEG)
            out.append(_online_softmax_step(s, v_ref[rows, _head_cols(h)], carries[h]))
        return tuple(out)

    n_wide = i // per
    carries = tuple(_softmax_init(blk) for _ in range(hp))
    carries = _pairwise_loop(n_wide, lambda n, c: chunk(n, c, per, False), carries)
    for r in range(per):
        @pl.when(i % per == r)
        def _(r=r):
            done = chunk(n_wide, carries, r + 1, True)
            for h in range(hp):
                o_ref[:, _head_cols(h)] = _softmax_finish(done[h][1]).astype(o_ref.dtype)


def _fox_attention(qkv, csum, *, batch, seq, hp=4, name):
    blk = ATTN_TILE
    h = N_FOX_HEADS
    nq = seq // blk
    hg = h // hp
    base = 3 * N_MOBA_HEADS // hp
    w = hp * HEAD_DIM
    return pl.pallas_call(
        functools.partial(_fox_kernel, blk=blk, wide=min(WIDE_CHUNK, seq), hp=hp),
        out_shape=jax.ShapeDtypeStruct((batch * seq, h * HEAD_DIM), BF16),
        grid=(batch, hg, nq),
        in_specs=[pl.BlockSpec((blk, w), lambda b, g, i: (b * nq + i, base + g)),
                  pl.BlockSpec((seq, w), lambda b, g, i: (b, base + hg + g)),
                  pl.BlockSpec((seq, w), lambda b, g, i: (b, base + 2 * hg + g)),
                  pl.BlockSpec((None, hp, nq, 1, blk), lambda b, g, i: (b, g, 0, 0, 0))],
        out_specs=pl.BlockSpec((blk, w), lambda b, g, i: (b * nq + i, g)),
        compiler_params=_cparams(("parallel", "parallel", "arbitrary")),
        name=name,
    )(qkv, qkv, qkv, csum)


def _forget_csum_kernel(fb_ref, b_ref, o_ref, carry_ref, *, tc):
    @pl.when(pl.program_id(1) == 0)
    def _():
        carry_ref[...] = jnp.zeros_like(carry_ref)

    z = fb_ref[...] + b_ref[...]
    run = jnp.minimum(z, 0.0) - jnp.log1p(jnp.exp(-jnp.abs(z)))
    row = lax.broadcasted_iota(I32, (tc, LANES), 0)
    shift = 1
    while shift < tc:
        run = run + jnp.where(row >= shift, pltpu.roll(run, shift, axis=0), 0.0)
        shift *= 2
    run = run + carry_ref[0:1, :]
    carry_ref[...] = jnp.broadcast_to(run[tc - 1:tc, :], carry_ref.shape)
    o_ref[...] = (run * LOG2E).T[:o_ref.shape[0], :]


def _forget_csum(fb, bias, *, batch, seq, n_heads, tc=512, name):
    tc = min(tc, seq)
    nt = seq // tc
    return pl.pallas_call(
        functools.partial(_forget_csum_kernel, tc=tc),
        out_shape=jax.ShapeDtypeStruct((batch, n_heads, seq), F32),
        grid=(batch, nt),
        in_specs=[pl.BlockSpec((tc, LANES), lambda b, j: (b * nt + j, 0)),
                  pl.BlockSpec((1, LANES), lambda b, j: (0, 0))],
        out_specs=pl.BlockSpec((None, n_heads, tc), lambda b, j: (b, 0, j)),
        scratch_shapes=[pltpu.VMEM((8, LANES), F32)],
        compiler_params=_cparams(("parallel", "arbitrary")),
        name=name,
    )(fb, bias)


def _dsa_in_kernel(x_ref, w_ref, gq_ref, gkv_ref, cq_ref, ckv_ref, kidx_ref, widx_ref):
    acc = _dot(x_ref[...].astype(BF16), w_ref[...])
    rq, rkv = DSA_Q_RANK, DSA_KV_RANK

    def rms(z, g):
        return z * lax.rsqrt(jnp.mean(z * z, axis=-1, keepdims=True) + RMS_EPS) * g

    cq_ref[...] = rms(acc[:, :rq], gq_ref[...]).astype(cq_ref.dtype)
    ckv_ref[...] = rms(acc[:, rq:rq + rkv], gkv_ref[...]).astype(ckv_ref.dtype)
    kidx_ref[...] = acc[:, rq + rkv:rq + rkv + LANES].astype(kidx_ref.dtype)
    widx_ref[...] = acc[:, rq + rkv + LANES:] * (IDX_HEADS ** -0.5 * IDX_DIM ** -0.5)


def _dsa_in_proj(x, w, gq, gkv, *, tm=512, name):
    m, k = x.shape
    n = w.shape[1]
    tm = min(tm, m)
    rq, rkv = DSA_Q_RANK, DSA_KV_RANK
    row = lambda i: (i, 0)
    const = lambda i: (0, 0)
    return pl.pallas_call(
        _dsa_in_kernel,
        out_shape=(jax.ShapeDtypeStruct((m, rq), BF16), jax.ShapeDtypeStruct((m, rkv), BF16),
                   jax.ShapeDtypeStruct((m, LANES), BF16), jax.ShapeDtypeStruct((m, LANES), F32)),
        grid=(m // tm,),
        in_specs=[pl.BlockSpec((tm, k), row), pl.BlockSpec((k, n), const),
                  pl.BlockSpec((1, rq), const), pl.BlockSpec((1, rkv), const)],
        out_specs=(pl.BlockSpec((tm, rq), row), pl.BlockSpec((tm, rkv), row),
                   pl.BlockSpec((tm, LANES), row), pl.BlockSpec((tm, LANES), row)),
        compiler_params=_cparams(("parallel",)),
        name=name,
    )(x, w, gq, gkv)


def _fold_lanes(x):
    part = x[:, :LANES]
    for g in range(1, x.shape[1] // LANES):
        part = part + x[:, g * LANES:(g + 1) * LANES]
    return part


def _idx_topk_kernel(qi_ref, kj_ref, q_ref, k_ref, w_ref, sc_ref, thr_ref, jc_ref, keys_ref,
                     *, t, tk, ksel, seq):
    p = pl.program_id(1)
    i, j = qi_ref[p], kj_ref[p]
    k = k_ref[...]
    for s0 in range(0, t, SCORE_STRIP):
        rows = slice(s0, s0 + SCORE_STRIP)
        stacked = jnp.concatenate([q_ref[h, rows, :] for h in range(IDX_HEADS)], axis=0)
        d = _dot_t(stacked, k)
        acc = jnp.zeros((SCORE_STRIP, tk), F32)
        for h in range(IDX_HEADS):
            part = d[h * SCORE_STRIP:(h + 1) * SCORE_STRIP]
            acc = acc + jnp.maximum(part, 0.0) * w_ref[rows, h:h + 1]
        sc_ref[rows, :] = acc
        row = i * t + s0 + lax.broadcasted_iota(I32, (SCORE_STRIP, tk), 0)
        lane_col = lax.broadcasted_iota(I32, (SCORE_STRIP, tk), 1)
        bits = pltpu.bitcast(acc, I32)
        key = jnp.where(bits < 0, bits ^ jnp.int32(0x7FFFFFFF), bits)
        key = jnp.where(bits == jnp.int32(INT_MIN), 0, key)
        keys_ref[j, rows, :] = jnp.where(j * tk + lane_col <= row, key, jnp.int32(INT_MIN))
    j_last = (i * t + t - 1) // tk

    @pl.when(j == j_last)
    def _():
        lane = lax.broadcasted_iota(I32, (COUNT_STRIP, LANES), 1)
        strips = [slice(s0, s0 + COUNT_STRIP) for s0 in range(0, t, COUNT_STRIP)]

        def count(ref, pred, *row_args):
            dt = ref.dtype
            reps = [[jnp.broadcast_to(a[rows], (COUNT_STRIP, LANES)).astype(dt) for a in row_args]
                    for rows in strips]
            cnts = []
            for rows, args in zip(strips, reps):
                def body(c, cnt, rows=rows, args=args):
                    for g in range(tk // LANES):
                        keys = ref[c, rows, g * LANES:(g + 1) * LANES]
                        cnt = cnt + pred(keys, c * tk + g * LANES, *args).astype(dt)
                    return cnt

                cnts.append(_pairwise_loop(j_last + 1, body, jnp.zeros((COUNT_STRIP, LANES), dt)))
            cnt = jnp.concatenate(cnts, axis=0)
            return jnp.sum(cnt.astype(F32), axis=-1, keepdims=True).astype(I32)

        row1 = i * t + lax.broadcasted_iota(I32, (t, 1), 0)
        full = row1 >= ksel

        def unsettled(state):
            _, lo, hi, settled = state
            return jnp.logical_and(settled == 0, hi - lo != 1)

        def probe(state):
            step, lo, hi, settled = state
            cand = lo + lax.shift_right_logical(hi - lo, 1)
            n_ge = count(keys_ref, lambda k, c0, cd: k >= cd, cand)
            move = jnp.logical_and(settled == 0, n_ge >= ksel)
            stop = jnp.logical_and(settled == 0, n_ge < ksel)
            return (step + 1, jnp.where(move, cand, lo), jnp.where(stop, cand, hi),
                    jnp.where(n_ge == ksel, 1, settled))

        def searching(state):
            busy = jnp.max(jnp.where(unsettled(state), 1.0, 0.0)) > 0.0
            return jnp.logical_and(state[0] < 34, busy)

        start = (jnp.int32(0), jnp.full((t, 1), INT_MIN, I32), jnp.full((t, 1), 2 ** 31 - 1, I32),
                 jnp.where(full, 0, 1))
        _, ans, _, settled = lax.while_loop(searching, probe, start)

        def tie_counts(_):
            return (ksel - count(keys_ref, lambda k, c0, a: k > a, ans),
                    count(keys_ref, lambda k, c0, a: k == a, ans))

        exact = jnp.max(jnp.where(settled == 0, 1.0, 0.0)) > 0.0
        need, ties = lax.cond(exact, tie_counts,
                              lambda _: (jnp.zeros((t, 1), I32), jnp.zeros((t, 1), I32)), 0)

        def tie_cut(_):
            nbits = seq.bit_length() - 1

            def bit_body(b, cut):
                cand = cut + lax.shift_left(jnp.int32(1), nbits - 1 - b)
                below = count(keys_ref, lambda k, c0, a, cd: jnp.logical_and(k == a, c0 + lane < cd),
                              ans, cand)
                return jnp.where(below < need, cand, cut)
            return lax.fori_loop(0, nbits, bit_body, jnp.zeros((t, 1), I32))

        contested = jnp.max(jnp.where(jnp.logical_and(full, ties > need), 1.0, 0.0)) > 0.0
        cut = lax.cond(contested, tie_cut, lambda _: jnp.full((t, 1), seq - 1, I32), 0)
        tbits = jnp.where(ans < 0, ans ^ jnp.int32(0x7FFFFFFF), ans)
        thr = jnp.where(full, pltpu.bitcast(tbits, F32), -jnp.inf)
        thr_ref[...] = jnp.broadcast_to(thr, thr_ref.shape)
        jc_ref[...] = jnp.broadcast_to(jnp.where(full, cut, seq - 1), jc_ref.shape)


def _causal_pairs(seq, t, tk):
    pairs = [(i, j) for i in range(seq // t) for j in range((i * t + t - 1) // tk + 1)]
    return jnp.asarray([p[0] for p in pairs], I32), jnp.asarray([p[1] for p in pairs], I32)


def _idx_topk(qh, kidx, widx, *, batch, seq, ksel, name):
    t, tk = ATTN_TILE, min(DSA_KEY_TILE, seq)
    nq, nk = seq // t, seq // tk
    assert seq & (seq - 1) == 0 and tk >= ksel
    qi, kj = _causal_pairs(seq, t, tk)
    npairs = qi.shape[0]
    qrow = lambda b, p, qi, kj: (b * nq + qi[p], 0)
    return pl.pallas_call(
        functools.partial(_idx_topk_kernel, t=t, tk=tk, ksel=ksel, seq=seq),
        out_shape=(jax.ShapeDtypeStruct((batch, npairs, t, tk), F32),
                   jax.ShapeDtypeStruct((batch * seq, LANES), F32),
                   jax.ShapeDtypeStruct((batch * seq, LANES), I32)),
        grid_spec=pltpu.PrefetchScalarGridSpec(
            num_scalar_prefetch=2,
            grid=(batch, npairs),
            in_specs=[pl.BlockSpec((IDX_HEADS, t, LANES), lambda b, p, qi, kj: (1, b * nq + qi[p], 0)),
                      pl.BlockSpec((tk, LANES), lambda b, p, qi, kj: (b * nk + kj[p], 0)),
                      pl.BlockSpec((t, LANES), qrow)],
            out_specs=(pl.BlockSpec((None, None, t, tk), lambda b, p, qi, kj: (b, p, 0, 0)),
                       pl.BlockSpec((t, LANES), qrow), pl.BlockSpec((t, LANES), qrow)),
            scratch_shapes=[pltpu.VMEM((nk, t, tk), I32)]),
        compiler_params=_cparams(("parallel", "arbitrary")),
        name=name,
    )(qi, kj, qh, kidx, widx)


def _dsa_attn_kernel(qi_ref, kj_ref, q_ref, k_ref, v_ref, sc_ref, thr_ref, jc_ref, bias_ref,
                     o_ref, m_ref, acc_ref, *, t, tk):
    p = pl.program_id(1)
    i, j = qi_ref[p], kj_ref[p]

    @pl.when(j == 0)
    def _():
        m_ref[...] = jnp.full(m_ref.shape, NEG, F32)
        acc_ref[...] = jnp.zeros_like(acc_ref)

    sc = sc_ref[...]
    thr = thr_ref[:, 0:1]
    col = j * tk + lax.broadcasted_iota(I32, (t, tk), 1)
    row = i * t + lax.broadcasted_iota(I32, (t, tk), 0)
    tie = jnp.logical_and(sc == thr, col <= jc_ref[:, 0:1])
    keep = jnp.logical_and(jnp.logical_or(sc > thr, tie), col <= row)
    offs = [i * t - (j * tk + u * t) for u in range(tk // t)]

    def run(with_bias, n_parts):
        w = n_parts * t
        for g in range(N_DSA_HEADS // DSA_HEAD_GROUP):
            heads = [g * DSA_HEAD_GROUP + u for u in range(DSA_HEAD_GROUP)]
            logits = [_dot_t(q_ref[h], k_ref[h, :w, :]) for h in heads]
            for h, s in zip(heads, logits):
                if with_bias:
                    parts = [jnp.where(off == 0, bias_ref[0, h],
                                       jnp.where(off == t, bias_ref[1, h], 0.0))
                             for off in offs[:n_parts]]
                    s = s + (parts[0] if n_parts == 1 else jnp.concatenate(parts, axis=1))
                s = jnp.where(keep[:, :w], s, NEG)
                m_prev = m_ref[h]
                m_next = jnp.maximum(m_prev, jnp.max(s, axis=-1, keepdims=True))
                alpha = jnp.exp2(m_prev - m_next)
                p = jnp.exp2((s - jnp.concatenate([m_next] * (w // HEAD_DIM), axis=1)).astype(BF16))
                acc_ref[h] = (jnp.concatenate([alpha, alpha], axis=1) * acc_ref[h]
                              + _dot(p, _with_ones(v_ref[h, :w, :])))
                m_ref[h] = m_next

    n_all = tk // t
    near = offs[-1] <= t
    starts_own = offs[0] == 0

    @pl.when(jnp.logical_and(near, starts_own))
    def _():
        run(True, 1)

    @pl.when(jnp.logical_and(near, jnp.logical_not(starts_own)))
    def _():
        run(True, n_all)

    @pl.when(jnp.logical_not(near))
    def _():
        run(False, n_all)

    @pl.when(j == (i * t + t - 1) // tk)
    def _():
        for h in range(N_DSA_HEADS):
            o_ref[:, _head_cols(h)] = _softmax_finish(acc_ref[h]).astype(o_ref.dtype)


def _dsa_attention(qh, kvh, scores, thr, jcut, bias, *, batch, seq, name):
    t, tk = ATTN_TILE, min(DSA_KEY_TILE, seq)
    nq, nk = seq // t, seq // tk
    h = N_DSA_HEADS
    qi, kj = _causal_pairs(seq, t, tk)
    qrow = lambda b, p, qi, kj: (b * nq + qi[p], 0)
    return pl.pallas_call(
        functools.partial(_dsa_attn_kernel, t=t, tk=tk),
        out_shape=jax.ShapeDtypeStruct((batch * seq, h * HEAD_DIM), BF16),
        grid_spec=pltpu.PrefetchScalarGridSpec(
            num_scalar_prefetch=2,
            grid=(batch, qi.shape[0]),
            in_specs=[pl.BlockSpec((h, t, HEAD_DIM), lambda b, p, qi, kj: (0, b * nq + qi[p], 0)),
                      pl.BlockSpec((h, tk, HEAD_DIM), lambda b, p, qi, kj: (0, b * nk + kj[p], 0)),
                      pl.BlockSpec((h, tk, HEAD_DIM), lambda b, p, qi, kj: (1, b * nk + kj[p], 0)),
                      pl.BlockSpec((None, None, t, tk), lambda b, p, qi, kj: (b, p, 0, 0)),
                      pl.BlockSpec((t, LANES), qrow),
                      pl.BlockSpec((t, LANES), qrow),
                      pl.BlockSpec((2, h, t, t), lambda b, p, qi, kj: (0, 0, 0, 0))],
            out_specs=pl.BlockSpec((t, h * HEAD_DIM), qrow),
            scratch_shapes=[pltpu.VMEM((h, t, HEAD_DIM), F32),
                            pltpu.VMEM((h, t, 2 * HEAD_DIM), F32)]),
        compiler_params=_cparams(("parallel", "arbitrary")),
        name=name,
    )(qi, kj, qh, kvh, kvh, scores, thr, jcut, bias)


def _route_kernel(lg_ref, info_ref, gate_ref, cnt_ref, carry_ref, *, tm):
    i = pl.program_id(0)

    @pl.when(i == 0)
    def _():
        carry_ref[...] = jnp.zeros_like(carry_ref)

    lane = lax.broadcasted_iota(I32, (tm, LANES), 1)
    lane_f = lane.astype(F32)
    lg = jnp.where(lane < N_EXPERTS, lg_ref[...], -jnp.inf)
    m1 = jnp.max(lg, axis=-1, keepdims=True)
    e1 = jnp.min(jnp.where(lg == m1, lane_f, float(LANES)), axis=-1, keepdims=True).astype(I32)
    lg2 = jnp.where(lane == e1, -jnp.inf, lg)
    m2 = jnp.max(lg2, axis=-1, keepdims=True)
    e2 = jnp.min(jnp.where(lg2 == m2, lane_f, float(LANES)), axis=-1, keepdims=True).astype(I32)
    ex = jnp.exp(m2 - m1)
    g1 = 1.0 / (1.0 + ex)
    g2 = ex / (1.0 + ex)
    onehot = jnp.where(jnp.logical_or(lane == e1, lane == e2), 1.0, 0.0)
    r = lax.broadcasted_iota(I32, (tm, tm), 0)
    c = lax.broadcasted_iota(I32, (tm, tm), 1)
    before = _dot(jnp.where(c < r, 1.0, 0.0).astype(BF16), onehot.astype(BF16)) + carry_ref[0:1, :]
    r1 = jnp.sum(jnp.where(lane == e1, before, 0.0), axis=-1, keepdims=True).astype(I32)
    r2 = jnp.sum(jnp.where(lane == e2, before, 0.0), axis=-1, keepdims=True).astype(I32)
    info = jnp.where(lane == 0, e1, jnp.where(lane == 1, e2, jnp.where(lane == 2, r1, r2)))
    info_ref[...] = info
    gate_ref[...] = jnp.where(lane == 0, g1, g2)
    total = carry_ref[0:1, :] + jnp.sum(onehot, axis=0, keepdims=True)
    carry_ref[...] = jnp.broadcast_to(total, carry_ref.shape)
    cnt_ref[...] = jnp.broadcast_to(total, cnt_ref.shape)


def _route(logits, *, tm=256, name):
    n = logits.shape[0]
    tm = min(tm, n)
    return pl.pallas_call(
        functools.partial(_route_kernel, tm=tm),
        out_shape=(jax.ShapeDtypeStruct((n, LANES), I32), jax.ShapeDtypeStruct((n, LANES), F32),
                   jax.ShapeDtypeStruct((8, LANES), F32)),
        grid=(n // tm,),
        in_specs=[pl.BlockSpec((tm, LANES), lambda i: (i, 0))],
        out_specs=(pl.BlockSpec((tm, LANES), lambda i: (i, 0)),
                   pl.BlockSpec((tm, LANES), lambda i: (i, 0)),
                   pl.BlockSpec((8, LANES), lambda i: (0, 0))),
        scratch_shapes=[pltpu.VMEM((8, LANES), F32)],
        compiler_params=_cparams(("arbitrary",)),
        name=name,
    )(logits)


def _combine_ln_kernel(d0_ref, d1_ref, h_ref, ys_ref, gate_ref, g_ref, b_ref, o_ref,
                       ya0_ref, ya1_ref, yb0_ref, yb1_ref, sems_a, sems_b, *, tm, slabs):
    i = pl.program_id(0)
    nt = pl.num_programs(0)
    bufs_a, bufs_b = (ya0_ref, ya1_ref), (yb0_ref, yb1_ref)

    def gather(tile, wait):
        for idx_ref, bufs, sems in ((d0_ref, bufs_a, sems_a), (d1_ref, bufs_b, sems_b)):
            _row_gather(ys_ref, idx_ref, tile * tm, bufs, sems, tile % 2, tm, slabs, wait=wait)

    @pl.when(i == 0)
    def _():
        gather(0, wait=False)

    gather(i, wait=True)

    @pl.when(i + 1 < nt)
    def _():
        gather(i + 1, wait=False)

    for s in range(2):
        @pl.when(i % 2 == s)
        def _(s=s):
            ff = (gate_ref[:, 0:1] * _slabs_to_rows(bufs_a[s], slabs)
                  + gate_ref[:, 1:2] * _slabs_to_rows(bufs_b[s], slabs))
            o_ref[...] = _layer_norm(DEEPNORM_ALPHA * h_ref[...] + ff, g_ref[...], b_ref[...])


def _combine_ln(dest0, dest1, h, ys, gates, g, b, *, tm=256, name):
    n, d = h.shape
    slabs = d // LANES
    tm = min(tm, n)
    row = lambda i, d0, d1: (i, 0)
    const = lambda i, d0, d1: (0, 0)
    slab_buf = pltpu.VMEM((tm * slabs, LANES), F32)
    return pl.pallas_call(
        functools.partial(_combine_ln_kernel, tm=tm, slabs=slabs),
        out_shape=jax.ShapeDtypeStruct((n, d), F32),
        grid_spec=pltpu.PrefetchScalarGridSpec(
            num_scalar_prefetch=2,
            grid=(n // tm,),
            in_specs=[pl.BlockSpec((tm, d), row),
                      pl.BlockSpec(memory_space=pl.ANY),
                      pl.BlockSpec((tm, LANES), row),
                      pl.BlockSpec((1, d), const),
                      pl.BlockSpec((1, d), const)],
            out_specs=pl.BlockSpec((tm, d), row),
            scratch_shapes=[slab_buf, slab_buf, slab_buf, slab_buf,
                            pltpu.SemaphoreType.DMA((2,)), pltpu.SemaphoreType.DMA((2,))]),
        compiler_params=_cparams(("arbitrary",)),
        name=name,
    )(dest0, dest1, h, ys, gates, g, b)


def _rel_bucket(dist):
    n = jnp.maximum(dist, 0)
    exact = REL_BUCKETS // 2
    nf = jnp.maximum(n, 1).astype(F32)
    large = exact + (jnp.log(nf / exact) / math.log(REL_MAX_DIST / exact) * (REL_BUCKETS - exact)).astype(I32)
    large = jnp.minimum(large, REL_BUCKETS - 1)
    return jnp.where(n < exact, n, large)


def _bias_tile_kernel(tab_ref, bucket_ref, o_ref):
    h = pl.program_id(1)
    bucket = bucket_ref[...]
    far = tab_ref[REL_BUCKETS - 1, h]
    acc = jnp.zeros(o_ref.shape, F32)
    for b in range(REL_BUCKETS - 1):
        acc = jnp.where(bucket == b, tab_ref[b, h] - far, acc)
    o_ref[...] = acc * LOG2E


def _bias_tiles(rel_table, n_heads, t, *, name):
    assert t >= REL_MAX_DIST
    r = jnp.arange(t)[:, None]
    c = jnp.arange(t)[None, :]
    buckets = jnp.stack([_rel_bucket(r - c + off) for off in (0, t)]).astype(I32)
    return pl.pallas_call(
        _bias_tile_kernel,
        out_shape=jax.ShapeDtypeStruct((2, n_heads, t, t), F32),
        grid=(2, n_heads),
        in_specs=[pl.BlockSpec(memory_space=pltpu.SMEM),
                  pl.BlockSpec((None, t, t), lambda k, h: (k, 0, 0))],
        out_specs=pl.BlockSpec((None, None, t, t), lambda k, h: (k, h, 0, 0)),
        compiler_params=_cparams(("parallel", "parallel")),
        name=name,
    )(rel_table.astype(F32), buckets)


def _pad_cols(w, n):
    return jnp.pad(w, ((0, 0), (0, n - w.shape[1])))


def _even_layer(h, rel_table, w_in, b_forget, w_out, ln1_g, ln1_b, w1, w3, w2, ln2_g, ln2_b,
                *, batch, seq):
    d = h.shape[1]
    wa = N_MOBA_HEADS * HEAD_DIM
    wb = N_FOX_HEADS * HEAD_DIM
    n_qkv = 3 * wa + 3 * wb
    scale = HEAD_DIM ** -0.5 * LOG2E
    ones, scl = jnp.ones((wa,), F32), jnp.full((wa,), scale, F32)
    colscale = jnp.concatenate([scl, ones, ones, scl, ones, ones])[None, :]
    qkv = _mm(h, w_in[:, :n_qkv].astype(BF16), colscale, out_dtype=BF16, tm=1024, tn=768,
              name="ev_qkv_proj")
    fb = _mm(h, _pad_cols(w_in[:, n_qkv:], LANES).astype(BF16), jnp.ones((1, LANES), F32),
             out_dtype=F32, tn=LANES, name="ev_forget_proj")
    csum = _forget_csum(fb, _pad_cols(b_forget.astype(F32)[None, :], LANES), batch=batch, seq=seq,
                        n_heads=N_FOX_HEADS, name="ev_forget_csum")
    csum = csum.reshape(batch, N_FOX_HEADS, seq // ATTN_TILE, 1, ATTN_TILE)
    oa = _moba_attention(qkv, _bias_tiles(rel_table, N_MOBA_HEADS, MOBA_BLOCK, name="ev_bias_tiles"),
                         batch=batch, seq=seq, name="ev_moba_attn")
    ob = _fox_attention(qkv, csum, batch=batch, seq=seq, name="ev_fox_attn")
    attn = jnp.concatenate([oa, ob], axis=-1)
    h = _mm_ln(attn, w_out.astype(BF16), h, ln1_g[None, :], ln1_b[None, :], name="ev_out_proj_ln")
    return _ffn_ln(h, w1.astype(BF16), w3.astype(BF16), w2.astype(BF16),
                   ln2_g[None, :], ln2_b[None, :], name="ev_swiglu_ln")


def _moe(h, h_slabs, logits, w1, w3, w2, ln_g, ln_b):
    n, d = h.shape
    tm = EXPERT_TILE
    info, gates, cnt = _route(logits, name="od_route")
    counts = cnt[0, :N_EXPERTS].astype(I32)
    padded = (counts + tm - 1) // tm * tm
    pend = jnp.cumsum(padded)
    pstart = pend - padded
    e = info[:, :MOE_TOPK]
    dest = (pstart[e] + info[:, MOE_TOPK:2 * MOE_TOPK]).astype(I32)
    n_rows = -(-(n * MOE_TOPK + N_EXPERTS * (tm - 1)) // tm) * tm
    n_tiles = n_rows // tm
    tile_start = jnp.arange(n_tiles, dtype=I32) * tm
    tile_valid = (tile_start < pend[-1]).astype(I32)
    last = jnp.maximum(pend[-1] - 1, 0)
    tile_e = jnp.minimum(jnp.searchsorted(pend, jnp.minimum(tile_start, last), side="right"),
                         N_EXPERTS - 1).astype(I32)
    slabs = d // LANES
    tok = jnp.arange(n, dtype=I32) * slabs
    row_tok = jnp.zeros((n_rows,), I32).at[dest.T.reshape(-1)].set(jnp.concatenate([tok, tok]))
    ys = _moe_ffn(tile_e, tile_valid, row_tok, h_slabs, w1.astype(BF16),
                  w3.astype(BF16), w2.astype(BF16), tm=tm, name="od_moe_swiglu")
    return _combine_ln(dest[:, 0] * slabs, dest[:, 1] * slabs, h, ys, gates, ln_g[None, :],
                       ln_b[None, :], name="od_moe_combine_ln")


def _odd_layer(h, rel_table, w_in, q_norm_g, kv_norm_g, w_uq, w_qidx, w_uk, w_uv, w_out,
               ln1_g, ln1_b, router, w1, w3, w2, ln2_g, ln2_b, *, batch, seq):
    nh = N_DSA_HEADS
    rq, rkv = DSA_Q_RANK, DSA_KV_RANK
    scale = HEAD_DIM ** -0.5 * LOG2E
    w_in_p = jnp.concatenate([w_in[:, :rq + rkv],
                              _pad_cols(w_in[:, rq + rkv:rq + rkv + IDX_DIM], LANES),
                              _pad_cols(w_in[:, rq + rkv + IDX_DIM:], LANES)], axis=1)
    cq, ckv, kidx, widx = _dsa_in_proj(h, w_in_p.astype(BF16), q_norm_g[None, :], kv_norm_g[None, :],
                                       name="od_in_proj_rms")
    w_qidx_p = jnp.pad(w_qidx.reshape(rq, IDX_HEADS, IDX_DIM), ((0, 0), (0, 0), (0, LANES - IDX_DIM)))
    wq = jnp.concatenate([w_uq, w_qidx_p.reshape(rq, IDX_HEADS * LANES)], axis=1)
    qscale = jnp.concatenate([jnp.full((nh * HEAD_DIM,), scale, F32),
                              jnp.ones((IDX_HEADS * LANES,), F32)])[None, :]
    qh = _mm(cq, wq.astype(BF16), qscale, out_dtype=BF16, head_major=True, tm=512, tn=2048,
             name="od_q_proj")
    wkv = jnp.concatenate([w_uk.transpose(1, 0, 2).reshape(rkv, nh * HEAD_DIM),
                           w_uv.transpose(1, 0, 2).reshape(rkv, nh * HEAD_DIM)], axis=1)
    kvh = _mm(ckv, wkv.astype(BF16), jnp.ones((1, 2 * nh * HEAD_DIM), F32), out_dtype=BF16,
              head_major=True, tm=512, tn=2048, name="od_kv_proj")
    scores, thr, jcut = _idx_topk(qh, kidx, widx, batch=batch, seq=seq,
                                  ksel=min(DSA_TOPK_MAX, seq // 4), name="od_idx_topk")
    attn = _dsa_attention(qh, kvh, scores, thr, jcut,
                          _bias_tiles(rel_table, nh, ATTN_TILE, name="od_bias_tiles"),
                          batch=batch, seq=seq, name="od_dsa_attn")
    h, h_slabs, logits = _mm_ln_route(attn, w_out.astype(BF16), h, ln1_g[None, :], ln1_b[None, :],
                                      _pad_cols(router, LANES).astype(BF16), name="od_out_proj_ln")
    return _moe(h, h_slabs, logits, w1, w3, w2, ln2_g, ln2_b)


def kernel(x, rel_table, ev_w_in, ev_b_forget, ev_w_out, ev_ln1_g, ev_ln1_b, ev_ffn_w1, ev_ffn_w3, ev_ffn_w2, ev_ln2_g, ev_ln2_b, od_w_in, od_q_norm_g, od_kv_norm_g, od_w_uq, od_w_qidx, od_w_uk, od_w_uv, od_w_out, od_ln1_g, od_ln1_b, od_router, od_exp_w1, od_exp_w3, od_exp_w2, od_ln2_g, od_ln2_b):
    batch, seq, d = x.shape
    h = x.reshape(batch * seq, d)
    for layer in range(DEPTH):
        i = layer // 2
        if layer % 2 == 0:
            h = _even_layer(h, rel_table, ev_w_in[i], ev_b_forget[i], ev_w_out[i], ev_ln1_g[i],
                            ev_ln1_b[i], ev_ffn_w1[i], ev_ffn_w3[i], ev_ffn_w2[i], ev_ln2_g[i],
                            ev_ln2_b[i], batch=batch, seq=seq)
        else:
            h = _odd_layer(h, rel_table, od_w_in[i], od_q_norm_g[i], od_kv_norm_g[i], od_w_uq[i],
                           od_w_qidx[i], od_w_uk[i], od_w_uv[i], od_w_out[i], od_ln1_g[i],
                           od_ln1_b[i], od_router[i], od_exp_w1[i], od_exp_w3[i], od_exp_w2[i],
                           od_ln2_g[i], od_ln2_b[i], batch=batch, seq=seq)
    return h.reshape(batch, seq, d)
```

```python
import functools
import math

import jax
import jax.numpy as jnp
from jax import lax
from jax.experimental import pallas as pl
from jax.experimental.pallas import tpu as pltpu

F32 = jnp.float32
BF16 = jnp.bfloat16
I32 = jnp.int32

HEAD_DIM = 128
N_MOBA_HEADS = 8
N_FOX_HEADS = 8
MOBA_BLOCK = 256
MOBA_TOPK = 3
N_DSA_HEADS = 16
DSA_Q_RANK = 512
DSA_KV_RANK = 512
IDX_HEADS = 16
IDX_DIM = 64
DSA_TOPK_MAX = 256
REL_BUCKETS = 32
REL_MAX_DIST = 128
N_EXPERTS = 8
MOE_TOPK = 2
LN_EPS = 1e-5
RMS_EPS = 1e-6
DEPTH = 2
DEEPNORM_ALPHA = (2 * DEPTH) ** 0.25

LANES = 128
ATTN_TILE = 256
WIDE_CHUNK = 1024
DSA_KEY_TILE = 512
DSA_HEAD_GROUP = 4
SCORE_STRIP = 64
COUNT_STRIP = 128
EXPERT_TILE = 512
ROW_DMA_UNROLL = 8
NEG = -1e30
LOG2E = math.log2(math.e)
INT_MIN = -(2 ** 31)
VMEM_LIMIT = 56 * 1024 * 1024


def _cparams(sem, vmem=VMEM_LIMIT):
    return pltpu.CompilerParams(dimension_semantics=sem, vmem_limit_bytes=vmem)


def _dot(a, b):
    return jnp.dot(a, b, preferred_element_type=F32)


def _dot_t(a, b):
    return lax.dot_general(a, b, (((1,), (1,)), ((), ())), preferred_element_type=F32)


def _layer_norm(y, g, b):
    mu = jnp.mean(y, axis=-1, keepdims=True)
    d = y - mu
    var = jnp.mean(d * d, axis=-1, keepdims=True)
    return d * lax.rsqrt(var + LN_EPS) * g + b


def _mm_kernel(x_ref, w_ref, cs_ref, o_ref, *, head_major):
    acc = _dot(x_ref[...].astype(BF16), w_ref[...]) * cs_ref[...]
    if head_major:
        for c in range(o_ref.shape[0]):
            o_ref[c] = acc[:, c * LANES:(c + 1) * LANES].astype(o_ref.dtype)
    else:
        o_ref[...] = acc.astype(o_ref.dtype)


def _mm(x, w, colscale, *, out_dtype, head_major=False, tm=512, tn=512, name):
    m, k = x.shape
    n = w.shape[1]
    tm, tn = min(tm, m), min(tn, n)
    assert m % tm == 0 and n % tn == 0 and tn % LANES == 0
    if head_major:
        out_shape = jax.ShapeDtypeStruct((n // LANES, m, LANES), out_dtype)
        out_spec = pl.BlockSpec((tn // LANES, tm, LANES), lambda i, j: (j, i, 0))
    else:
        out_shape = jax.ShapeDtypeStruct((m, n), out_dtype)
        out_spec = pl.BlockSpec((tm, tn), lambda i, j: (i, j))
    return pl.pallas_call(
        functools.partial(_mm_kernel, head_major=head_major),
        out_shape=out_shape,
        grid=(m // tm, n // tn),
        in_specs=[pl.BlockSpec((tm, k), lambda i, j: (i, 0)),
                  pl.BlockSpec((k, tn), lambda i, j: (0, j)),
                  pl.BlockSpec((1, tn), lambda i, j: (0, j))],
        out_specs=out_spec,
        compiler_params=_cparams(("parallel", "arbitrary")),
        name=name,
    )(x, w, colscale)


def _mm_ln_kernel(x_ref, w_ref, res_ref, g_ref, b_ref, o_ref):
    y = DEEPNORM_ALPHA * res_ref[...] + _dot(x_ref[...], w_ref[...])
    o_ref[...] = _layer_norm(y, g_ref[...], b_ref[...])


def _mm_ln(x, w, res, g, b, *, tm=512, name):
    m, k = x.shape
    d = w.shape[1]
    tm = min(tm, m)
    return pl.pallas_call(
        _mm_ln_kernel,
        out_shape=jax.ShapeDtypeStruct((m, d), F32),
        grid=(m // tm,),
        in_specs=[pl.BlockSpec((tm, k), lambda i: (i, 0)),
                  pl.BlockSpec((k, d), lambda i: (0, 0)),
                  pl.BlockSpec((tm, d), lambda i: (i, 0)),
                  pl.BlockSpec((1, d), lambda i: (0, 0)),
                  pl.BlockSpec((1, d), lambda i: (0, 0))],
        out_specs=pl.BlockSpec((tm, d), lambda i: (i, 0)),
        compiler_params=_cparams(("parallel",)),
        name=name,
    )(x, w, res, g, b)


def _mm_ln_route_kernel(x_ref, w_ref, res_ref, g_ref, b_ref, wr_ref, o_ref, slab_ref, lg_ref,
                        *, slabs):
    y = _layer_norm(DEEPNORM_ALPHA * res_ref[...] + _dot(x_ref[...], w_ref[...]),
                    g_ref[...], b_ref[...])
    o_ref[...] = y
    tm = y.shape[0]
    for c in range(slabs):
        slab_ref[pl.ds(c, tm, stride=slabs), :] = y[:, c * LANES:(c + 1) * LANES]
    lg_ref[...] = _dot(y.astype(BF16), wr_ref[...])


def _mm_ln_route(x, w, res, g, b, w_router, *, tm=512, name):
    m, k = x.shape
    d = w.shape[1]
    tm = min(tm, m)
    slabs = d // LANES
    row = lambda i: (i, 0)
    const = lambda i: (0, 0)
    return pl.pallas_call(
        functools.partial(_mm_ln_route_kernel, slabs=slabs),
        out_shape=(jax.ShapeDtypeStruct((m, d), F32), jax.ShapeDtypeStruct((m * slabs, LANES), F32),
                   jax.ShapeDtypeStruct((m, LANES), F32)),
        grid=(m // tm,),
        in_specs=[pl.BlockSpec((tm, k), row), pl.BlockSpec((k, d), const), pl.BlockSpec((tm, d), row),
                  pl.BlockSpec((1, d), const), pl.BlockSpec((1, d), const),
                  pl.BlockSpec((d, LANES), const)],
        out_specs=(pl.BlockSpec((tm, d), row), pl.BlockSpec((tm * slabs, LANES), row),
                   pl.BlockSpec((tm, LANES), row)),
        compiler_params=_cparams(("parallel",)),
        name=name,
    )(x, w, res, g, b, w_router)


def _swiglu_step(xb, w1_ref, w3_ref, w2_ref):
    a = _dot(xb, w1_ref[...])
    c = _dot(xb, w3_ref[...])
    hmid = a / (1.0 + jnp.exp(-a)) * c
    return _dot(hmid.astype(BF16), w2_ref[...])


def _ffn_ln_kernel(x_ref, w1_ref, w3_ref, w2_ref, g_ref, b_ref, o_ref, acc_ref, xb_ref):
    f = pl.program_id(1)

    @pl.when(f == 0)
    def _():
        acc_ref[...] = jnp.zeros_like(acc_ref)
        xb_ref[...] = x_ref[...].astype(BF16)

    acc_ref[...] += _swiglu_step(xb_ref[...], w1_ref, w3_ref, w2_ref)

    @pl.when(f == pl.num_programs(1) - 1)
    def _():
        y = DEEPNORM_ALPHA * x_ref[...] + acc_ref[...]
        o_ref[...] = _layer_norm(y, g_ref[...], b_ref[...])


def _ffn_ln(x, w1, w3, w2, g, b, *, tm=512, tf=512, name):
    m, d = x.shape
    dff = w1.shape[1]
    tm, tf = min(tm, m), min(tf, dff)
    assert m % tm == 0 and dff % tf == 0
    return pl.pallas_call(
        _ffn_ln_kernel,
        out_shape=jax.ShapeDtypeStruct((m, d), F32),
        grid=(m // tm, dff // tf),
        in_specs=[pl.BlockSpec((tm, d), lambda i, f: (i, 0)),
                  pl.BlockSpec((d, tf), lambda i, f: (0, f)),
                  pl.BlockSpec((d, tf), lambda i, f: (0, f)),
                  pl.BlockSpec((tf, d), lambda i, f: (f, 0)),
                  pl.BlockSpec((1, d), lambda i, f: (0, 0)),
                  pl.BlockSpec((1, d), lambda i, f: (0, 0))],
        out_specs=pl.BlockSpec((tm, d), lambda i, f: (i, 0)),
        scratch_shapes=[pltpu.VMEM((tm, d), F32), pltpu.VMEM((tm, d), BF16)],
        compiler_params=_cparams(("parallel", "arbitrary")),
        name=name,
    )(x, w1, w3, w2, g, b)


def _row_gather(src_hbm, idx_ref, base, bufs, sems, slot, n_rows, slabs, *, wait,
                two_queues=False):
    for s, buf in enumerate(bufs):
        @pl.when(slot == s)
        def _(s=s, buf=buf):
            def body(pair, _):
                for odd in range(2):
                    r = 2 * pair + odd
                    first = 0 if wait else pl.multiple_of(idx_ref[base + r], slabs)
                    src = src_hbm.at[pl.ds(first, slabs)]
                    dst = buf.at[pl.ds(pl.multiple_of(r * slabs, slabs), slabs)]
                    cp = pltpu.make_async_copy(src, dst, sems.at[s])
                    if wait:
                        cp.wait()
                    else:
                        cp.start(priority=odd if two_queues else 0)
                return 0
            lax.fori_loop(0, n_rows // 2, body, 0, unroll=ROW_DMA_UNROLL // 2)


def _slabs_to_rows(buf, slabs):
    rows = buf.shape[0] // slabs
    return jnp.concatenate([buf[pl.ds(c, rows, stride=slabs), :] for c in range(slabs)], axis=1)


def _moe_ffn_kernel(te_ref, tv_ref, tok_ref, h_ref, w1_ref, w3_ref, w2_ref, o_ref,
                    acc_ref, xb_ref, xg0_ref, xg1_ref, sems, *, tm, slabs):
    i, f = pl.program_id(0), pl.program_id(1)
    nt = pl.num_programs(0)
    bufs = (xg0_ref, xg1_ref)
    gather = functools.partial(_row_gather, h_ref, tok_ref, bufs=bufs, sems=sems, n_rows=tm,
                               slabs=slabs)

    @pl.when(tv_ref[i] > 0)
    def _():
        @pl.when(f == 0)
        def _():
            @pl.when(i == 0)
            def _():
                gather(base=0, slot=0, wait=False)

            gather(base=i * tm, slot=i % 2, wait=True)
            nxt = jnp.minimum(i + 1, nt - 1)

            @pl.when(jnp.logical_and(i + 1 < nt, tv_ref[nxt] > 0))
            def _():
                gather(base=nxt * tm, slot=nxt % 2, wait=False)

            for s, buf in enumerate(bufs):
                @pl.when(i % 2 == s)
                def _(buf=buf):
                    xb_ref[...] = _slabs_to_rows(buf, slabs).astype(BF16)

            acc_ref[...] = jnp.zeros_like(acc_ref)

        acc_ref[...] += _swiglu_step(xb_ref[...], w1_ref, w3_ref, w2_ref)

        @pl.when(f == pl.num_programs(1) - 1)
        def _():
            for c in range(slabs):
                o_ref[pl.ds(c, tm, stride=slabs), :] = acc_ref[:, c * LANES:(c + 1) * LANES]

    @pl.when(tv_ref[i] == 0)
    def _():
        o_ref[...] = jnp.zeros_like(o_ref)


def _moe_ffn(tile_e, tile_valid, row_tok, h_slabs, w1, w3, w2, *, tm, tf=1024, name):
    n_rows = row_tok.shape[0]
    d = w1.shape[1]
    slabs = d // LANES
    dff = w1.shape[2]
    tf = min(tf, dff)
    nf = dff // tf
    assert n_rows % tm == 0 and dff % tf == 0

    def fidx(i, f, tv):
        return jnp.where(tv[i] > 0, f, nf - 1)

    return pl.pallas_call(
        functools.partial(_moe_ffn_kernel, tm=tm, slabs=slabs),
        out_shape=jax.ShapeDtypeStruct((n_rows * slabs, LANES), F32),
        grid_spec=pltpu.PrefetchScalarGridSpec(
            num_scalar_prefetch=3,
            grid=(n_rows // tm, nf),
            in_specs=[pl.BlockSpec(memory_space=pl.ANY),
                      pl.BlockSpec((None, d, tf), lambda i, f, te, tv, tok: (te[i], 0, fidx(i, f, tv))),
                      pl.BlockSpec((None, d, tf), lambda i, f, te, tv, tok: (te[i], 0, fidx(i, f, tv))),
                      pl.BlockSpec((None, tf, d), lambda i, f, te, tv, tok: (te[i], fidx(i, f, tv), 0))],
            out_specs=pl.BlockSpec((tm * slabs, LANES), lambda i, f, te, tv, tok: (i, 0)),
            scratch_shapes=[pltpu.VMEM((tm, d), F32), pltpu.VMEM((tm, d), BF16),
                            pltpu.VMEM((tm * slabs, LANES), F32), pltpu.VMEM((tm * slabs, LANES), F32),
                            pltpu.SemaphoreType.DMA((2,))]),
        compiler_params=_cparams(("arbitrary", "arbitrary")),
        name=name,
    )(tile_e, tile_valid, row_tok, h_slabs, w1, w3, w2)


def _online_softmax_step(s, v, carry):
    m, acc = carry
    m_new = jnp.maximum(m, jnp.max(s, axis=-1, keepdims=True))
    alpha = jnp.exp2(m - m_new)
    p = jnp.exp2((s - m_new).astype(BF16))
    return m_new, alpha * acc + _dot(p, _with_ones(v))


def _with_ones(v):
    return jnp.concatenate([v, jnp.ones_like(v)], axis=1)


def _softmax_init(tq):
    return jnp.full((tq, 1), NEG, F32), jnp.zeros((tq, 2 * HEAD_DIM), F32)


def _softmax_finish(acc):
    return acc[:, :HEAD_DIM] / acc[:, HEAD_DIM:]


def _pairwise_loop(n, body, carry):
    carry = lax.fori_loop(0, n // 2, lambda k, c: body(2 * k + 1, body(2 * k, c)), carry)
    return lax.fori_loop(n // 2 * 2, n, body, carry)


def _head_cols(h):
    return slice(h * HEAD_DIM, (h + 1) * HEAD_DIM)


def _kv_block(k_ref, v_ref, n, width, h):
    rows = pl.ds(pl.multiple_of(n * width, width), width)
    return k_ref[rows, _head_cols(h)], v_ref[rows, _head_cols(h)]


def _causal_mask(blk):
    r = lax.broadcasted_iota(I32, (blk, blk), 0)
    c = lax.broadcasted_iota(I32, (blk, blk), 1)
    return c <= r


def _moba_select(q, km, i, blk):
    nbp = km.shape[0]
    gate = _dot_t(q, km.astype(BF16))
    lane = lax.broadcasted_iota(I32, (blk, nbp), 1)
    lane_f = lane.astype(F32)
    g = jnp.where(lane < i, gate, -jnp.inf)
    sel = jnp.zeros((blk, nbp), F32)
    for _ in range(MOBA_TOPK):
        mx = jnp.max(g, axis=-1, keepdims=True)
        first = jnp.min(jnp.where(g == mx, lane_f, float(nbp)), axis=-1, keepdims=True)
        pick = jnp.logical_and(lane_f == first, mx > -jnp.inf)
        sel = jnp.where(pick, 1.0, sel)
        g = jnp.where(pick, -jnp.inf, g)
    return sel, lane


def _moba_kernel(q_ref, k_ref, v_ref, bias_ref, o_ref, km_ref, *, blk, nb, wide, hp):
    i = pl.program_id(2)
    per = wide // blk

    @pl.when(i == 0)
    def _():
        km_ref[...] = jnp.zeros_like(km_ref)

        def mean_body(n, _):
            for h in range(hp):
                kblk, _ = _kv_block(k_ref, v_ref, n, blk, h)
                km_ref[h, pl.ds(n, 1), :] = jnp.mean(kblk.astype(F32), axis=0, keepdims=True)
            return 0

        lax.fori_loop(0, nb, mean_body, 0)

    qs = [q_ref[:, _head_cols(h)] for h in range(hp)]
    sels = [_moba_select(qs[h], km_ref[h], i, blk) for h in range(hp)]

    def chunk(n, carries, nblk, near):
        rows = pl.ds(pl.multiple_of(n * wide, wide), nblk * blk)
        logits = [_dot_t(qs[h], k_ref[rows, _head_cols(h)]) for h in range(hp)]
        causal = _causal_mask(blk)
        out = []
        for h in range(hp):
            s, vblk = logits[h], v_ref[rows, _head_cols(h)]
            sel, lane = sels[h]
            parts = []
            for u in range(nblk):
                blk_id = n * per + u
                part = s[:, u * blk:(u + 1) * blk]
                chosen = jnp.sum(jnp.where(lane == blk_id, sel, 0.0), axis=-1, keepdims=True) > 0.0
                if near:
                    part = part + jnp.where(blk_id == i, bias_ref[0, h],
                                            jnp.where(blk_id == i - 1, bias_ref[1, h], 0.0))
                    parts.append(jnp.where(blk_id == i, jnp.where(causal, part, NEG),
                                           jnp.where(chosen, part, NEG)))
                else:
                    parts.append(jnp.where(chosen, part, NEG))
            s = parts[0] if nblk == 1 else jnp.concatenate(parts, axis=1)
            out.append(_online_softmax_step(s, vblk, carries[h]))
        return tuple(out)

    first_near = jnp.maximum(i - 1, 0) // per
    last = i // per
    carries = tuple(_softmax_init(blk) for _ in range(hp))
    carries = _pairwise_loop(first_near, lambda n, c: chunk(n, c, per, False), carries)
    carries = lax.fori_loop(first_near, last, lambda n, c: chunk(n, c, per, True), carries)
    for r in range(per):
        @pl.when(i % per == r)
        def _(r=r):
            done = chunk(last, carries, r + 1, True)
            for h in range(hp):
                o_ref[:, _head_cols(h)] = _softmax_finish(done[h][1]).astype(o_ref.dtype)


def _moba_attention(qkv, bias, *, batch, seq, hp=4, name):
    blk = MOBA_BLOCK
    nb = seq // blk
    nbp = -(-nb // LANES) * LANES
    h = N_MOBA_HEADS
    nq = seq // blk
    hg = h // hp
    w = hp * HEAD_DIM
    return pl.pallas_call(
        functools.partial(_moba_kernel, blk=blk, nb=nb, wide=min(WIDE_CHUNK, seq), hp=hp),
        out_shape=jax.ShapeDtypeStruct((batch * seq, h * HEAD_DIM), BF16),
        grid=(batch, hg, nq),
        in_specs=[pl.BlockSpec((blk, w), lambda b, g, i: (b * nq + i, g)),
                  pl.BlockSpec((seq, w), lambda b, g, i: (b, hg + g)),
                  pl.BlockSpec((seq, w), lambda b, g, i: (b, 2 * hg + g)),
                  pl.BlockSpec((2, hp, blk, blk), lambda b, g, i: (0, g, 0, 0))],
        out_specs=pl.BlockSpec((blk, w), lambda b, g, i: (b * nq + i, g)),
        scratch_shapes=[pltpu.VMEM((hp, nbp, HEAD_DIM), F32)],
        compiler_params=_cparams(("parallel", "parallel", "arbitrary")),
        name=name,
    )(qkv, qkv, qkv, bias)


def _fox_kernel(q_ref, k_ref, v_ref, ck_ref, o_ref, *, blk, wide, hp):
    i = pl.program_id(2)
    per = wide // blk
    qs = [q_ref[:, _head_cols(h)] for h in range(hp)]

    def chunk(n, carries, nblk, causal):
        width = nblk * blk
        rows = pl.ds(pl.multiple_of(n * wide, wide), width)
        logits = [_dot_t(qs[h], k_ref[rows, _head_cols(h)]) for h in range(hp)]
        if causal:
            row = i * blk + lax.broadcasted_iota(I32, (blk, width), 0)
            col = n * wide + lax.broadcasted_iota(I32, (blk, width), 1)
            visible = col <= row
        out = []
        for h in range(hp):
            ck = [ck_ref[h, n * per + u] for u in range(nblk)]
            s = logits[h] - (ck[0] if nblk == 1 else jnp.concatenate(ck, axis=1))
            if causal:
                s = jnp.where(visible, s, NEG)
            out.append(_online_softmax_step(s, v_ref[rows, _head_cols(h)], carries[h]))
        return tuple(out)

    n_wide = i // per
    carries = tuple(_softmax_init(blk) for _ in range(hp))
    carries = _pairwise_loop(n_wide, lambda n, c: chunk(n, c, per, False), carries)
    for r in range(per):
        @pl.when(i % per == r)
        def _(r=r):
            done = chunk(n_wide, carries, r + 1, True)
            for h in range(hp):
                o_ref[:, _head_cols(h)] = _softmax_finish(done[h][1]).astype(o_ref.dtype)


def _fox_attention(qkv, csum, *, batch, seq, hp=4, name):
    blk = ATTN_TILE
    h = N_FOX_HEADS
    nq = seq // blk
    hg = h // hp
    base = 3 * N_MOBA_HEADS // hp
    w = hp * HEAD_DIM
    return pl.pallas_call(
        functools.partial(_fox_kernel, blk=blk, wide=min(WIDE_CHUNK, seq), hp=hp),
        out_shape=jax.ShapeDtypeStruct((batch * seq, h * HEAD_DIM), BF16),
        grid=(batch, hg, nq),
        in_specs=[pl.BlockSpec((blk, w), lambda b, g, i: (b * nq + i, base + g)),
                  pl.BlockSpec((seq, w), lambda b, g, i: (b, base + hg + g)),
                  pl.BlockSpec((seq, w), lambda b, g, i: (b, base + 2 * hg + g)),
                  pl.BlockSpec((None, hp, nq, 1, blk), lambda b, g, i: (b, g, 0, 0, 0))],
        out_specs=pl.BlockSpec((blk, w), lambda b, g, i: (b * nq + i, g)),
        compiler_params=_cparams(("parallel", "parallel", "arbitrary")),
        name=name,
    )(qkv, qkv, qkv, csum)


def _forget_csum_kernel(fb_ref, b_ref, o_ref, carry_ref, *, tc):
    @pl.when(pl.program_id(1) == 0)
    def _():
        carry_ref[...] = jnp.zeros_like(carry_ref)

    z = fb_ref[...] + b_ref[...]
    run = jnp.minimum(z, 0.0) - jnp.log1p(jnp.exp(-jnp.abs(z)))
    row = lax.broadcasted_iota(I32, (tc, LANES), 0)
    shift = 1
    while shift < tc:
        run = run + jnp.where(row >= shift, pltpu.roll(run, shift, axis=0), 0.0)
        shift *= 2
    run = run + carry_ref[0:1, :]
    carry_ref[...] = jnp.broadcast_to(run[tc - 1:tc, :], carry_ref.shape)
    o_ref[...] = (run * LOG2E).T[:o_ref.shape[0], :]


def _forget_csum(fb, bias, *, batch, seq, n_heads, tc=512, name):
    tc = min(tc, seq)
    nt = seq // tc
    return pl.pallas_call(
        functools.partial(_forget_csum_kernel, tc=tc),
        out_shape=jax.ShapeDtypeStruct((batch, n_heads, seq), F32),
        grid=(batch, nt),
        in_specs=[pl.BlockSpec((tc, LANES), lambda b, j: (b * nt + j, 0)),
                  pl.BlockSpec((1, LANES), lambda b, j: (0, 0))],
        out_specs=pl.BlockSpec((None, n_heads, tc), lambda b, j: (b, 0, j)),
        scratch_shapes=[pltpu.VMEM((8, LANES), F32)],
        compiler_params=_cparams(("parallel", "arbitrary")),
        name=name,
    )(fb, bias)


def _dsa_in_kernel(x_ref, w_ref, gq_ref, gkv_ref, cq_ref, ckv_ref, kidx_ref, widx_ref):
    acc = _dot(x_ref[...].astype(BF16), w_ref[...])
    rq, rkv = DSA_Q_RANK, DSA_KV_RANK

    def rms(z, g):
        return z * lax.rsqrt(jnp.mean(z * z, axis=-1, keepdims=True) + RMS_EPS) * g

    cq_ref[...] = rms(acc[:, :rq], gq_ref[...]).astype(cq_ref.dtype)
    ckv_ref[...] = rms(acc[:, rq:rq + rkv], gkv_ref[...]).astype(ckv_ref.dtype)
    kidx_ref[...] = acc[:, rq + rkv:rq + rkv + LANES].astype(kidx_ref.dtype)
    widx_ref[...] = acc[:, rq + rkv + LANES:] * (IDX_HEADS ** -0.5 * IDX_DIM ** -0.5)


def _dsa_in_proj(x, w, gq, gkv, *, tm=512, name):
    m, k = x.shape
    n = w.shape[1]
    tm = min(tm, m)
    rq, rkv = DSA_Q_RANK, DSA_KV_RANK
    row = lambda i: (i, 0)
    const = lambda i: (0, 0)
    return pl.pallas_call(
        _dsa_in_kernel,
        out_shape=(jax.ShapeDtypeStruct((m, rq), BF16), jax.ShapeDtypeStruct((m, rkv), BF16),
                   jax.ShapeDtypeStruct((m, LANES), BF16), jax.ShapeDtypeStruct((m, LANES), F32)),
        grid=(m // tm,),
        in_specs=[pl.BlockSpec((tm, k), row), pl.BlockSpec((k, n), const),
                  pl.BlockSpec((1, rq), const), pl.BlockSpec((1, rkv), const)],
        out_specs=(pl.BlockSpec((tm, rq), row), pl.BlockSpec((tm, rkv), row),
                   pl.BlockSpec((tm, LANES), row), pl.BlockSpec((tm, LANES), row)),
        compiler_params=_cparams(("parallel",)),
        name=name,
    )(x, w, gq, gkv)


def _fold_lanes(x):
    part = x[:, :LANES]
    for g in range(1, x.shape[1] // LANES):
        part = part + x[:, g * LANES:(g + 1) * LANES]
    return part


def _idx_topk_kernel(qi_ref, kj_ref, q_ref, k_ref, w_ref, sc_ref, thr_ref, jc_ref, keys_ref,
                     *, t, tk, ksel, seq):
    p = pl.program_id(1)
    i, j = qi_ref[p], kj_ref[p]
    k = k_ref[...]
    for s0 in range(0, t, SCORE_STRIP):
        rows = slice(s0, s0 + SCORE_STRIP)
        stacked = jnp.concatenate([q_ref[h, rows, :] for h in range(IDX_HEADS)], axis=0)
        d = _dot_t(stacked, k)
        acc = jnp.zeros((SCORE_STRIP, tk), F32)
        for h in range(IDX_HEADS):
            part = d[h * SCORE_STRIP:(h + 1) * SCORE_STRIP]
            acc = acc + jnp.maximum(part, 0.0) * w_ref[rows, h:h + 1]
        sc_ref[rows, :] = acc
        row = i * t + s0 + lax.broadcasted_iota(I32, (SCORE_STRIP, tk), 0)
        lane_col = lax.broadcasted_iota(I32, (SCORE_STRIP, tk), 1)
        bits = pltpu.bitcast(acc, I32)
        key = jnp.where(bits < 0, bits ^ jnp.int32(0x7FFFFFFF), bits)
        key = jnp.where(bits == jnp.int32(INT_MIN), 0, key)
        keys_ref[j, rows, :] = jnp.where(j * tk + lane_col <= row, key, jnp.int32(INT_MIN))
    j_last = (i * t + t - 1) // tk

    @pl.when(j == j_last)
    def _():
        lane = lax.broadcasted_iota(I32, (COUNT_STRIP, LANES), 1)
        strips = [slice(s0, s0 + COUNT_STRIP) for s0 in range(0, t, COUNT_STRIP)]

        def count(ref, pred, *row_args):
            dt = ref.dtype
            reps = [[jnp.broadcast_to(a[rows], (COUNT_STRIP, LANES)).astype(dt) for a in row_args]
                    for rows in strips]
            cnts = []
            for rows, args in zip(strips, reps):
                def body(c, cnt, rows=rows, args=args):
                    for g in range(tk // LANES):
                        keys = ref[c, rows, g * LANES:(g + 1) * LANES]
                        cnt = cnt + pred(keys, c * tk + g * LANES, *args).astype(dt)
                    return cnt

                cnts.append(_pairwise_loop(j_last + 1, body, jnp.zeros((COUNT_STRIP, LANES), dt)))
            cnt = jnp.concatenate(cnts, axis=0)
            return jnp.sum(cnt.astype(F32), axis=-1, keepdims=True).astype(I32)

        row1 = i * t + lax.broadcasted_iota(I32, (t, 1), 0)
        full = row1 >= ksel

        def unsettled(state):
            _, lo, hi, settled = state
            return jnp.logical_and(settled == 0, hi - lo != 1)

        def probe(state):
            step, lo, hi, settled = state
            cand = lo + lax.shift_right_logical(hi - lo, 1)
            n_ge = count(keys_ref, lambda k, c0, cd: k >= cd, cand)
            move = jnp.logical_and(settled == 0, n_ge >= ksel)
            stop = jnp.logical_and(settled == 0, n_ge < ksel)
            return (step + 1, jnp.where(move, cand, lo), jnp.where(stop, cand, hi),
                    jnp.where(n_ge == ksel, 1, settled))

        def searching(state):
            busy = jnp.max(jnp.where(unsettled(state), 1.0, 0.0)) > 0.0
            return jnp.logical_and(state[0] < 34, busy)

        start = (jnp.int32(0), jnp.full((t, 1), INT_MIN, I32), jnp.full((t, 1), 2 ** 31 - 1, I32),
                 jnp.where(full, 0, 1))
        _, ans, _, settled = lax.while_loop(searching, probe, start)

        def tie_counts(_):
            return (ksel - count(keys_ref, lambda k, c0, a: k > a, ans),
                    count(keys_ref, lambda k, c0, a: k == a, ans))

        exact = jnp.max(jnp.where(settled == 0, 1.0, 0.0)) > 0.0
        need, ties = lax.cond(exact, tie_counts,
                              lambda _: (jnp.zeros((t, 1), I32), jnp.zeros((t, 1), I32)), 0)

        def tie_cut(_):
            nbits = seq.bit_length() - 1

            def bit_body(b, cut):
                cand = cut + lax.shift_left(jnp.int32(1), nbits - 1 - b)
                below = count(keys_ref, lambda k, c0, a, cd: jnp.logical_and(k == a, c0 + lane < cd),
                              ans, cand)
                return jnp.where(below < need, cand, cut)
            return lax.fori_loop(0, nbits, bit_body, jnp.zeros((t, 1), I32))

        contested = jnp.max(jnp.where(jnp.logical_and(full, ties > need), 1.0, 0.0)) > 0.0
        cut = lax.cond(contested, tie_cut, lambda _: jnp.full((t, 1), seq - 1, I32), 0)
        tbits = jnp.where(ans < 0, ans ^ jnp.int32(0x7FFFFFFF), ans)
        thr = jnp.where(full, pltpu.bitcast(tbits, F32), -jnp.inf)
        thr_ref[...] = jnp.broadcast_to(thr, thr_ref.shape)
        jc_ref[...] = jnp.broadcast_to(jnp.where(full, cut, seq - 1), jc_ref.shape)


def _causal_pairs(seq, t, tk):
    pairs = [(i, j) for i in range(seq // t) for j in range((i * t + t - 1) // tk + 1)]
    return jnp.asarray([p[0] for p in pairs], I32), jnp.asarray([p[1] for p in pairs], I32)


def _idx_topk(qh, kidx, widx, *, batch, seq, ksel, name):
    t, tk = ATTN_TILE, min(DSA_KEY_TILE, seq)
    nq, nk = seq // t, seq // tk
    assert seq & (seq - 1) == 0 and tk >= ksel
    qi, kj = _causal_pairs(seq, t, tk)
    npairs = qi.shape[0]
    qrow = lambda b, p, qi, kj: (b * nq + qi[p], 0)
    return pl.pallas_call(
        functools.partial(_idx_topk_kernel, t=t, tk=tk, ksel=ksel, seq=seq),
        out_shape=(jax.ShapeDtypeStruct((batch, npairs, t, tk), F32),
                   jax.ShapeDtypeStruct((batch * seq, LANES), F32),
                   jax.ShapeDtypeStruct((batch * seq, LANES), I32)),
        grid_spec=pltpu.PrefetchScalarGridSpec(
            num_scalar_prefetch=2,
            grid=(batch, npairs),
            in_specs=[pl.BlockSpec((IDX_HEADS, t, LANES), lambda b, p, qi, kj: (1, b * nq + qi[p], 0)),
                      pl.BlockSpec((tk, LANES), lambda b, p, qi, kj: (b * nk + kj[p], 0)),
                      pl.BlockSpec((t, LANES), qrow)],
            out_specs=(pl.BlockSpec((None, None, t, tk), lambda b, p, qi, kj: (b, p, 0, 0)),
                       pl.BlockSpec((t, LANES), qrow), pl.BlockSpec((t, LANES), qrow)),
            scratch_shapes=[pltpu.VMEM((nk, t, tk), I32)]),
        compiler_params=_cparams(("parallel", "arbitrary")),
        name=name,
    )(qi, kj, qh, kidx, widx)


def _dsa_attn_kernel(qi_ref, kj_ref, q_ref, k_ref, v_ref, sc_ref, thr_ref, jc_ref, bias_ref,
                     o_ref, m_ref, acc_ref, *, t, tk):
    p = pl.program_id(1)
    i, j = qi_ref[p], kj_ref[p]

    @pl.when(j == 0)
    def _():
        m_ref[...] = jnp.full(m_ref.shape, NEG, F32)
        acc_ref[...] = jnp.zeros_like(acc_ref)

    sc = sc_ref[...]
    thr = thr_ref[:, 0:1]
    col = j * tk + lax.broadcasted_iota(I32, (t, tk), 1)
    row = i * t + lax.broadcasted_iota(I32, (t, tk), 0)
    tie = jnp.logical_and(sc == thr, col <= jc_ref[:, 0:1])
    keep = jnp.logical_and(jnp.logical_or(sc > thr, tie), col <= row)
    offs = [i * t - (j * tk + u * t) for u in range(tk // t)]

    def run(with_bias, n_parts):
        w = n_parts * t
        for g in range(N_DSA_HEADS // DSA_HEAD_GROUP):
            heads = [g * DSA_HEAD_GROUP + u for u in range(DSA_HEAD_GROUP)]
            logits = [_dot_t(q_ref[h], k_ref[h, :w, :]) for h in heads]
            for h, s in zip(heads, logits):
                if with_bias:
                    parts = [jnp.where(off == 0, bias_ref[0, h],
                                       jnp.where(off == t, bias_ref[1, h], 0.0))
                             for off in offs[:n_parts]]
                    s = s + (parts[0] if n_parts == 1 else jnp.concatenate(parts, axis=1))
                s = jnp.where(keep[:, :w], s, NEG)
                m_prev = m_ref[h]
                m_next = jnp.maximum(m_prev, jnp.max(s, axis=-1, keepdims=True))
                alpha = jnp.exp2(m_prev - m_next)
                p = jnp.exp2((s - jnp.concatenate([m_next] * (w // HEAD_DIM), axis=1)).astype(BF16))
                acc_ref[h] = (jnp.concatenate([alpha, alpha], axis=1) * acc_ref[h]
                              + _dot(p, _with_ones(v_ref[h, :w, :])))
                m_ref[h] = m_next

    n_all = tk // t
    near = offs[-1] <= t
    starts_own = offs[0] == 0

    @pl.when(jnp.logical_and(near, starts_own))
    def _():
        run(True, 1)

    @pl.when(jnp.logical_and(near, jnp.logical_not(starts_own)))
    def _():
        run(True, n_all)

    @pl.when(jnp.logical_not(near))
    def _():
        run(False, n_all)

    @pl.when(j == (i * t + t - 1) // tk)
    def _():
        for h in range(N_DSA_HEADS):
            o_ref[:, _head_cols(h)] = _softmax_finish(acc_ref[h]).astype(o_ref.dtype)


def _dsa_attention(qh, kvh, scores, thr, jcut, bias, *, batch, seq, name):
    t, tk = ATTN_TILE, min(DSA_KEY_TILE, seq)
    nq, nk = seq // t, seq // tk
    h = N_DSA_HEADS
    qi, kj = _causal_pairs(seq, t, tk)
    qrow = lambda b, p, qi, kj: (b * nq + qi[p], 0)
    return pl.pallas_call(
        functools.partial(_dsa_attn_kernel, t=t, tk=tk),
        out_shape=jax.ShapeDtypeStruct((batch * seq, h * HEAD_DIM), BF16),
        grid_spec=pltpu.PrefetchScalarGridSpec(
            num_scalar_prefetch=2,
            grid=(batch, qi.shape[0]),
            in_specs=[pl.BlockSpec((h, t, HEAD_DIM), lambda b, p, qi, kj: (0, b * nq + qi[p], 0)),
                      pl.BlockSpec((h, tk, HEAD_DIM), lambda b, p, qi, kj: (0, b * nk + kj[p], 0)),
                      pl.BlockSpec((h, tk, HEAD_DIM), lambda b, p, qi, kj: (1, b * nk + kj[p], 0)),
                      pl.BlockSpec((None, None, t, tk), lambda b, p, qi, kj: (b, p, 0, 0)),
                      pl.BlockSpec((t, LANES), qrow),
                      pl.BlockSpec((t, LANES), qrow),
                      pl.BlockSpec((2, h, t, t), lambda b, p, qi, kj: (0, 0, 0, 0))],
            out_specs=pl.BlockSpec((t, h * HEAD_DIM), qrow),
            scratch_shapes=[pltpu.VMEM((h, t, HEAD_DIM), F32),
                            pltpu.VMEM((h, t, 2 * HEAD_DIM), F32)]),
        compiler_params=_cparams(("parallel", "arbitrary")),
        name=name,
    )(qi, kj, qh, kvh, kvh, scores, thr, jcut, bias)


def _route_kernel(lg_ref, info_ref, gate_ref, cnt_ref, carry_ref, *, tm):
    i = pl.program_id(0)

    @pl.when(i == 0)
    def _():
        carry_ref[...] = jnp.zeros_like(carry_ref)

    lane = lax.broadcasted_iota(I32, (tm, LANES), 1)
    lane_f = lane.astype(F32)
    lg = jnp.where(lane < N_EXPERTS, lg_ref[...], -jnp.inf)
    m1 = jnp.max(lg, axis=-1, keepdims=True)
    e1 = jnp.min(jnp.where(lg == m1, lane_f, float(LANES)), axis=-1, keepdims=True).astype(I32)
    lg2 = jnp.where(lane == e1, -jnp.inf, lg)
    m2 = jnp.max(lg2, axis=-1, keepdims=True)
    e2 = jnp.min(jnp.where(lg2 == m2, lane_f, float(LANES)), axis=-1, keepdims=True).astype(I32)
    ex = jnp.exp(m2 - m1)
    g1 = 1.0 / (1.0 + ex)
    g2 = ex / (1.0 + ex)
    onehot = jnp.where(jnp.logical_or(lane == e1, lane == e2), 1.0, 0.0)
    r = lax.broadcasted_iota(I32, (tm, tm), 0)
    c = lax.broadcasted_iota(I32, (tm, tm), 1)
    before = _dot(jnp.where(c < r, 1.0, 0.0).astype(BF16), onehot.astype(BF16)) + carry_ref[0:1, :]
    r1 = jnp.sum(jnp.where(lane == e1, before, 0.0), axis=-1, keepdims=True).astype(I32)
    r2 = jnp.sum(jnp.where(lane == e2, before, 0.0), axis=-1, keepdims=True).astype(I32)
    info = jnp.where(lane == 0, e1, jnp.where(lane == 1, e2, jnp.where(lane == 2, r1, r2)))
    info_ref[...] = info
    gate_ref[...] = jnp.where(lane == 0, g1, g2)
    total = carry_ref[0:1, :] + jnp.sum(onehot, axis=0, keepdims=True)
    carry_ref[...] = jnp.broadcast_to(total, carry_ref.shape)
    cnt_ref[...] = jnp.broadcast_to(total, cnt_ref.shape)


def _route(logits, *, tm=256, name):
    n = logits.shape[0]
    tm = min(tm, n)
    return pl.pallas_call(
        functools.partial(_route_kernel, tm=tm),
        out_shape=(jax.ShapeDtypeStruct((n, LANES), I32), jax.ShapeDtypeStruct((n, LANES), F32),
                   jax.ShapeDtypeStruct((8, LANES), F32)),
        grid=(n // tm,),
        in_specs=[pl.BlockSpec((tm, LANES), lambda i: (i, 0))],
        out_specs=(pl.BlockSpec((tm, LANES), lambda i: (i, 0)),
                   pl.BlockSpec((tm, LANES), lambda i: (i, 0)),
                   pl.BlockSpec((8, LANES), lambda i: (0, 0))),
        scratch_shapes=[pltpu.VMEM((8, LANES), F32)],
        compiler_params=_cparams(("arbitrary",)),
        name=name,
    )(logits)


def _combine_ln_kernel(d0_ref, d1_ref, h_ref, ys_ref, gate_ref, g_ref, b_ref, o_ref,
                       ya0_ref, ya1_ref, yb0_ref, yb1_ref, sems_a, sems_b, *, tm, slabs):
    i = pl.program_id(0)
    nt = pl.num_programs(0)
    bufs_a, bufs_b = (ya0_ref, ya1_ref), (yb0_ref, yb1_ref)

    def gather(tile, wait):
        for idx_ref, bufs, sems in ((d0_ref, bufs_a, sems_a), (d1_ref, bufs_b, sems_b)):
            _row_gather(ys_ref, idx_ref, tile * tm, bufs, sems, tile % 2, tm, slabs, wait=wait,
                        two_queues=True)

    @pl.when(i == 0)
    def _():
        gather(0, wait=False)

    gather(i, wait=True)

    @pl.when(i + 1 < nt)
    def _():
        gather(i + 1, wait=False)

    for s in range(2):
        @pl.when(i % 2 == s)
        def _(s=s):
            ff = (gate_ref[:, 0:1] * _slabs_to_rows(bufs_a[s], slabs)
                  + gate_ref[:, 1:2] * _slabs_to_rows(bufs_b[s], slabs))
            o_ref[...] = _layer_norm(DEEPNORM_ALPHA * h_ref[...] + ff, g_ref[...], b_ref[...])


def _combine_ln(dest0, dest1, h, ys, gates, g, b, *, tm=256, name):
    n, d = h.shape
    slabs = d // LANES
    tm = min(tm, n)
    row = lambda i, d0, d1: (i, 0)
    const = lambda i, d0, d1: (0, 0)
    slab_buf = pltpu.VMEM((tm * slabs, LANES), F32)
    return pl.pallas_call(
        functools.partial(_combine_ln_kernel, tm=tm, slabs=slabs),
        out_shape=jax.ShapeDtypeStruct((n, d), F32),
        grid_spec=pltpu.PrefetchScalarGridSpec(
            num_scalar_prefetch=2,
            grid=(n // tm,),
            in_specs=[pl.BlockSpec((tm, d), row),
                      pl.BlockSpec(memory_space=pl.ANY),
                      pl.BlockSpec((tm, LANES), row),
                      pl.BlockSpec((1, d), const),
                      pl.BlockSpec((1, d), const)],
            out_specs=pl.BlockSpec((tm, d), row),
            scratch_shapes=[slab_buf, slab_buf, slab_buf, slab_buf,
                            pltpu.SemaphoreType.DMA((2,)), pltpu.SemaphoreType.DMA((2,))]),
        compiler_params=_cparams(("arbitrary",)),
        name=name,
    )(dest0, dest1, h, ys, gates, g, b)


def _rel_bucket(dist):
    n = jnp.maximum(dist, 0)
    exact = REL_BUCKETS // 2
    nf = jnp.maximum(n, 1).astype(F32)
    large = exact + (jnp.log(nf / exact) / math.log(REL_MAX_DIST / exact) * (REL_BUCKETS - exact)).astype(I32)
    large = jnp.minimum(large, REL_BUCKETS - 1)
    return jnp.where(n < exact, n, large)


def _bias_tile_kernel(tab_ref, bucket_ref, o_ref):
    h = pl.program_id(1)
    bucket = bucket_ref[...]
    far = tab_ref[REL_BUCKETS - 1, h]
    acc = jnp.zeros(o_ref.shape, F32)
    for b in range(REL_BUCKETS - 1):
        acc = jnp.where(bucket == b, tab_ref[b, h] - far, acc)
    o_ref[...] = acc * LOG2E


def _bias_tiles(rel_table, n_heads, t, *, name):
    assert t >= REL_MAX_DIST
    r = jnp.arange(t)[:, None]
    c = jnp.arange(t)[None, :]
    buckets = jnp.stack([_rel_bucket(r - c + off) for off in (0, t)]).astype(I32)
    return pl.pallas_call(
        _bias_tile_kernel,
        out_shape=jax.ShapeDtypeStruct((2, n_heads, t, t), F32),
        grid=(2, n_heads),
        in_specs=[pl.BlockSpec(memory_space=pltpu.SMEM),
                  pl.BlockSpec((None, t, t), lambda k, h: (k, 0, 0))],
        out_specs=pl.BlockSpec((None, None, t, t), lambda k, h: (k, h, 0, 0)),
        compiler_params=_cparams(("parallel", "parallel")),
        name=name,
    )(rel_table.astype(F32), buckets)


def _pad_cols(w, n):
    return jnp.pad(w, ((0, 0), (0, n - w.shape[1])))


def _even_layer(h, rel_table, w_in, b_forget, w_out, ln1_g, ln1_b, w1, w3, w2, ln2_g, ln2_b,
                *, batch, seq):
    d = h.shape[1]
    wa = N_MOBA_HEADS * HEAD_DIM
    wb = N_FOX_HEADS * HEAD_DIM
    n_qkv = 3 * wa + 3 * wb
    scale = HEAD_DIM ** -0.5 * LOG2E
    ones, scl = jnp.ones((wa,), F32), jnp.full((wa,), scale, F32)
    colscale = jnp.concatenate([scl, ones, ones, scl, ones, ones])[None, :]
    qkv = _mm(h, w_in[:, :n_qkv].astype(BF16), colscale, out_dtype=BF16, tm=1024, tn=768,
              name="ev_qkv_proj")
    fb = _mm(h, _pad_cols(w_in[:, n_qkv:], LANES).astype(BF16), jnp.ones((1, LANES), F32),
             out_dtype=F32, tn=LANES, name="ev_forget_proj")
    csum = _forget_csum(fb, _pad_cols(b_forget.astype(F32)[None, :], LANES), batch=batch, seq=seq,
                        n_heads=N_FOX_HEADS, name="ev_forget_csum")
    csum = csum.reshape(batch, N_FOX_HEADS, seq // ATTN_TILE, 1, ATTN_TILE)
    oa = _moba_attention(qkv, _bias_tiles(rel_table, N_MOBA_HEADS, MOBA_BLOCK, name="ev_bias_tiles"),
                         batch=batch, seq=seq, name="ev_moba_attn")
    ob = _fox_attention(qkv, csum, batch=batch, seq=seq, name="ev_fox_attn")
    attn = jnp.concatenate([oa, ob], axis=-1)
    h = _mm_ln(attn, w_out.astype(BF16), h, ln1_g[None, :], ln1_b[None, :], name="ev_out_proj_ln")
    return _ffn_ln(h, w1.astype(BF16), w3.astype(BF16), w2.astype(BF16),
                   ln2_g[None, :], ln2_b[None, :], name="ev_swiglu_ln")


def _moe(h, h_slabs, logits, w1, w3, w2, ln_g, ln_b):
    n, d = h.shape
    tm = EXPERT_TILE
    info, gates, cnt = _route(logits, name="od_route")
    counts = cnt[0, :N_EXPERTS].astype(I32)
    padded = (counts + tm - 1) // tm * tm
    pend = jnp.cumsum(padded)
    pstart = pend - padded
    e = info[:, :MOE_TOPK]
    dest = (pstart[e] + info[:, MOE_TOPK:2 * MOE_TOPK]).astype(I32)
    n_rows = -(-(n * MOE_TOPK + N_EXPERTS * (tm - 1)) // tm) * tm
    n_tiles = n_rows // tm
    tile_start = jnp.arange(n_tiles, dtype=I32) * tm
    tile_valid = (tile_start < pend[-1]).astype(I32)
    last = jnp.maximum(pend[-1] - 1, 0)
    tile_e = jnp.minimum(jnp.searchsorted(pend, jnp.minimum(tile_start, last), side="right"),
                         N_EXPERTS - 1).astype(I32)
    slabs = d // LANES
    tok = jnp.arange(n, dtype=I32) * slabs
    row_tok = jnp.zeros((n_rows,), I32).at[dest.T.reshape(-1)].set(jnp.concatenate([tok, tok]))
    ys = _moe_ffn(tile_e, tile_valid, row_tok, h_slabs, w1.astype(BF16),
                  w3.astype(BF16), w2.astype(BF16), tm=tm, name="od_moe_swiglu")
    return _combine_ln(dest[:, 0] * slabs, dest[:, 1] * slabs, h, ys, gates, ln_g[None, :],
                       ln_b[None, :], name="od_moe_combine_ln")


def _odd_layer(h, rel_table, w_in, q_norm_g, kv_norm_g, w_uq, w_qidx, w_uk, w_uv, w_out,
               ln1_g, ln1_b, router, w1, w3, w2, ln2_g, ln2_b, *, batch, seq):
    nh = N_DSA_HEADS
    rq, rkv = DSA_Q_RANK, DSA_KV_RANK
    scale = HEAD_DIM ** -0.5 * LOG2E
    w_in_p = jnp.concatenate([w_in[:, :rq + rkv],
                              _pad_cols(w_in[:, rq + rkv:rq + rkv + IDX_DIM], LANES),
                              _pad_cols(w_in[:, rq + rkv + IDX_DIM:], LANES)], axis=1)
    cq, ckv, kidx, widx = _dsa_in_proj(h, w_in_p.astype(BF16), q_norm_g[None, :], kv_norm_g[None, :],
                                       name="od_in_proj_rms")
    w_qidx_p = jnp.pad(w_qidx.reshape(rq, IDX_HEADS, IDX_DIM), ((0, 0), (0, 0), (0, LANES - IDX_DIM)))
    wq = jnp.concatenate([w_uq, w_qidx_p.reshape(rq, IDX_HEADS * LANES)], axis=1)
    qscale = jnp.concatenate([jnp.full((nh * HEAD_DIM,), scale, F32),
                              jnp.ones((IDX_HEADS * LANES,), F32)])[None, :]
    qh = _mm(cq, wq.astype(BF16), qscale, out_dtype=BF16, head_major=True, tm=512, tn=2048,
             name="od_q_proj")
    wkv = jnp.concatenate([w_uk.transpose(1, 0, 2).reshape(rkv, nh * HEAD_DIM),
                           w_uv.transpose(1, 0, 2).reshape(rkv, nh * HEAD_DIM)], axis=1)
    kvh = _mm(ckv, wkv.astype(BF16), jnp.ones((1, 2 * nh * HEAD_DIM), F32), out_dtype=BF16,
              head_major=True, tm=512, tn=2048, name="od_kv_proj")
    scores, thr, jcut = _idx_topk(qh, kidx, widx, batch=batch, seq=seq,
                                  ksel=min(DSA_TOPK_MAX, seq // 4), name="od_idx_topk")
    attn = _dsa_attention(qh, kvh, scores, thr, jcut,
                          _bias_tiles(rel_table, nh, ATTN_TILE, name="od_bias_tiles"),
                          batch=batch, seq=seq, name="od_dsa_attn")
    h, h_slabs, logits = _mm_ln_route(attn, w_out.astype(BF16), h, ln1_g[None, :], ln1_b[None, :],
                                      _pad_cols(router, LANES).astype(BF16), name="od_out_proj_ln")
    return _moe(h, h_slabs, logits, w1, w3, w2, ln2_g, ln2_b)


def kernel(x, rel_table, ev_w_in, ev_b_forget, ev_w_out, ev_ln1_g, ev_ln1_b, ev_ffn_w1, ev_ffn_w3, ev_ffn_w2, ev_ln2_g, ev_ln2_b, od_w_in, od_q_norm_g, od_kv_norm_g, od_w_uq, od_w_qidx, od_w_uk, od_w_uv, od_w_out, od_ln1_g, od_ln1_b, od_router, od_exp_w1, od_exp_w3, od_exp_w2, od_ln2_g, od_ln2_b):
    batch, seq, d = x.shape
    h = x.reshape(batch * seq, d)
    for layer in range(DEPTH):
        i = layer // 2
        if layer % 2 == 0:
            h = _even_layer(h, rel_table, ev_w_in[i], ev_b_forget[i], ev_w_out[i], ev_ln1_g[i],
                            ev_ln1_b[i], ev_ffn_w1[i], ev_ffn_w3[i], ev_ffn_w2[i], ev_ln2_g[i],
                            ev_ln2_b[i], batch=batch, seq=seq)
        else:
            h = _odd_layer(h, rel_table, od_w_in[i], od_q_norm_g[i], od_kv_norm_g[i], od_w_uq[i],
                           od_w_qidx[i], od_w_uk[i], od_w_uv[i], od_w_out[i], od_ln1_g[i],
                           od_ln1_b[i], od_router[i], od_exp_w1[i], od_exp_w3[i], od_exp_w2[i],
                           od_ln2_g[i], od_ln2_b[i], batch=batch, seq=seq)
    return h.reshape(batch, seq, d)
```
